```python
import math
import jax
import jax.numpy as jnp
import numpy as np

D_MODEL = 1024
BATCH = 4
SEQ = 4096
DEPTH = 2

N_A_LAYERS = DEPTH // 2
N_B_LAYERS = DEPTH - N_A_LAYERS
HEAD_DIM = 64
N_FOX_HEADS = 12
FOX_WIDTH = N_FOX_HEADS * HEAD_DIM
N_DIFF_HEADS = 6
DIFF_QK_WIDTH = N_DIFF_HEADS * 2 * HEAD_DIM
DIFF_V_WIDTH = N_DIFF_HEADS * 2 * HEAD_DIM
N_MEM = 256
N_MEM_HEADS = 4
MEM_WIDTH = N_MEM_HEADS * HEAD_DIM
MIX_WIDTH = FOX_WIDTH + MEM_WIDTH
A_IN_WIDTH = 3 * FOX_WIDTH + N_FOX_HEADS + MEM_WIDTH
B_IN_WIDTH = DIFF_QK_WIDTH + MEM_WIDTH
SHARED_KV_WIDTH = DIFF_QK_WIDTH + DIFF_V_WIDTH
D_FF = -(-8 * D_MODEL // (3 * 256)) * 256
BLOCK_Q = 128
ROPE_THETA = 10000.0
NORM_EPS = 1e-6

kernel_name = "yoco_fox_diffattn_hybrid"


def rmsnorm(x, g):
    xf = x.astype(jnp.float32)
    y = xf * jax.lax.rsqrt(jnp.mean(xf * xf, axis=-1, keepdims=True) + NORM_EPS)
    return (y * g.astype(jnp.float32)).astype(x.dtype)


def split_heads(t, n_heads):
    B, S, W = t.shape
    return t.reshape(B, S, n_heads, W // n_heads).transpose(0, 2, 1, 3)


def merge_heads(t):
    B, H, S, Dh = t.shape
    return t.transpose(0, 2, 1, 3).reshape(B, S, H * Dh)


def rope(t):
    S, Dh = t.shape[-2], t.shape[-1]
    half = Dh // 2
    inv_freq = jnp.power(ROPE_THETA, -jnp.arange(half, dtype=jnp.float32) * (2.0 / Dh))
    ang = jnp.arange(S, dtype=jnp.float32)[:, None] * inv_freq[None, :]
    cos, sin = jnp.cos(ang), jnp.sin(ang)
    tf = t.astype(jnp.float32)
    t1, t2 = tf[..., :half], tf[..., half:]
    return jnp.concatenate([t1 * cos - t2 * sin, t1 * sin + t2 * cos], axis=-1).astype(t.dtype)


def causal_mask(q0, k_len):
    qi = q0 + jnp.arange(BLOCK_Q)[:, None]
    ki = jnp.arange(k_len)[None, :]
    return qi >= ki


def fox_attention(q, k, v, log_f):
    S, Dh = q.shape[2], q.shape[3]
    scale = Dh ** -0.5
    c = jnp.cumsum(log_f, axis=-1)
    outs = []
    for blk in range(S // BLOCK_Q):
        q0 = blk * BLOCK_Q
        q_end = q0 + BLOCK_Q
        s = jnp.einsum('bhqd,bhkd->bhqk', q[:, :, q0:q_end], k[:, :, :q_end]).astype(jnp.float32) * scale
        s = s + (c[:, :, q0:q_end, None] - c[:, :, None, :q_end])
        p = jax.nn.softmax(jnp.where(causal_mask(q0, q_end), s, -jnp.inf), axis=-1)
        outs.append(jnp.einsum('bhqk,bhkd->bhqd', p.astype(v.dtype), v[:, :, :q_end]))
    return jnp.concatenate(outs, axis=2)


def diff_attention(q1, q2, k1, k2, v, lam):
    S, Dh = q1.shape[2], q1.shape[3]
    scale = Dh ** -0.5
    outs = []
    for blk in range(S // BLOCK_Q):
        q0 = blk * BLOCK_Q
        q_end = q0 + BLOCK_Q
        mask = causal_mask(q0, q_end)
        s1 = jnp.einsum('bhqd,bhkd->bhqk', q1[:, :, q0:q_end], k1[:, :, :q_end]).astype(jnp.float32) * scale
        s2 = jnp.einsum('bhqd,bhkd->bhqk', q2[:, :, q0:q_end], k2[:, :, :q_end]).astype(jnp.float32) * scale
        p = (jax.nn.softmax(jnp.where(mask, s1, -jnp.inf), axis=-1)
             - lam * jax.nn.softmax(jnp.where(mask, s2, -jnp.inf), axis=-1))
        outs.append(jnp.einsum('bhqk,bhkd->bhqd', p.astype(v.dtype), v[:, :, :q_end]))
    return jnp.concatenate(outs, axis=2)


def memory_kv(mem, g, w_kv):
    kv = rmsnorm(mem, g) @ w_kv
    k, v = jnp.split(kv, 2, axis=-1)
    return split_heads(k, N_MEM_HEADS), split_heads(v, N_MEM_HEADS)


def memory_attention(mq, mem_k, mem_v):
    scale = mq.shape[-1] ** -0.5
    s = jnp.einsum('bhqd,bhmd->bhqm', mq, mem_k).astype(jnp.float32) * scale
    p = jax.nn.softmax(s, axis=-1)
    return jnp.einsum('bhqm,bhmd->bhqd', p.astype(mem_v.dtype), mem_v)


def fox_mixer(h, mem_k, mem_v, w_in, b_f):
    proj = h @ w_in
    q, k, v, f_logit, mq = jnp.split(
        proj, [FOX_WIDTH, 2 * FOX_WIDTH, 3 * FOX_WIDTH, 3 * FOX_WIDTH + N_FOX_HEADS], axis=-1)
    log_f = jax.nn.log_sigmoid(f_logit.astype(jnp.float32) + b_f.astype(jnp.float32)).transpose(0, 2, 1)
    y_fox = fox_attention(split_heads(q, N_FOX_HEADS), split_heads(k, N_FOX_HEADS),
                          split_heads(v, N_FOX_HEADS), log_f)
    y_mem = memory_attention(split_heads(mq, N_MEM_HEADS), mem_k, mem_v)
    return jnp.concatenate([merge_heads(y_fox), merge_heads(y_mem)], axis=-1)


def shared_kv(x, g, w):
    B, S, _ = x.shape
    kv = rmsnorm(x, g) @ w
    k, v = jnp.split(kv, [DIFF_QK_WIDTH], axis=-1)
    k = k.reshape(B, S, N_DIFF_HEADS, 2, HEAD_DIM).transpose(0, 2, 3, 1, 4)
    return rope(k[:, :, 0]), rope(k[:, :, 1]), split_heads(v, N_DIFF_HEADS)


def diff_mixer(h, k1, k2, v, mem_k, mem_v, w_in, lq1, lk1, lq2, lk2, subln_g, lambda_init):
    B, S, _ = h.shape
    proj = h @ w_in
    q, mq = jnp.split(proj, [DIFF_QK_WIDTH], axis=-1)
    q = q.reshape(B, S, N_DIFF_HEADS, 2, HEAD_DIM).transpose(0, 2, 3, 1, 4)
    q1, q2 = rope(q[:, :, 0]), rope(q[:, :, 1])
    lam = (jnp.exp(jnp.sum(lq1.astype(jnp.float32) * lk1.astype(jnp.float32)))
           - jnp.exp(jnp.sum(lq2.astype(jnp.float32) * lk2.astype(jnp.float32))) + lambda_init)
    y = diff_attention(q1, q2, k1, k2, v, lam)
    y = rmsnorm(y, subln_g) * (1.0 - lambda_init)
    y_mem = memory_attention(split_heads(mq, N_MEM_HEADS), mem_k, mem_v)
    return jnp.concatenate([merge_heads(y), merge_heads(y_mem)], axis=-1)


def swiglu(h, w_gate_up, w_down):
    g, u = jnp.split(h @ w_gate_up, 2, axis=-1)
    return (jax.nn.silu(g) * u) @ w_down


def setup_inputs(seed: int = 0) -> dict:
    key = jax.random.key(seed)
    ks = jax.random.split(key, 20)

    def nrm(k, shape, fan_in):
        return jax.random.normal(k, shape, jnp.float32) * (fan_in ** -0.5)

    def gain(k, shape):
        return 1.0 + 0.02 * jax.random.normal(k, shape, jnp.float32)

    return {
        "x": jax.random.normal(ks[0], (BATCH, SEQ, D_MODEL), jnp.float32),
        "mem": jax.random.normal(ks[1], (BATCH, N_MEM, D_MODEL), jnp.float32),
        "attn_norm_g": gain(ks[2], (DEPTH, D_MODEL)),
        "mem_norm_g": gain(ks[3], (DEPTH, D_MODEL)),
        "w_mem_kv": nrm(ks[4], (DEPTH, D_MODEL, 2 * MEM_WIDTH), D_MODEL),
        "w_out": nrm(ks[5], (DEPTH, MIX_WIDTH, D_MODEL), MIX_WIDTH),
        "ffn_norm_g": gain(ks[6], (DEPTH, D_MODEL)),
        "w_gate_up": nrm(ks[7], (DEPTH, D_MODEL, 2 * D_FF), D_MODEL),
        "w_down": nrm(ks[8], (DEPTH, D_FF, D_MODEL), D_FF),
        "a_w_in": nrm(ks[9], (N_A_LAYERS, D_MODEL, A_IN_WIDTH), D_MODEL),
        "a_b_f": jax.random.uniform(ks[10], (N_A_LAYERS, N_FOX_HEADS), jnp.float32, 1.0, 4.0),
        "b_w_in": nrm(ks[11], (N_B_LAYERS, D_MODEL, B_IN_WIDTH), D_MODEL),
        "b_lambda_q1": 0.1 * jax.random.normal(ks[12], (N_B_LAYERS, HEAD_DIM), jnp.float32),
        "b_lambda_k1": 0.1 * jax.random.normal(ks[13], (N_B_LAYERS, HEAD_DIM), jnp.float32),
        "b_lambda_q2": 0.1 * jax.random.normal(ks[14], (N_B_LAYERS, HEAD_DIM), jnp.float32),
        "b_lambda_k2": 0.1 * jax.random.normal(ks[15], (N_B_LAYERS, HEAD_DIM), jnp.float32),
        "b_subln_g": gain(ks[16], (N_B_LAYERS, 2 * HEAD_DIM)),
        "kv_norm_g": gain(ks[17], (D_MODEL,)),
        "w_kv_shared": nrm(ks[18], (D_MODEL, SHARED_KV_WIDTH), D_MODEL),
        "final_norm_g": gain(ks[19], (D_MODEL,)),
    }


def reference(x, mem, attn_norm_g, mem_norm_g, w_mem_kv, w_out, ffn_norm_g, w_gate_up, w_down,
              a_w_in, a_b_f, b_w_in, b_lambda_q1, b_lambda_k1, b_lambda_q2, b_lambda_k2,
              b_subln_g, kv_norm_g, w_kv_shared, final_norm_g):
    k1 = k2 = v_shared = None
    for layer in range(DEPTH):
        mem_k, mem_v = memory_kv(mem, mem_norm_g[layer], w_mem_kv[layer])
        if layer < N_A_LAYERS:
            h = rmsnorm(x, attn_norm_g[layer])
            mix = fox_mixer(h, mem_k, mem_v, a_w_in[layer], a_b_f[layer])
        else:
            if layer == N_A_LAYERS:
                k1, k2, v_shared = shared_kv(x, kv_norm_g, w_kv_shared)
            j = layer - N_A_LAYERS
            lambda_init = 0.8 - 0.6 * math.exp(-0.3 * layer)
            h = rmsnorm(x, attn_norm_g[layer])
            mix = diff_mixer(h, k1, k2, v_shared, mem_k, mem_v, b_w_in[j],
                             b_lambda_q1[j], b_lambda_k1[j], b_lambda_q2[j], b_lambda_k2[j],
                             b_subln_g[j], lambda_init)
        x = x + mix @ w_out[layer]
        x = x + swiglu(rmsnorm(x, ffn_norm_g[layer]), w_gate_up[layer], w_down[layer])
    return rmsnorm(x, final_norm_g)
```

```python
import functools
import math

import jax
import jax.numpy as jnp
from jax import lax
from jax.experimental import pallas as pl
from jax.experimental.pallas import tpu as pltpu

HEAD_DIM = 64
N_MEM_HEADS = 4
ROPE_THETA = 10000.0
NORM_EPS = 1e-6
LANES = 128
ATTN_TILE = 256
PROJ_TM = 512
PROJ_TN = 256
FFN_TM = 512
FFN_TF = 256
VMEM_LIMIT = 48 * 1024 * 1024

F32 = jnp.float32
BF16 = jnp.bfloat16
_NT = (((1,), (1,)), ((), ()))


def _rms(x, g):
    ms = jnp.mean(x * x, axis=-1, keepdims=True)
    return x * lax.rsqrt(ms + NORM_EPS) * g


def _norm_proj_kernel(*refs, rope_blocks, gates, n_seq_tiles):
    it = iter(refs)
    x_ref, g_ref, w_ref = next(it), next(it), next(it)
    if rope_blocks:
        cos_ref, sin_ref = next(it), next(it)
    if gates:
        wf_ref, bf_ref = next(it), next(it)
    o_ref = next(it)
    if gates:
        c_ref = next(it)
    hn_ref = next(it)
    if gates:
        carry_ref = next(it)

    i = pl.program_id(0)
    j = pl.program_id(1)
    tm = x_ref.shape[0]

    @pl.when(j == 0)
    def _():
        hn = _rms(x_ref[...], g_ref[...])
        hn_ref[...] = hn.astype(BF16)
        if gates:
            z = jnp.dot(hn, wf_ref[...], precision=lax.Precision.HIGHEST,
                        preferred_element_type=F32) + bf_ref[...]
            lf = jnp.minimum(z, 0.0) - jnp.log1p(jnp.exp(-jnp.abs(z)))
            row = lax.broadcasted_iota(jnp.int32, (tm, tm), 0)
            col = lax.broadcasted_iota(jnp.int32, (tm, tm), 1)
            tri = (row >= col).astype(F32)
            cs = jnp.dot(tri, lf, precision=lax.Precision.HIGHEST, preferred_element_type=F32)
            @pl.when((i % n_seq_tiles) == 0)
            def _():
                carry_ref[...] = jnp.zeros_like(carry_ref)

            c = cs + carry_ref[...]
            c_ref[...] = c
            carry_ref[...] = c[tm - 1:tm, :]

    acc = jnp.dot(hn_ref[...], w_ref[...], preferred_element_type=F32)

    if rope_blocks == 0:
        o_ref[...] = acc.astype(o_ref.dtype)
    else:
        @pl.when(j < rope_blocks)
        def _():
            cos = cos_ref[...]
            sin = sin_ref[...]
            lane = lax.broadcasted_iota(jnp.int32, (tm, LANES), 1)
            first_half = (lane % HEAD_DIM) < (HEAD_DIM // 2)
            for c0 in range(0, acc.shape[1], LANES):
                t = acc[:, c0:c0 + LANES]
                up = pltpu.roll(t, HEAD_DIM // 2, 1)
                dn = pltpu.roll(t, LANES - HEAD_DIM // 2, 1)
                sw = jnp.where(first_half, dn, up)
                o_ref[:, c0:c0 + LANES] = (t * cos + sw * sin).astype(o_ref.dtype)

        @pl.when(j >= rope_blocks)
        def _():
            o_ref[...] = acc.astype(o_ref.dtype)


def _norm_proj(x2d, g, w, *, seq, rope_cols=0, rope_tabs=None, gate_w=None, gate_b=None):
    T, D = x2d.shape
    N = w.shape[1]
    tm = min(PROJ_TM, T)
    tn = min(PROJ_TN, N)
    assert T % tm == 0 and N % tn == 0 and rope_cols % tn == 0
    gates = gate_w is not None
    rope_blocks = rope_cols // tn
    assert seq % tm == 0 or not (gates or rope_blocks)
    n_seq_tiles = max(1, seq // tm)

    in_specs = [
        pl.BlockSpec((tm, D), lambda i, j: (i, 0)),
        pl.BlockSpec((1, D), lambda i, j: (0, 0)),
        pl.BlockSpec((D, tn), lambda i, j: (0, j)),
    ]
    args = [x2d, g.reshape(1, D), w]
    if rope_blocks:
        tab_spec = pl.BlockSpec((tm, LANES), lambda i, j: (i % n_seq_tiles, 0))
        in_specs += [tab_spec, tab_spec]
        args += list(rope_tabs)
    if gates:
        in_specs += [pl.BlockSpec((D, LANES), lambda i, j: (0, 0)),
                     pl.BlockSpec((1, LANES), lambda i, j: (0, 0))]
        args += [gate_w, gate_b]

    out_shape = [jax.ShapeDtypeStruct((T, N), BF16)]
    out_specs = [pl.BlockSpec((tm, tn), lambda i, j: (i, j))]
    scratch = [pltpu.VMEM((tm, D), BF16)]
    if gates:
        out_shape.append(jax.ShapeDtypeStruct((T, LANES), F32))
        out_specs.append(pl.BlockSpec((tm, LANES), lambda i, j: (i, 0)))
        scratch.append(pltpu.VMEM((1, LANES), F32))

    outs = pl.pallas_call(
        functools.partial(_norm_proj_kernel, rope_blocks=rope_blocks, gates=gates,
                          n_seq_tiles=n_seq_tiles),
        grid=(T // tm, N // tn),
        in_specs=in_specs,
        out_specs=out_specs,
        out_shape=out_shape,
        scratch_shapes=scratch,
        compiler_params=pltpu.CompilerParams(
            dimension_semantics=("arbitrary", "arbitrary"), vmem_limit_bytes=VMEM_LIMIT),
        name="norm_proj_gates" if gates else ("norm_proj_rope" if rope_blocks else "norm_proj"),
    )(*args)
    return outs if gates else outs[0]


def _fox_attn_kernel(q_ref, k_ref, vt_ref, c_ref, o_ref, qa_ref, ka_ref, yt_ref, *, tile):
    seq = q_ref.shape[1]
    n_tiles = seq // tile
    hp = pl.program_id(1)
    lane = lax.broadcasted_iota(jnp.int32, (tile, LANES), 1)

    def split3(v):
        hi = v.astype(BF16).astype(F32)
        r1 = v - hi
        mid = r1.astype(BF16).astype(F32)
        return hi, mid, r1 - mid

    def build(ci, carry):
        r0 = pl.multiple_of(ci * tile, tile)
        qp = q_ref[0, pl.ds(r0, tile), :].astype(F32)
        kp = k_ref[0, pl.ds(r0, tile), :].astype(F32)
        cc = c_ref[0, pl.ds(r0, tile), :]
        for h2 in range(2):
            head = 2 * hp + h2
            ch = jnp.sum(jnp.where(lane == head, cc, 0.0), axis=1, keepdims=True)
            hi, mid, lo = split3(jnp.broadcast_to(ch, (tile, LANES)))
            if h2 == 0:
                qx, kx = qp, kp
            else:
                qx = pltpu.roll(qp, HEAD_DIM, 1)
                kx = pltpu.roll(kp, HEAD_DIM, 1)
            d = HEAD_DIM
            k_extra = jnp.where(lane == d, hi, jnp.where(lane == d + 1, mid, jnp.where(
                lane == d + 2, lo, jnp.where(lane < d + 6, 1.0, 0.0))))
            q_extra = jnp.where(lane < d + 3, -1.0, jnp.where(lane == d + 3, hi, jnp.where(
                lane == d + 4, mid, jnp.where(lane == d + 5, lo, 0.0))))
            ka_ref[h2, pl.ds(r0, tile), :] = jnp.where(lane < d, kx, k_extra).astype(BF16)
            qa_ref[h2, pl.ds(r0, tile), :] = jnp.where(lane < d, qx, q_extra).astype(BF16)
        return carry

    lax.fori_loop(0, n_tiles, build, 0)

    kk = lax.broadcasted_iota(jnp.int32, (tile, tile), 0)
    qq = lax.broadcasted_iota(jnp.int32, (tile, tile), 1)
    causal = kk <= qq

    for h2 in range(2):
        rows = slice(h2 * HEAD_DIM, (h2 + 1) * HEAD_DIM)

        def q_body(qi, carry, h2=h2, rows=rows):
            q0 = pl.multiple_of(qi * tile, tile)
            qa = qa_ref[h2, pl.ds(q0, tile), :]
            st = lax.dot_general(ka_ref[h2, pl.ds(q0, tile), :], qa, _NT,
                                 preferred_element_type=F32)
            st = jnp.where(causal, st, -jnp.inf)
            m = jnp.max(st, axis=0, keepdims=True)
            p = jnp.exp(st - m)
            l = jnp.sum(p, axis=0, keepdims=True)
            acc = jnp.dot(vt_ref[0, 0, qi, rows, :], p.astype(BF16), preferred_element_type=F32)

            def kv_body(j, c):
                m, l, acc = c
                k0 = pl.multiple_of(j * tile, tile)
                st = lax.dot_general(ka_ref[h2, pl.ds(k0, tile), :], qa, _NT,
                                     preferred_element_type=F32)
                m_new = jnp.maximum(m, jnp.max(st, axis=0, keepdims=True))
                alpha = jnp.exp(m - m_new)
                p = jnp.exp(st - m_new)
                l = alpha * l + jnp.sum(p, axis=0, keepdims=True)
                acc = alpha * acc + jnp.dot(vt_ref[0, 0, j, rows, :], p.astype(BF16),
                                            preferred_element_type=F32)
                return m_new, l, acc

            m, l, acc = lax.fori_loop(0, qi, kv_body, (m, l, acc))
            yt_ref[qi, rows, :] = acc / l
            return carry

        lax.fori_loop(0, n_tiles, q_body, 0)

    def emit(qi, carry):
        q0 = pl.multiple_of(qi * tile, tile)
        o_ref[0, pl.ds(q0, tile), :] = yt_ref[qi].T.astype(o_ref.dtype)
        return carry

    lax.fori_loop(0, n_tiles, emit, 0)


def _fox_attn(proj, vt, c, *, n_heads, k_col, out_width):
    B, S, _ = proj.shape
    tile = ATTN_TILE
    n_tiles = S // tile
    n_pairs = n_heads // 2
    return pl.pallas_call(
        functools.partial(_fox_attn_kernel, tile=tile),
        grid=(B, n_pairs),
        in_specs=[
            pl.BlockSpec((1, S, LANES), lambda b, h: (b, 0, h)),
            pl.BlockSpec((1, S, LANES), lambda b, h: (b, 0, k_col + h)),
            pl.BlockSpec((1, 1, n_tiles, LANES, tile), lambda b, h: (b, h, 0, 0, 0)),
            pl.BlockSpec((1, S, LANES), lambda b, h: (b, 0, 0)),
        ],
        out_specs=pl.BlockSpec((1, S, LANES), lambda b, h: (b, 0, h)),
        out_shape=jax.ShapeDtypeStruct((B, S, out_width), BF16),
        scratch_shapes=[
            pltpu.VMEM((2, S, LANES), BF16),
            pltpu.VMEM((2, S, LANES), BF16),
            pltpu.VMEM((n_tiles, LANES, tile), F32),
        ],
        compiler_params=pltpu.CompilerParams(
            dimension_semantics=("arbitrary", "arbitrary"), vmem_limit_bytes=VMEM_LIMIT),
        name="fox_attn",
    )(proj, proj, vt, c)


def _diff_attn_kernel(q_ref, k_ref, vt_ref, lq1_ref, lk1_ref, lq2_ref, lk2_ref, g_ref, o_ref, *,
                      tile, lambda_init):
    seq = q_ref.shape[1]
    n_tiles = seq // tile
    lane = lax.broadcasted_iota(jnp.int32, (tile, LANES), 1)
    kk = lax.broadcasted_iota(jnp.int32, (tile, tile), 0)
    qq = lax.broadcasted_iota(jnp.int32, (tile, tile), 1)
    causal = kk <= qq
    lam = (jnp.exp(jnp.sum(lq1_ref[...] * lk1_ref[...], axis=1, keepdims=True))
           - jnp.exp(jnp.sum(lq2_ref[...] * lk2_ref[...], axis=1, keepdims=True)) + lambda_init)

    def q_body(qi, carry):
        q0 = pl.multiple_of(qi * tile, tile)
        qp = q_ref[0, pl.ds(q0, tile), :]
        zero = jnp.zeros_like(qp)
        qms = (jnp.where(lane < HEAD_DIM, qp, zero), jnp.where(lane >= HEAD_DIM, qp, zero))
        kd = k_ref[0, pl.ds(q0, tile), :]
        vd = vt_ref[0, 0, qi]
        state = []
        for qm in qms:
            st = lax.dot_general(kd, qm, _NT, preferred_element_type=F32)
            st = jnp.where(causal, st, -jnp.inf)
            m = jnp.max(st, axis=0, keepdims=True)
            p = jnp.exp(st - m)
            l = jnp.sum(p, axis=0, keepdims=True)
            acc = jnp.dot(vd, p.astype(BF16), preferred_element_type=F32)
            state += [m, l, acc]

        def kv_body(j, c):
            k0 = pl.multiple_of(j * tile, tile)
            kj = k_ref[0, pl.ds(k0, tile), :]
            vj = vt_ref[0, 0, j]
            out = []
            for s_i, qm in enumerate(qms):
                m, l, acc = c[3 * s_i:3 * s_i + 3]
                st = lax.dot_general(kj, qm, _NT, preferred_element_type=F32)
                m_new = jnp.maximum(m, jnp.max(st, axis=0, keepdims=True))
                alpha = jnp.exp(m - m_new)
                p = jnp.exp(st - m_new)
                l = alpha * l + jnp.sum(p, axis=0, keepdims=True)
                acc = alpha * acc + jnp.dot(vj, p.astype(BF16), preferred_element_type=F32)
                out += [m_new, l, acc]
            return tuple(out)

        _, l1, a1, _, l2, a2 = lax.fori_loop(0, qi, kv_body, tuple(state))
        yt = a1 / l1 - lam * (a2 / l2)
        y = _rms(yt.T, g_ref[...]) * (1.0 - lambda_init)
        o_ref[0, pl.ds(q0, tile), :] = y.astype(o_ref.dtype)
        return carry

    lax.fori_loop(0, n_tiles, q_body, 0)


def _diff_attn(qproj, kvproj, vt, lq1, lk1, lq2, lk2, subln_g, *, n_heads, lambda_init):
    B, S, _ = qproj.shape
    tile = ATTN_TILE
    n_tiles = S // tile
    vec = pl.BlockSpec((1, HEAD_DIM), lambda b, h: (0, 0))
    return pl.pallas_call(
        functools.partial(_diff_attn_kernel, tile=tile, lambda_init=lambda_init),
        grid=(B, n_heads),
        in_specs=[
            pl.BlockSpec((1, S, LANES), lambda b, h: (b, 0, h)),
            pl.BlockSpec((1, S, LANES), lambda b, h: (b, 0, h)),
            pl.BlockSpec((1, 1, n_tiles, LANES, tile), lambda b, h: (b, h, 0, 0, 0)),
            vec, vec, vec, vec,
            pl.BlockSpec((1, 2 * HEAD_DIM), lambda b, h: (0, 0)),
        ],
        out_specs=pl.BlockSpec((1, S, LANES), lambda b, h: (b, 0, h)),
        out_shape=jax.ShapeDtypeStruct((B, S, n_heads * 2 * HEAD_DIM), BF16),
        compiler_params=pltpu.CompilerParams(
            dimension_semantics=("arbitrary", "arbitrary"), vmem_limit_bytes=VMEM_LIMIT),
        name="diff_attn",
    )(qproj, kvproj, vt, lq1.reshape(1, -1), lk1.reshape(1, -1), lq2.reshape(1, -1),
      lk2.reshape(1, -1), subln_g.reshape(1, -1))


def _mix_out_kernel(x_ref, y_ref, mq_ref, mk_ref, mv_ref, wo_ref, o_ref):
    tm = x_ref.shape[0]
    mw = mq_ref.shape[1]
    y_width = y_ref.shape[1]
    mq = mq_ref[...]
    mk = mk_ref[0]
    mv = mv_ref[0]
    q_head = lax.broadcasted_iota(jnp.int32, (tm, mw), 1) // HEAD_DIM
    v_head = lax.broadcasted_iota(jnp.int32, mv.shape, 1) // HEAD_DIM
    ymem = jnp.zeros((tm, mw), F32)
    for h in range(mw // HEAD_DIM):
        qh = jnp.where(q_head == h, mq, jnp.zeros_like(mq))
        s = lax.dot_general(qh, mk, _NT, preferred_element_type=F32)
        p = jnp.exp(s - jnp.max(s, axis=1, keepdims=True))
        l = jnp.sum(p, axis=1, keepdims=True)
        vh = jnp.where(v_head == h, mv, jnp.zeros_like(mv))
        ymem = ymem + jnp.dot(p.astype(BF16), vh, preferred_element_type=F32) / l
    acc = jnp.dot(y_ref[...], wo_ref[0:y_width, :], preferred_element_type=F32)
    acc = acc + jnp.dot(ymem.astype(BF16), wo_ref[y_width:y_width + mw, :],
                        preferred_element_type=F32)
    o_ref[...] = x_ref[...] + acc


def _mix_out(x2d, y2d, proj2d, mq_col, memkv, wo, *, seq):
    T, D = x2d.shape
    yw = y2d.shape[1]
    n_mem, mw2 = memkv.shape[1], memkv.shape[2]
    mw = mw2 // 2
    tm = PROJ_TM
    n_seq_tiles = seq // tm
    return pl.pallas_call(
        _mix_out_kernel,
        grid=(T // tm,),
        in_specs=[
            pl.BlockSpec((tm, D), lambda i: (i, 0)),
            pl.BlockSpec((tm, yw), lambda i: (i, 0)),
            pl.BlockSpec((tm, mw), lambda i: (i, mq_col)),
            pl.BlockSpec((1, n_mem, mw), lambda i: (i // n_seq_tiles, 0, 0)),
            pl.BlockSpec((1, n_mem, mw), lambda i: (i // n_seq_tiles, 0, 1)),
            pl.BlockSpec((yw + mw, D), lambda i: (0, 0)),
        ],
        out_specs=pl.BlockSpec((tm, D), lambda i: (i, 0)),
        out_shape=jax.ShapeDtypeStruct((T, D), F32),
        compiler_params=pltpu.CompilerParams(
            dimension_semantics=("arbitrary",), vmem_limit_bytes=VMEM_LIMIT),
        name="mix_out",
    )(x2d, y2d, proj2d, memkv, memkv, wo)


def _ffn_kernel(*refs, final):
    if final:
        x_ref, g_ref, wg_ref, wu_ref, wd_ref, fg_ref, o_ref, hn_ref, acc_ref = refs
    else:
        x_ref, g_ref, wg_ref, wu_ref, wd_ref, o_ref, hn_ref, acc_ref = refs
    k = pl.program_id(1)

    @pl.when(k == 0)
    def _():
        hn_ref[...] = _rms(x_ref[...], g_ref[...]).astype(BF16)
        acc_ref[...] = jnp.zeros_like(acc_ref)

    h = hn_ref[...]
    gate = jnp.dot(h, wg_ref[...], preferred_element_type=F32)
    up = jnp.dot(h, wu_ref[...], preferred_element_type=F32)
    a = gate * jax.nn.sigmoid(gate) * up
    acc_ref[...] += jnp.dot(a.astype(BF16), wd_ref[...], preferred_element_type=F32)

    @pl.when(k == pl.num_programs(1) - 1)
    def _():
        out = x_ref[...] + acc_ref[...]
        if final:
            out = _rms(out, fg_ref[...])
        o_ref[...] = out


def _ffn(x2d, g, w_gate_up, w_down, final_g=None):
    T, D = x2d.shape
    d_ff = w_down.shape[0]
    tm, tf = FFN_TM, FFN_TF
    assert T % tm == 0 and d_ff % tf == 0
    nk = d_ff // tf
    final = final_g is not None
    in_specs = [
        pl.BlockSpec((tm, D), lambda i, k: (i, 0)),
        pl.BlockSpec((1, D), lambda i, k: (0, 0)),
        pl.BlockSpec((D, tf), lambda i, k: (0, k)),
        pl.BlockSpec((D, tf), lambda i, k: (0, k + nk)),
        pl.BlockSpec((tf, D), lambda i, k: (k, 0)),
    ]
    args = [x2d, g.reshape(1, D), w_gate_up, w_gate_up, w_down]
    if final:
        in_specs.append(pl.BlockSpec((1, D), lambda i, k: (0, 0)))
        args.append(final_g.reshape(1, D))
    return pl.pallas_call(
        functools.partial(_ffn_kernel, final=final),
        grid=(T // tm, nk),
        in_specs=in_specs,
        out_specs=pl.BlockSpec((tm, D), lambda i, k: (i, 0)),
        out_shape=jax.ShapeDtypeStruct((T, D), F32),
        scratch_shapes=[pltpu.VMEM((tm, D), BF16), pltpu.VMEM((tm, D), F32)],
        compiler_params=pltpu.CompilerParams(
            dimension_semantics=("arbitrary", "arbitrary"), vmem_limit_bytes=VMEM_LIMIT),
        name="ffn_final" if final else "ffn",
    )(*args)


def _rope_tables(seq):
    half = HEAD_DIM // 2
    inv_freq = jnp.power(ROPE_THETA, -jnp.arange(half, dtype=F32) * (2.0 / HEAD_DIM))
    ang = jnp.arange(seq, dtype=F32)[:, None] * inv_freq[None, :]
    cos, sin = jnp.cos(ang), jnp.sin(ang)
    reps = LANES // HEAD_DIM
    cos_t = jnp.tile(jnp.concatenate([cos, cos], axis=-1), (1, reps))
    sin_t = jnp.tile(jnp.concatenate([-sin, sin], axis=-1), (1, reps))
    return cos_t, sin_t


def _value_tiles_t(v, tile):
    B, S, W = v.shape
    v = v.reshape(B, S // tile, tile, W // LANES, LANES)
    return v.transpose(0, 3, 1, 4, 2)


def kernel(x, mem, attn_norm_g, mem_norm_g, w_mem_kv, w_out, ffn_norm_g, w_gate_up, w_down,
           a_w_in, a_b_f, b_w_in, b_lambda_q1, b_lambda_k1, b_lambda_q2, b_lambda_k2,
           b_subln_g, kv_norm_g, w_kv_shared, final_norm_g):
    B, S, D = x.shape
    depth = attn_norm_g.shape[0]
    n_a = a_w_in.shape[0]
    n_mem = mem.shape[1]
    mem_w = w_mem_kv.shape[2] // 2
    n_fox = a_b_f.shape[1]
    fox_w = n_fox * HEAD_DIM
    diff_w = b_w_in.shape[1] - mem_w
    n_diff = diff_w // (2 * HEAD_DIM)
    scale = HEAD_DIM ** -0.5
    T = B * S

    x2d = x.reshape(T, D)
    mem2d = mem.reshape(B * n_mem, D)
    rope_tabs = _rope_tables(S)
    k_sh = vt_sh = None

    for layer in range(depth):
        memkv = _norm_proj(mem2d, mem_norm_g[layer], w_mem_kv[layer].astype(BF16), seq=n_mem)
        memkv = memkv.reshape(B, n_mem, 2 * mem_w)
        wo = w_out[layer].astype(BF16)
        if layer < n_a:
            w_in = a_w_in[layer]
            w_main = jnp.concatenate(
                [w_in[:, :fox_w] * scale, w_in[:, fox_w:3 * fox_w],
                 w_in[:, 3 * fox_w + n_fox:] * scale], axis=1).astype(BF16)
            gate_w = jnp.pad(w_in[:, 3 * fox_w:3 * fox_w + n_fox], ((0, 0), (0, LANES - n_fox)))
            gate_b = jnp.pad(a_b_f[layer], (0, LANES - n_fox)).reshape(1, LANES)
            proj, c = _norm_proj(x2d, attn_norm_g[layer], w_main, seq=S,
                                 gate_w=gate_w, gate_b=gate_b)
            proj3 = proj.reshape(B, S, -1)
            vt = _value_tiles_t(proj3[:, :, 2 * fox_w:3 * fox_w], ATTN_TILE)
            y = _fox_attn(proj3, vt, c.reshape(B, S, LANES), n_heads=n_fox,
                          k_col=fox_w // LANES, out_width=fox_w)
            mq_col = 3 * fox_w // mem_w
        else:
            j = layer - n_a
            if layer == n_a:
                kvp = _norm_proj(x2d, kv_norm_g, w_kv_shared.astype(BF16), seq=S,
                                 rope_cols=diff_w, rope_tabs=rope_tabs)
                k_sh = kvp.reshape(B, S, -1)
                vt_sh = _value_tiles_t(k_sh[:, :, diff_w:], ATTN_TILE)
            w_q = b_w_in[j] * scale
            proj = _norm_proj(x2d, attn_norm_g[layer], w_q.astype(BF16), seq=S,
                              rope_cols=diff_w, rope_tabs=rope_tabs)
            lambda_init = 0.8 - 0.6 * math.exp(-0.3 * layer)
            y = _diff_attn(proj.reshape(B, S, -1), k_sh, vt_sh, b_lambda_q1[j], b_lambda_k1[j],
                           b_lambda_q2[j], b_lambda_k2[j], b_subln_g[j], n_heads=n_diff,
                           lambda_init=lambda_init)
            mq_col = diff_w // mem_w
        x2d = _mix_out(x2d, y.reshape(T, -1), proj, mq_col, memkv, wo, seq=S)
        last = layer == depth - 1
        x2d = _ffn(x2d, ffn_norm_g[layer], w_gate_up[layer].astype(BF16),
                   w_down[layer].astype(BF16), final_g=final_norm_g if last else None)
    return x2d.reshape(B, S, D)
```

```python
import functools
import math

import jax
import jax.numpy as jnp
from jax import lax
from jax.experimental import pallas as pl
from jax.experimental.pallas import tpu as pltpu

HEAD_DIM = 64
N_MEM_HEADS = 4
ROPE_THETA = 10000.0
NORM_EPS = 1e-6
LANES = 128
ATTN_TILE = 512
PROJ_TM = 512
PROJ_TN = 256
FFN_TM = 512
FFN_TF = 256
VMEM_LIMIT = 48 * 1024 * 1024
LOG2E = math.log2(math.e)

F32 = jnp.float32
BF16 = jnp.bfloat16
_NT = (((1,), (1,)), ((), ()))


def _rms(x, g):
    ms = jnp.mean(x * x, axis=-1, keepdims=True)
    return x * lax.rsqrt(ms + NORM_EPS) * g


def _norm_proj_kernel(*refs, rope_blocks, gates, n_seq_tiles):
    it = iter(refs)
    x_ref, g_ref, w_ref = next(it), next(it), next(it)
    if rope_blocks:
        cos_ref, sin_ref = next(it), next(it)
    if gates:
        wf_ref, bf_ref = next(it), next(it)
    o_ref = next(it)
    if gates:
        c_ref = next(it)
    hn_ref = next(it)
    if gates:
        carry_ref = next(it)

    i = pl.program_id(0)
    j = pl.program_id(1)
    tm = x_ref.shape[0]

    @pl.when(j == 0)
    def _():
        hn = _rms(x_ref[...], g_ref[...])
        hn_ref[...] = hn.astype(BF16)
        if gates:
            z = jnp.dot(hn, wf_ref[...], precision=lax.Precision.HIGHEST,
                        preferred_element_type=F32) + bf_ref[...]
            lf = jnp.minimum(z, 0.0) - jnp.log1p(jnp.exp(-jnp.abs(z)))
            row = lax.broadcasted_iota(jnp.int32, (tm, tm), 0)
            col = lax.broadcasted_iota(jnp.int32, (tm, tm), 1)
            tri = (row >= col).astype(F32)
            cs = jnp.dot(tri, lf, precision=lax.Precision.HIGHEST, preferred_element_type=F32)

            @pl.when((i % n_seq_tiles) == 0)
            def _():
                carry_ref[...] = jnp.zeros_like(carry_ref)

            c = cs + carry_ref[...]
            c_ref[...] = c
            carry_ref[...] = c[tm - 1:tm, :]

    acc = jnp.dot(hn_ref[...], w_ref[...], preferred_element_type=F32)

    if rope_blocks == 0:
        o_ref[...] = acc.astype(o_ref.dtype)
    else:
        @pl.when(j < rope_blocks)
        def _():
            cos = cos_ref[...]
            sin = sin_ref[...]
            lane = lax.broadcasted_iota(jnp.int32, (tm, LANES), 1)
            first_half = (lane % HEAD_DIM) < (HEAD_DIM // 2)
            for c0 in range(0, acc.shape[1], LANES):
                t = acc[:, c0:c0 + LANES]
                up = pltpu.roll(t, HEAD_DIM // 2, 1)
                dn = pltpu.roll(t, LANES - HEAD_DIM // 2, 1)
                sw = jnp.where(first_half, dn, up)
                o_ref[:, c0:c0 + LANES] = (t * cos + sw * sin).astype(o_ref.dtype)

        @pl.when(j >= rope_blocks)
        def _():
            o_ref[...] = acc.astype(o_ref.dtype)


def _norm_proj(x2d, g, w, *, seq, rope_cols=0, rope_tabs=None, gate_w=None, gate_b=None):
    T, D = x2d.shape
    N = w.shape[1]
    tm = min(PROJ_TM, T)
    tn = min(PROJ_TN, N)
    assert T % tm == 0 and N % tn == 0 and rope_cols % tn == 0
    gates = gate_w is not None
    rope_blocks = rope_cols // tn
    assert seq % tm == 0 or not (gates or rope_blocks)
    n_seq_tiles = max(1, seq // tm)

    in_specs = [
        pl.BlockSpec((tm, D), lambda i, j: (i, 0)),
        pl.BlockSpec((1, D), lambda i, j: (0, 0)),
        pl.BlockSpec((D, tn), lambda i, j: (0, j)),
    ]
    args = [x2d, g.reshape(1, D), w]
    if rope_blocks:
        tab_spec = pl.BlockSpec((tm, LANES), lambda i, j: (i % n_seq_tiles, 0))
        in_specs += [tab_spec, tab_spec]
        args += list(rope_tabs)
    if gates:
        in_specs += [pl.BlockSpec((D, LANES), lambda i, j: (0, 0)),
                     pl.BlockSpec((1, LANES), lambda i, j: (0, 0))]
        args += [gate_w, gate_b]

    out_shape = [jax.ShapeDtypeStruct((T, N), BF16)]
    out_specs = [pl.BlockSpec((tm, tn), lambda i, j: (i, j))]
    scratch = [pltpu.VMEM((tm, D), BF16)]
    if gates:
        out_shape.append(jax.ShapeDtypeStruct((T, LANES), F32))
        out_specs.append(pl.BlockSpec((tm, LANES), lambda i, j: (i, 0)))
        scratch.append(pltpu.VMEM((1, LANES), F32))

    outs = pl.pallas_call(
        functools.partial(_norm_proj_kernel, rope_blocks=rope_blocks, gates=gates,
                          n_seq_tiles=n_seq_tiles),
        grid=(T // tm, N // tn),
        in_specs=in_specs,
        out_specs=out_specs,
        out_shape=out_shape,
        scratch_shapes=scratch,
        compiler_params=pltpu.CompilerParams(
            dimension_semantics=("arbitrary", "arbitrary"), vmem_limit_bytes=VMEM_LIMIT),
        name="norm_proj_gates" if gates else ("norm_proj_rope" if rope_blocks else "norm_proj"),
    )(*args)
    return outs if gates else outs[0]


def _flash_step(k_ops, q_ops, vt_ops, state, mask):
    sts = [lax.dot_general(ka, qa, _NT, preferred_element_type=F32)
           for ka, qa in zip(k_ops, q_ops)]
    out = []
    for st, vt, prev in zip(sts, vt_ops, state):
        if mask is not None:
            st = jnp.where(mask, st, -jnp.inf)
        bm = jnp.max(st, axis=0, keepdims=True)
        if prev is None:
            m_new = bm
            p = jnp.exp2(st - m_new)
            l = jnp.sum(p, axis=0, keepdims=True)
            acc = jnp.dot(vt, p.astype(BF16), preferred_element_type=F32)
        else:
            m, l, acc = prev
            m_new = jnp.maximum(m, bm)
            alpha = jnp.exp2(m - m_new)
            p = jnp.exp2(st - m_new)
            l = alpha * l + jnp.sum(p, axis=0, keepdims=True)
            acc = alpha * acc + jnp.dot(vt, p.astype(BF16), preferred_element_type=F32)
        out.append((m_new, l, acc))
    return out


def _causal_flash(qi, tile, q_ops, k_block, vt_block):
    n_maps = len(q_ops)
    kk = lax.broadcasted_iota(jnp.int32, (tile, tile), 0)
    qq = lax.broadcasted_iota(jnp.int32, (tile, tile), 1)
    q0 = pl.multiple_of(qi * tile, tile)
    state = _flash_step(k_block(q0), q_ops, vt_block(qi), [None] * n_maps, kk <= qq)

    def kv_body(j, flat):
        prev = [tuple(flat[3 * i:3 * i + 3]) for i in range(n_maps)]
        k0 = pl.multiple_of(j * tile, tile)
        new = _flash_step(k_block(k0), q_ops, vt_block(j), prev, None)
        return tuple(v for s in new for v in s)

    flat = lax.fori_loop(0, qi, kv_body, tuple(v for s in state for v in s))
    return [(flat[3 * i + 1], flat[3 * i + 2]) for i in range(n_maps)]


def _fox_attn_kernel(q_ref, k_ref, vt_ref, c_ref, o_ref, qa_ref, ka_ref, *, tile):
    seq = q_ref.shape[1]
    n_tiles = seq // tile
    hp = pl.program_id(1)
    lane = lax.broadcasted_iota(jnp.int32, (tile, LANES), 1)
    d = HEAD_DIM

    def split3(v):
        hi = v.astype(BF16).astype(F32)
        r1 = v - hi
        mid = r1.astype(BF16).astype(F32)
        return hi, mid, r1 - mid

    def build(ci, carry):
        r0 = pl.multiple_of(ci * tile, tile)
        qp = q_ref[0, pl.ds(r0, tile), :].astype(F32)
        kp = k_ref[0, pl.ds(r0, tile), :].astype(F32)
        cc = c_ref[0, pl.ds(r0, tile), :] * LOG2E
        for h2 in range(2):
            head = 2 * hp + h2
            ch = jnp.sum(jnp.where(lane == head, cc, 0.0), axis=1, keepdims=True)
            hi, mid, lo = split3(jnp.broadcast_to(ch, (tile, LANES)))
            if h2 == 0:
                qx, kx = qp, kp
            else:
                qx = pltpu.roll(qp, HEAD_DIM, 1)
                kx = pltpu.roll(kp, HEAD_DIM, 1)
            k_extra = jnp.where(lane == d, hi, jnp.where(lane == d + 1, mid, jnp.where(
                lane == d + 2, lo, jnp.where(lane < d + 6, 1.0, 0.0))))
            q_extra = jnp.where(lane < d + 3, -1.0, jnp.where(lane == d + 3, hi, jnp.where(
                lane == d + 4, mid, jnp.where(lane == d + 5, lo, 0.0))))
            ka_ref[h2, pl.ds(r0, tile), :] = jnp.where(lane < d, kx, k_extra).astype(BF16)
            qa_ref[h2, pl.ds(r0, tile), :] = jnp.where(lane < d, qx, q_extra).astype(BF16)
        return carry

    lax.fori_loop(0, n_tiles, build, 0)

    def q_body(qi, carry):
        q0 = pl.multiple_of(qi * tile, tile)
        q_ops = [qa_ref[h2, pl.ds(q0, tile), :] for h2 in range(2)]

        def k_block(start):
            return [ka_ref[h2, pl.ds(start, tile), :] for h2 in range(2)]

        def vt_block(j):
            return [vt_ref[0, 0, j, h2 * d:(h2 + 1) * d, :] for h2 in range(2)]

        (l0, a0), (l1, a1) = _causal_flash(qi, tile, q_ops, k_block, vt_block)
        yt = jnp.concatenate([a0 / l0, a1 / l1], axis=0)
        o_ref[0, pl.ds(q0, tile), :] = yt.T.astype(o_ref.dtype)
        return carry

    lax.fori_loop(0, n_tiles, q_body, 0)


def _fox_attn(proj, vt, c, *, n_heads, k_col, out_width):
    B, S, _ = proj.shape
    tile = ATTN_TILE
    n_tiles = S // tile
    n_pairs = n_heads // 2
    return pl.pallas_call(
        functools.partial(_fox_attn_kernel, tile=tile),
        grid=(B, n_pairs),
        in_specs=[
            pl.BlockSpec((1, S, LANES), lambda b, h: (b, 0, h)),
            pl.BlockSpec((1, S, LANES), lambda b, h: (b, 0, k_col + h)),
            pl.BlockSpec((1, 1, n_tiles, LANES, tile), lambda b, h: (b, h, 0, 0, 0)),
            pl.BlockSpec((1, S, LANES), lambda b, h: (b, 0, 0)),
        ],
        out_specs=pl.BlockSpec((1, S, LANES), lambda b, h: (b, 0, h)),
        out_shape=jax.ShapeDtypeStruct((B, S, out_width), BF16),
        scratch_shapes=[
            pltpu.VMEM((2, S, LANES), BF16),
            pltpu.VMEM((2, S, LANES), BF16),
        ],
        compiler_params=pltpu.CompilerParams(
            dimension_semantics=("arbitrary", "arbitrary"), vmem_limit_bytes=VMEM_LIMIT),
        name="fox_attn",
    )(proj, proj, vt, c)


def _diff_attn_kernel(q_ref, k_ref, vt_ref, lq1_ref, lk1_ref, lq2_ref, lk2_ref, g_ref, o_ref, *,
                      tile, lambda_init):
    seq = q_ref.shape[1]
    n_tiles = seq // tile
    lane = lax.broadcasted_iota(jnp.int32, (tile, LANES), 1)
    lam = (jnp.exp(jnp.sum(lq1_ref[...] * lk1_ref[...], axis=1, keepdims=True))
           - jnp.exp(jnp.sum(lq2_ref[...] * lk2_ref[...], axis=1, keepdims=True)) + lambda_init)

    def q_body(qi, carry):
        q0 = pl.multiple_of(qi * tile, tile)
        qp = q_ref[0, pl.ds(q0, tile), :]
        zero = jnp.zeros_like(qp)
        q_ops = [jnp.where(lane < HEAD_DIM, qp, zero), jnp.where(lane >= HEAD_DIM, qp, zero)]

        def k_block(start):
            kb = k_ref[0, pl.ds(start, tile), :]
            return [kb, kb]

        def vt_block(j):
            vb = vt_ref[0, 0, j]
            return [vb, vb]

        (l1, a1), (l2, a2) = _causal_flash(qi, tile, q_ops, k_block, vt_block)
        yt = a1 / l1 - lam * (a2 / l2)
        y = _rms(yt.T, g_ref[...]) * (1.0 - lambda_init)
        o_ref[0, pl.ds(q0, tile), :] = y.astype(o_ref.dtype)
        return carry

    lax.fori_loop(0, n_tiles, q_body, 0)


def _diff_attn(qproj, kvproj, vt, lq1, lk1, lq2, lk2, subln_g, *, n_heads, lambda_init):
    B, S, _ = qproj.shape
    tile = ATTN_TILE
    n_tiles = S // tile
    vec = pl.BlockSpec((1, HEAD_DIM), lambda b, h: (0, 0))
    return pl.pallas_call(
        functools.partial(_diff_attn_kernel, tile=tile, lambda_init=lambda_init),
        grid=(B, n_heads),
        in_specs=[
            pl.BlockSpec((1, S, LANES), lambda b, h: (b, 0, h)),
            pl.BlockSpec((1, S, LANES), lambda b, h: (b, 0, h)),
            pl.BlockSpec((1, 1, n_tiles, LANES, tile), lambda b, h: (b, h, 0, 0, 0)),
            vec, vec, vec, vec,
            pl.BlockSpec((1, 2 * HEAD_DIM), lambda b, h: (0, 0)),
        ],
        out_specs=pl.BlockSpec((1, S, LANES), lambda b, h: (b, 0, h)),
        out_shape=jax.ShapeDtypeStruct((B, S, n_heads * 2 * HEAD_DIM), BF16),
        compiler_params=pltpu.CompilerParams(
            dimension_semantics=("arbitrary", "arbitrary"), vmem_limit_bytes=VMEM_LIMIT),
        name="diff_attn",
    )(qproj, kvproj, vt, lq1.reshape(1, -1), lk1.reshape(1, -1), lq2.reshape(1, -1),
      lk2.reshape(1, -1), subln_g.reshape(1, -1))


def _mix_out_kernel(x_ref, y_ref, mq_ref, mk_ref, mv_ref, wo_ref, o_ref):
    tm = x_ref.shape[0]
    mw = mq_ref.shape[1]
    y_width = y_ref.shape[1]
    mq = mq_ref[...]
    mk = mk_ref[0]
    mv = mv_ref[0]
    q_head = lax.broadcasted_iota(jnp.int32, (tm, mw), 1) // HEAD_DIM
    v_head = lax.broadcasted_iota(jnp.int32, mv.shape, 1) // HEAD_DIM
    ymem = jnp.zeros((tm, mw), F32)
    for h in range(mw // HEAD_DIM):
        qh = jnp.where(q_head == h, mq, jnp.zeros_like(mq))
        s = lax.dot_general(qh, mk, _NT, preferred_element_type=F32)
        p = jnp.exp2(s - jnp.max(s, axis=1, keepdims=True))
        l = jnp.sum(p, axis=1, keepdims=True)
        vh = jnp.where(v_head == h, mv, jnp.zeros_like(mv))
        ymem = ymem + jnp.dot(p.astype(BF16), vh, preferred_element_type=F32) / l
    acc = jnp.dot(y_ref[...], wo_ref[0:y_width, :], preferred_element_type=F32)
    acc = acc + jnp.dot(ymem.astype(BF16), wo_ref[y_width:y_width + mw, :],
                        preferred_element_type=F32)
    o_ref[...] = x_ref[...] + acc


def _mix_out(x2d, y2d, proj2d, mq_col, memkv, wo, *, seq):
    T, D = x2d.shape
    yw = y2d.shape[1]
    n_mem, mw2 = memkv.shape[1], memkv.shape[2]
    mw = mw2 // 2
    tm = PROJ_TM
    n_seq_tiles = seq // tm
    return pl.pallas_call(
        _mix_out_kernel,
        grid=(T // tm,),
        in_specs=[
            pl.BlockSpec((tm, D), lambda i: (i, 0)),
            pl.BlockSpec((tm, yw), lambda i: (i, 0)),
            pl.BlockSpec((tm, mw), lambda i: (i, mq_col)),
            pl.BlockSpec((1, n_mem, mw), lambda i: (i // n_seq_tiles, 0, 0)),
            pl.BlockSpec((1, n_mem, mw), lambda i: (i // n_seq_tiles, 0, 1)),
            pl.BlockSpec((yw + mw, D), lambda i: (0, 0)),
        ],
        out_specs=pl.BlockSpec((tm, D), lambda i: (i, 0)),
        out_shape=jax.ShapeDtypeStruct((T, D), F32),
        compiler_params=pltpu.CompilerParams(
            dimension_semantics=("arbitrary",), vmem_limit_bytes=VMEM_LIMIT),
        name="mix_out",
    )(x2d, y2d, proj2d, memkv, memkv, wo)


def _ffn_kernel(*refs, final):
    if final:
        x_ref, g_ref, wg_ref, wu_ref, wd_ref, fg_ref, o_ref, hn_ref, acc_ref = refs
    else:
        x_ref, g_ref, wg_ref, wu_ref, wd_ref, o_ref, hn_ref, acc_ref = refs
    k = pl.program_id(1)

    @pl.when(k == 0)
    def _():
        hn_ref[...] = _rms(x_ref[...], g_ref[...]).astype(BF16)
        acc_ref[...] = jnp.zeros_like(acc_ref)

    h = hn_ref[...]
    gate = jnp.dot(h, wg_ref[...], preferred_element_type=F32)
    up = jnp.dot(h, wu_ref[...], preferred_element_type=F32)
    a = gate * jax.nn.sigmoid(gate) * up
    acc_ref[...] += jnp.dot(a.astype(BF16), wd_ref[...], preferred_element_type=F32)

    @pl.when(k == pl.num_programs(1) - 1)
    def _():
        out = x_ref[...] + acc_ref[...]
        if final:
            out = _rms(out, fg_ref[...])
        o_ref[...] = out


def _ffn(x2d, g, w_gate_up, w_down, final_g=None):
    T, D = x2d.shape
    d_ff = w_down.shape[0]
    tm, tf = FFN_TM, FFN_TF
    assert T % tm == 0 and d_ff % tf == 0
    nk = d_ff // tf
    final = final_g is not None
    in_specs = [
        pl.BlockSpec((tm, D), lambda i, k: (i, 0)),
        pl.BlockSpec((1, D), lambda i, k: (0, 0)),
        pl.BlockSpec((D, tf), lambda i, k: (0, k)),
        pl.BlockSpec((D, tf), lambda i, k: (0, k + nk)),
        pl.BlockSpec((tf, D), lambda i, k: (k, 0)),
    ]
    args = [x2d, g.reshape(1, D), w_gate_up, w_gate_up, w_down]
    if final:
        in_specs.append(pl.BlockSpec((1, D), lambda i, k: (0, 0)))
        args.append(final_g.reshape(1, D))
    return pl.pallas_call(
        functools.partial(_ffn_kernel, final=final),
        grid=(T // tm, nk),
        in_specs=in_specs,
        out_specs=pl.BlockSpec((tm, D), lambda i, k: (i, 0)),
        out_shape=jax.ShapeDtypeStruct((T, D), F32),
        scratch_shapes=[pltpu.VMEM((tm, D), BF16), pltpu.VMEM((tm, D), F32)],
        compiler_params=pltpu.CompilerParams(
            dimension_semantics=("arbitrary", "arbitrary"), vmem_limit_bytes=VMEM_LIMIT),
        name="ffn_final" if final else "ffn",
    )(*args)


def _rope_tables(seq):
    half = HEAD_DIM // 2
    inv_freq = jnp.power(ROPE_THETA, -jnp.arange(half, dtype=F32) * (2.0 / HEAD_DIM))
    ang = jnp.arange(seq, dtype=F32)[:, None] * inv_freq[None, :]
    cos, sin = jnp.cos(ang), jnp.sin(ang)
    reps = LANES // HEAD_DIM
    cos_t = jnp.tile(jnp.concatenate([cos, cos], axis=-1), (1, reps))
    sin_t = jnp.tile(jnp.concatenate([-sin, sin], axis=-1), (1, reps))
    return cos_t, sin_t


def _value_tiles_t(v, tile):
    B, S, W = v.shape
    v = v.reshape(B, S // tile, tile, W // LANES, LANES)
    return v.transpose(0, 3, 1, 4, 2)


def kernel(x, mem, attn_norm_g, mem_norm_g, w_mem_kv, w_out, ffn_norm_g, w_gate_up, w_down,
           a_w_in, a_b_f, b_w_in, b_lambda_q1, b_lambda_k1, b_lambda_q2, b_lambda_k2,
           b_subln_g, kv_norm_g, w_kv_shared, final_norm_g):
    B, S, D = x.shape
    depth = attn_norm_g.shape[0]
    n_a = a_w_in.shape[0]
    n_mem = mem.shape[1]
    mem_w = w_mem_kv.shape[2] // 2
    n_fox = a_b_f.shape[1]
    fox_w = n_fox * HEAD_DIM
    diff_w = b_w_in.shape[1] - mem_w
    n_diff = diff_w // (2 * HEAD_DIM)
    scale = HEAD_DIM ** -0.5 * LOG2E
    T = B * S

    x2d = x.reshape(T, D)
    mem2d = mem.reshape(B * n_mem, D)
    rope_tabs = _rope_tables(S)
    k_sh = vt_sh = None

    for layer in range(depth):
        memkv = _norm_proj(mem2d, mem_norm_g[layer], w_mem_kv[layer].astype(BF16), seq=n_mem)
        memkv = memkv.reshape(B, n_mem, 2 * mem_w)
        wo = w_out[layer].astype(BF16)
        if layer < n_a:
            w_in = a_w_in[layer]
            w_main = jnp.concatenate(
                [w_in[:, :fox_w] * scale, w_in[:, fox_w:3 * fox_w],
                 w_in[:, 3 * fox_w + n_fox:] * scale], axis=1).astype(BF16)
            gate_w = jnp.pad(w_in[:, 3 * fox_w:3 * fox_w + n_fox], ((0, 0), (0, LANES - n_fox)))
            gate_b = jnp.pad(a_b_f[layer], (0, LANES - n_fox)).reshape(1, LANES)
            proj, c = _norm_proj(x2d, attn_norm_g[layer], w_main, seq=S,
                                 gate_w=gate_w, gate_b=gate_b)
            proj3 = proj.reshape(B, S, -1)
            vt = _value_tiles_t(proj3[:, :, 2 * fox_w:3 * fox_w], ATTN_TILE)
            y = _fox_attn(proj3, vt, c.reshape(B, S, LANES), n_heads=n_fox,
                          k_col=fox_w // LANES, out_width=fox_w)
            mq_col = 3 * fox_w // mem_w
        else:
            j = layer - n_a
            if layer == n_a:
                kvp = _norm_proj(x2d, kv_norm_g, w_kv_shared.astype(BF16), seq=S,
                                 rope_cols=diff_w, rope_tabs=rope_tabs)
                k_sh = kvp.reshape(B, S, -1)
                vt_sh = _value_tiles_t(k_sh[:, :, diff_w:], ATTN_TILE)
            w_q = b_w_in[j] * scale
            proj = _norm_proj(x2d, attn_norm_g[layer], w_q.astype(BF16), seq=S,
                              rope_cols=diff_w, rope_tabs=rope_tabs)
            lambda_init = 0.8 - 0.6 * math.exp(-0.3 * layer)
            y = _diff_attn(proj.reshape(B, S, -1), k_sh, vt_sh, b_lambda_q1[j], b_lambda_k1[j],
                           b_lambda_q2[j], b_lambda_k2[j], b_subln_g[j], n_heads=n_diff,
                           lambda_init=lambda_init)
            mq_col = diff_w // mem_w
        x2d = _mix_out(x2d, y.reshape(T, -1), proj, mq_col, memkv, wo, seq=S)
        last = layer == depth - 1
        x2d = _ffn(x2d, ffn_norm_g[layer], w_gate_up[layer].astype(BF16),
                   w_down[layer].astype(BF16), final_g=final_norm_g if last else None)
    return x2d.reshape(B, S, D)
```

```python
import functools
import math

import jax
import jax.numpy as jnp
from jax import lax
from jax.experimental import pallas as pl
from jax.experimental.pallas import tpu as pltpu

HEAD_DIM = 64
ROPE_THETA = 10000.0
NORM_EPS = 1e-6
LANES = 128
ATTN_TILE = 512
PROJ_TM = ATTN_TILE
FFN_TM = 1024
FFN_TF = 256
VMEM_CAP = 56 * 1024 * 1024
VMEM_TEMPS = 16 * 1024 * 1024
LOG2E = math.log2(math.e)

F32 = jnp.float32
BF16 = jnp.bfloat16
_NT = (((1,), (1,)), ((), ()))
_RESIDENT = pl.Buffered(1)


def _nbytes(shape, dtype):
    return math.prod(shape) * jnp.dtype(dtype).itemsize


def _params(n_grid, block_bytes):
    limit = min(VMEM_CAP, block_bytes + VMEM_TEMPS)
    return pltpu.CompilerParams(dimension_semantics=("arbitrary",) * n_grid,
                                vmem_limit_bytes=limit)


def _rms(x, g):
    ms = jnp.mean(x * x, axis=-1, keepdims=True)
    return x * lax.rsqrt(ms + NORM_EPS) * g


def _split_bf16(v, n):
    pieces = []
    for _ in range(n - 1):
        p = v.astype(BF16)
        pieces.append(p)
        v = v - p.astype(F32)
    pieces.append(v.astype(BF16))
    return pieces


def _proj_kernel(*refs, streams, gates, n_seq_tiles):
    it = iter(refs)
    x_ref = next(it)
    gw = [(next(it), next(it)) for _ in streams]
    any_rope = any(s[0] for s in streams)
    if any_rope:
        cos_ref, sin_ref = next(it), next(it)
    if gates:
        wf_ref, bf_ref = next(it), next(it)
    outs = []
    for (_, _, vt_cols) in streams:
        row_ref = next(it)
        outs.append((row_ref, next(it) if vt_cols else None))
    if gates:
        c_ref, carry_ref = next(it), next(it)

    tm = x_ref.shape[0]
    x = x_ref[...]
    xn = x * lax.rsqrt(jnp.mean(x * x, axis=-1, keepdims=True) + NORM_EPS)
    if any_rope:
        cos = cos_ref[...]
        sin = sin_ref[...]
        lane = lax.broadcasted_iota(jnp.int32, (tm, LANES), 1)
        first_half = (lane % HEAD_DIM) < (HEAD_DIM // 2)

    for si, ((rope_cols, row_cols, vt_cols), (g_ref, w_ref), (row_ref, vt_ref)) in enumerate(
            zip(streams, gw, outs)):
        hn32 = xn * g_ref[...]
        hn = hn32.astype(BF16)
        res = jnp.dot(hn, w_ref[...], preferred_element_type=F32)
        for c0 in range(0, rope_cols, LANES):
            t = res[:, c0:c0 + LANES]
            up = pltpu.roll(t, HEAD_DIM // 2, 1)
            dn = pltpu.roll(t, LANES - HEAD_DIM // 2, 1)
            sw = jnp.where(first_half, dn, up)
            row_ref[:, c0:c0 + LANES] = (t * cos + sw * sin).astype(BF16)
        if row_cols > rope_cols:
            row_ref[:, rope_cols:row_cols] = res[:, rope_cols:row_cols].astype(BF16)
        for gi in range(vt_cols // LANES):
            c0 = row_cols + gi * LANES
            vt_ref[0, gi, 0] = res[:, c0:c0 + LANES].T.astype(BF16)

        if gates and si == 0:
            lo = (hn32 - hn.astype(F32)).astype(BF16)
            wf = wf_ref[...]
            zz = jnp.dot(hn, wf, preferred_element_type=F32)
            z = (zz[:, :LANES] + zz[:, LANES:]
                 + jnp.dot(lo, wf[:, :LANES], preferred_element_type=F32) + bf_ref[...])
            lf = jnp.minimum(z, 0.0) - jnp.log1p(jnp.exp(-jnp.abs(z)))
            l_hi, l_mid, l_lo = _split_bf16(lf, 3)
            row = lax.broadcasted_iota(jnp.int32, (tm, tm), 0)
            col = lax.broadcasted_iota(jnp.int32, (tm, tm), 1)
            tri = (row >= col).astype(BF16)
            cc = jnp.dot(tri, jnp.concatenate([l_hi, l_mid], axis=1), preferred_element_type=F32)
            cs = cc[:, :LANES] + cc[:, LANES:] + jnp.dot(tri, l_lo, preferred_element_type=F32)

            @pl.when((pl.program_id(0) % n_seq_tiles) == 0)
            def _():
                carry_ref[...] = jnp.zeros_like(carry_ref)

            c = cs + carry_ref[...]
            c_ref[...] = c
            carry_ref[...] = c[tm - 1:tm, :]


def _proj(x2d, streams, *, seq, rope_tabs=None, gate_w=None, gate_b=None):
    T, D = x2d.shape
    tm = min(PROJ_TM, T)
    assert T % tm == 0
    gates = gate_w is not None
    any_rope = any(s[2] for s in streams)
    any_vt = any(s[4] for s in streams)
    assert seq % tm == 0 or not (gates or any_rope or any_vt)
    n_seq_tiles = max(1, seq // tm)
    n_batch = T // seq

    in_specs = [pl.BlockSpec((tm, D), lambda i: (i, 0))]
    args = [x2d]
    block_bytes = 2 * _nbytes((tm, D), F32)
    for g, w, _, _, _ in streams:
        in_specs += [pl.BlockSpec((1, D), lambda i: (0, 0)),
                     pl.BlockSpec(w.shape, lambda i: (0, 0), pipeline_mode=_RESIDENT)]
        args += [g.reshape(1, D), w]
        block_bytes += _nbytes(w.shape, BF16) + _nbytes((tm, w.shape[1]), F32)
    if any_rope:
        tab_spec = pl.BlockSpec((tm, LANES), lambda i: (i % n_seq_tiles, 0))
        in_specs += [tab_spec, tab_spec]
        args += list(rope_tabs)
        block_bytes += 4 * _nbytes((tm, LANES), F32)
    if gates:
        in_specs += [pl.BlockSpec(gate_w.shape, lambda i: (0, 0), pipeline_mode=_RESIDENT),
                     pl.BlockSpec((1, LANES), lambda i: (0, 0))]
        args += [gate_w, gate_b]
        block_bytes += _nbytes(gate_w.shape, BF16)

    out_shape, out_specs = [], []
    for _, w, _, row_cols, vt_cols in streams:
        out_shape.append(jax.ShapeDtypeStruct((T, row_cols), BF16))
        out_specs.append(pl.BlockSpec((tm, row_cols), lambda i: (i, 0)))
        block_bytes += 2 * _nbytes((tm, w.shape[1]), BF16)
        if vt_cols:
            n_groups = vt_cols // LANES
            out_shape.append(
                jax.ShapeDtypeStruct((n_batch, n_groups, n_seq_tiles, LANES, tm), BF16))
            out_specs.append(pl.BlockSpec(
                (1, n_groups, 1, LANES, tm),
                lambda i: (i // n_seq_tiles, 0, i % n_seq_tiles, 0, 0)))
    scratch = []
    if gates:
        out_shape.append(jax.ShapeDtypeStruct((T, LANES), F32))
        out_specs.append(pl.BlockSpec((tm, LANES), lambda i: (i, 0)))
        scratch.append(pltpu.VMEM((1, LANES), F32))

    cfg = tuple((s[2], s[3], s[4]) for s in streams)
    return pl.pallas_call(
        functools.partial(_proj_kernel, streams=cfg, gates=gates, n_seq_tiles=n_seq_tiles),
        grid=(T // tm,),
        in_specs=in_specs,
        out_specs=out_specs,
        out_shape=out_shape,
        scratch_shapes=scratch,
        compiler_params=_params(1, block_bytes),
        name="proj_gates" if gates else ("proj_rope" if any_rope else "proj"),
    )(*args)


def _flash_step(k_ops, q_ops, vt_ops, state, mask):
    sts = [lax.dot_general(ka, qa, _NT, preferred_element_type=F32)
           for ka, qa in zip(k_ops, q_ops)]
    out = []
    for st, vt, prev in zip(sts, vt_ops, state):
        if mask is not None:
            st = jnp.where(mask, st, -jnp.inf)
        bm = jnp.max(st, axis=0, keepdims=True)
        if prev is None:
            m_new = bm
            p = jnp.exp2(st - m_new)
            l = jnp.sum(p, axis=0, keepdims=True)
            acc = jnp.dot(vt, p.astype(BF16), preferred_element_type=F32)
        else:
            m, l, acc = prev
            m_new = jnp.maximum(m, bm)
            alpha = jnp.exp2(m - m_new)
            p = jnp.exp2(st - m_new)
            l = alpha * l + jnp.sum(p, axis=0, keepdims=True)
            acc = alpha * acc + jnp.dot(vt, p.astype(BF16), preferred_element_type=F32)
        out.append((m_new, l, acc))
    return out


def _causal_flash(qi, tile, q_ops, k_block, vt_block):
    n_maps = len(q_ops)
    kk = lax.broadcasted_iota(jnp.int32, (tile, tile), 0)
    qq = lax.broadcasted_iota(jnp.int32, (tile, tile), 1)
    q0 = pl.multiple_of(qi * tile, tile)
    state = _flash_step(k_block(q0), q_ops, vt_block(qi), [None] * n_maps, kk <= qq)

    def kv_body(j, flat):
        prev = [tuple(flat[3 * i:3 * i + 3]) for i in range(n_maps)]
        k0 = pl.multiple_of(j * tile, tile)
        new = _flash_step(k_block(k0), q_ops, vt_block(j), prev, None)
        return tuple(v for s in new for v in s)

    flat = lax.fori_loop(0, qi, kv_body, tuple(v for s in state for v in s))
    return [(flat[3 * i + 1], flat[3 * i + 2]) for i in range(n_maps)]


def _fox_attn_kernel(q_ref, k_ref, vt_ref, c_ref, o_ref, qa_ref, ka_ref, *, tile):
    seq = q_ref.shape[1]
    n_tiles = seq // tile
    hp = pl.program_id(1)
    lane = lax.broadcasted_iota(jnp.int32, (tile, LANES), 1)
    d = HEAD_DIM

    def build(ci, carry):
        r0 = pl.multiple_of(ci * tile, tile)
        qp = q_ref[0, pl.ds(r0, tile), :].astype(F32)
        kp = k_ref[0, pl.ds(r0, tile), :].astype(F32)
        cc = c_ref[0, pl.ds(r0, tile), :] * LOG2E
        for h2 in range(2):
            head = 2 * hp + h2
            ch = jnp.sum(jnp.where(lane == head, cc, 0.0), axis=1, keepdims=True)
            hi, mid, lo = [p.astype(F32)
                           for p in _split_bf16(jnp.broadcast_to(ch, (tile, LANES)), 3)]
            if h2 == 0:
                qx, kx = qp, kp
            else:
                qx = pltpu.roll(qp, HEAD_DIM, 1)
                kx = pltpu.roll(kp, HEAD_DIM, 1)
            k_extra = jnp.where(lane == d, hi, jnp.where(lane == d + 1, mid, jnp.where(
                lane == d + 2, lo, jnp.where(lane < d + 6, 1.0, 0.0))))
            q_extra = jnp.where(lane < d + 3, -1.0, jnp.where(lane == d + 3, hi, jnp.where(
                lane == d + 4, mid, jnp.where(lane == d + 5, lo, 0.0))))
            ka_ref[h2, pl.ds(r0, tile), :] = jnp.where(lane < d, kx, k_extra).astype(BF16)
            qa_ref[h2, pl.ds(r0, tile), :] = jnp.where(lane < d, qx, q_extra).astype(BF16)
        return carry

    lax.fori_loop(0, n_tiles, build, 0)

    def q_body(qi, carry):
        q0 = pl.multiple_of(qi * tile, tile)
        q_ops = [qa_ref[h2, pl.ds(q0, tile), :] for h2 in range(2)]

        def k_block(start):
            return [ka_ref[h2, pl.ds(start, tile), :] for h2 in range(2)]

        def vt_block(j):
            return [vt_ref[0, 0, j, h2 * d:(h2 + 1) * d, :] for h2 in range(2)]

        (l0, a0), (l1, a1) = _causal_flash(qi, tile, q_ops, k_block, vt_block)
        yt = jnp.concatenate([a0 / l0, a1 / l1], axis=0)
        o_ref[0, pl.ds(q0, tile), :] = yt.T.astype(o_ref.dtype)
        return carry

    lax.fori_loop(0, n_tiles, q_body, 0)


def _fox_attn(proj, vt, c, *, n_heads, k_col, out_width):
    B, S, _ = proj.shape
    tile = ATTN_TILE
    n_tiles = S // tile
    n_pairs = n_heads // 2
    seq_block = _nbytes((S, LANES), BF16)
    block_bytes = 2 * (4 * seq_block + _nbytes((S, LANES), F32)) + 4 * seq_block
    return pl.pallas_call(
        functools.partial(_fox_attn_kernel, tile=tile),
        grid=(B, n_pairs),
        in_specs=[
            pl.BlockSpec((1, S, LANES), lambda b, h: (b, 0, h)),
            pl.BlockSpec((1, S, LANES), lambda b, h: (b, 0, k_col + h)),
            pl.BlockSpec((1, 1, n_tiles, LANES, tile), lambda b, h: (b, h, 0, 0, 0)),
            pl.BlockSpec((1, S, LANES), lambda b, h: (b, 0, 0)),
        ],
        out_specs=pl.BlockSpec((1, S, LANES), lambda b, h: (b, 0, h)),
        out_shape=jax.ShapeDtypeStruct((B, S, out_width), BF16),
        scratch_shapes=[
            pltpu.VMEM((2, S, LANES), BF16),
            pltpu.VMEM((2, S, LANES), BF16),
        ],
        compiler_params=_params(2, block_bytes),
        name="fox_attn",
    )(proj, proj, vt, c)


def _diff_attn_kernel(q_ref, k_ref, vt_ref, lq1_ref, lk1_ref, lq2_ref, lk2_ref, g_ref, o_ref, *,
                      tile, lambda_init):
    seq = q_ref.shape[1]
    n_tiles = seq // tile
    lane = lax.broadcasted_iota(jnp.int32, (tile, LANES), 1)
    lam = (jnp.exp(jnp.sum(lq1_ref[...] * lk1_ref[...], axis=1, keepdims=True))
           - jnp.exp(jnp.sum(lq2_ref[...] * lk2_ref[...], axis=1, keepdims=True)) + lambda_init)

    def q_body(qi, carry):
        q0 = pl.multiple_of(qi * tile, tile)
        qp = q_ref[0, pl.ds(q0, tile), :]
        zero = jnp.zeros_like(qp)
        q_ops = [jnp.where(lane < HEAD_DIM, qp, zero), jnp.where(lane >= HEAD_DIM, qp, zero)]

        def k_block(start):
            kb = k_ref[0, pl.ds(start, tile), :]
            return [kb, kb]

        def vt_block(j):
            vb = vt_ref[0, 0, j]
            return [vb, vb]

        (l1, a1), (l2, a2) = _causal_flash(qi, tile, q_ops, k_block, vt_block)
        yt = a1 / l1 - lam * (a2 / l2)
        y = _rms(yt.T, g_ref[...]) * (1.0 - lambda_init)
        o_ref[0, pl.ds(q0, tile), :] = y.astype(o_ref.dtype)
        return carry

    lax.fori_loop(0, n_tiles, q_body, 0)


def _diff_attn(qproj, k_sh, vt, lq1, lk1, lq2, lk2, subln_g, *, n_heads, lambda_init):
    B, S, _ = qproj.shape
    tile = ATTN_TILE
    n_tiles = S // tile
    vec = pl.BlockSpec((1, HEAD_DIM), lambda b, h: (0, 0))
    block_bytes = 2 * 4 * _nbytes((S, LANES), BF16)
    return pl.pallas_call(
        functools.partial(_diff_attn_kernel, tile=tile, lambda_init=lambda_init),
        grid=(B, n_heads),
        in_specs=[
            pl.BlockSpec((1, S, LANES), lambda b, h: (b, 0, h)),
            pl.BlockSpec((1, S, LANES), lambda b, h: (b, 0, h)),
            pl.BlockSpec((1, 1, n_tiles, LANES, tile), lambda b, h: (b, h, 0, 0, 0)),
            vec, vec, vec, vec,
            pl.BlockSpec((1, 2 * HEAD_DIM), lambda b, h: (0, 0)),
        ],
        out_specs=pl.BlockSpec((1, S, LANES), lambda b, h: (b, 0, h)),
        out_shape=jax.ShapeDtypeStruct((B, S, n_heads * 2 * HEAD_DIM), BF16),
        compiler_params=_params(2, block_bytes),
        name="diff_attn",
    )(qproj, k_sh, vt, lq1.reshape(1, -1), lk1.reshape(1, -1), lq2.reshape(1, -1),
      lk2.reshape(1, -1), subln_g.reshape(1, -1))


def _mix_out_kernel(x_ref, y_ref, mq_ref, mk_ref, mv_ref, wo_ref, o_ref):
    tm = x_ref.shape[0]
    mw = mq_ref.shape[1]
    y_width = y_ref.shape[1]
    mq = mq_ref[...]
    mk = mk_ref[0]
    mv = mv_ref[0]
    q_head = lax.broadcasted_iota(jnp.int32, (tm, mw), 1) // HEAD_DIM
    v_head = lax.broadcasted_iota(jnp.int32, mv.shape, 1) // HEAD_DIM
    ymem = jnp.zeros((tm, mw), F32)
    for h in range(mw // HEAD_DIM):
        qh = jnp.where(q_head == h, mq, jnp.zeros_like(mq))
        s = lax.dot_general(qh, mk, _NT, preferred_element_type=F32)
        p = jnp.exp2(s - jnp.max(s, axis=1, keepdims=True))
        l = jnp.sum(p, axis=1, keepdims=True)
        vh = jnp.where(v_head == h, mv, jnp.zeros_like(mv))
        ymem = ymem + jnp.dot(p.astype(BF16), vh, preferred_element_type=F32) / l
    acc = jnp.dot(y_ref[...], wo_ref[0:y_width, :], preferred_element_type=F32)
    acc = acc + jnp.dot(ymem.astype(BF16), wo_ref[y_width:y_width + mw, :],
                        preferred_element_type=F32)
    o_ref[...] = x_ref[...] + acc


def _mix_out(x2d, y2d, proj2d, mq_col, memkv, wo, *, seq):
    T, D = x2d.shape
    yw = y2d.shape[1]
    n_mem, mw2 = memkv.shape[1], memkv.shape[2]
    mw = mw2 // 2
    tm = PROJ_TM
    n_seq_tiles = seq // tm
    block_bytes = (4 * _nbytes((tm, D), F32) + 2 * _nbytes((tm, yw + mw), BF16)
                   + 4 * _nbytes((n_mem, mw), BF16) + _nbytes((yw + mw, D), BF16))
    return pl.pallas_call(
        _mix_out_kernel,
        grid=(T // tm,),
        in_specs=[
            pl.BlockSpec((tm, D), lambda i: (i, 0)),
            pl.BlockSpec((tm, yw), lambda i: (i, 0)),
            pl.BlockSpec((tm, mw), lambda i: (i, mq_col)),
            pl.BlockSpec((1, n_mem, mw), lambda i: (i // n_seq_tiles, 0, 0)),
            pl.BlockSpec((1, n_mem, mw), lambda i: (i // n_seq_tiles, 0, 1)),
            pl.BlockSpec((yw + mw, D), lambda i: (0, 0), pipeline_mode=_RESIDENT),
        ],
        out_specs=pl.BlockSpec((tm, D), lambda i: (i, 0)),
        out_shape=jax.ShapeDtypeStruct((T, D), F32),
        compiler_params=_params(1, block_bytes),
        name="mix_out",
    )(x2d, y2d, proj2d, memkv, memkv, wo)


def _ffn_kernel(*refs, final, n_chunks):
    if final:
        x_ref, g_ref, wgu_ref, wd_ref, fg_ref, o_ref, hn_ref, acc_ref = refs
    else:
        x_ref, g_ref, wgu_ref, wd_ref, o_ref, hn_ref, acc_ref = refs
    x = x_ref[...]
    hn_ref[...] = _rms(x, g_ref[...]).astype(BF16)
    acc_ref[...] = x

    def body(k, carry):
        h = hn_ref[...]
        gate = jnp.dot(h, wgu_ref[k], preferred_element_type=F32)
        up = jnp.dot(h, wgu_ref[n_chunks + k], preferred_element_type=F32)
        a = gate * jax.nn.sigmoid(gate) * up
        acc_ref[...] += jnp.dot(a.astype(BF16), wd_ref[k], preferred_element_type=F32)
        return carry

    lax.fori_loop(0, n_chunks, body, 0)
    out = acc_ref[...]
    if final:
        out = _rms(out, fg_ref[...])
    o_ref[...] = out


def _ffn(x2d, g, wgu, wd, final_g=None):
    T, D = x2d.shape
    n_chunks, tf, _ = wd.shape
    tm = FFN_TM
    assert T % tm == 0
    final = final_g is not None
    in_specs = [
        pl.BlockSpec((tm, D), lambda i: (i, 0)),
        pl.BlockSpec((1, D), lambda i: (0, 0)),
        pl.BlockSpec(wgu.shape, lambda i: (0, 0, 0), pipeline_mode=_RESIDENT),
        pl.BlockSpec(wd.shape, lambda i: (0, 0, 0), pipeline_mode=_RESIDENT),
    ]
    args = [x2d, g.reshape(1, D), wgu, wd]
    if final:
        in_specs.append(pl.BlockSpec((1, D), lambda i: (0, 0)))
        args.append(final_g.reshape(1, D))
    block_bytes = (5 * _nbytes((tm, D), F32) + _nbytes((tm, D), BF16)
                   + _nbytes(wgu.shape, BF16) + _nbytes(wd.shape, BF16))
    return pl.pallas_call(
        functools.partial(_ffn_kernel, final=final, n_chunks=n_chunks),
        grid=(T // tm,),
        in_specs=in_specs,
        out_specs=pl.BlockSpec((tm, D), lambda i: (i, 0)),
        out_shape=jax.ShapeDtypeStruct((T, D), F32),
        scratch_shapes=[pltpu.VMEM((tm, D), BF16), pltpu.VMEM((tm, D), F32)],
        compiler_params=_params(1, block_bytes),
        name="ffn_final" if final else "ffn",
    )(*args)


def _rope_tables(seq):
    half = HEAD_DIM // 2
    inv_freq = jnp.power(ROPE_THETA, -jnp.arange(half, dtype=F32) * (2.0 / HEAD_DIM))
    ang = jnp.arange(seq, dtype=F32)[:, None] * inv_freq[None, :]
    cos, sin = jnp.cos(ang), jnp.sin(ang)
    reps = LANES // HEAD_DIM
    cos_t = jnp.tile(jnp.concatenate([cos, cos], axis=-1), (1, reps))
    sin_t = jnp.tile(jnp.concatenate([-sin, sin], axis=-1), (1, reps))
    return cos_t, sin_t


def _ffn_weights(w_gate_up, w_down):
    D, two_ff = w_gate_up.shape
    n_chunks = two_ff // 2 // FFN_TF
    wgu = w_gate_up.astype(BF16).reshape(D, 2 * n_chunks, FFN_TF).transpose(1, 0, 2)
    wd = w_down.astype(BF16).reshape(n_chunks, FFN_TF, D)
    return wgu, wd


def kernel(x, mem, attn_norm_g, mem_norm_g, w_mem_kv, w_out, ffn_norm_g, w_gate_up, w_down,
           a_w_in, a_b_f, b_w_in, b_lambda_q1, b_lambda_k1, b_lambda_q2, b_lambda_k2,
           b_subln_g, kv_norm_g, w_kv_shared, final_norm_g):
    B, S, D = x.shape
    depth = attn_norm_g.shape[0]
    n_a = a_w_in.shape[0]
    n_mem = mem.shape[1]
    mem_w = w_mem_kv.shape[2] // 2
    n_fox = a_b_f.shape[1]
    fox_w = n_fox * HEAD_DIM
    diff_w = b_w_in.shape[1] - mem_w
    n_diff = diff_w // (2 * HEAD_DIM)
    scale = HEAD_DIM ** -0.5 * LOG2E
    T = B * S

    x2d = x.reshape(T, D)
    mem2d = mem.reshape(B * n_mem, D)
    rope_tabs = _rope_tables(S)
    k_sh = vt_sh = None

    for layer in range(depth):
        (memkv,) = _proj(mem2d, [(mem_norm_g[layer], w_mem_kv[layer].astype(BF16),
                                  0, 2 * mem_w, 0)], seq=n_mem)
        memkv = memkv.reshape(B, n_mem, 2 * mem_w)
        wo = w_out[layer].astype(BF16)
        if layer < n_a:
            w_in = a_w_in[layer]
            w_main = jnp.concatenate(
                [w_in[:, :fox_w] * scale, w_in[:, fox_w:2 * fox_w],
                 w_in[:, 3 * fox_w + n_fox:] * scale, w_in[:, 2 * fox_w:3 * fox_w]],
                axis=1).astype(BF16)
            gw = jnp.pad(w_in[:, 3 * fox_w:3 * fox_w + n_fox], ((0, 0), (0, LANES - n_fox)))
            gw_hi = gw.astype(BF16)
            gw_lo = (gw - gw_hi.astype(F32)).astype(BF16)
            gate_w = jnp.concatenate([gw_hi, gw_lo], axis=1)
            gate_b = jnp.pad(a_b_f[layer], (0, LANES - n_fox)).reshape(1, LANES)
            row_cols = 2 * fox_w + mem_w
            proj, vt, c = _proj(x2d, [(attn_norm_g[layer], w_main, 0, row_cols, fox_w)], seq=S,
                                gate_w=gate_w, gate_b=gate_b)
            y = _fox_attn(proj.reshape(B, S, row_cols), vt, c.reshape(B, S, LANES),
                          n_heads=n_fox, k_col=fox_w // LANES, out_width=fox_w)
            mq_col = 2 * fox_w // mem_w
        else:
            j = layer - n_a
            streams = [(attn_norm_g[layer], (b_w_in[j] * scale).astype(BF16),
                        diff_w, diff_w + mem_w, 0)]
            if layer == n_a:
                streams.append((kv_norm_g, w_kv_shared.astype(BF16), diff_w, diff_w,
                                w_kv_shared.shape[1] - diff_w))
                proj, k_sh, vt_sh = _proj(x2d, streams, seq=S, rope_tabs=rope_tabs)
                k_sh = k_sh.reshape(B, S, diff_w)
            else:
                (proj,) = _proj(x2d, streams, seq=S, rope_tabs=rope_tabs)
            lambda_init = 0.8 - 0.6 * math.exp(-0.3 * layer)
            y = _diff_attn(proj.reshape(B, S, -1), k_sh, vt_sh, b_lambda_q1[j], b_lambda_k1[j],
                           b_lambda_q2[j], b_lambda_k2[j], b_subln_g[j], n_heads=n_diff,
                           lambda_init=lambda_init)
            mq_col = diff_w // mem_w
        x2d = _mix_out(x2d, y.reshape(T, -1), proj, mq_col, memkv, wo, seq=S)
        last = layer == depth - 1
        wgu, wd = _ffn_weights(w_gate_up[layer], w_down[layer])
        x2d = _ffn(x2d, ffn_norm_g[layer], wgu, wd, final_g=final_norm_g if last else None)
    return x2d.reshape(B, S, D)
```

```python
import functools
import math

import jax
import jax.numpy as jnp
from jax import lax
from jax.experimental import pallas as pl
from jax.experimental.pallas import tpu as pltpu

HEAD_DIM = 64
ROPE_THETA = 10000.0
NORM_EPS = 1e-6
LANES = 128
ATTN_TILE = 512
PROJ_TM = ATTN_TILE
FFN_TM = 1024
FFN_TF = 256
VMEM_CAP = 56 * 1024 * 1024
VMEM_TEMPS = 16 * 1024 * 1024
LOG2E = math.log2(math.e)

F32 = jnp.float32
BF16 = jnp.bfloat16
_NT = (((1,), (1,)), ((), ()))
_RESIDENT = pl.Buffered(1)


def _nbytes(shape, dtype):
    return math.prod(shape) * jnp.dtype(dtype).itemsize


def _params(n_grid, block_bytes):
    limit = min(VMEM_CAP, block_bytes + VMEM_TEMPS)
    return pltpu.CompilerParams(dimension_semantics=("arbitrary",) * n_grid,
                                vmem_limit_bytes=limit)


def _rms(x, g):
    ms = jnp.mean(x * x, axis=-1, keepdims=True)
    return x * lax.rsqrt(ms + NORM_EPS) * g


def _split_bf16(v, n):
    pieces = []
    for _ in range(n - 1):
        p = v.astype(BF16)
        pieces.append(p)
        v = v - p.astype(F32)
    pieces.append(v.astype(BF16))
    return pieces


def _proj_kernel(*refs, streams, gates, n_seq_tiles):
    it = iter(refs)
    x_ref = next(it)
    gw = [(next(it), next(it)) for _ in streams]
    any_rope = any(s[0] for s in streams)
    if any_rope:
        cos_ref, sin_ref = next(it), next(it)
    if gates:
        wf_ref, bf_ref = next(it), next(it)
    outs = []
    for (_, _, vt_cols) in streams:
        row_ref = next(it)
        outs.append((row_ref, next(it) if vt_cols else None))
    if gates:
        c_ref, carry_ref = next(it), next(it)

    tm = x_ref.shape[0]
    x = x_ref[...]
    xn = x * lax.rsqrt(jnp.mean(x * x, axis=-1, keepdims=True) + NORM_EPS)
    if any_rope:
        cos = cos_ref[...]
        sin = sin_ref[...]
        lane = lax.broadcasted_iota(jnp.int32, (tm, LANES), 1)
        first_half = (lane % HEAD_DIM) < (HEAD_DIM // 2)

    for si, ((rope_cols, row_cols, vt_cols), (g_ref, w_ref), (row_ref, vt_ref)) in enumerate(
            zip(streams, gw, outs)):
        hn32 = xn * g_ref[...]
        hn = hn32.astype(BF16)
        res = jnp.dot(hn, w_ref[...], preferred_element_type=F32)
        for c0 in range(0, rope_cols, LANES):
            t = res[:, c0:c0 + LANES]
            up = pltpu.roll(t, HEAD_DIM // 2, 1)
            dn = pltpu.roll(t, LANES - HEAD_DIM // 2, 1)
            sw = jnp.where(first_half, dn, up)
            row_ref[:, c0:c0 + LANES] = (t * cos + sw * sin).astype(BF16)
        if row_cols > rope_cols:
            row_ref[:, rope_cols:row_cols] = res[:, rope_cols:row_cols].astype(BF16)
        for gi in range(vt_cols // LANES):
            c0 = row_cols + gi * LANES
            vt_ref[0, gi, 0] = res[:, c0:c0 + LANES].T.astype(BF16)

        if gates and si == 0:
            lo = (hn32 - hn.astype(F32)).astype(BF16)
            wf = wf_ref[...]
            zz = jnp.dot(hn, wf, preferred_element_type=F32)
            z = (zz[:, :LANES] + zz[:, LANES:]
                 + jnp.dot(lo, wf[:, :LANES], preferred_element_type=F32) + bf_ref[...])
            lf = jnp.minimum(z, 0.0) - jnp.log1p(jnp.exp(-jnp.abs(z)))
            l_hi, l_mid, l_lo = _split_bf16(lf, 3)
            row = lax.broadcasted_iota(jnp.int32, (tm, tm), 0)
            col = lax.broadcasted_iota(jnp.int32, (tm, tm), 1)
            tri = (row >= col).astype(BF16)
            cc = jnp.dot(tri, jnp.concatenate([l_hi, l_mid], axis=1), preferred_element_type=F32)
            cs = cc[:, :LANES] + cc[:, LANES:] + jnp.dot(tri, l_lo, preferred_element_type=F32)

            @pl.when((pl.program_id(0) % n_seq_tiles) == 0)
            def _():
                carry_ref[...] = jnp.zeros_like(carry_ref)

            c = cs + carry_ref[...]
            c_ref[...] = c
            carry_ref[...] = c[tm - 1:tm, :]


def _proj(x2d, streams, *, seq, rope_tabs=None, gate_w=None, gate_b=None):
    T, D = x2d.shape
    tm = min(PROJ_TM, T)
    assert T % tm == 0
    gates = gate_w is not None
    any_rope = any(s[2] for s in streams)
    any_vt = any(s[4] for s in streams)
    assert seq % tm == 0 or not (gates or any_rope or any_vt)
    n_seq_tiles = max(1, seq // tm)
    n_batch = T // seq

    in_specs = [pl.BlockSpec((tm, D), lambda i: (i, 0))]
    args = [x2d]
    block_bytes = 2 * _nbytes((tm, D), F32)
    for g, w, _, _, _ in streams:
        in_specs += [pl.BlockSpec((1, D), lambda i: (0, 0)),
                     pl.BlockSpec(w.shape, lambda i: (0, 0), pipeline_mode=_RESIDENT)]
        args += [g.reshape(1, D), w]
        block_bytes += _nbytes(w.shape, BF16) + _nbytes((tm, w.shape[1]), F32)
    if any_rope:
        tab_spec = pl.BlockSpec((tm, LANES), lambda i: (i % n_seq_tiles, 0))
        in_specs += [tab_spec, tab_spec]
        args += list(rope_tabs)
        block_bytes += 4 * _nbytes((tm, LANES), F32)
    if gates:
        in_specs += [pl.BlockSpec(gate_w.shape, lambda i: (0, 0), pipeline_mode=_RESIDENT),
                     pl.BlockSpec((1, LANES), lambda i: (0, 0))]
        args += [gate_w, gate_b]
        block_bytes += _nbytes(gate_w.shape, BF16)

    out_shape, out_specs = [], []
    for _, w, _, row_cols, vt_cols in streams:
        out_shape.append(jax.ShapeDtypeStruct((T, row_cols), BF16))
        out_specs.append(pl.BlockSpec((tm, row_cols), lambda i: (i, 0)))
        block_bytes += 2 * _nbytes((tm, w.shape[1]), BF16)
        if vt_cols:
            n_groups = vt_cols // LANES
            out_shape.append(
                jax.ShapeDtypeStruct((n_batch, n_groups, n_seq_tiles, LANES, tm), BF16))
            out_specs.append(pl.BlockSpec(
                (1, n_groups, 1, LANES, tm),
                lambda i: (i // n_seq_tiles, 0, i % n_seq_tiles, 0, 0)))
    scratch = []
    if gates:
        out_shape.append(jax.ShapeDtypeStruct((T, LANES), F32))
        out_specs.append(pl.BlockSpec((tm, LANES), lambda i: (i, 0)))
        scratch.append(pltpu.VMEM((1, LANES), F32))

    cfg = tuple((s[2], s[3], s[4]) for s in streams)
    return pl.pallas_call(
        functools.partial(_proj_kernel, streams=cfg, gates=gates, n_seq_tiles=n_seq_tiles),
        grid=(T // tm,),
        in_specs=in_specs,
        out_specs=out_specs,
        out_shape=out_shape,
        scratch_shapes=scratch,
        compiler_params=_params(1, block_bytes),
        name="proj_gates" if gates else ("proj_rope" if any_rope else "proj"),
    )(*args)


ONES_ROWS = 16
M_INIT = -1e30


def _flash_scratch(n_maps, tile, dv):
    return [pltpu.VMEM((n_maps, tile, tile), F32), pltpu.VMEM((n_maps, tile, tile), F32),
            pltpu.VMEM((n_maps, 1, tile), F32), pltpu.VMEM((n_maps, dv + ONES_ROWS, tile), F32)]


def _flash_scratch_bytes(n_maps, tile, dv):
    return (2 * _nbytes((n_maps, tile, tile), F32) + _nbytes((n_maps, 8, tile), F32)
            + _nbytes((n_maps, dv + ONES_ROWS, tile), F32))


def _causal_flash(qi, tile, q_ops, k_block, vt_block, dv, scratch):
    s_a, s_b, m_ref, acc_ref = scratch
    n_maps = len(q_ops)
    ones = jnp.ones((ONES_ROWS, tile), BF16)

    def issue(j, s_ref):
        for i, (ka, qa) in enumerate(zip(k_block(j), q_ops)):
            s_ref[i] = lax.dot_general(ka, qa, _NT, preferred_element_type=F32)

    def process(s_ref, j, mask):
        for i, vt in enumerate(vt_block(j)):
            st = s_ref[i]
            if mask is not None:
                st = jnp.where(mask, st, -jnp.inf)
            m = m_ref[i]
            m_new = jnp.maximum(m, jnp.max(st, axis=0, keepdims=True))
            alpha = jnp.exp2(m - m_new)
            p = jnp.exp2(st - m_new)
            pv = jnp.dot(jnp.concatenate([vt, ones], axis=0), p.astype(BF16),
                         preferred_element_type=F32)
            acc_ref[i] = alpha * acc_ref[i] + pv
            m_ref[i] = m_new

    m_ref[...] = jnp.full(m_ref.shape, M_INIT, F32)
    acc_ref[...] = jnp.zeros(acc_ref.shape, F32)
    issue(0, s_a)

    def pair(u, carry):
        issue(2 * u + 1, s_b)
        process(s_a, 2 * u, None)
        issue(2 * u + 2, s_a)
        process(s_b, 2 * u + 1, None)
        return carry

    lax.fori_loop(0, qi // 2, pair, 0)
    kk = lax.broadcasted_iota(jnp.int32, (tile, tile), 0)
    qq = lax.broadcasted_iota(jnp.int32, (tile, tile), 1)
    causal = kk <= qq

    @pl.when(qi % 2 == 1)
    def _():
        issue(qi, s_b)
        process(s_a, qi - 1, None)
        process(s_b, qi, causal)

    @pl.when(qi % 2 == 0)
    def _():
        process(s_a, qi, causal)

    return [(acc_ref[i, dv:dv + 1, :], acc_ref[i, :dv, :]) for i in range(n_maps)]


def _fox_attn_kernel(q_ref, k_ref, vt_ref, c_ref, o_ref, qa_ref, ka_ref, *flash_scratch, tile):
    seq = q_ref.shape[1]
    n_tiles = seq // tile
    hp = pl.program_id(1)
    lane = lax.broadcasted_iota(jnp.int32, (tile, LANES), 1)
    d = HEAD_DIM

    def build(ci, carry):
        r0 = pl.multiple_of(ci * tile, tile)
        qp = q_ref[0, pl.ds(r0, tile), :].astype(F32)
        kp = k_ref[0, pl.ds(r0, tile), :].astype(F32)
        cc = c_ref[0, pl.ds(r0, tile), :] * LOG2E
        for h2 in range(2):
            head = 2 * hp + h2
            ch = jnp.sum(jnp.where(lane == head, cc, 0.0), axis=1, keepdims=True)
            hi, mid, lo = [p.astype(F32)
                           for p in _split_bf16(jnp.broadcast_to(ch, (tile, LANES)), 3)]
            if h2 == 0:
                qx, kx = qp, kp
            else:
                qx = pltpu.roll(qp, HEAD_DIM, 1)
                kx = pltpu.roll(kp, HEAD_DIM, 1)
            k_extra = jnp.where(lane == d, hi, jnp.where(lane == d + 1, mid, jnp.where(
                lane == d + 2, lo, jnp.where(lane < d + 6, 1.0, 0.0))))
            q_extra = jnp.where(lane < d + 3, -1.0, jnp.where(lane == d + 3, hi, jnp.where(
                lane == d + 4, mid, jnp.where(lane == d + 5, lo, 0.0))))
            ka_ref[h2, pl.ds(r0, tile), :] = jnp.where(lane < d, kx, k_extra).astype(BF16)
            qa_ref[h2, pl.ds(r0, tile), :] = jnp.where(lane < d, qx, q_extra).astype(BF16)
        return carry

    lax.fori_loop(0, n_tiles, build, 0)

    def q_body(qi, carry):
        q0 = pl.multiple_of(qi * tile, tile)
        q_ops = [qa_ref[h2, pl.ds(q0, tile), :] for h2 in range(2)]

        def k_block(j):
            start = pl.multiple_of(j * tile, tile)
            return [ka_ref[h2, pl.ds(start, tile), :] for h2 in range(2)]

        def vt_block(j):
            return [vt_ref[0, 0, j, h2 * d:(h2 + 1) * d, :] for h2 in range(2)]

        (l0, a0), (l1, a1) = _causal_flash(qi, tile, q_ops, k_block, vt_block, d, flash_scratch)
        yt = jnp.concatenate([a0 / l0, a1 / l1], axis=0)
        o_ref[0, pl.ds(q0, tile), :] = yt.T.astype(o_ref.dtype)
        return carry

    lax.fori_loop(0, n_tiles, q_body, 0)


def _fox_attn(proj, vt, c, *, n_heads, k_col, out_width):
    B, S, _ = proj.shape
    tile = ATTN_TILE
    n_tiles = S // tile
    n_pairs = n_heads // 2
    seq_block = _nbytes((S, LANES), BF16)
    block_bytes = (2 * (4 * seq_block + _nbytes((S, LANES), F32)) + 4 * seq_block
                   + _flash_scratch_bytes(2, tile, HEAD_DIM))
    return pl.pallas_call(
        functools.partial(_fox_attn_kernel, tile=tile),
        grid=(B, n_pairs),
        in_specs=[
            pl.BlockSpec((1, S, LANES), lambda b, h: (b, 0, h)),
            pl.BlockSpec((1, S, LANES), lambda b, h: (b, 0, k_col + h)),
            pl.BlockSpec((1, 1, n_tiles, LANES, tile), lambda b, h: (b, h, 0, 0, 0)),
            pl.BlockSpec((1, S, LANES), lambda b, h: (b, 0, 0)),
        ],
        out_specs=pl.BlockSpec((1, S, LANES), lambda b, h: (b, 0, h)),
        out_shape=jax.ShapeDtypeStruct((B, S, out_width), BF16),
        scratch_shapes=[
            pltpu.VMEM((2, S, LANES), BF16),
            pltpu.VMEM((2, S, LANES), BF16),
        ] + _flash_scratch(2, tile, HEAD_DIM),
        compiler_params=_params(2, block_bytes),
        name="fox_attn",
    )(proj, proj, vt, c)


def _diff_attn_kernel(q_ref, k_ref, vt_ref, lq1_ref, lk1_ref, lq2_ref, lk2_ref, g_ref, o_ref,
                      *flash_scratch, tile, lambda_init):
    seq = q_ref.shape[1]
    n_tiles = seq // tile
    lane = lax.broadcasted_iota(jnp.int32, (tile, LANES), 1)
    lam = (jnp.exp(jnp.sum(lq1_ref[...] * lk1_ref[...], axis=1, keepdims=True))
           - jnp.exp(jnp.sum(lq2_ref[...] * lk2_ref[...], axis=1, keepdims=True)) + lambda_init)

    def q_body(qi, carry):
        q0 = pl.multiple_of(qi * tile, tile)
        qp = q_ref[0, pl.ds(q0, tile), :]
        zero = jnp.zeros_like(qp)
        q_ops = [jnp.where(lane < HEAD_DIM, qp, zero), jnp.where(lane >= HEAD_DIM, qp, zero)]

        def k_block(j):
            kb = k_ref[0, pl.ds(pl.multiple_of(j * tile, tile), tile), :]
            return [kb, kb]

        def vt_block(j):
            vb = vt_ref[0, 0, j]
            return [vb, vb]

        (l1, a1), (l2, a2) = _causal_flash(qi, tile, q_ops, k_block, vt_block, 2 * HEAD_DIM,
                                           flash_scratch)
        yt = a1 / l1 - lam * (a2 / l2)
        y = _rms(yt.T, g_ref[...]) * (1.0 - lambda_init)
        o_ref[0, pl.ds(q0, tile), :] = y.astype(o_ref.dtype)
        return carry

    lax.fori_loop(0, n_tiles, q_body, 0)


def _diff_attn(qproj, k_sh, vt, lq1, lk1, lq2, lk2, subln_g, *, n_heads, lambda_init):
    B, S, _ = qproj.shape
    tile = ATTN_TILE
    n_tiles = S // tile
    vec = pl.BlockSpec((1, HEAD_DIM), lambda b, h: (0, 0))
    block_bytes = (2 * 4 * _nbytes((S, LANES), BF16)
                   + _flash_scratch_bytes(2, tile, 2 * HEAD_DIM))
    return pl.pallas_call(
        functools.partial(_diff_attn_kernel, tile=tile, lambda_init=lambda_init),
        grid=(B, n_heads),
        in_specs=[
            pl.BlockSpec((1, S, LANES), lambda b, h: (b, 0, h)),
            pl.BlockSpec((1, S, LANES), lambda b, h: (b, 0, h)),
            pl.BlockSpec((1, 1, n_tiles, LANES, tile), lambda b, h: (b, h, 0, 0, 0)),
            vec, vec, vec, vec,
            pl.BlockSpec((1, 2 * HEAD_DIM), lambda b, h: (0, 0)),
        ],
        out_specs=pl.BlockSpec((1, S, LANES), lambda b, h: (b, 0, h)),
        out_shape=jax.ShapeDtypeStruct((B, S, n_heads * 2 * HEAD_DIM), BF16),
        scratch_shapes=_flash_scratch(2, tile, 2 * HEAD_DIM),
        compiler_params=_params(2, block_bytes),
        name="diff_attn",
    )(qproj, k_sh, vt, lq1.reshape(1, -1), lk1.reshape(1, -1), lq2.reshape(1, -1),
      lk2.reshape(1, -1), subln_g.reshape(1, -1))


def _mix_out_kernel(x_ref, y_ref, mq_ref, mk_ref, mv_ref, wo_ref, o_ref):
    tm = x_ref.shape[0]
    mw = mq_ref.shape[1]
    y_width = y_ref.shape[1]
    mq = mq_ref[...]
    mk = mk_ref[0]
    mv = mv_ref[0]
    q_head = lax.broadcasted_iota(jnp.int32, (tm, mw), 1) // HEAD_DIM
    v_head = lax.broadcasted_iota(jnp.int32, mv.shape, 1) // HEAD_DIM
    ymem = jnp.zeros((tm, mw), F32)
    for h in range(mw // HEAD_DIM):
        qh = jnp.where(q_head == h, mq, jnp.zeros_like(mq))
        s = lax.dot_general(qh, mk, _NT, preferred_element_type=F32)
        p = jnp.exp2(s - jnp.max(s, axis=1, keepdims=True))
        l = jnp.sum(p, axis=1, keepdims=True)
        vh = jnp.where(v_head == h, mv, jnp.zeros_like(mv))
        ymem = ymem + jnp.dot(p.astype(BF16), vh, preferred_element_type=F32) / l
    acc = jnp.dot(y_ref[...], wo_ref[0:y_width, :], preferred_element_type=F32)
    acc = acc + jnp.dot(ymem.astype(BF16), wo_ref[y_width:y_width + mw, :],
                        preferred_element_type=F32)
    o_ref[...] = x_ref[...] + acc


def _mix_out(x2d, y2d, proj2d, mq_col, memkv, wo, *, seq):
    T, D = x2d.shape
    yw = y2d.shape[1]
    n_mem, mw2 = memkv.shape[1], memkv.shape[2]
    mw = mw2 // 2
    tm = PROJ_TM
    n_seq_tiles = seq // tm
    block_bytes = (4 * _nbytes((tm, D), F32) + 2 * _nbytes((tm, yw + mw), BF16)
                   + 4 * _nbytes((n_mem, mw), BF16) + _nbytes((yw + mw, D), BF16))
    return pl.pallas_call(
        _mix_out_kernel,
        grid=(T // tm,),
        in_specs=[
            pl.BlockSpec((tm, D), lambda i: (i, 0)),
            pl.BlockSpec((tm, yw), lambda i: (i, 0)),
            pl.BlockSpec((tm, mw), lambda i: (i, mq_col)),
            pl.BlockSpec((1, n_mem, mw), lambda i: (i // n_seq_tiles, 0, 0)),
            pl.BlockSpec((1, n_mem, mw), lambda i: (i // n_seq_tiles, 0, 1)),
            pl.BlockSpec((yw + mw, D), lambda i: (0, 0), pipeline_mode=_RESIDENT),
        ],
        out_specs=pl.BlockSpec((tm, D), lambda i: (i, 0)),
        out_shape=jax.ShapeDtypeStruct((T, D), F32),
        compiler_params=_params(1, block_bytes),
        name="mix_out",
    )(x2d, y2d, proj2d, memkv, memkv, wo)


def _ffn_kernel(*refs, final, n_chunks):
    if final:
        x_ref, g_ref, wgu_ref, wd_ref, fg_ref, o_ref, hn_ref, acc_ref = refs
    else:
        x_ref, g_ref, wgu_ref, wd_ref, o_ref, hn_ref, acc_ref = refs
    x = x_ref[...]
    hn_ref[...] = _rms(x, g_ref[...]).astype(BF16)
    acc_ref[...] = x

    def body(k, carry):
        h = hn_ref[...]
        gate = jnp.dot(h, wgu_ref[k], preferred_element_type=F32)
        up = jnp.dot(h, wgu_ref[n_chunks + k], preferred_element_type=F32)
        a = gate * jax.nn.sigmoid(gate) * up
        acc_ref[...] += jnp.dot(a.astype(BF16), wd_ref[k], preferred_element_type=F32)
        return carry

    lax.fori_loop(0, n_chunks, body, 0)
    out = acc_ref[...]
    if final:
        out = _rms(out, fg_ref[...])
    o_ref[...] = out


def _ffn(x2d, g, wgu, wd, final_g=None):
    T, D = x2d.shape
    n_chunks, tf, _ = wd.shape
    tm = FFN_TM
    assert T % tm == 0
    final = final_g is not None
    in_specs = [
        pl.BlockSpec((tm, D), lambda i: (i, 0)),
        pl.BlockSpec((1, D), lambda i: (0, 0)),
        pl.BlockSpec(wgu.shape, lambda i: (0, 0, 0), pipeline_mode=_RESIDENT),
        pl.BlockSpec(wd.shape, lambda i: (0, 0, 0), pipeline_mode=_RESIDENT),
    ]
    args = [x2d, g.reshape(1, D), wgu, wd]
    if final:
        in_specs.append(pl.BlockSpec((1, D), lambda i: (0, 0)))
        args.append(final_g.reshape(1, D))
    block_bytes = (5 * _nbytes((tm, D), F32) + _nbytes((tm, D), BF16)
                   + _nbytes(wgu.shape, BF16) + _nbytes(wd.shape, BF16))
    return pl.pallas_call(
        functools.partial(_ffn_kernel, final=final, n_chunks=n_chunks),
        grid=(T // tm,),
        in_specs=in_specs,
        out_specs=pl.BlockSpec((tm, D), lambda i: (i, 0)),
        out_shape=jax.ShapeDtypeStruct((T, D), F32),
        scratch_shapes=[pltpu.VMEM((tm, D), BF16), pltpu.VMEM((tm, D), F32)],
        compiler_params=_params(1, block_bytes),
        name="ffn_final" if final else "ffn",
    )(*args)


def _rope_tables(seq):
    half = HEAD_DIM // 2
    inv_freq = jnp.power(ROPE_THETA, -jnp.arange(half, dtype=F32) * (2.0 / HEAD_DIM))
    ang = jnp.arange(seq, dtype=F32)[:, None] * inv_freq[None, :]
    cos, sin = jnp.cos(ang), jnp.sin(ang)
    reps = LANES // HEAD_DIM
    cos_t = jnp.tile(jnp.concatenate([cos, cos], axis=-1), (1, reps))
    sin_t = jnp.tile(jnp.concatenate([-sin, sin], axis=-1), (1, reps))
    return cos_t, sin_t


def _ffn_weights(w_gate_up, w_down):
    D, two_ff = w_gate_up.shape
    n_chunks = two_ff // 2 // FFN_TF
    wgu = w_gate_up.astype(BF16).reshape(D, 2 * n_chunks, FFN_TF).transpose(1, 0, 2)
    wd = w_down.astype(BF16).reshape(n_chunks, FFN_TF, D)
    return wgu, wd


def kernel(x, mem, attn_norm_g, mem_norm_g, w_mem_kv, w_out, ffn_norm_g, w_gate_up, w_down,
           a_w_in, a_b_f, b_w_in, b_lambda_q1, b_lambda_k1, b_lambda_q2, b_lambda_k2,
           b_subln_g, kv_norm_g, w_kv_shared, final_norm_g):
    B, S, D = x.shape
    depth = attn_norm_g.shape[0]
    n_a = a_w_in.shape[0]
    n_mem = mem.shape[1]
    mem_w = w_mem_kv.shape[2] // 2
    n_fox = a_b_f.shape[1]
    fox_w = n_fox * HEAD_DIM
    diff_w = b_w_in.shape[1] - mem_w
    n_diff = diff_w // (2 * HEAD_DIM)
    scale = HEAD_DIM ** -0.5 * LOG2E
    T = B * S

    x2d = x.reshape(T, D)
    mem2d = mem.reshape(B * n_mem, D)
    rope_tabs = _rope_tables(S)
    k_sh = vt_sh = None

    for layer in range(depth):
        (memkv,) = _proj(mem2d, [(mem_norm_g[layer], w_mem_kv[layer].astype(BF16),
                                  0, 2 * mem_w, 0)], seq=n_mem)
        memkv = memkv.reshape(B, n_mem, 2 * mem_w)
        wo = w_out[layer].astype(BF16)
        if layer < n_a:
            w_in = a_w_in[layer]
            w_main = jnp.concatenate(
                [w_in[:, :fox_w] * scale, w_in[:, fox_w:2 * fox_w],
                 w_in[:, 3 * fox_w + n_fox:] * scale, w_in[:, 2 * fox_w:3 * fox_w]],
                axis=1).astype(BF16)
            gw = jnp.pad(w_in[:, 3 * fox_w:3 * fox_w + n_fox], ((0, 0), (0, LANES - n_fox)))
            gw_hi = gw.astype(BF16)
            gw_lo = (gw - gw_hi.astype(F32)).astype(BF16)
            gate_w = jnp.concatenate([gw_hi, gw_lo], axis=1)
            gate_b = jnp.pad(a_b_f[layer], (0, LANES - n_fox)).reshape(1, LANES)
            row_cols = 2 * fox_w + mem_w
            proj, vt, c = _proj(x2d, [(attn_norm_g[layer], w_main, 0, row_cols, fox_w)], seq=S,
                                gate_w=gate_w, gate_b=gate_b)
            y = _fox_attn(proj.reshape(B, S, row_cols), vt, c.reshape(B, S, LANES),
                          n_heads=n_fox, k_col=fox_w // LANES, out_width=fox_w)
            mq_col = 2 * fox_w // mem_w
        else:
            j = layer - n_a
            streams = [(attn_norm_g[layer], (b_w_in[j] * scale).astype(BF16),
                        diff_w, diff_w + mem_w, 0)]
            if layer == n_a:
                streams.append((kv_norm_g, w_kv_shared.astype(BF16), diff_w, diff_w,
                                w_kv_shared.shape[1] - diff_w))
                proj, k_sh, vt_sh = _proj(x2d, streams, seq=S, rope_tabs=rope_tabs)
                k_sh = k_sh.reshape(B, S, diff_w)
            else:
                (proj,) = _proj(x2d, streams, seq=S, rope_tabs=rope_tabs)
            lambda_init = 0.8 - 0.6 * math.exp(-0.3 * layer)
            y = _diff_attn(proj.reshape(B, S, -1), k_sh, vt_sh, b_lambda_q1[j], b_lambda_k1[j],
                           b_lambda_q2[j], b_lambda_k2[j], b_subln_g[j], n_heads=n_diff,
                           lambda_init=lambda_init)
            mq_col = diff_w // mem_w
        x2d = _mix_out(x2d, y.reshape(T, -1), proj, mq_col, memkv, wo, seq=S)
        last = layer == depth - 1
        wgu, wd = _ffn_weights(w_gate_up[layer], w_down[layer])
        x2d = _ffn(x2d, ffn_norm_g[layer], wgu, wd, final_g=final_norm_g if last else None)
    return x2d.reshape(B, S, D)
```

```python
import functools
import math

import jax
import jax.numpy as jnp
from jax import lax
from jax.experimental import pallas as pl
from jax.experimental.pallas import tpu as pltpu

HEAD_DIM = 64
ROPE_THETA = 10000.0
NORM_EPS = 1e-6
LANES = 128
ATTN_TILE = 512
PROJ_TM = ATTN_TILE
FFN_TM = 1024
FFN_TF = 256
VMEM_CAP = 56 * 1024 * 1024
VMEM_TEMPS = 16 * 1024 * 1024
LOG2E = math.log2(math.e)

F32 = jnp.float32
BF16 = jnp.bfloat16
_NT = (((1,), (1,)), ((), ()))
_RESIDENT = pl.Buffered(1)


def _nbytes(shape, dtype):
    return math.prod(shape) * jnp.dtype(dtype).itemsize


def _params(n_grid, block_bytes):
    limit = min(VMEM_CAP, block_bytes + VMEM_TEMPS)
    return pltpu.CompilerParams(dimension_semantics=("arbitrary",) * n_grid,
                                vmem_limit_bytes=limit)


def _rms(x, g):
    ms = jnp.mean(x * x, axis=-1, keepdims=True)
    return x * lax.rsqrt(ms + NORM_EPS) * g


def _split_bf16(v, n):
    pieces = []
    for _ in range(n - 1):
        p = v.astype(BF16)
        pieces.append(p)
        v = v - p.astype(F32)
    pieces.append(v.astype(BF16))
    return pieces


def _proj_kernel(*refs, streams, gates, n_seq_tiles):
    it = iter(refs)
    x_ref = next(it)
    gw = [(next(it), next(it)) for _ in streams]
    any_rope = any(s[0] for s in streams)
    if any_rope:
        cos_ref, sin_ref = next(it), next(it)
    if gates:
        wf_ref, bf_ref = next(it), next(it)
    outs = []
    for (_, _, vt_cols) in streams:
        row_ref = next(it)
        outs.append((row_ref, next(it) if vt_cols else None))
    if gates:
        c_ref, carry_ref = next(it), next(it)

    tm = x_ref.shape[0]
    x = x_ref[...]
    xn = x * lax.rsqrt(jnp.mean(x * x, axis=-1, keepdims=True) + NORM_EPS)
    if any_rope:
        cos = cos_ref[...]
        sin = sin_ref[...]
        lane = lax.broadcasted_iota(jnp.int32, (tm, LANES), 1)
        first_half = (lane % HEAD_DIM) < (HEAD_DIM // 2)

    for si, ((rope_cols, row_cols, vt_cols), (g_ref, w_ref), (row_ref, vt_ref)) in enumerate(
            zip(streams, gw, outs)):
        hn32 = xn * g_ref[...]
        hn = hn32.astype(BF16)
        res = jnp.dot(hn, w_ref[...], preferred_element_type=F32)
        for c0 in range(0, rope_cols, LANES):
            t = res[:, c0:c0 + LANES]
            up = pltpu.roll(t, HEAD_DIM // 2, 1)
            dn = pltpu.roll(t, LANES - HEAD_DIM // 2, 1)
            sw = jnp.where(first_half, dn, up)
            row_ref[:, c0:c0 + LANES] = (t * cos + sw * sin).astype(BF16)
        if row_cols > rope_cols:
            row_ref[:, rope_cols:row_cols] = res[:, rope_cols:row_cols].astype(BF16)
        for gi in range(vt_cols // LANES):
            c0 = row_cols + gi * LANES
            vt_ref[0, gi, 0] = res[:, c0:c0 + LANES].T.astype(BF16)

        if gates and si == 0:
            lo = (hn32 - hn.astype(F32)).astype(BF16)
            wf = wf_ref[...]
            zz = jnp.dot(hn, wf, preferred_element_type=F32)
            z = (zz[:, :LANES] + zz[:, LANES:]
                 + jnp.dot(lo, wf[:, :LANES], preferred_element_type=F32) + bf_ref[...])
            lf = jnp.minimum(z, 0.0) - jnp.log1p(jnp.exp(-jnp.abs(z)))
            l_hi, l_mid, l_lo = _split_bf16(lf, 3)
            row = lax.broadcasted_iota(jnp.int32, (tm, tm), 0)
            col = lax.broadcasted_iota(jnp.int32, (tm, tm), 1)
            tri = (row >= col).astype(BF16)
            cc = jnp.dot(tri, jnp.concatenate([l_hi, l_mid], axis=1), preferred_element_type=F32)
            cs = cc[:, :LANES] + cc[:, LANES:] + jnp.dot(tri, l_lo, preferred_element_type=F32)

            @pl.when((pl.program_id(0) % n_seq_tiles) == 0)
            def _():
                carry_ref[...] = jnp.zeros_like(carry_ref)

            c = cs + carry_ref[...]
            c_ref[...] = c
            carry_ref[...] = c[tm - 1:tm, :]


def _proj(x2d, streams, *, seq, rope_tabs=None, gate_w=None, gate_b=None):
    T, D = x2d.shape
    tm = min(PROJ_TM, T)
    assert T % tm == 0
    gates = gate_w is not None
    any_rope = any(s[2] for s in streams)
    any_vt = any(s[4] for s in streams)
    assert seq % tm == 0 or not (gates or any_rope or any_vt)
    n_seq_tiles = max(1, seq // tm)
    n_batch = T // seq

    in_specs = [pl.BlockSpec((tm, D), lambda i: (i, 0))]
    args = [x2d]
    block_bytes = 2 * _nbytes((tm, D), F32)
    for g, w, _, _, _ in streams:
        in_specs += [pl.BlockSpec((1, D), lambda i: (0, 0)),
                     pl.BlockSpec(w.shape, lambda i: (0, 0), pipeline_mode=_RESIDENT)]
        args += [g.reshape(1, D), w]
        block_bytes += _nbytes(w.shape, BF16) + _nbytes((tm, w.shape[1]), F32)
    if any_rope:
        tab_spec = pl.BlockSpec((tm, LANES), lambda i: (i % n_seq_tiles, 0))
        in_specs += [tab_spec, tab_spec]
        args += list(rope_tabs)
        block_bytes += 4 * _nbytes((tm, LANES), F32)
    if gates:
        in_specs += [pl.BlockSpec(gate_w.shape, lambda i: (0, 0), pipeline_mode=_RESIDENT),
                     pl.BlockSpec((1, LANES), lambda i: (0, 0))]
        args += [gate_w, gate_b]
        block_bytes += _nbytes(gate_w.shape, BF16)

    out_shape, out_specs = [], []
    for _, w, _, row_cols, vt_cols in streams:
        out_shape.append(jax.ShapeDtypeStruct((T, row_cols), BF16))
        out_specs.append(pl.BlockSpec((tm, row_cols), lambda i: (i, 0)))
        block_bytes += 2 * _nbytes((tm, w.shape[1]), BF16)
        if vt_cols:
            n_groups = vt_cols // LANES
            out_shape.append(
                jax.ShapeDtypeStruct((n_batch, n_groups, n_seq_tiles, LANES, tm), BF16))
            out_specs.append(pl.BlockSpec(
                (1, n_groups, 1, LANES, tm),
                lambda i: (i // n_seq_tiles, 0, i % n_seq_tiles, 0, 0)))
    scratch = []
    if gates:
        out_shape.append(jax.ShapeDtypeStruct((T, LANES), F32))
        out_specs.append(pl.BlockSpec((tm, LANES), lambda i: (i, 0)))
        scratch.append(pltpu.VMEM((1, LANES), F32))

    cfg = tuple((s[2], s[3], s[4]) for s in streams)
    return pl.pallas_call(
        functools.partial(_proj_kernel, streams=cfg, gates=gates, n_seq_tiles=n_seq_tiles),
        grid=(T // tm,),
        in_specs=in_specs,
        out_specs=out_specs,
        out_shape=out_shape,
        scratch_shapes=scratch,
        compiler_params=_params(1, block_bytes),
        name="proj_gates" if gates else ("proj_rope" if any_rope else "proj"),
    )(*args)


ONES_ROWS = 16
M_INIT = -1e30


def _flash_scratch(n_maps, tile, dv):
    scores = pltpu.VMEM((n_maps, tile, tile), F32)
    row = pltpu.VMEM((n_maps, 1, tile), F32)
    return [scores, row, scores, row, row, pltpu.VMEM((n_maps, dv + ONES_ROWS, tile), F32)]


def _flash_scratch_bytes(n_maps, tile, dv):
    return (2 * _nbytes((n_maps, tile, tile), F32) + 3 * _nbytes((n_maps, 8, tile), F32)
            + _nbytes((n_maps, dv + ONES_ROWS, tile), F32))


def _causal_flash(qi, tile, q_ops, k_block, vt_block, dv, scratch):
    buf_a, buf_b, m_ref, acc_ref = scratch[:2], scratch[2:4], scratch[4], scratch[5]
    n_maps = len(q_ops)
    ones = jnp.ones((ONES_ROWS, tile), BF16)

    def issue(j, buf):
        s_ref, bm_ref = buf
        for i, (ka, qa) in enumerate(zip(k_block(j), q_ops)):
            st = lax.dot_general(ka, qa, _NT, preferred_element_type=F32)
            s_ref[i] = st
            bm_ref[i] = jnp.max(st, axis=0, keepdims=True)

    def process(buf, j, mask):
        s_ref, bm_ref = buf
        for i, vt in enumerate(vt_block(j)):
            st = s_ref[i]
            if mask is None:
                bm = bm_ref[i]
            else:
                st = jnp.where(mask, st, -jnp.inf)
                bm = jnp.max(st, axis=0, keepdims=True)
            m = m_ref[i]
            m_new = jnp.maximum(m, bm)
            alpha = jnp.exp2(m - m_new)
            p = jnp.exp2(st - m_new).astype(BF16)
            pv = jnp.dot(jnp.concatenate([vt, ones], axis=0), p, preferred_element_type=F32)
            acc_ref[i] = alpha * acc_ref[i] + pv
            m_ref[i] = m_new

    m_ref[...] = jnp.full(m_ref.shape, M_INIT, F32)
    acc_ref[...] = jnp.zeros(acc_ref.shape, F32)
    issue(0, buf_a)

    def pair(u, carry):
        issue(2 * u + 1, buf_b)
        process(buf_a, 2 * u, None)
        issue(2 * u + 2, buf_a)
        process(buf_b, 2 * u + 1, None)
        return carry

    lax.fori_loop(0, qi // 2, pair, 0)
    kk = lax.broadcasted_iota(jnp.int32, (tile, tile), 0)
    qq = lax.broadcasted_iota(jnp.int32, (tile, tile), 1)
    causal = kk <= qq

    @pl.when(qi % 2 == 1)
    def _():
        issue(qi, buf_b)
        process(buf_a, qi - 1, None)
        process(buf_b, qi, causal)

    @pl.when(qi % 2 == 0)
    def _():
        process(buf_a, qi, causal)

    return [(acc_ref[i, dv:dv + 1, :], acc_ref[i, :dv, :]) for i in range(n_maps)]


def _fox_attn_kernel(q_ref, k_ref, vt_ref, c_ref, o_ref, qa_ref, ka_ref, *flash_scratch, tile):
    seq = q_ref.shape[1]
    n_tiles = seq // tile
    hp = pl.program_id(1)
    lane = lax.broadcasted_iota(jnp.int32, (tile, LANES), 1)
    d = HEAD_DIM

    def build(ci, carry):
        r0 = pl.multiple_of(ci * tile, tile)
        qp = q_ref[0, pl.ds(r0, tile), :].astype(F32)
        kp = k_ref[0, pl.ds(r0, tile), :].astype(F32)
        cc = c_ref[0, pl.ds(r0, tile), :] * LOG2E
        for h2 in range(2):
            head = 2 * hp + h2
            ch = jnp.sum(jnp.where(lane == head, cc, 0.0), axis=1, keepdims=True)
            hi, mid, lo = [p.astype(F32)
                           for p in _split_bf16(jnp.broadcast_to(ch, (tile, LANES)), 3)]
            if h2 == 0:
                qx, kx = qp, kp
            else:
                qx = pltpu.roll(qp, HEAD_DIM, 1)
                kx = pltpu.roll(kp, HEAD_DIM, 1)
            k_extra = jnp.where(lane == d, hi, jnp.where(lane == d + 1, mid, jnp.where(
                lane == d + 2, lo, jnp.where(lane < d + 6, 1.0, 0.0))))
            q_extra = jnp.where(lane < d + 3, -1.0, jnp.where(lane == d + 3, hi, jnp.where(
                lane == d + 4, mid, jnp.where(lane == d + 5, lo, 0.0))))
            ka_ref[h2, pl.ds(r0, tile), :] = jnp.where(lane < d, kx, k_extra).astype(BF16)
            qa_ref[h2, pl.ds(r0, tile), :] = jnp.where(lane < d, qx, q_extra).astype(BF16)
        return carry

    lax.fori_loop(0, n_tiles, build, 0)

    def q_body(qi, carry):
        q0 = pl.multiple_of(qi * tile, tile)
        q_ops = [qa_ref[h2, pl.ds(q0, tile), :] for h2 in range(2)]

        def k_block(j):
            start = pl.multiple_of(j * tile, tile)
            return [ka_ref[h2, pl.ds(start, tile), :] for h2 in range(2)]

        def vt_block(j):
            return [vt_ref[0, 0, j, h2 * d:(h2 + 1) * d, :] for h2 in range(2)]

        (l0, a0), (l1, a1) = _causal_flash(qi, tile, q_ops, k_block, vt_block, d, flash_scratch)
        yt = jnp.concatenate([a0 / l0, a1 / l1], axis=0)
        o_ref[0, pl.ds(q0, tile), :] = yt.T.astype(o_ref.dtype)
        return carry

    lax.fori_loop(0, n_tiles, q_body, 0)


def _fox_attn(proj, vt, c, *, n_heads, k_col, out_width):
    B, S, _ = proj.shape
    tile = ATTN_TILE
    n_tiles = S // tile
    n_pairs = n_heads // 2
    seq_block = _nbytes((S, LANES), BF16)
    block_bytes = (2 * (4 * seq_block + _nbytes((S, LANES), F32)) + 4 * seq_block
                   + _flash_scratch_bytes(2, tile, HEAD_DIM))
    return pl.pallas_call(
        functools.partial(_fox_attn_kernel, tile=tile),
        grid=(B, n_pairs),
        in_specs=[
            pl.BlockSpec((1, S, LANES), lambda b, h: (b, 0, h)),
            pl.BlockSpec((1, S, LANES), lambda b, h: (b, 0, k_col + h)),
            pl.BlockSpec((1, 1, n_tiles, LANES, tile), lambda b, h: (b, h, 0, 0, 0)),
            pl.BlockSpec((1, S, LANES), lambda b, h: (b, 0, 0)),
        ],
        out_specs=pl.BlockSpec((1, S, LANES), lambda b, h: (b, 0, h)),
        out_shape=jax.ShapeDtypeStruct((B, S, out_width), BF16),
        scratch_shapes=[
            pltpu.VMEM((2, S, LANES), BF16),
            pltpu.VMEM((2, S, LANES), BF16),
        ] + _flash_scratch(2, tile, HEAD_DIM),
        compiler_params=_params(2, block_bytes),
        name="fox_attn",
    )(proj, proj, vt, c)


def _diff_attn_kernel(q_ref, k_ref, vt_ref, lq1_ref, lk1_ref, lq2_ref, lk2_ref, g_ref, o_ref,
                      *flash_scratch, tile, lambda_init):
    seq = q_ref.shape[1]
    n_tiles = seq // tile
    lane = lax.broadcasted_iota(jnp.int32, (tile, LANES), 1)
    lam = (jnp.exp(jnp.sum(lq1_ref[...] * lk1_ref[...], axis=1, keepdims=True))
           - jnp.exp(jnp.sum(lq2_ref[...] * lk2_ref[...], axis=1, keepdims=True)) + lambda_init)

    def q_body(qi, carry):
        q0 = pl.multiple_of(qi * tile, tile)
        qp = q_ref[0, pl.ds(q0, tile), :]
        zero = jnp.zeros_like(qp)
        q_ops = [jnp.where(lane < HEAD_DIM, qp, zero), jnp.where(lane >= HEAD_DIM, qp, zero)]

        def k_block(j):
            kb = k_ref[0, pl.ds(pl.multiple_of(j * tile, tile), tile), :]
            return [kb, kb]

        def vt_block(j):
            vb = vt_ref[0, 0, j]
            return [vb, vb]

        (l1, a1), (l2, a2) = _causal_flash(qi, tile, q_ops, k_block, vt_block, 2 * HEAD_DIM,
                                           flash_scratch)
        yt = a1 / l1 - lam * (a2 / l2)
        y = _rms(yt.T, g_ref[...]) * (1.0 - lambda_init)
        o_ref[0, pl.ds(q0, tile), :] = y.astype(o_ref.dtype)
        return carry

    lax.fori_loop(0, n_tiles, q_body, 0)


def _diff_attn(qproj, k_sh, vt, lq1, lk1, lq2, lk2, subln_g, *, n_heads, lambda_init):
    B, S, _ = qproj.shape
    tile = ATTN_TILE
    n_tiles = S // tile
    vec = pl.BlockSpec((1, HEAD_DIM), lambda b, h: (0, 0))
    block_bytes = (2 * 4 * _nbytes((S, LANES), BF16)
                   + _flash_scratch_bytes(2, tile, 2 * HEAD_DIM))
    return pl.pallas_call(
        functools.partial(_diff_attn_kernel, tile=tile, lambda_init=lambda_init),
        grid=(B, n_heads),
        in_specs=[
            pl.BlockSpec((1, S, LANES), lambda b, h: (b, 0, h)),
            pl.BlockSpec((1, S, LANES), lambda b, h: (b, 0, h)),
            pl.BlockSpec((1, 1, n_tiles, LANES, tile), lambda b, h: (b, h, 0, 0, 0)),
            vec, vec, vec, vec,
            pl.BlockSpec((1, 2 * HEAD_DIM), lambda b, h: (0, 0)),
        ],
        out_specs=pl.BlockSpec((1, S, LANES), lambda b, h: (b, 0, h)),
        out_shape=jax.ShapeDtypeStruct((B, S, n_heads * 2 * HEAD_DIM), BF16),
        scratch_shapes=_flash_scratch(2, tile, 2 * HEAD_DIM),
        compiler_params=_params(2, block_bytes),
        name="diff_attn",
    )(qproj, k_sh, vt, lq1.reshape(1, -1), lk1.reshape(1, -1), lq2.reshape(1, -1),
      lk2.reshape(1, -1), subln_g.reshape(1, -1))


def _mix_out_kernel(x_ref, y_ref, mq_ref, mk_ref, mv_ref, wo_ref, o_ref):
    tm = x_ref.shape[0]
    mw = mq_ref.shape[1]
    y_width = y_ref.shape[1]
    mq = mq_ref[...]
    mk = mk_ref[0]
    mv = mv_ref[0]
    q_head = lax.broadcasted_iota(jnp.int32, (tm, mw), 1) // HEAD_DIM
    v_head = lax.broadcasted_iota(jnp.int32, mv.shape, 1) // HEAD_DIM
    ymem = jnp.zeros((tm, mw), F32)
    for h in range(mw // HEAD_DIM):
        qh = jnp.where(q_head == h, mq, jnp.zeros_like(mq))
        s = lax.dot_general(qh, mk, _NT, preferred_element_type=F32)
        p = jnp.exp2(s - jnp.max(s, axis=1, keepdims=True))
        l = jnp.sum(p, axis=1, keepdims=True)
        vh = jnp.where(v_head == h, mv, jnp.zeros_like(mv))
        ymem = ymem + jnp.dot(p.astype(BF16), vh, preferred_element_type=F32) / l
    acc = jnp.dot(y_ref[...], wo_ref[0:y_width, :], preferred_element_type=F32)
    acc = acc + jnp.dot(ymem.astype(BF16), wo_ref[y_width:y_width + mw, :],
                        preferred_element_type=F32)
    o_ref[...] = x_ref[...] + acc


def _mix_out(x2d, y2d, proj2d, mq_col, memkv, wo, *, seq):
    T, D = x2d.shape
    yw = y2d.shape[1]
    n_mem, mw2 = memkv.shape[1], memkv.shape[2]
    mw = mw2 // 2
    tm = PROJ_TM
    n_seq_tiles = seq // tm
    block_bytes = (4 * _nbytes((tm, D), F32) + 2 * _nbytes((tm, yw + mw), BF16)
                   + 4 * _nbytes((n_mem, mw), BF16) + _nbytes((yw + mw, D), BF16))
    return pl.pallas_call(
        _mix_out_kernel,
        grid=(T // tm,),
        in_specs=[
            pl.BlockSpec((tm, D), lambda i: (i, 0)),
            pl.BlockSpec((tm, yw), lambda i: (i, 0)),
            pl.BlockSpec((tm, mw), lambda i: (i, mq_col)),
            pl.BlockSpec((1, n_mem, mw), lambda i: (i // n_seq_tiles, 0, 0)),
            pl.BlockSpec((1, n_mem, mw), lambda i: (i // n_seq_tiles, 0, 1)),
            pl.BlockSpec((yw + mw, D), lambda i: (0, 0), pipeline_mode=_RESIDENT),
        ],
        out_specs=pl.BlockSpec((tm, D), lambda i: (i, 0)),
        out_shape=jax.ShapeDtypeStruct((T, D), F32),
        compiler_params=_params(1, block_bytes),
        name="mix_out",
    )(x2d, y2d, proj2d, memkv, memkv, wo)


def _ffn_kernel(*refs, final, n_chunks):
    if final:
        x_ref, g_ref, wgu_ref, wd_ref, fg_ref, o_ref, hn_ref, acc_ref = refs
    else:
        x_ref, g_ref, wgu_ref, wd_ref, o_ref, hn_ref, acc_ref = refs
    x = x_ref[...]
    hn_ref[...] = _rms(x, g_ref[...]).astype(BF16)
    acc_ref[...] = x

    d_ff = wd_ref.shape[0]
    tf = d_ff // n_chunks
    for k in range(n_chunks):
        h = hn_ref[...]
        gate = jnp.dot(h, wgu_ref[:, k * tf:(k + 1) * tf], preferred_element_type=F32)
        up = jnp.dot(h, wgu_ref[:, d_ff + k * tf:d_ff + (k + 1) * tf],
                     preferred_element_type=F32)
        a = gate * jax.nn.sigmoid(gate) * up
        acc_ref[...] += jnp.dot(a.astype(BF16), wd_ref[k * tf:(k + 1) * tf, :],
                                preferred_element_type=F32)
    out = acc_ref[...]
    if final:
        out = _rms(out, fg_ref[...])
    o_ref[...] = out


def _ffn(x2d, g, wgu, wd, final_g=None):
    T, D = x2d.shape
    d_ff = wd.shape[0]
    tm = FFN_TM
    assert T % tm == 0 and d_ff % FFN_TF == 0
    n_chunks = d_ff // FFN_TF
    final = final_g is not None
    in_specs = [
        pl.BlockSpec((tm, D), lambda i: (i, 0)),
        pl.BlockSpec((1, D), lambda i: (0, 0)),
        pl.BlockSpec(wgu.shape, lambda i: (0, 0), pipeline_mode=_RESIDENT),
        pl.BlockSpec(wd.shape, lambda i: (0, 0), pipeline_mode=_RESIDENT),
    ]
    args = [x2d, g.reshape(1, D), wgu, wd]
    if final:
        in_specs.append(pl.BlockSpec((1, D), lambda i: (0, 0)))
        args.append(final_g.reshape(1, D))
    block_bytes = (5 * _nbytes((tm, D), F32) + _nbytes((tm, D), BF16)
                   + _nbytes(wgu.shape, BF16) + _nbytes(wd.shape, BF16))
    return pl.pallas_call(
        functools.partial(_ffn_kernel, final=final, n_chunks=n_chunks),
        grid=(T // tm,),
        in_specs=in_specs,
        out_specs=pl.BlockSpec((tm, D), lambda i: (i, 0)),
        out_shape=jax.ShapeDtypeStruct((T, D), F32),
        scratch_shapes=[pltpu.VMEM((tm, D), BF16), pltpu.VMEM((tm, D), F32)],
        compiler_params=_params(1, block_bytes),
        name="ffn_final" if final else "ffn",
    )(*args)


def _rope_tables(seq):
    half = HEAD_DIM // 2
    inv_freq = jnp.power(ROPE_THETA, -jnp.arange(half, dtype=F32) * (2.0 / HEAD_DIM))
    ang = jnp.arange(seq, dtype=F32)[:, None] * inv_freq[None, :]
    cos, sin = jnp.cos(ang), jnp.sin(ang)
    reps = LANES // HEAD_DIM
    cos_t = jnp.tile(jnp.concatenate([cos, cos], axis=-1), (1, reps))
    sin_t = jnp.tile(jnp.concatenate([-sin, sin], axis=-1), (1, reps))
    return cos_t, sin_t


def kernel(x, mem, attn_norm_g, mem_norm_g, w_mem_kv, w_out, ffn_norm_g, w_gate_up, w_down,
           a_w_in, a_b_f, b_w_in, b_lambda_q1, b_lambda_k1, b_lambda_q2, b_lambda_k2,
           b_subln_g, kv_norm_g, w_kv_shared, final_norm_g):
    B, S, D = x.shape
    depth = attn_norm_g.shape[0]
    n_a = a_w_in.shape[0]
    n_mem = mem.shape[1]
    mem_w = w_mem_kv.shape[2] // 2
    n_fox = a_b_f.shape[1]
    fox_w = n_fox * HEAD_DIM
    diff_w = b_w_in.shape[1] - mem_w
    n_diff = diff_w // (2 * HEAD_DIM)
    scale = HEAD_DIM ** -0.5 * LOG2E
    T = B * S

    x2d = x.reshape(T, D)
    mem2d = mem.reshape(B * n_mem, D)
    rope_tabs = _rope_tables(S)
    k_sh = vt_sh = None

    for layer in range(depth):
        (memkv,) = _proj(mem2d, [(mem_norm_g[layer], w_mem_kv[layer].astype(BF16),
                                  0, 2 * mem_w, 0)], seq=n_mem)
        memkv = memkv.reshape(B, n_mem, 2 * mem_w)
        wo = w_out[layer].astype(BF16)
        if layer < n_a:
            w_in = a_w_in[layer]
            w_main = jnp.concatenate(
                [w_in[:, :fox_w] * scale, w_in[:, fox_w:2 * fox_w],
                 w_in[:, 3 * fox_w + n_fox:] * scale, w_in[:, 2 * fox_w:3 * fox_w]],
                axis=1).astype(BF16)
            gw = jnp.pad(w_in[:, 3 * fox_w:3 * fox_w + n_fox], ((0, 0), (0, LANES - n_fox)))
            gw_hi = gw.astype(BF16)
            gw_lo = (gw - gw_hi.astype(F32)).astype(BF16)
            gate_w = jnp.concatenate([gw_hi, gw_lo], axis=1)
            gate_b = jnp.pad(a_b_f[layer], (0, LANES - n_fox)).reshape(1, LANES)
            row_cols = 2 * fox_w + mem_w
            proj, vt, c = _proj(x2d, [(attn_norm_g[layer], w_main, 0, row_cols, fox_w)], seq=S,
                                gate_w=gate_w, gate_b=gate_b)
            y = _fox_attn(proj.reshape(B, S, row_cols), vt, c.reshape(B, S, LANES),
                          n_heads=n_fox, k_col=fox_w // LANES, out_width=fox_w)
            mq_col = 2 * fox_w // mem_w
        else:
            j = layer - n_a
            streams = [(attn_norm_g[layer], (b_w_in[j] * scale).astype(BF16),
                        diff_w, diff_w + mem_w, 0)]
            if layer == n_a:
                streams.append((kv_norm_g, w_kv_shared.astype(BF16), diff_w, diff_w,
                                w_kv_shared.shape[1] - diff_w))
                proj, k_sh, vt_sh = _proj(x2d, streams, seq=S, rope_tabs=rope_tabs)
                k_sh = k_sh.reshape(B, S, diff_w)
            else:
                (proj,) = _proj(x2d, streams, seq=S, rope_tabs=rope_tabs)
            lambda_init = 0.8 - 0.6 * math.exp(-0.3 * layer)
            y = _diff_attn(proj.reshape(B, S, -1), k_sh, vt_sh, b_lambda_q1[j], b_lambda_k1[j],
                           b_lambda_q2[j], b_lambda_k2[j], b_subln_g[j], n_heads=n_diff,
                           lambda_init=lambda_init)
            mq_col = diff_w // mem_w
        x2d = _mix_out(x2d, y.reshape(T, -1), proj, mq_col, memkv, wo, seq=S)
        last = layer == depth - 1
        x2d = _ffn(x2d, ffn_norm_g[layer], w_gate_up[layer].astype(BF16),
                   w_down[layer].astype(BF16), final_g=final_norm_g if last else None)
    return x2d.reshape(B, S, D)
```

```python
import functools
import math

import jax
import jax.numpy as jnp
from jax import lax
from jax.experimental import pallas as pl
from jax.experimental.pallas import tpu as pltpu

HEAD_DIM = 64
ROPE_THETA = 10000.0
NORM_EPS = 1e-6
LANES = 128
ATTN_TILE = 512
PROJ_TM = ATTN_TILE
FFN_TM = 1024
FFN_TF = 256
VMEM_CAP = 56 * 1024 * 1024
VMEM_TEMPS = 16 * 1024 * 1024
LOG2E = math.log2(math.e)

F32 = jnp.float32
BF16 = jnp.bfloat16
_NT = (((1,), (1,)), ((), ()))
_RESIDENT = pl.Buffered(1)


def _nbytes(shape, dtype):
    return math.prod(shape) * jnp.dtype(dtype).itemsize


def _params(n_grid, block_bytes):
    limit = min(VMEM_CAP, block_bytes + VMEM_TEMPS)
    return pltpu.CompilerParams(dimension_semantics=("arbitrary",) * n_grid,
                                vmem_limit_bytes=limit)


def _rms(x, g):
    ms = jnp.mean(x * x, axis=-1, keepdims=True)
    return x * lax.rsqrt(ms + NORM_EPS) * g


def _split_bf16(v, n):
    pieces = []
    for _ in range(n - 1):
        p = v.astype(BF16)
        pieces.append(p)
        v = v - p.astype(F32)
    pieces.append(v.astype(BF16))
    return pieces


def _proj_kernel(*refs, streams, gates, n_seq_tiles):
    it = iter(refs)
    x_ref = next(it)
    gw = [(next(it), next(it)) for _ in streams]
    any_rope = any(s[0] for s in streams)
    if any_rope:
        cos_ref, sin_ref = next(it), next(it)
    if gates:
        wf_ref, bf_ref = next(it), next(it)
    outs = []
    for (_, _, vt_cols) in streams:
        row_ref = next(it)
        outs.append((row_ref, next(it) if vt_cols else None))
    if gates:
        c_ref, carry_ref = next(it), next(it)

    tm = x_ref.shape[0]
    x = x_ref[...]
    xn = x * lax.rsqrt(jnp.mean(x * x, axis=-1, keepdims=True) + NORM_EPS)
    if any_rope:
        cos = cos_ref[...]
        sin = sin_ref[...]
        lane = lax.broadcasted_iota(jnp.int32, (tm, LANES), 1)
        first_half = (lane % HEAD_DIM) < (HEAD_DIM // 2)

    for si, ((rope_cols, row_cols, vt_cols), (g_ref, w_ref), (row_ref, vt_ref)) in enumerate(
            zip(streams, gw, outs)):
        hn32 = xn * g_ref[...]
        hn = hn32.astype(BF16)
        res = jnp.dot(hn, w_ref[...], preferred_element_type=F32)
        for c0 in range(0, rope_cols, LANES):
            t = res[:, c0:c0 + LANES]
            up = pltpu.roll(t, HEAD_DIM // 2, 1)
            dn = pltpu.roll(t, LANES - HEAD_DIM // 2, 1)
            sw = jnp.where(first_half, dn, up)
            row_ref[:, c0:c0 + LANES] = (t * cos + sw * sin).astype(BF16)
        if row_cols > rope_cols:
            row_ref[:, rope_cols:row_cols] = res[:, rope_cols:row_cols].astype(BF16)
        for gi in range(vt_cols // LANES):
            c0 = row_cols + gi * LANES
            vt_ref[0, gi, 0] = res[:, c0:c0 + LANES].T.astype(BF16)

        if gates and si == 0:
            lo = (hn32 - hn.astype(F32)).astype(BF16)
            wf = wf_ref[...]
            zz = jnp.dot(hn, wf, preferred_element_type=F32)
            z = (zz[:, :LANES] + zz[:, LANES:]
                 + jnp.dot(lo, wf[:, :LANES], preferred_element_type=F32) + bf_ref[...])
            lf = jnp.minimum(z, 0.0) - jnp.log1p(jnp.exp(-jnp.abs(z)))
            l_hi, l_mid, l_lo = _split_bf16(lf, 3)
            row = lax.broadcasted_iota(jnp.int32, (tm, tm), 0)
            col = lax.broadcasted_iota(jnp.int32, (tm, tm), 1)
            tri = (row >= col).astype(BF16)
            cc = jnp.dot(tri, jnp.concatenate([l_hi, l_mid], axis=1), preferred_element_type=F32)
            cs = cc[:, :LANES] + cc[:, LANES:] + jnp.dot(tri, l_lo, preferred_element_type=F32)

            @pl.when((pl.program_id(0) % n_seq_tiles) == 0)
            def _():
                carry_ref[...] = jnp.zeros_like(carry_ref)

            c = cs + carry_ref[...]
            c_ref[...] = c
            carry_ref[...] = c[tm - 1:tm, :]


def _proj(x2d, streams, *, seq, rope_tabs=None, gate_w=None, gate_b=None):
    T, D = x2d.shape
    tm = min(PROJ_TM, T)
    assert T % tm == 0
    gates = gate_w is not None
    any_rope = any(s[2] for s in streams)
    any_vt = any(s[4] for s in streams)
    assert seq % tm == 0 or not (gates or any_rope or any_vt)
    n_seq_tiles = max(1, seq // tm)
    n_batch = T // seq

    in_specs = [pl.BlockSpec((tm, D), lambda i: (i, 0))]
    args = [x2d]
    block_bytes = 2 * _nbytes((tm, D), F32)
    for g, w, _, _, _ in streams:
        in_specs += [pl.BlockSpec((1, D), lambda i: (0, 0)),
                     pl.BlockSpec(w.shape, lambda i: (0, 0), pipeline_mode=_RESIDENT)]
        args += [g.reshape(1, D), w]
        block_bytes += _nbytes(w.shape, BF16) + _nbytes((tm, w.shape[1]), F32)
    if any_rope:
        tab_spec = pl.BlockSpec((tm, LANES), lambda i: (i % n_seq_tiles, 0))
        in_specs += [tab_spec, tab_spec]
        args += list(rope_tabs)
        block_bytes += 4 * _nbytes((tm, LANES), F32)
    if gates:
        in_specs += [pl.BlockSpec(gate_w.shape, lambda i: (0, 0), pipeline_mode=_RESIDENT),
                     pl.BlockSpec((1, LANES), lambda i: (0, 0))]
        args += [gate_w, gate_b]
        block_bytes += _nbytes(gate_w.shape, BF16)

    out_shape, out_specs = [], []
    for _, w, _, row_cols, vt_cols in streams:
        out_shape.append(jax.ShapeDtypeStruct((T, row_cols), BF16))
        out_specs.append(pl.BlockSpec((tm, row_cols), lambda i: (i, 0)))
        block_bytes += 2 * _nbytes((tm, w.shape[1]), BF16)
        if vt_cols:
            n_groups = vt_cols // LANES
            out_shape.append(
                jax.ShapeDtypeStruct((n_batch, n_groups, n_seq_tiles, LANES, tm), BF16))
            out_specs.append(pl.BlockSpec(
                (1, n_groups, 1, LANES, tm),
                lambda i: (i // n_seq_tiles, 0, i % n_seq_tiles, 0, 0)))
    scratch = []
    if gates:
        out_shape.append(jax.ShapeDtypeStruct((T, LANES), F32))
        out_specs.append(pl.BlockSpec((tm, LANES), lambda i: (i, 0)))
        scratch.append(pltpu.VMEM((1, LANES), F32))

    cfg = tuple((s[2], s[3], s[4]) for s in streams)
    return pl.pallas_call(
        functools.partial(_proj_kernel, streams=cfg, gates=gates, n_seq_tiles=n_seq_tiles),
        grid=(T // tm,),
        in_specs=in_specs,
        out_specs=out_specs,
        out_shape=out_shape,
        scratch_shapes=scratch,
        compiler_params=_params(1, block_bytes),
        name="proj_gates" if gates else ("proj_rope" if any_rope else "proj"),
    )(*args)


ONES_ROWS = 16
OFF_UNROLL = 2


def _flash_scratch(n_maps, n_tiles, tile, dv):
    scores = pltpu.VMEM((n_maps, tile, tile), F32)
    row = pltpu.VMEM((n_maps, 1, tile), F32)
    return [scores, row, scores, row,
            pltpu.VMEM((n_tiles, n_maps, 1, tile), F32),
            pltpu.VMEM((n_tiles, n_maps, dv + ONES_ROWS, tile), F32)]


def _flash_scratch_bytes(n_maps, n_tiles, tile, dv):
    return (2 * _nbytes((n_maps, tile, tile), F32) + 2 * _nbytes((n_maps, 8, tile), F32)
            + _nbytes((n_tiles, n_maps, 8, tile), F32)
            + _nbytes((n_tiles, n_maps, dv + ONES_ROWS, tile), F32))


def _causal_flash(n_tiles, tile, q_ops_of, k_block, vt_block, scratch):
    buf_a, buf_b, m_all, acc_all = scratch[:2], scratch[2:4], scratch[4], scratch[5]
    n_off = n_tiles * (n_tiles - 1) // 2
    assert n_tiles % 2 == 0 and n_off % 2 == 0
    ones = jnp.ones((ONES_ROWS, tile), BF16)
    kk = lax.broadcasted_iota(jnp.int32, (tile, tile), 0)
    qq = lax.broadcasted_iota(jnp.int32, (tile, tile), 1)
    causal = kk <= qq

    def issue(qi, j, buf):
        s_ref, bm_ref = buf
        for i, (ka, qa) in enumerate(zip(k_block(j), q_ops_of(qi))):
            st = lax.dot_general(ka, qa, _NT, preferred_element_type=F32)
            s_ref[i] = st
            bm_ref[i] = jnp.max(st, axis=0, keepdims=True)

    def process_diag(buf, qi):
        s_ref, _ = buf
        for i, vt in enumerate(vt_block(qi)):
            st = jnp.where(causal, s_ref[i], -jnp.inf)
            m_new = jnp.max(st, axis=0, keepdims=True)
            p = jnp.exp2(st - m_new).astype(BF16)
            acc_all[qi, i] = jnp.dot(jnp.concatenate([vt, ones], axis=0), p,
                                     preferred_element_type=F32)
            m_all[qi, i] = m_new

    def process(buf, qi, j):
        s_ref, bm_ref = buf
        for i, vt in enumerate(vt_block(j)):
            m = m_all[qi, i]
            m_new = jnp.maximum(m, bm_ref[i])
            alpha = jnp.exp2(m - m_new)
            p = jnp.exp2(s_ref[i] - m_new).astype(BF16)
            pv = jnp.dot(jnp.concatenate([vt, ones], axis=0), p, preferred_element_type=F32)
            acc_all[qi, i] = alpha * acc_all[qi, i] + pv
            m_all[qi, i] = m_new

    issue(0, 0, buf_a)

    def diag_pair(u, carry):
        qi = 2 * u
        issue(qi + 1, qi + 1, buf_b)
        process_diag(buf_a, qi)
        more = qi + 2 < n_tiles
        issue(jnp.where(more, qi + 2, 1), jnp.where(more, qi + 2, 0), buf_a)
        process_diag(buf_b, qi + 1)
        return carry

    lax.fori_loop(0, n_tiles // 2, diag_pair, 0)

    def following(qi, j):
        wrap = j + 1 == qi
        return jnp.where(wrap, qi + 1, qi), jnp.where(wrap, 0, j + 1)

    def off_pair(qi, j):
        qi1, j1 = following(qi, j)
        issue(qi1, j1, buf_b)
        process(buf_a, qi, j)
        qi2, j2 = following(qi1, j1)
        past = qi2 >= n_tiles
        qi2, j2 = jnp.where(past, 1, qi2), jnp.where(past, 0, j2)
        issue(qi2, j2, buf_a)
        process(buf_b, qi1, j1)
        return qi2, j2

    def off_trip(u, carry):
        for _ in range(OFF_UNROLL):
            carry = off_pair(*carry)
        return carry

    assert n_off % (2 * OFF_UNROLL) == 0
    lax.fori_loop(0, n_off // (2 * OFF_UNROLL), off_trip, (jnp.int32(1), jnp.int32(0)))


def _fox_attn_kernel(q_ref, k_ref, vt_ref, c_ref, o_ref, qa_ref, ka_ref, *flash_scratch, tile):
    seq = q_ref.shape[1]
    n_tiles = seq // tile
    hp = pl.program_id(1)
    lane = lax.broadcasted_iota(jnp.int32, (tile, LANES), 1)
    d = HEAD_DIM

    def build(ci, carry):
        r0 = pl.multiple_of(ci * tile, tile)
        qp = q_ref[0, pl.ds(r0, tile), :].astype(F32)
        kp = k_ref[0, pl.ds(r0, tile), :].astype(F32)
        cc = c_ref[0, pl.ds(r0, tile), :] * LOG2E
        for h2 in range(2):
            head = 2 * hp + h2
            ch = jnp.sum(jnp.where(lane == head, cc, 0.0), axis=1, keepdims=True)
            hi, mid, lo = [p.astype(F32)
                           for p in _split_bf16(jnp.broadcast_to(ch, (tile, LANES)), 3)]
            if h2 == 0:
                qx, kx = qp, kp
            else:
                qx = pltpu.roll(qp, HEAD_DIM, 1)
                kx = pltpu.roll(kp, HEAD_DIM, 1)
            k_extra = jnp.where(lane == d, hi, jnp.where(lane == d + 1, mid, jnp.where(
                lane == d + 2, lo, jnp.where(lane < d + 6, 1.0, 0.0))))
            q_extra = jnp.where(lane < d + 3, -1.0, jnp.where(lane == d + 3, hi, jnp.where(
                lane == d + 4, mid, jnp.where(lane == d + 5, lo, 0.0))))
            ka_ref[h2, pl.ds(r0, tile), :] = jnp.where(lane < d, kx, k_extra).astype(BF16)
            qa_ref[h2, pl.ds(r0, tile), :] = jnp.where(lane < d, qx, q_extra).astype(BF16)
        return carry

    lax.fori_loop(0, n_tiles, build, 0)

    def q_ops_of(qi):
        q0 = pl.multiple_of(qi * tile, tile)
        return [qa_ref[h2, pl.ds(q0, tile), :] for h2 in range(2)]

    def k_block(j):
        start = pl.multiple_of(j * tile, tile)
        return [ka_ref[h2, pl.ds(start, tile), :] for h2 in range(2)]

    def vt_block(j):
        return [vt_ref[0, 0, j, h2 * d:(h2 + 1) * d, :] for h2 in range(2)]

    _causal_flash(n_tiles, tile, q_ops_of, k_block, vt_block, flash_scratch)
    acc_all = flash_scratch[5]

    def emit(qi, carry):
        a0 = acc_all[qi, 0]
        a1 = acc_all[qi, 1]
        yt = jnp.concatenate([a0[:d] / a0[d:d + 1], a1[:d] / a1[d:d + 1]], axis=0)
        o_ref[0, pl.ds(pl.multiple_of(qi * tile, tile), tile), :] = yt.T.astype(o_ref.dtype)
        return carry

    lax.fori_loop(0, n_tiles, emit, 0)


def _fox_attn(proj, vt, c, *, n_heads, k_col, out_width):
    B, S, _ = proj.shape
    tile = ATTN_TILE
    n_tiles = S // tile
    n_pairs = n_heads // 2
    seq_block = _nbytes((S, LANES), BF16)
    block_bytes = (2 * (4 * seq_block + _nbytes((S, LANES), F32)) + 4 * seq_block
                   + _flash_scratch_bytes(2, n_tiles, tile, HEAD_DIM))
    return pl.pallas_call(
        functools.partial(_fox_attn_kernel, tile=tile),
        grid=(B, n_pairs),
        in_specs=[
            pl.BlockSpec((1, S, LANES), lambda b, h: (b, 0, h)),
            pl.BlockSpec((1, S, LANES), lambda b, h: (b, 0, k_col + h)),
            pl.BlockSpec((1, 1, n_tiles, LANES, tile), lambda b, h: (b, h, 0, 0, 0)),
            pl.BlockSpec((1, S, LANES), lambda b, h: (b, 0, 0)),
        ],
        out_specs=pl.BlockSpec((1, S, LANES), lambda b, h: (b, 0, h)),
        out_shape=jax.ShapeDtypeStruct((B, S, out_width), BF16),
        scratch_shapes=[
            pltpu.VMEM((2, S, LANES), BF16),
            pltpu.VMEM((2, S, LANES), BF16),
        ] + _flash_scratch(2, n_tiles, tile, HEAD_DIM),
        compiler_params=_params(2, block_bytes),
        name="fox_attn",
    )(proj, proj, vt, c)


def _diff_attn_kernel(q_ref, k_ref, vt_ref, lq1_ref, lk1_ref, lq2_ref, lk2_ref, g_ref, o_ref,
                      *flash_scratch, tile, lambda_init):
    seq = q_ref.shape[1]
    n_tiles = seq // tile
    lane = lax.broadcasted_iota(jnp.int32, (tile, LANES), 1)
    lam = (jnp.exp(jnp.sum(lq1_ref[...] * lk1_ref[...], axis=1, keepdims=True))
           - jnp.exp(jnp.sum(lq2_ref[...] * lk2_ref[...], axis=1, keepdims=True)) + lambda_init)

    dv = 2 * HEAD_DIM

    def q_ops_of(qi):
        qp = q_ref[0, pl.ds(pl.multiple_of(qi * tile, tile), tile), :]
        zero = jnp.zeros_like(qp)
        return [jnp.where(lane < HEAD_DIM, qp, zero), jnp.where(lane >= HEAD_DIM, qp, zero)]

    def k_block(j):
        kb = k_ref[0, pl.ds(pl.multiple_of(j * tile, tile), tile), :]
        return [kb, kb]

    def vt_block(j):
        vb = vt_ref[0, 0, j]
        return [vb, vb]

    _causal_flash(n_tiles, tile, q_ops_of, k_block, vt_block, flash_scratch)
    acc_all = flash_scratch[5]

    def emit(qi, carry):
        a1 = acc_all[qi, 0]
        a2 = acc_all[qi, 1]
        yt = a1[:dv] / a1[dv:dv + 1] - lam * (a2[:dv] / a2[dv:dv + 1])
        y = _rms(yt.T, g_ref[...]) * (1.0 - lambda_init)
        o_ref[0, pl.ds(pl.multiple_of(qi * tile, tile), tile), :] = y.astype(o_ref.dtype)
        return carry

    lax.fori_loop(0, n_tiles, emit, 0)


def _diff_attn(qproj, k_sh, vt, lq1, lk1, lq2, lk2, subln_g, *, n_heads, lambda_init):
    B, S, _ = qproj.shape
    tile = ATTN_TILE
    n_tiles = S // tile
    vec = pl.BlockSpec((1, HEAD_DIM), lambda b, h: (0, 0))
    block_bytes = (2 * 4 * _nbytes((S, LANES), BF16)
                   + _flash_scratch_bytes(2, n_tiles, tile, 2 * HEAD_DIM))
    return pl.pallas_call(
        functools.partial(_diff_attn_kernel, tile=tile, lambda_init=lambda_init),
        grid=(B, n_heads),
        in_specs=[
            pl.BlockSpec((1, S, LANES), lambda b, h: (b, 0, h)),
            pl.BlockSpec((1, S, LANES), lambda b, h: (b, 0, h)),
            pl.BlockSpec((1, 1, n_tiles, LANES, tile), lambda b, h: (b, h, 0, 0, 0)),
            vec, vec, vec, vec,
            pl.BlockSpec((1, 2 * HEAD_DIM), lambda b, h: (0, 0)),
        ],
        out_specs=pl.BlockSpec((1, S, LANES), lambda b, h: (b, 0, h)),
        out_shape=jax.ShapeDtypeStruct((B, S, n_heads * 2 * HEAD_DIM), BF16),
        scratch_shapes=_flash_scratch(2, n_tiles, tile, 2 * HEAD_DIM),
        compiler_params=_params(2, block_bytes),
        name="diff_attn",
    )(qproj, k_sh, vt, lq1.reshape(1, -1), lk1.reshape(1, -1), lq2.reshape(1, -1),
      lk2.reshape(1, -1), subln_g.reshape(1, -1))


def _mix_out_kernel(x_ref, y_ref, mq_ref, mk_ref, mv_ref, wo_ref, o_ref):
    tm = x_ref.shape[0]
    mw = mq_ref.shape[1]
    y_width = y_ref.shape[1]
    mq = mq_ref[...]
    mk = mk_ref[0]
    mv = mv_ref[0]
    q_head = lax.broadcasted_iota(jnp.int32, (tm, mw), 1) // HEAD_DIM
    v_head = lax.broadcasted_iota(jnp.int32, mv.shape, 1) // HEAD_DIM
    ymem = jnp.zeros((tm, mw), F32)
    for h in range(mw // HEAD_DIM):
        qh = jnp.where(q_head == h, mq, jnp.zeros_like(mq))
        s = lax.dot_general(qh, mk, _NT, preferred_element_type=F32)
        p = jnp.exp2(s - jnp.max(s, axis=1, keepdims=True))
        l = jnp.sum(p, axis=1, keepdims=True)
        vh = jnp.where(v_head == h, mv, jnp.zeros_like(mv))
        ymem = ymem + jnp.dot(p.astype(BF16), vh, preferred_element_type=F32) / l
    acc = jnp.dot(y_ref[...], wo_ref[0:y_width, :], preferred_element_type=F32)
    acc = acc + jnp.dot(ymem.astype(BF16), wo_ref[y_width:y_width + mw, :],
                        preferred_element_type=F32)
    o_ref[...] = x_ref[...] + acc


def _mix_out(x2d, y2d, proj2d, mq_col, memkv, wo, *, seq):
    T, D = x2d.shape
    yw = y2d.shape[1]
    n_mem, mw2 = memkv.shape[1], memkv.shape[2]
    mw = mw2 // 2
    tm = PROJ_TM
    n_seq_tiles = seq // tm
    block_bytes = (4 * _nbytes((tm, D), F32) + 2 * _nbytes((tm, yw + mw), BF16)
                   + 4 * _nbytes((n_mem, mw), BF16) + _nbytes((yw + mw, D), BF16))
    return pl.pallas_call(
        _mix_out_kernel,
        grid=(T // tm,),
        in_specs=[
            pl.BlockSpec((tm, D), lambda i: (i, 0)),
            pl.BlockSpec((tm, yw), lambda i: (i, 0)),
            pl.BlockSpec((tm, mw), lambda i: (i, mq_col)),
            pl.BlockSpec((1, n_mem, mw), lambda i: (i // n_seq_tiles, 0, 0)),
            pl.BlockSpec((1, n_mem, mw), lambda i: (i // n_seq_tiles, 0, 1)),
            pl.BlockSpec((yw + mw, D), lambda i: (0, 0), pipeline_mode=_RESIDENT),
        ],
        out_specs=pl.BlockSpec((tm, D), lambda i: (i, 0)),
        out_shape=jax.ShapeDtypeStruct((T, D), F32),
        compiler_params=_params(1, block_bytes),
        name="mix_out",
    )(x2d, y2d, proj2d, memkv, memkv, wo)


def _ffn_kernel(*refs, final, n_chunks):
    if final:
        x_ref, g_ref, wgu_ref, wd_ref, fg_ref, o_ref, hn_ref, acc_ref = refs
    else:
        x_ref, g_ref, wgu_ref, wd_ref, o_ref, hn_ref, acc_ref = refs
    x = x_ref[...]
    hn_ref[...] = _rms(x, g_ref[...]).astype(BF16)
    acc_ref[...] = x

    d_ff = wd_ref.shape[0]
    tf = d_ff // n_chunks
    for k in range(n_chunks):
        h = hn_ref[...]
        gate = jnp.dot(h, wgu_ref[:, k * tf:(k + 1) * tf], preferred_element_type=F32)
        up = jnp.dot(h, wgu_ref[:, d_ff + k * tf:d_ff + (k + 1) * tf],
                     preferred_element_type=F32)
        a = gate * jax.nn.sigmoid(gate) * up
        acc_ref[...] += jnp.dot(a.astype(BF16), wd_ref[k * tf:(k + 1) * tf, :],
                                preferred_element_type=F32)
    out = acc_ref[...]
    if final:
        out = _rms(out, fg_ref[...])
    o_ref[...] = out


def _ffn(x2d, g, wgu, wd, final_g=None):
    T, D = x2d.shape
    d_ff = wd.shape[0]
    tm = FFN_TM
    assert T % tm == 0 and d_ff % FFN_TF == 0
    n_chunks = d_ff // FFN_TF
    final = final_g is not None
    in_specs = [
        pl.BlockSpec((tm, D), lambda i: (i, 0)),
        pl.BlockSpec((1, D), lambda i: (0, 0)),
        pl.BlockSpec(wgu.shape, lambda i: (0, 0), pipeline_mode=_RESIDENT),
        pl.BlockSpec(wd.shape, lambda i: (0, 0), pipeline_mode=_RESIDENT),
    ]
    args = [x2d, g.reshape(1, D), wgu, wd]
    if final:
        in_specs.append(pl.BlockSpec((1, D), lambda i: (0, 0)))
        args.append(final_g.reshape(1, D))
    block_bytes = (5 * _nbytes((tm, D), F32) + _nbytes((tm, D), BF16)
                   + _nbytes(wgu.shape, BF16) + _nbytes(wd.shape, BF16))
    return pl.pallas_call(
        functools.partial(_ffn_kernel, final=final, n_chunks=n_chunks),
        grid=(T // tm,),
        in_specs=in_specs,
        out_specs=pl.BlockSpec((tm, D), lambda i: (i, 0)),
        out_shape=jax.ShapeDtypeStruct((T, D), F32),
        scratch_shapes=[pltpu.VMEM((tm, D), BF16), pltpu.VMEM((tm, D), F32)],
        compiler_params=_params(1, block_bytes),
        name="ffn_final" if final else "ffn",
    )(*args)


def _rope_tables(seq):
    half = HEAD_DIM // 2
    inv_freq = jnp.power(ROPE_THETA, -jnp.arange(half, dtype=F32) * (2.0 / HEAD_DIM))
    ang = jnp.arange(seq, dtype=F32)[:, None] * inv_freq[None, :]
    cos, sin = jnp.cos(ang), jnp.sin(ang)
    reps = LANES // HEAD_DIM
    cos_t = jnp.tile(jnp.concatenate([cos, cos], axis=-1), (1, reps))
    sin_t = jnp.tile(jnp.concatenate([-sin, sin], axis=-1), (1, reps))
    return cos_t, sin_t


def kernel(x, mem, attn_norm_g, mem_norm_g, w_mem_kv, w_out, ffn_norm_g, w_gate_up, w_down,
           a_w_in, a_b_f, b_w_in, b_lambda_q1, b_lambda_k1, b_lambda_q2, b_lambda_k2,
           b_subln_g, kv_norm_g, w_kv_shared, final_norm_g):
    B, S, D = x.shape
    depth = attn_norm_g.shape[0]
    n_a = a_w_in.shape[0]
    n_mem = mem.shape[1]
    mem_w = w_mem_kv.shape[2] // 2
    n_fox = a_b_f.shape[1]
    fox_w = n_fox * HEAD_DIM
    diff_w = b_w_in.shape[1] - mem_w
    n_diff = diff_w // (2 * HEAD_DIM)
    scale = HEAD_DIM ** -0.5 * LOG2E
    T = B * S

    x2d = x.reshape(T, D)
    mem2d = mem.reshape(B * n_mem, D)
    rope_tabs = _rope_tables(S)
    k_sh = vt_sh = None

    for layer in range(depth):
        (memkv,) = _proj(mem2d, [(mem_norm_g[layer], w_mem_kv[layer].astype(BF16),
                                  0, 2 * mem_w, 0)], seq=n_mem)
        memkv = memkv.reshape(B, n_mem, 2 * mem_w)
        wo = w_out[layer].astype(BF16)
        if layer < n_a:
            w_in = a_w_in[layer]
            w_main = jnp.concatenate(
                [w_in[:, :fox_w] * scale, w_in[:, fox_w:2 * fox_w],
                 w_in[:, 3 * fox_w + n_fox:] * scale, w_in[:, 2 * fox_w:3 * fox_w]],
                axis=1).astype(BF16)
            gw = jnp.pad(w_in[:, 3 * fox_w:3 * fox_w + n_fox], ((0, 0), (0, LANES - n_fox)))
            gw_hi = gw.astype(BF16)
            gw_lo = (gw - gw_hi.astype(F32)).astype(BF16)
            gate_w = jnp.concatenate([gw_hi, gw_lo], axis=1)
            gate_b = jnp.pad(a_b_f[layer], (0, LANES - n_fox)).reshape(1, LANES)
            row_cols = 2 * fox_w + mem_w
            proj, vt, c = _proj(x2d, [(attn_norm_g[layer], w_main, 0, row_cols, fox_w)], seq=S,
                                gate_w=gate_w, gate_b=gate_b)
            y = _fox_attn(proj.reshape(B, S, row_cols), vt, c.reshape(B, S, LANES),
                          n_heads=n_fox, k_col=fox_w // LANES, out_width=fox_w)
            mq_col = 2 * fox_w // mem_w
        else:
            j = layer - n_a
            streams = [(attn_norm_g[layer], (b_w_in[j] * scale).astype(BF16),
                        diff_w, diff_w + mem_w, 0)]
            if layer == n_a:
                streams.append((kv_norm_g, w_kv_shared.astype(BF16), diff_w, diff_w,
                                w_kv_shared.shape[1] - diff_w))
                proj, k_sh, vt_sh = _proj(x2d, streams, seq=S, rope_tabs=rope_tabs)
                k_sh = k_sh.reshape(B, S, diff_w)
            else:
                (proj,) = _proj(x2d, streams, seq=S, rope_tabs=rope_tabs)
            lambda_init = 0.8 - 0.6 * math.exp(-0.3 * layer)
            y = _diff_attn(proj.reshape(B, S, -1), k_sh, vt_sh, b_lambda_q1[j], b_lambda_k1[j],
                           b_lambda_q2[j], b_lambda_k2[j], b_subln_g[j], n_heads=n_diff,
                           lambda_init=lambda_init)
            mq_col = diff_w // mem_w
        x2d = _mix_out(x2d, y.reshape(T, -1), proj, mq_col, memkv, wo, seq=S)
        last = layer == depth - 1
        x2d = _ffn(x2d, ffn_norm_g[layer], w_gate_up[layer].astype(BF16),
                   w_down[layer].astype(BF16), final_g=final_norm_g if last else None)
    return x2d.reshape(B, S, D)
```

```python
import functools
import math

import jax
import jax.numpy as jnp
from jax import lax
from jax.experimental import pallas as pl
from jax.experimental.pallas import tpu as pltpu

HEAD_DIM = 64
ROPE_THETA = 10000.0
NORM_EPS = 1e-6
LANES = 128
ATTN_TILE = 512
PROJ_TM = ATTN_TILE
FFN_TM = 1024
FFN_TF = 256
VMEM_CAP = 56 * 1024 * 1024
VMEM_TEMPS = 16 * 1024 * 1024
LOG2E = math.log2(math.e)

F32 = jnp.float32
BF16 = jnp.bfloat16
_NT = (((1,), (1,)), ((), ()))
_RESIDENT = pl.Buffered(1)


def _nbytes(shape, dtype):
    return math.prod(shape) * jnp.dtype(dtype).itemsize


def _params(n_grid, block_bytes):
    limit = min(VMEM_CAP, block_bytes + VMEM_TEMPS)
    return pltpu.CompilerParams(dimension_semantics=("arbitrary",) * n_grid,
                                vmem_limit_bytes=limit)


def _rms(x, g):
    ms = jnp.mean(x * x, axis=-1, keepdims=True)
    return x * lax.rsqrt(ms + NORM_EPS) * g


def _split_bf16(v, n):
    pieces = []
    for _ in range(n - 1):
        p = v.astype(BF16)
        pieces.append(p)
        v = v - p.astype(F32)
    pieces.append(v.astype(BF16))
    return pieces


def _proj_kernel(*refs, streams, n_gates, n_seq_tiles):
    gates = n_gates > 0
    it = iter(refs)
    x_ref = next(it)
    gw = [(next(it), next(it)) for _ in streams]
    any_rope = any(s[0] for s in streams)
    if any_rope:
        cos_ref, sin_ref = next(it), next(it)
    if gates:
        wf_ref, bf_ref = next(it), next(it)
    outs = []
    for (_, _, vt_cols) in streams:
        row_ref = next(it)
        outs.append((row_ref, next(it) if vt_cols else None))
    if gates:
        cp_ref, carry_ref = next(it), next(it)

    tm = x_ref.shape[0]
    x = x_ref[...]
    xn = x * lax.rsqrt(jnp.mean(x * x, axis=-1, keepdims=True) + NORM_EPS)
    if any_rope:
        cos = cos_ref[...]
        sin = sin_ref[...]
        lane = lax.broadcasted_iota(jnp.int32, (tm, LANES), 1)
        first_half = (lane % HEAD_DIM) < (HEAD_DIM // 2)

    for si, ((rope_cols, row_cols, vt_cols), (g_ref, w_ref), (row_ref, vt_ref)) in enumerate(
            zip(streams, gw, outs)):
        hn32 = xn * g_ref[...]
        hn = hn32.astype(BF16)
        res = jnp.dot(hn, w_ref[...], preferred_element_type=F32)
        for c0 in range(0, rope_cols, LANES):
            t = res[:, c0:c0 + LANES]
            up = pltpu.roll(t, HEAD_DIM // 2, 1)
            dn = pltpu.roll(t, LANES - HEAD_DIM // 2, 1)
            sw = jnp.where(first_half, dn, up)
            row_ref[:, c0:c0 + LANES] = (t * cos + sw * sin).astype(BF16)
        if row_cols > rope_cols:
            row_ref[:, rope_cols:row_cols] = res[:, rope_cols:row_cols].astype(BF16)
        for gi in range(vt_cols // LANES):
            c0 = row_cols + gi * LANES
            vt_ref[0, gi, 0] = res[:, c0:c0 + LANES].T.astype(BF16)

        if gates and si == 0:
            lo = (hn32 - hn.astype(F32)).astype(BF16)
            wf = wf_ref[...]
            zz = jnp.dot(hn, wf, preferred_element_type=F32)
            z = (zz[:, :LANES] + zz[:, LANES:]
                 + jnp.dot(lo, wf[:, :LANES], preferred_element_type=F32) + bf_ref[...])
            lf = jnp.minimum(z, 0.0) - jnp.log1p(jnp.exp(-jnp.abs(z)))
            l_hi, l_mid, l_lo = _split_bf16(lf, 3)
            row = lax.broadcasted_iota(jnp.int32, (tm, tm), 0)
            col = lax.broadcasted_iota(jnp.int32, (tm, tm), 1)
            tri = (row >= col).astype(BF16)
            cc = jnp.dot(tri, jnp.concatenate([l_hi, l_mid], axis=1), preferred_element_type=F32)
            cs = cc[:, :LANES] + cc[:, LANES:] + jnp.dot(tri, l_lo, preferred_element_type=F32)

            @pl.when((pl.program_id(0) % n_seq_tiles) == 0)
            def _():
                carry_ref[...] = jnp.zeros_like(carry_ref)

            c = cs + carry_ref[...]
            carry_ref[...] = c[tm - 1:tm, :]
            p_hi, p_mid, p_lo = [p.astype(F32) for p in _split_bf16(c * LOG2E, 3)]
            lane_c = lax.broadcasted_iota(jnp.int32, (tm, LANES), 1)
            packed = jnp.where(lane_c < n_gates, p_hi, jnp.where(
                lane_c < 2 * n_gates, pltpu.roll(p_mid, n_gates, 1),
                pltpu.roll(p_lo, 2 * n_gates, 1)))
            cp_ref[...] = packed.astype(BF16)


def _proj(x2d, streams, *, seq, rope_tabs=None, gate_w=None, gate_b=None, n_gates=0):
    T, D = x2d.shape
    tm = min(PROJ_TM, T)
    assert T % tm == 0
    gates = n_gates > 0
    assert 3 * n_gates <= LANES
    any_rope = any(s[2] for s in streams)
    any_vt = any(s[4] for s in streams)
    assert seq % tm == 0 or not (gates or any_rope or any_vt)
    n_seq_tiles = max(1, seq // tm)
    n_batch = T // seq

    in_specs = [pl.BlockSpec((tm, D), lambda i: (i, 0))]
    args = [x2d]
    block_bytes = 2 * _nbytes((tm, D), F32)
    for g, w, _, _, _ in streams:
        in_specs += [pl.BlockSpec((1, D), lambda i: (0, 0)),
                     pl.BlockSpec(w.shape, lambda i: (0, 0), pipeline_mode=_RESIDENT)]
        args += [g.reshape(1, D), w]
        block_bytes += _nbytes(w.shape, BF16) + _nbytes((tm, w.shape[1]), F32)
    if any_rope:
        tab_spec = pl.BlockSpec((tm, LANES), lambda i: (i % n_seq_tiles, 0))
        in_specs += [tab_spec, tab_spec]
        args += list(rope_tabs)
        block_bytes += 4 * _nbytes((tm, LANES), F32)
    if gates:
        in_specs += [pl.BlockSpec(gate_w.shape, lambda i: (0, 0), pipeline_mode=_RESIDENT),
                     pl.BlockSpec((1, LANES), lambda i: (0, 0))]
        args += [gate_w, gate_b]
        block_bytes += _nbytes(gate_w.shape, BF16)

    out_shape, out_specs = [], []
    for _, w, _, row_cols, vt_cols in streams:
        out_shape.append(jax.ShapeDtypeStruct((T, row_cols), BF16))
        out_specs.append(pl.BlockSpec((tm, row_cols), lambda i: (i, 0)))
        block_bytes += 2 * _nbytes((tm, w.shape[1]), BF16)
        if vt_cols:
            n_groups = vt_cols // LANES
            out_shape.append(
                jax.ShapeDtypeStruct((n_batch, n_groups, n_seq_tiles, LANES, tm), BF16))
            out_specs.append(pl.BlockSpec(
                (1, n_groups, 1, LANES, tm),
                lambda i: (i // n_seq_tiles, 0, i % n_seq_tiles, 0, 0)))
    scratch = []
    if gates:
        out_shape.append(jax.ShapeDtypeStruct((T, LANES), BF16))
        out_specs.append(pl.BlockSpec((tm, LANES), lambda i: (i, 0)))
        scratch.append(pltpu.VMEM((1, LANES), F32))

    cfg = tuple((s[2], s[3], s[4]) for s in streams)
    return pl.pallas_call(
        functools.partial(_proj_kernel, streams=cfg, n_gates=n_gates, n_seq_tiles=n_seq_tiles),
        grid=(T // tm,),
        in_specs=in_specs,
        out_specs=out_specs,
        out_shape=out_shape,
        scratch_shapes=scratch,
        compiler_params=_params(1, block_bytes),
        name="proj_gates" if gates else ("proj_rope" if any_rope else "proj"),
    )(*args)


ONES_ROWS = 16
OFF_UNROLL = 2


def _flash_scratch(n_maps, n_tiles, tile, dv):
    scores = pltpu.VMEM((n_maps, tile, tile), F32)
    row = pltpu.VMEM((n_maps, 1, tile), F32)
    return [scores, row, scores, row,
            pltpu.VMEM((n_tiles, n_maps, 1, tile), F32),
            pltpu.VMEM((n_tiles, n_maps, dv + ONES_ROWS, tile), F32)]


def _flash_scratch_bytes(n_maps, n_tiles, tile, dv):
    return (2 * _nbytes((n_maps, tile, tile), F32) + 2 * _nbytes((n_maps, 8, tile), F32)
            + _nbytes((n_tiles, n_maps, 8, tile), F32)
            + _nbytes((n_tiles, n_maps, dv + ONES_ROWS, tile), F32))


def _causal_flash(n_tiles, tile, q_ops_of, k_block, vt_block, scratch):
    buf_a, buf_b, m_all, acc_all = scratch[:2], scratch[2:4], scratch[4], scratch[5]
    n_off = n_tiles * (n_tiles - 1) // 2
    assert n_tiles % 2 == 0 and n_off % 2 == 0
    ones = jnp.ones((ONES_ROWS, tile), BF16)
    kk = lax.broadcasted_iota(jnp.int32, (tile, tile), 0)
    qq = lax.broadcasted_iota(jnp.int32, (tile, tile), 1)
    causal = kk <= qq

    def issue(qi, j, buf):
        s_ref, bm_ref = buf
        for i, (ka, qa) in enumerate(zip(k_block(j), q_ops_of(qi))):
            st = lax.dot_general(ka, qa, _NT, preferred_element_type=F32)
            s_ref[i] = st
            bm_ref[i] = jnp.max(st, axis=0, keepdims=True)

    def process_diag(buf, qi):
        s_ref, _ = buf
        for i, vt in enumerate(vt_block(qi)):
            st = jnp.where(causal, s_ref[i], -jnp.inf)
            m_new = jnp.max(st, axis=0, keepdims=True)
            p = jnp.exp2(st - m_new).astype(BF16)
            acc_all[qi, i] = jnp.dot(jnp.concatenate([vt, ones], axis=0), p,
                                     preferred_element_type=F32)
            m_all[qi, i] = m_new

    def process(buf, qi, j):
        s_ref, bm_ref = buf
        for i, vt in enumerate(vt_block(j)):
            m = m_all[qi, i]
            m_new = jnp.maximum(m, bm_ref[i])
            alpha = jnp.exp2(m - m_new)
            p = jnp.exp2(s_ref[i] - m_new).astype(BF16)
            pv = jnp.dot(jnp.concatenate([vt, ones], axis=0), p, preferred_element_type=F32)
            acc_all[qi, i] = alpha * acc_all[qi, i] + pv
            m_all[qi, i] = m_new

    issue(0, 0, buf_a)

    def diag_pair(u, carry):
        qi = 2 * u
        issue(qi + 1, qi + 1, buf_b)
        process_diag(buf_a, qi)
        more = qi + 2 < n_tiles
        issue(jnp.where(more, qi + 2, 1), jnp.where(more, qi + 2, 0), buf_a)
        process_diag(buf_b, qi + 1)
        return carry

    lax.fori_loop(0, n_tiles // 2, diag_pair, 0)

    def following(qi, j):
        wrap = j + 1 == qi
        return jnp.where(wrap, qi + 1, qi), jnp.where(wrap, 0, j + 1)

    def off_pair(qi, j):
        qi1, j1 = following(qi, j)
        issue(qi1, j1, buf_b)
        process(buf_a, qi, j)
        qi2, j2 = following(qi1, j1)
        past = qi2 >= n_tiles
        qi2, j2 = jnp.where(past, 1, qi2), jnp.where(past, 0, j2)
        issue(qi2, j2, buf_a)
        process(buf_b, qi1, j1)
        return qi2, j2

    def off_trip(u, carry):
        for _ in range(OFF_UNROLL):
            carry = off_pair(*carry)
        return carry

    assert n_off % (2 * OFF_UNROLL) == 0
    lax.fori_loop(0, n_off // (2 * OFF_UNROLL), off_trip, (jnp.int32(1), jnp.int32(0)))


def _fox_attn_kernel(q_ref, k_ref, vt_ref, cp_ref, o_ref, qa_ref, ka_ref, *flash_scratch, tile,
                     n_heads):
    seq = q_ref.shape[1]
    n_tiles = seq // tile
    hp = pl.program_id(1)
    lane = lax.broadcasted_iota(jnp.int32, (tile, LANES), 1)
    d = HEAD_DIM
    is_head = lane < d
    in_a = (lane == d) | (lane == d + n_heads) | (lane == d + 2 * n_heads)
    in_b = (lane == d + 1) | (lane == d + 1 + n_heads) | (lane == d + 1 + 2 * n_heads)

    def build(ci, carry):
        r0 = pl.multiple_of(ci * tile, tile)
        qp = q_ref[0, pl.ds(r0, tile), :].astype(F32)
        kp = k_ref[0, pl.ds(r0, tile), :].astype(F32)
        cp = cp_ref[0, pl.ds(r0, tile), :].astype(F32)
        for h2 in range(2):
            head = 2 * hp + h2
            if h2 == 0:
                qx, kx = qp, kp
            else:
                qx = pltpu.roll(qp, d, 1)
                kx = pltpu.roll(kp, d, 1)
            kc = pltpu.roll(cp, d - head, 1)
            qc = pltpu.roll(cp, d + 1 - head, 1)
            k_extra = jnp.where(in_a, kc, jnp.where(in_b, 1.0, 0.0))
            q_extra = jnp.where(in_b, qc, jnp.where(in_a, -1.0, 0.0))
            ka_ref[h2, pl.ds(r0, tile), :] = jnp.where(is_head, kx, k_extra).astype(BF16)
            qa_ref[h2, pl.ds(r0, tile), :] = jnp.where(is_head, qx, q_extra).astype(BF16)
        return carry

    lax.fori_loop(0, n_tiles, build, 0)

    def q_ops_of(qi):
        q0 = pl.multiple_of(qi * tile, tile)
        return [qa_ref[h2, pl.ds(q0, tile), :] for h2 in range(2)]

    def k_block(j):
        start = pl.multiple_of(j * tile, tile)
        return [ka_ref[h2, pl.ds(start, tile), :] for h2 in range(2)]

    def vt_block(j):
        return [vt_ref[0, 0, j, h2 * d:(h2 + 1) * d, :] for h2 in range(2)]

    _causal_flash(n_tiles, tile, q_ops_of, k_block, vt_block, flash_scratch)
    acc_all = flash_scratch[5]

    def emit(qi, carry):
        a0 = acc_all[qi, 0]
        a1 = acc_all[qi, 1]
        yt = jnp.concatenate([a0[:d] / a0[d:d + 1], a1[:d] / a1[d:d + 1]], axis=0)
        o_ref[0, pl.ds(pl.multiple_of(qi * tile, tile), tile), :] = yt.T.astype(o_ref.dtype)
        return carry

    lax.fori_loop(0, n_tiles, emit, 0)


def _fox_attn(proj, vt, cp, *, n_heads, k_col, out_width):
    assert HEAD_DIM + 2 + 2 * n_heads <= LANES
    B, S, _ = proj.shape
    tile = ATTN_TILE
    n_tiles = S // tile
    n_pairs = n_heads // 2
    seq_block = _nbytes((S, LANES), BF16)
    block_bytes = (2 * 5 * seq_block + 4 * seq_block
                   + _flash_scratch_bytes(2, n_tiles, tile, HEAD_DIM))
    return pl.pallas_call(
        functools.partial(_fox_attn_kernel, tile=tile, n_heads=n_heads),
        grid=(B, n_pairs),
        in_specs=[
            pl.BlockSpec((1, S, LANES), lambda b, h: (b, 0, h)),
            pl.BlockSpec((1, S, LANES), lambda b, h: (b, 0, k_col + h)),
            pl.BlockSpec((1, 1, n_tiles, LANES, tile), lambda b, h: (b, h, 0, 0, 0)),
            pl.BlockSpec((1, S, LANES), lambda b, h: (b, 0, 0)),
        ],
        out_specs=pl.BlockSpec((1, S, LANES), lambda b, h: (b, 0, h)),
        out_shape=jax.ShapeDtypeStruct((B, S, out_width), BF16),
        scratch_shapes=[
            pltpu.VMEM((2, S, LANES), BF16),
            pltpu.VMEM((2, S, LANES), BF16),
        ] + _flash_scratch(2, n_tiles, tile, HEAD_DIM),
        compiler_params=_params(2, block_bytes),
        name="fox_attn",
    )(proj, proj, vt, cp)


def _diff_attn_kernel(q_ref, k_ref, vt_ref, lq1_ref, lk1_ref, lq2_ref, lk2_ref, g_ref, o_ref,
                      *flash_scratch, tile, lambda_init):
    seq = q_ref.shape[1]
    n_tiles = seq // tile
    lane = lax.broadcasted_iota(jnp.int32, (tile, LANES), 1)
    lam = (jnp.exp(jnp.sum(lq1_ref[...] * lk1_ref[...], axis=1, keepdims=True))
           - jnp.exp(jnp.sum(lq2_ref[...] * lk2_ref[...], axis=1, keepdims=True)) + lambda_init)

    dv = 2 * HEAD_DIM

    def q_ops_of(qi):
        qp = q_ref[0, pl.ds(pl.multiple_of(qi * tile, tile), tile), :]
        zero = jnp.zeros_like(qp)
        return [jnp.where(lane < HEAD_DIM, qp, zero), jnp.where(lane >= HEAD_DIM, qp, zero)]

    def k_block(j):
        kb = k_ref[0, pl.ds(pl.multiple_of(j * tile, tile), tile), :]
        return [kb, kb]

    def vt_block(j):
        vb = vt_ref[0, 0, j]
        return [vb, vb]

    _causal_flash(n_tiles, tile, q_ops_of, k_block, vt_block, flash_scratch)
    acc_all = flash_scratch[5]

    def emit(qi, carry):
        a1 = acc_all[qi, 0]
        a2 = acc_all[qi, 1]
        yt = a1[:dv] / a1[dv:dv + 1] - lam * (a2[:dv] / a2[dv:dv + 1])
        ms = jnp.mean(yt * yt, axis=0, keepdims=True)
        y = (yt * lax.rsqrt(ms + NORM_EPS)).T * g_ref[...] * (1.0 - lambda_init)
        o_ref[0, pl.ds(pl.multiple_of(qi * tile, tile), tile), :] = y.astype(o_ref.dtype)
        return carry

    lax.fori_loop(0, n_tiles, emit, 0)


def _diff_attn(qproj, k_sh, vt, lq1, lk1, lq2, lk2, subln_g, *, n_heads, lambda_init):
    B, S, _ = qproj.shape
    tile = ATTN_TILE
    n_tiles = S // tile
    vec = pl.BlockSpec((1, HEAD_DIM), lambda b, h: (0, 0))
    block_bytes = (2 * 4 * _nbytes((S, LANES), BF16)
                   + _flash_scratch_bytes(2, n_tiles, tile, 2 * HEAD_DIM))
    return pl.pallas_call(
        functools.partial(_diff_attn_kernel, tile=tile, lambda_init=lambda_init),
        grid=(B, n_heads),
        in_specs=[
            pl.BlockSpec((1, S, LANES), lambda b, h: (b, 0, h)),
            pl.BlockSpec((1, S, LANES), lambda b, h: (b, 0, h)),
            pl.BlockSpec((1, 1, n_tiles, LANES, tile), lambda b, h: (b, h, 0, 0, 0)),
            vec, vec, vec, vec,
            pl.BlockSpec((1, 2 * HEAD_DIM), lambda b, h: (0, 0)),
        ],
        out_specs=pl.BlockSpec((1, S, LANES), lambda b, h: (b, 0, h)),
        out_shape=jax.ShapeDtypeStruct((B, S, n_heads * 2 * HEAD_DIM), BF16),
        scratch_shapes=_flash_scratch(2, n_tiles, tile, 2 * HEAD_DIM),
        compiler_params=_params(2, block_bytes),
        name="diff_attn",
    )(qproj, k_sh, vt, lq1.reshape(1, -1), lk1.reshape(1, -1), lq2.reshape(1, -1),
      lk2.reshape(1, -1), subln_g.reshape(1, -1))


def _mix_out(x, y_ref, mq_ref, mk_ref, mv_ref, wo_ref):
    tm = x.shape[0]
    mw = mq_ref.shape[1]
    y_width = y_ref.shape[1]
    mq = mq_ref[...]
    mk = mk_ref[0]
    mv = mv_ref[0]
    q_head = lax.broadcasted_iota(jnp.int32, (tm, mw), 1) // HEAD_DIM
    v_head = lax.broadcasted_iota(jnp.int32, mv.shape, 1) // HEAD_DIM
    ymem = jnp.zeros((tm, mw), F32)
    for h in range(mw // HEAD_DIM):
        qh = jnp.where(q_head == h, mq, jnp.zeros_like(mq))
        s = lax.dot_general(qh, mk, _NT, preferred_element_type=F32)
        p = jnp.exp2(s - jnp.max(s, axis=1, keepdims=True))
        l = jnp.sum(p, axis=1, keepdims=True)
        vh = jnp.where(v_head == h, mv, jnp.zeros_like(mv))
        ymem = ymem + jnp.dot(p.astype(BF16), vh, preferred_element_type=F32) / l
    acc = jnp.dot(y_ref[...], wo_ref[0:y_width, :], preferred_element_type=F32)
    acc = acc + jnp.dot(ymem.astype(BF16), wo_ref[y_width:y_width + mw, :],
                        preferred_element_type=F32)
    return x + acc


def _post_kernel(*refs, final, n_chunks):
    x_ref, y_ref, mq_ref, mk_ref, mv_ref, wo_ref, g_ref, wgu_ref, wd_ref = refs[:9]
    if final:
        fg_ref, o_ref, hn_ref, acc_ref = refs[9:]
    else:
        o_ref, hn_ref, acc_ref = refs[9:]
    x = _mix_out(x_ref[...], y_ref, mq_ref, mk_ref, mv_ref, wo_ref)
    hn_ref[...] = _rms(x, g_ref[...]).astype(BF16)
    acc_ref[...] = x

    d_ff = wd_ref.shape[0]
    tf = d_ff // n_chunks
    for k in range(n_chunks):
        h = hn_ref[...]
        gate = jnp.dot(h, wgu_ref[:, k * tf:(k + 1) * tf], preferred_element_type=F32)
        up = jnp.dot(h, wgu_ref[:, d_ff + k * tf:d_ff + (k + 1) * tf],
                     preferred_element_type=F32)
        a = gate * jax.nn.sigmoid(gate) * up
        acc_ref[...] += jnp.dot(a.astype(BF16), wd_ref[k * tf:(k + 1) * tf, :],
                                preferred_element_type=F32)
    out = acc_ref[...]
    if final:
        out = _rms(out, fg_ref[...])
    o_ref[...] = out


def _post(x2d, y2d, proj2d, mq_col, memkv, wo, g, wgu, wd, *, seq, final_g=None):
    T, D = x2d.shape
    yw = y2d.shape[1]
    n_mem, mw = memkv.shape[1], memkv.shape[2] // 2
    d_ff = wd.shape[0]
    tm = FFN_TM
    assert T % tm == 0 and seq % tm == 0 and d_ff % FFN_TF == 0
    n_chunks = d_ff // FFN_TF
    n_seq_tiles = seq // tm
    final = final_g is not None
    in_specs = [
        pl.BlockSpec((tm, D), lambda i: (i, 0)),
        pl.BlockSpec((tm, yw), lambda i: (i, 0)),
        pl.BlockSpec((tm, mw), lambda i: (i, mq_col)),
        pl.BlockSpec((1, n_mem, mw), lambda i: (i // n_seq_tiles, 0, 0)),
        pl.BlockSpec((1, n_mem, mw), lambda i: (i // n_seq_tiles, 0, 1)),
        pl.BlockSpec((yw + mw, D), lambda i: (0, 0), pipeline_mode=_RESIDENT),
        pl.BlockSpec((1, D), lambda i: (0, 0)),
        pl.BlockSpec(wgu.shape, lambda i: (0, 0), pipeline_mode=_RESIDENT),
        pl.BlockSpec(wd.shape, lambda i: (0, 0), pipeline_mode=_RESIDENT),
    ]
    args = [x2d, y2d, proj2d, memkv, memkv, wo, g.reshape(1, D), wgu, wd]
    if final:
        in_specs.append(pl.BlockSpec((1, D), lambda i: (0, 0)))
        args.append(final_g.reshape(1, D))
    block_bytes = (5 * _nbytes((tm, D), F32) + _nbytes((tm, D), BF16)
                   + 2 * _nbytes((tm, yw + mw), BF16) + 4 * _nbytes((n_mem, mw), BF16)
                   + _nbytes(wo.shape, BF16) + _nbytes(wgu.shape, BF16) + _nbytes(wd.shape, BF16))
    return pl.pallas_call(
        functools.partial(_post_kernel, final=final, n_chunks=n_chunks),
        grid=(T // tm,),
        in_specs=in_specs,
        out_specs=pl.BlockSpec((tm, D), lambda i: (i, 0)),
        out_shape=jax.ShapeDtypeStruct((T, D), F32),
        scratch_shapes=[pltpu.VMEM((tm, D), BF16), pltpu.VMEM((tm, D), F32)],
        compiler_params=_params(1, block_bytes),
        name="post_final" if final else "post",
    )(*args)


def _rope_tables(seq):
    half = HEAD_DIM // 2
    inv_freq = jnp.power(ROPE_THETA, -jnp.arange(half, dtype=F32) * (2.0 / HEAD_DIM))
    ang = jnp.arange(seq, dtype=F32)[:, None] * inv_freq[None, :]
    cos, sin = jnp.cos(ang), jnp.sin(ang)
    reps = LANES // HEAD_DIM
    cos_t = jnp.tile(jnp.concatenate([cos, cos], axis=-1), (1, reps))
    sin_t = jnp.tile(jnp.concatenate([-sin, sin], axis=-1), (1, reps))
    return cos_t, sin_t


def kernel(x, mem, attn_norm_g, mem_norm_g, w_mem_kv, w_out, ffn_norm_g, w_gate_up, w_down,
           a_w_in, a_b_f, b_w_in, b_lambda_q1, b_lambda_k1, b_lambda_q2, b_lambda_k2,
           b_subln_g, kv_norm_g, w_kv_shared, final_norm_g):
    B, S, D = x.shape
    depth = attn_norm_g.shape[0]
    n_a = a_w_in.shape[0]
    n_mem = mem.shape[1]
    mem_w = w_mem_kv.shape[2] // 2
    n_fox = a_b_f.shape[1]
    fox_w = n_fox * HEAD_DIM
    diff_w = b_w_in.shape[1] - mem_w
    n_diff = diff_w // (2 * HEAD_DIM)
    scale = HEAD_DIM ** -0.5 * LOG2E
    T = B * S

    x2d = x.reshape(T, D)
    mem2d = mem.reshape(B * n_mem, D)
    rope_tabs = _rope_tables(S)
    k_sh = vt_sh = None

    for layer in range(depth):
        (memkv,) = _proj(mem2d, [(mem_norm_g[layer], w_mem_kv[layer].astype(BF16),
                                  0, 2 * mem_w, 0)], seq=n_mem)
        memkv = memkv.reshape(B, n_mem, 2 * mem_w)
        wo = w_out[layer].astype(BF16)
        if layer < n_a:
            w_in = a_w_in[layer]
            w_main = jnp.concatenate(
                [w_in[:, :fox_w] * scale, w_in[:, fox_w:2 * fox_w],
                 w_in[:, 3 * fox_w + n_fox:] * scale, w_in[:, 2 * fox_w:3 * fox_w]],
                axis=1).astype(BF16)
            gw = jnp.pad(w_in[:, 3 * fox_w:3 * fox_w + n_fox], ((0, 0), (0, LANES - n_fox)))
            gw_hi = gw.astype(BF16)
            gw_lo = (gw - gw_hi.astype(F32)).astype(BF16)
            gate_w = jnp.concatenate([gw_hi, gw_lo], axis=1)
            gate_b = jnp.pad(a_b_f[layer], (0, LANES - n_fox)).reshape(1, LANES)
            row_cols = 2 * fox_w + mem_w
            proj, vt, cp = _proj(x2d, [(attn_norm_g[layer], w_main, 0, row_cols, fox_w)], seq=S,
                                 gate_w=gate_w, gate_b=gate_b, n_gates=n_fox)
            y = _fox_attn(proj.reshape(B, S, row_cols), vt, cp.reshape(B, S, LANES),
                          n_heads=n_fox, k_col=fox_w // LANES, out_width=fox_w)
            mq_col = 2 * fox_w // mem_w
        else:
            j = layer - n_a
            streams = [(attn_norm_g[layer], (b_w_in[j] * scale).astype(BF16),
                        diff_w, diff_w + mem_w, 0)]
            if layer == n_a:
                streams.append((kv_norm_g, w_kv_shared.astype(BF16), diff_w, diff_w,
                                w_kv_shared.shape[1] - diff_w))
                proj, k_sh, vt_sh = _proj(x2d, streams, seq=S, rope_tabs=rope_tabs)
                k_sh = k_sh.reshape(B, S, diff_w)
            else:
                (proj,) = _proj(x2d, streams, seq=S, rope_tabs=rope_tabs)
            lambda_init = 0.8 - 0.6 * math.exp(-0.3 * layer)
            y = _diff_attn(proj.reshape(B, S, -1), k_sh, vt_sh, b_lambda_q1[j], b_lambda_k1[j],
                           b_lambda_q2[j], b_lambda_k2[j], b_subln_g[j], n_heads=n_diff,
                           lambda_init=lambda_init)
            mq_col = diff_w // mem_w
        last = layer == depth - 1
        x2d = _post(x2d, y.reshape(T, -1), proj, mq_col, memkv, wo, ffn_norm_g[layer],
                    w_gate_up[layer].astype(BF16), w_down[layer].astype(BF16), seq=S,
                    final_g=final_norm_g if last else None)
    return x2d.reshape(B, S, D)
```

```python
import functools
import math

import jax
import jax.numpy as jnp
from jax import lax
from jax.experimental import pallas as pl
from jax.experimental.pallas import tpu as pltpu

HEAD_DIM = 64
ROPE_THETA = 10000.0
NORM_EPS = 1e-6
LANES = 128
ATTN_TILE = 512
PROJ_TM = ATTN_TILE
FFN_TM = 1024
FFN_TF = 256
VMEM_CAP = 56 * 1024 * 1024
VMEM_TEMPS = 16 * 1024 * 1024
LOG2E = math.log2(math.e)

F32 = jnp.float32
BF16 = jnp.bfloat16
_NT = (((1,), (1,)), ((), ()))
_RESIDENT = pl.Buffered(1)


def _nbytes(shape, dtype):
    return math.prod(shape) * jnp.dtype(dtype).itemsize


def _params(n_grid, block_bytes):
    limit = min(VMEM_CAP, block_bytes + VMEM_TEMPS)
    return pltpu.CompilerParams(dimension_semantics=("arbitrary",) * n_grid,
                                vmem_limit_bytes=limit)


def _rms(x, g):
    ms = jnp.mean(x * x, axis=-1, keepdims=True)
    return x * lax.rsqrt(ms + NORM_EPS) * g


def _split_bf16(v, n):
    pieces = []
    for _ in range(n - 1):
        p = v.astype(BF16)
        pieces.append(p)
        v = v - p.astype(F32)
    pieces.append(v.astype(BF16))
    return pieces


def _proj_kernel(*refs, streams, n_gates, n_seq_tiles):
    gates = n_gates > 0
    it = iter(refs)
    x_ref = next(it)
    gw = [(next(it), next(it)) for _ in streams]
    any_rope = any(s[0] for s in streams)
    if any_rope:
        cos_ref, sin_ref = next(it), next(it)
    if gates:
        wf_ref, bf_ref = next(it), next(it)
    outs = []
    for (_, _, vt_cols) in streams:
        row_ref = next(it)
        outs.append((row_ref, next(it) if vt_cols else None))
    if gates:
        cp_ref, carry_ref = next(it), next(it)

    tm = x_ref.shape[0]
    x = x_ref[...]
    xn = x * lax.rsqrt(jnp.mean(x * x, axis=-1, keepdims=True) + NORM_EPS)
    if any_rope:
        cos = cos_ref[...]
        sin = sin_ref[...]
        lane = lax.broadcasted_iota(jnp.int32, (tm, LANES), 1)
        first_half = (lane % HEAD_DIM) < (HEAD_DIM // 2)

    for si, ((rope_cols, row_cols, vt_cols), (g_ref, w_ref), (row_ref, vt_ref)) in enumerate(
            zip(streams, gw, outs)):
        hn32 = xn * g_ref[...]
        hn = hn32.astype(BF16)
        res = jnp.dot(hn, w_ref[...], preferred_element_type=F32)
        for c0 in range(0, rope_cols, LANES):
            t = res[:, c0:c0 + LANES]
            up = pltpu.roll(t, HEAD_DIM // 2, 1)
            dn = pltpu.roll(t, LANES - HEAD_DIM // 2, 1)
            sw = jnp.where(first_half, dn, up)
            row_ref[:, c0:c0 + LANES] = (t * cos + sw * sin).astype(BF16)
        if row_cols > rope_cols:
            row_ref[:, rope_cols:row_cols] = res[:, rope_cols:row_cols].astype(BF16)
        for gi in range(vt_cols // LANES):
            c0 = row_cols + gi * LANES
            vt_ref[0, gi, 0] = res[:, c0:c0 + LANES].T.astype(BF16)

        if gates and si == 0:
            lo = (hn32 - hn.astype(F32)).astype(BF16)
            wf = wf_ref[...]
            zz = jnp.dot(hn, wf, preferred_element_type=F32)
            z = (zz[:, :LANES] + zz[:, LANES:]
                 + jnp.dot(lo, wf[:, :LANES], preferred_element_type=F32) + bf_ref[...])
            lf = jnp.minimum(z, 0.0) - jnp.log1p(jnp.exp(-jnp.abs(z)))
            l_hi, l_mid, l_lo = _split_bf16(lf, 3)
            row = lax.broadcasted_iota(jnp.int32, (tm, tm), 0)
            col = lax.broadcasted_iota(jnp.int32, (tm, tm), 1)
            tri = (row >= col).astype(BF16)
            cc = jnp.dot(tri, jnp.concatenate([l_hi, l_mid], axis=1), preferred_element_type=F32)
            cs = cc[:, :LANES] + cc[:, LANES:] + jnp.dot(tri, l_lo, preferred_element_type=F32)

            @pl.when((pl.program_id(0) % n_seq_tiles) == 0)
            def _():
                carry_ref[...] = jnp.zeros_like(carry_ref)

            c = cs + carry_ref[...]
            carry_ref[...] = c[tm - 1:tm, :]
            p_hi, p_mid, p_lo = [p.astype(F32) for p in _split_bf16(c * LOG2E, 3)]
            lane_c = lax.broadcasted_iota(jnp.int32, (tm, LANES), 1)
            packed = jnp.where(lane_c < n_gates, p_hi, jnp.where(
                lane_c < 2 * n_gates, pltpu.roll(p_mid, n_gates, 1),
                pltpu.roll(p_lo, 2 * n_gates, 1)))
            cp_ref[...] = packed.astype(BF16)


def _proj(x2d, streams, *, seq, rope_tabs=None, gate_w=None, gate_b=None, n_gates=0):
    T, D = x2d.shape
    tm = min(PROJ_TM, T)
    assert T % tm == 0
    gates = n_gates > 0
    assert 3 * n_gates <= LANES
    any_rope = any(s[2] for s in streams)
    any_vt = any(s[4] for s in streams)
    assert seq % tm == 0 or not (gates or any_rope or any_vt)
    n_seq_tiles = max(1, seq // tm)
    n_batch = T // seq

    in_specs = [pl.BlockSpec((tm, D), lambda i: (i, 0))]
    args = [x2d]
    block_bytes = 2 * _nbytes((tm, D), F32)
    for g, w, _, _, _ in streams:
        in_specs += [pl.BlockSpec((1, D), lambda i: (0, 0)),
                     pl.BlockSpec(w.shape, lambda i: (0, 0), pipeline_mode=_RESIDENT)]
        args += [g.reshape(1, D), w]
        block_bytes += _nbytes(w.shape, BF16) + _nbytes((tm, w.shape[1]), F32)
    if any_rope:
        tab_spec = pl.BlockSpec((tm, LANES), lambda i: (i % n_seq_tiles, 0))
        in_specs += [tab_spec, tab_spec]
        args += list(rope_tabs)
        block_bytes += 4 * _nbytes((tm, LANES), F32)
    if gates:
        in_specs += [pl.BlockSpec(gate_w.shape, lambda i: (0, 0), pipeline_mode=_RESIDENT),
                     pl.BlockSpec((1, LANES), lambda i: (0, 0))]
        args += [gate_w, gate_b]
        block_bytes += _nbytes(gate_w.shape, BF16)

    out_shape, out_specs = [], []
    for _, w, _, row_cols, vt_cols in streams:
        out_shape.append(jax.ShapeDtypeStruct((T, row_cols), BF16))
        out_specs.append(pl.BlockSpec((tm, row_cols), lambda i: (i, 0)))
        block_bytes += 2 * _nbytes((tm, w.shape[1]), BF16)
        if vt_cols:
            n_groups = vt_cols // LANES
            out_shape.append(
                jax.ShapeDtypeStruct((n_batch, n_groups, n_seq_tiles, LANES, tm), BF16))
            out_specs.append(pl.BlockSpec(
                (1, n_groups, 1, LANES, tm),
                lambda i: (i // n_seq_tiles, 0, i % n_seq_tiles, 0, 0)))
    scratch = []
    if gates:
        out_shape.append(jax.ShapeDtypeStruct((T, LANES), BF16))
        out_specs.append(pl.BlockSpec((tm, LANES), lambda i: (i, 0)))
        scratch.append(pltpu.VMEM((1, LANES), F32))

    cfg = tuple((s[2], s[3], s[4]) for s in streams)
    return pl.pallas_call(
        functools.partial(_proj_kernel, streams=cfg, n_gates=n_gates, n_seq_tiles=n_seq_tiles),
        grid=(T // tm,),
        in_specs=in_specs,
        out_specs=out_specs,
        out_shape=out_shape,
        scratch_shapes=scratch,
        compiler_params=_params(1, block_bytes),
        name="proj_gates" if gates else ("proj_rope" if any_rope else "proj"),
    )(*args)


ONES_ROWS = 16
OFF_UNROLL = 7


def _flash_scratch(n_maps, n_tiles, tile, dv):
    scores = pltpu.VMEM((n_maps, tile, tile), F32)
    row = pltpu.VMEM((n_maps, 1, tile), F32)
    return [scores, row, scores, row,
            pltpu.VMEM((n_tiles, n_maps, 1, tile), F32),
            pltpu.VMEM((n_tiles, n_maps, dv + ONES_ROWS, tile), F32)]


def _flash_scratch_bytes(n_maps, n_tiles, tile, dv):
    return (2 * _nbytes((n_maps, tile, tile), F32) + 2 * _nbytes((n_maps, 8, tile), F32)
            + _nbytes((n_tiles, n_maps, 8, tile), F32)
            + _nbytes((n_tiles, n_maps, dv + ONES_ROWS, tile), F32))


def _causal_flash(n_tiles, tile, q_ops_of, k_block, vt_block, scratch):
    buf_a, buf_b, m_all, acc_all = scratch[:2], scratch[2:4], scratch[4], scratch[5]
    n_off = n_tiles * (n_tiles - 1) // 2
    assert n_tiles % 2 == 0 and n_off % 2 == 0
    ones = jnp.ones((ONES_ROWS, tile), BF16)
    kk = lax.broadcasted_iota(jnp.int32, (tile, tile), 0)
    qq = lax.broadcasted_iota(jnp.int32, (tile, tile), 1)
    causal = kk <= qq

    def issue(qi, j, buf):
        s_ref, bm_ref = buf
        for i, (ka, qa) in enumerate(zip(k_block(j), q_ops_of(qi))):
            st = lax.dot_general(ka, qa, _NT, preferred_element_type=F32)
            s_ref[i] = st
            bm_ref[i] = jnp.max(st, axis=0, keepdims=True)

    def process_diag(buf, qi):
        s_ref, _ = buf
        for i, vt in enumerate(vt_block(qi)):
            st = jnp.where(causal, s_ref[i], -jnp.inf)
            m_new = jnp.max(st, axis=0, keepdims=True)
            p = jnp.exp2(st - m_new).astype(BF16)
            acc_all[qi, i] = jnp.dot(jnp.concatenate([vt, ones], axis=0), p,
                                     preferred_element_type=F32)
            m_all[qi, i] = m_new

    def process(buf, qi, j):
        s_ref, bm_ref = buf
        for i, vt in enumerate(vt_block(j)):
            m = m_all[qi, i]
            m_new = jnp.maximum(m, bm_ref[i])
            alpha = jnp.exp2(m - m_new)
            p = jnp.exp2(s_ref[i] - m_new).astype(BF16)
            pv = jnp.dot(jnp.concatenate([vt, ones], axis=0), p, preferred_element_type=F32)
            acc_all[qi, i] = alpha * acc_all[qi, i] + pv
            m_all[qi, i] = m_new

    issue(0, 0, buf_a)

    for qi in range(0, n_tiles, 2):
        issue(qi + 1, qi + 1, buf_b)
        process_diag(buf_a, qi)
        nxt = (qi + 2, qi + 2) if qi + 2 < n_tiles else (1, 0)
        issue(*nxt, buf_a)
        process_diag(buf_b, qi + 1)

    def following(qi, j):
        wrap = j + 1 == qi
        return jnp.where(wrap, qi + 1, qi), jnp.where(wrap, 0, j + 1)

    def off_pair(qi, j):
        qi1, j1 = following(qi, j)
        issue(qi1, j1, buf_b)
        process(buf_a, qi, j)
        qi2, j2 = following(qi1, j1)
        past = qi2 >= n_tiles
        qi2, j2 = jnp.where(past, 1, qi2), jnp.where(past, 0, j2)
        issue(qi2, j2, buf_a)
        process(buf_b, qi1, j1)
        return qi2, j2

    def off_trip(u, carry):
        for _ in range(OFF_UNROLL):
            carry = off_pair(*carry)
        return carry

    assert n_off % (2 * OFF_UNROLL) == 0
    lax.fori_loop(0, n_off // (2 * OFF_UNROLL), off_trip, (jnp.int32(1), jnp.int32(0)))


def _fox_attn_kernel(q_ref, k_ref, vt_ref, cp_ref, o_ref, qa_ref, ka_ref, *flash_scratch, tile,
                     n_heads):
    seq = q_ref.shape[1]
    n_tiles = seq // tile
    hp = pl.program_id(1)
    lane = lax.broadcasted_iota(jnp.int32, (tile, LANES), 1)
    d = HEAD_DIM
    is_head = lane < d
    in_a = (lane == d) | (lane == d + n_heads) | (lane == d + 2 * n_heads)
    in_b = (lane == d + 1) | (lane == d + 1 + n_heads) | (lane == d + 1 + 2 * n_heads)

    def build(ci, carry):
        r0 = pl.multiple_of(ci * tile, tile)
        qp = q_ref[0, pl.ds(r0, tile), :].astype(F32)
        kp = k_ref[0, pl.ds(r0, tile), :].astype(F32)
        cp = cp_ref[0, pl.ds(r0, tile), :].astype(F32)
        for h2 in range(2):
            head = 2 * hp + h2
            if h2 == 0:
                qx, kx = qp, kp
            else:
                qx = pltpu.roll(qp, d, 1)
                kx = pltpu.roll(kp, d, 1)
            kc = pltpu.roll(cp, d - head, 1)
            qc = pltpu.roll(cp, d + 1 - head, 1)
            k_extra = jnp.where(in_a, kc, jnp.where(in_b, 1.0, 0.0))
            q_extra = jnp.where(in_b, qc, jnp.where(in_a, -1.0, 0.0))
            ka_ref[h2, pl.ds(r0, tile), :] = jnp.where(is_head, kx, k_extra).astype(BF16)
            qa_ref[h2, pl.ds(r0, tile), :] = jnp.where(is_head, qx, q_extra).astype(BF16)
        return carry

    lax.fori_loop(0, n_tiles, build, 0)

    def q_ops_of(qi):
        q0 = pl.multiple_of(qi * tile, tile)
        return [qa_ref[h2, pl.ds(q0, tile), :] for h2 in range(2)]

    def k_block(j):
        start = pl.multiple_of(j * tile, tile)
        return [ka_ref[h2, pl.ds(start, tile), :] for h2 in range(2)]

    def vt_block(j):
        return [vt_ref[0, 0, j, h2 * d:(h2 + 1) * d, :] for h2 in range(2)]

    _causal_flash(n_tiles, tile, q_ops_of, k_block, vt_block, flash_scratch)
    acc_all = flash_scratch[5]

    def emit(qi, carry):
        a0 = acc_all[qi, 0]
        a1 = acc_all[qi, 1]
        yt = jnp.concatenate([a0[:d] / a0[d:d + 1], a1[:d] / a1[d:d + 1]], axis=0)
        o_ref[0, pl.ds(pl.multiple_of(qi * tile, tile), tile), :] = yt.T.astype(o_ref.dtype)
        return carry

    lax.fori_loop(0, n_tiles, emit, 0)


def _fox_attn(proj, vt, cp, *, n_heads, k_col, out_width):
    assert HEAD_DIM + 2 + 2 * n_heads <= LANES
    B, S, _ = proj.shape
    tile = ATTN_TILE
    n_tiles = S // tile
    n_pairs = n_heads // 2
    seq_block = _nbytes((S, LANES), BF16)
    block_bytes = (2 * 5 * seq_block + 4 * seq_block
                   + _flash_scratch_bytes(2, n_tiles, tile, HEAD_DIM))
    return pl.pallas_call(
        functools.partial(_fox_attn_kernel, tile=tile, n_heads=n_heads),
        grid=(B, n_pairs),
        in_specs=[
            pl.BlockSpec((1, S, LANES), lambda b, h: (b, 0, h)),
            pl.BlockSpec((1, S, LANES), lambda b, h: (b, 0, k_col + h)),
            pl.BlockSpec((1, 1, n_tiles, LANES, tile), lambda b, h: (b, h, 0, 0, 0)),
            pl.BlockSpec((1, S, LANES), lambda b, h: (b, 0, 0)),
        ],
        out_specs=pl.BlockSpec((1, S, LANES), lambda b, h: (b, 0, h)),
        out_shape=jax.ShapeDtypeStruct((B, S, out_width), BF16),
        scratch_shapes=[
            pltpu.VMEM((2, S, LANES), BF16),
            pltpu.VMEM((2, S, LANES), BF16),
        ] + _flash_scratch(2, n_tiles, tile, HEAD_DIM),
        compiler_params=_params(2, block_bytes),
        name="fox_attn",
    )(proj, proj, vt, cp)


def _diff_attn_kernel(q_ref, k_ref, vt_ref, lq1_ref, lk1_ref, lq2_ref, lk2_ref, g_ref, o_ref,
                      *flash_scratch, tile, lambda_init):
    seq = q_ref.shape[1]
    n_tiles = seq // tile
    lane = lax.broadcasted_iota(jnp.int32, (tile, LANES), 1)
    lam = (jnp.exp(jnp.sum(lq1_ref[...] * lk1_ref[...], axis=1, keepdims=True))
           - jnp.exp(jnp.sum(lq2_ref[...] * lk2_ref[...], axis=1, keepdims=True)) + lambda_init)

    dv = 2 * HEAD_DIM

    def q_ops_of(qi):
        qp = q_ref[0, pl.ds(pl.multiple_of(qi * tile, tile), tile), :]
        zero = jnp.zeros_like(qp)
        return [jnp.where(lane < HEAD_DIM, qp, zero), jnp.where(lane >= HEAD_DIM, qp, zero)]

    def k_block(j):
        kb = k_ref[0, pl.ds(pl.multiple_of(j * tile, tile), tile), :]
        return [kb, kb]

    def vt_block(j):
        vb = vt_ref[0, 0, j]
        return [vb, vb]

    _causal_flash(n_tiles, tile, q_ops_of, k_block, vt_block, flash_scratch)
    acc_all = flash_scratch[5]

    def emit(qi, carry):
        a1 = acc_all[qi, 0]
        a2 = acc_all[qi, 1]
        yt = a1[:dv] / a1[dv:dv + 1] - lam * (a2[:dv] / a2[dv:dv + 1])
        ms = jnp.mean(yt * yt, axis=0, keepdims=True)
        y = (yt * lax.rsqrt(ms + NORM_EPS)).T * g_ref[...] * (1.0 - lambda_init)
        o_ref[0, pl.ds(pl.multiple_of(qi * tile, tile), tile), :] = y.astype(o_ref.dtype)
        return carry

    lax.fori_loop(0, n_tiles, emit, 0)


def _diff_attn(qproj, k_sh, vt, lq1, lk1, lq2, lk2, subln_g, *, n_heads, lambda_init):
    B, S, _ = qproj.shape
    tile = ATTN_TILE
    n_tiles = S // tile
    vec = pl.BlockSpec((1, HEAD_DIM), lambda b, h: (0, 0))
    block_bytes = (2 * 4 * _nbytes((S, LANES), BF16)
                   + _flash_scratch_bytes(2, n_tiles, tile, 2 * HEAD_DIM))
    return pl.pallas_call(
        functools.partial(_diff_attn_kernel, tile=tile, lambda_init=lambda_init),
        grid=(B, n_heads),
        in_specs=[
            pl.BlockSpec((1, S, LANES), lambda b, h: (b, 0, h)),
            pl.BlockSpec((1, S, LANES), lambda b, h: (b, 0, h)),
            pl.BlockSpec((1, 1, n_tiles, LANES, tile), lambda b, h: (b, h, 0, 0, 0)),
            vec, vec, vec, vec,
            pl.BlockSpec((1, 2 * HEAD_DIM), lambda b, h: (0, 0)),
        ],
        out_specs=pl.BlockSpec((1, S, LANES), lambda b, h: (b, 0, h)),
        out_shape=jax.ShapeDtypeStruct((B, S, n_heads * 2 * HEAD_DIM), BF16),
        scratch_shapes=_flash_scratch(2, n_tiles, tile, 2 * HEAD_DIM),
        compiler_params=_params(2, block_bytes),
        name="diff_attn",
    )(qproj, k_sh, vt, lq1.reshape(1, -1), lk1.reshape(1, -1), lq2.reshape(1, -1),
      lk2.reshape(1, -1), subln_g.reshape(1, -1))


def _mix_out(x, y_ref, mq_ref, mk_ref, mv_ref, wo_ref):
    tm = x.shape[0]
    mw = mq_ref.shape[1]
    y_width = y_ref.shape[1]
    mq = mq_ref[...]
    mk = mk_ref[0]
    mv = mv_ref[0]
    q_head = lax.broadcasted_iota(jnp.int32, (tm, mw), 1) // HEAD_DIM
    v_head = lax.broadcasted_iota(jnp.int32, mv.shape, 1) // HEAD_DIM
    ymem = jnp.zeros((tm, mw), F32)
    for h in range(mw // HEAD_DIM):
        qh = jnp.where(q_head == h, mq, jnp.zeros_like(mq))
        s = lax.dot_general(qh, mk, _NT, preferred_element_type=F32)
        p = jnp.exp2(s - jnp.max(s, axis=1, keepdims=True))
        l = jnp.sum(p, axis=1, keepdims=True)
        vh = jnp.where(v_head == h, mv, jnp.zeros_like(mv))
        ymem = ymem + jnp.dot(p.astype(BF16), vh, preferred_element_type=F32) / l
    acc = jnp.dot(y_ref[...], wo_ref[0:y_width, :], preferred_element_type=F32)
    acc = acc + jnp.dot(ymem.astype(BF16), wo_ref[y_width:y_width + mw, :],
                        preferred_element_type=F32)
    return x + acc


def _post_kernel(*refs, final, n_chunks):
    x_ref, y_ref, mq_ref, mk_ref, mv_ref, wo_ref, g_ref, wgu_ref, wd_ref = refs[:9]
    if final:
        fg_ref, o_ref, hn_ref, acc_ref = refs[9:]
    else:
        o_ref, hn_ref, acc_ref = refs[9:]
    x = _mix_out(x_ref[...], y_ref, mq_ref, mk_ref, mv_ref, wo_ref)
    hn_ref[...] = _rms(x, g_ref[...]).astype(BF16)
    acc_ref[...] = x

    d_ff = wd_ref.shape[0]
    tf = d_ff // n_chunks
    for k in range(n_chunks):
        h = hn_ref[...]
        gate = jnp.dot(h, wgu_ref[:, k * tf:(k + 1) * tf], preferred_element_type=F32)
        up = jnp.dot(h, wgu_ref[:, d_ff + k * tf:d_ff + (k + 1) * tf],
                     preferred_element_type=F32)
        a = gate * jax.nn.sigmoid(gate) * up
        acc_ref[...] += jnp.dot(a.astype(BF16), wd_ref[k * tf:(k + 1) * tf, :],
                                preferred_element_type=F32)
    out = acc_ref[...]
    if final:
        out = _rms(out, fg_ref[...])
    o_ref[...] = out


def _post(x2d, y2d, proj2d, mq_col, memkv, wo, g, wgu, wd, *, seq, final_g=None):
    T, D = x2d.shape
    yw = y2d.shape[1]
    n_mem, mw = memkv.shape[1], memkv.shape[2] // 2
    d_ff = wd.shape[0]
    tm = FFN_TM
    assert T % tm == 0 and seq % tm == 0 and d_ff % FFN_TF == 0
    n_chunks = d_ff // FFN_TF
    n_seq_tiles = seq // tm
    final = final_g is not None
    in_specs = [
        pl.BlockSpec((tm, D), lambda i: (i, 0)),
        pl.BlockSpec((tm, yw), lambda i: (i, 0)),
        pl.BlockSpec((tm, mw), lambda i: (i, mq_col)),
        pl.BlockSpec((1, n_mem, mw), lambda i: (i // n_seq_tiles, 0, 0)),
        pl.BlockSpec((1, n_mem, mw), lambda i: (i // n_seq_tiles, 0, 1)),
        pl.BlockSpec((yw + mw, D), lambda i: (0, 0), pipeline_mode=_RESIDENT),
        pl.BlockSpec((1, D), lambda i: (0, 0)),
        pl.BlockSpec(wgu.shape, lambda i: (0, 0), pipeline_mode=_RESIDENT),
        pl.BlockSpec(wd.shape, lambda i: (0, 0), pipeline_mode=_RESIDENT),
    ]
    args = [x2d, y2d, proj2d, memkv, memkv, wo, g.reshape(1, D), wgu, wd]
    if final:
        in_specs.append(pl.BlockSpec((1, D), lambda i: (0, 0)))
        args.append(final_g.reshape(1, D))
    block_bytes = (5 * _nbytes((tm, D), F32) + _nbytes((tm, D), BF16)
                   + 2 * _nbytes((tm, yw + mw), BF16) + 4 * _nbytes((n_mem, mw), BF16)
                   + _nbytes(wo.shape, BF16) + _nbytes(wgu.shape, BF16) + _nbytes(wd.shape, BF16))
    return pl.pallas_call(
        functools.partial(_post_kernel, final=final, n_chunks=n_chunks),
        grid=(T // tm,),
        in_specs=in_specs,
        out_specs=pl.BlockSpec((tm, D), lambda i: (i, 0)),
        out_shape=jax.ShapeDtypeStruct((T, D), F32),
        scratch_shapes=[pltpu.VMEM((tm, D), BF16), pltpu.VMEM((tm, D), F32)],
        compiler_params=_params(1, block_bytes),
        name="post_final" if final else "post",
    )(*args)


def _rope_tables(seq):
    half = HEAD_DIM // 2
    inv_freq = jnp.power(ROPE_THETA, -jnp.arange(half, dtype=F32) * (2.0 / HEAD_DIM))
    ang = jnp.arange(seq, dtype=F32)[:, None] * inv_freq[None, :]
    cos, sin = jnp.cos(ang), jnp.sin(ang)
    reps = LANES // HEAD_DIM
    cos_t = jnp.tile(jnp.concatenate([cos, cos], axis=-1), (1, reps))
    sin_t = jnp.tile(jnp.concatenate([-sin, sin], axis=-1), (1, reps))
    return cos_t, sin_t


def kernel(x, mem, attn_norm_g, mem_norm_g, w_mem_kv, w_out, ffn_norm_g, w_gate_up, w_down,
           a_w_in, a_b_f, b_w_in, b_lambda_q1, b_lambda_k1, b_lambda_q2, b_lambda_k2,
           b_subln_g, kv_norm_g, w_kv_shared, final_norm_g):
    B, S, D = x.shape
    depth = attn_norm_g.shape[0]
    n_a = a_w_in.shape[0]
    n_mem = mem.shape[1]
    mem_w = w_mem_kv.shape[2] // 2
    n_fox = a_b_f.shape[1]
    fox_w = n_fox * HEAD_DIM
    diff_w = b_w_in.shape[1] - mem_w
    n_diff = diff_w // (2 * HEAD_DIM)
    scale = HEAD_DIM ** -0.5 * LOG2E
    T = B * S

    x2d = x.reshape(T, D)
    mem2d = mem.reshape(B * n_mem, D)
    rope_tabs = _rope_tables(S)
    k_sh = vt_sh = None

    for layer in range(depth):
        (memkv,) = _proj(mem2d, [(mem_norm_g[layer], w_mem_kv[layer].astype(BF16),
                                  0, 2 * mem_w, 0)], seq=n_mem)
        memkv = memkv.reshape(B, n_mem, 2 * mem_w)
        wo = w_out[layer].astype(BF16)
        if layer < n_a:
            w_in = a_w_in[layer]
            w_main = jnp.concatenate(
                [w_in[:, :fox_w] * scale, w_in[:, fox_w:2 * fox_w],
                 w_in[:, 3 * fox_w + n_fox:] * scale, w_in[:, 2 * fox_w:3 * fox_w]],
                axis=1).astype(BF16)
            gw = jnp.pad(w_in[:, 3 * fox_w:3 * fox_w + n_fox], ((0, 0), (0, LANES - n_fox)))
            gw_hi = gw.astype(BF16)
            gw_lo = (gw - gw_hi.astype(F32)).astype(BF16)
            gate_w = jnp.concatenate([gw_hi, gw_lo], axis=1)
            gate_b = jnp.pad(a_b_f[layer], (0, LANES - n_fox)).reshape(1, LANES)
            row_cols = 2 * fox_w + mem_w
            proj, vt, cp = _proj(x2d, [(attn_norm_g[layer], w_main, 0, row_cols, fox_w)], seq=S,
                                 gate_w=gate_w, gate_b=gate_b, n_gates=n_fox)
            y = _fox_attn(proj.reshape(B, S, row_cols), vt, cp.reshape(B, S, LANES),
                          n_heads=n_fox, k_col=fox_w // LANES, out_width=fox_w)
            mq_col = 2 * fox_w // mem_w
        else:
            j = layer - n_a
            streams = [(attn_norm_g[layer], (b_w_in[j] * scale).astype(BF16),
                        diff_w, diff_w + mem_w, 0)]
            if layer == n_a:
                streams.append((kv_norm_g, w_kv_shared.astype(BF16), diff_w, diff_w,
                                w_kv_shared.shape[1] - diff_w))
                proj, k_sh, vt_sh = _proj(x2d, streams, seq=S, rope_tabs=rope_tabs)
                k_sh = k_sh.reshape(B, S, diff_w)
            else:
                (proj,) = _proj(x2d, streams, seq=S, rope_tabs=rope_tabs)
            lambda_init = 0.8 - 0.6 * math.exp(-0.3 * layer)
            y = _diff_attn(proj.reshape(B, S, -1), k_sh, vt_sh, b_lambda_q1[j], b_lambda_k1[j],
                           b_lambda_q2[j], b_lambda_k2[j], b_subln_g[j], n_heads=n_diff,
                           lambda_init=lambda_init)
            mq_col = diff_w // mem_w
        last = layer == depth - 1
        x2d = _post(x2d, y.reshape(T, -1), proj, mq_col, memkv, wo, ffn_norm_g[layer],
                    w_gate_up[layer].astype(BF16), w_down[layer].astype(BF16), seq=S,
                    final_g=final_norm_g if last else None)
    return x2d.reshape(B, S, D)
```

```python
import functools
import math

import jax
import jax.numpy as jnp
from jax import lax
from jax.experimental import pallas as pl
from jax.experimental.pallas import tpu as pltpu

HEAD_DIM = 64
ROPE_THETA = 10000.0
NORM_EPS = 1e-6
LANES = 128
ATTN_TILE = 512
PROJ_TM = ATTN_TILE
FFN_TM = 1024
FFN_TF = 256
VMEM_CAP = 56 * 1024 * 1024
VMEM_TEMPS = 16 * 1024 * 1024
LOG2E = math.log2(math.e)

F32 = jnp.float32
BF16 = jnp.bfloat16
_NT = (((1,), (1,)), ((), ()))
_RESIDENT = pl.Buffered(1)


def _nbytes(shape, dtype):
    return math.prod(shape) * jnp.dtype(dtype).itemsize


def _params(n_grid, block_bytes):
    limit = min(VMEM_CAP, block_bytes + VMEM_TEMPS)
    return pltpu.CompilerParams(dimension_semantics=("arbitrary",) * n_grid,
                                vmem_limit_bytes=limit)


def _rms(x, g):
    ms = jnp.mean(x * x, axis=-1, keepdims=True)
    return x * lax.rsqrt(ms + NORM_EPS) * g


def _split_bf16(v, n):
    pieces = []
    for _ in range(n - 1):
        p = v.astype(BF16)
        pieces.append(p)
        v = v - p.astype(F32)
    pieces.append(v.astype(BF16))
    return pieces


def _proj_kernel(*refs, streams, n_gates, n_seq_tiles):
    gates = n_gates > 0
    it = iter(refs)
    x_ref = next(it)
    gw = [(next(it), next(it)) for _ in streams]
    any_rope = any(s[0] for s in streams)
    if any_rope:
        cos_ref, sin_ref = next(it), next(it)
    if gates:
        wf_ref, bf_ref = next(it), next(it)
    outs = []
    for (_, _, vt_cols) in streams:
        row_ref = next(it)
        outs.append((row_ref, next(it) if vt_cols else None))
    if gates:
        cp_ref, carry_ref = next(it), next(it)

    tm = x_ref.shape[0]
    x = x_ref[...]
    xn = x * lax.rsqrt(jnp.mean(x * x, axis=-1, keepdims=True) + NORM_EPS)
    if any_rope:
        cos = cos_ref[...]
        sin = sin_ref[...]
        lane = lax.broadcasted_iota(jnp.int32, (tm, LANES), 1)
        first_half = (lane % HEAD_DIM) < (HEAD_DIM // 2)

    for si, ((rope_cols, row_cols, vt_cols), (g_ref, w_ref), (row_ref, vt_ref)) in enumerate(
            zip(streams, gw, outs)):
        hn32 = xn * g_ref[...]
        hn = hn32.astype(BF16)
        res = jnp.dot(hn, w_ref[...], preferred_element_type=F32)
        for c0 in range(0, rope_cols, LANES):
            t = res[:, c0:c0 + LANES]
            up = pltpu.roll(t, HEAD_DIM // 2, 1)
            dn = pltpu.roll(t, LANES - HEAD_DIM // 2, 1)
            sw = jnp.where(first_half, dn, up)
            row_ref[:, c0:c0 + LANES] = (t * cos + sw * sin).astype(BF16)
        if row_cols > rope_cols:
            row_ref[:, rope_cols:row_cols] = res[:, rope_cols:row_cols].astype(BF16)
        for gi in range(vt_cols // LANES):
            c0 = row_cols + gi * LANES
            vt_ref[0, gi, 0] = res[:, c0:c0 + LANES].T.astype(BF16)

        if gates and si == 0:
            lo = (hn32 - hn.astype(F32)).astype(BF16)
            wf = wf_ref[...]
            zz = jnp.dot(hn, wf, preferred_element_type=F32)
            z = (zz[:, :LANES] + zz[:, LANES:]
                 + jnp.dot(lo, wf[:, :LANES], preferred_element_type=F32) + bf_ref[...])
            lf = jnp.minimum(z, 0.0) - jnp.log1p(jnp.exp(-jnp.abs(z)))
            l_hi, l_mid, l_lo = _split_bf16(lf, 3)
            row = lax.broadcasted_iota(jnp.int32, (tm, tm), 0)
            col = lax.broadcasted_iota(jnp.int32, (tm, tm), 1)
            tri = (row >= col).astype(BF16)
            cc = jnp.dot(tri, jnp.concatenate([l_hi, l_mid], axis=1), preferred_element_type=F32)
            cs = cc[:, :LANES] + cc[:, LANES:] + jnp.dot(tri, l_lo, preferred_element_type=F32)

            @pl.when((pl.program_id(0) % n_seq_tiles) == 0)
            def _():
                carry_ref[...] = jnp.zeros_like(carry_ref)

            c = cs + carry_ref[...]
            carry_ref[...] = c[tm - 1:tm, :]
            p_hi, p_mid, p_lo = [p.astype(F32) for p in _split_bf16(c * LOG2E, 3)]
            lane_c = lax.broadcasted_iota(jnp.int32, (tm, LANES), 1)
            packed = jnp.where(lane_c < n_gates, p_hi, jnp.where(
                lane_c < 2 * n_gates, pltpu.roll(p_mid, n_gates, 1),
                pltpu.roll(p_lo, 2 * n_gates, 1)))
            cp_ref[...] = packed.astype(BF16)


def _proj(x2d, streams, *, seq, rope_tabs=None, gate_w=None, gate_b=None, n_gates=0):
    T, D = x2d.shape
    tm = min(PROJ_TM, T)
    assert T % tm == 0
    gates = n_gates > 0
    assert 3 * n_gates <= LANES
    any_rope = any(s[2] for s in streams)
    any_vt = any(s[4] for s in streams)
    assert seq % tm == 0 or not (gates or any_rope or any_vt)
    n_seq_tiles = max(1, seq // tm)
    n_batch = T // seq

    in_specs = [pl.BlockSpec((tm, D), lambda i: (i, 0))]
    args = [x2d]
    block_bytes = 2 * _nbytes((tm, D), F32)
    for g, w, _, _, _ in streams:
        in_specs += [pl.BlockSpec((1, D), lambda i: (0, 0)),
                     pl.BlockSpec(w.shape, lambda i: (0, 0), pipeline_mode=_RESIDENT)]
        args += [g.reshape(1, D), w]
        block_bytes += _nbytes(w.shape, BF16) + _nbytes((tm, w.shape[1]), F32)
    if any_rope:
        tab_spec = pl.BlockSpec((tm, LANES), lambda i: (i % n_seq_tiles, 0))
        in_specs += [tab_spec, tab_spec]
        args += list(rope_tabs)
        block_bytes += 4 * _nbytes((tm, LANES), F32)
    if gates:
        in_specs += [pl.BlockSpec(gate_w.shape, lambda i: (0, 0), pipeline_mode=_RESIDENT),
                     pl.BlockSpec((1, LANES), lambda i: (0, 0))]
        args += [gate_w, gate_b]
        block_bytes += _nbytes(gate_w.shape, BF16)

    out_shape, out_specs = [], []
    for _, w, _, row_cols, vt_cols in streams:
        out_shape.append(jax.ShapeDtypeStruct((T, row_cols), BF16))
        out_specs.append(pl.BlockSpec((tm, row_cols), lambda i: (i, 0)))
        block_bytes += 2 * _nbytes((tm, w.shape[1]), BF16)
        if vt_cols:
            n_groups = vt_cols // LANES
            out_shape.append(
                jax.ShapeDtypeStruct((n_batch, n_groups, n_seq_tiles, LANES, tm), BF16))
            out_specs.append(pl.BlockSpec(
                (1, n_groups, 1, LANES, tm),
                lambda i: (i // n_seq_tiles, 0, i % n_seq_tiles, 0, 0)))
    scratch = []
    if gates:
        out_shape.append(jax.ShapeDtypeStruct((T, LANES), BF16))
        out_specs.append(pl.BlockSpec((tm, LANES), lambda i: (i, 0)))
        scratch.append(pltpu.VMEM((1, LANES), F32))

    cfg = tuple((s[2], s[3], s[4]) for s in streams)
    return pl.pallas_call(
        functools.partial(_proj_kernel, streams=cfg, n_gates=n_gates, n_seq_tiles=n_seq_tiles),
        grid=(T // tm,),
        in_specs=in_specs,
        out_specs=out_specs,
        out_shape=out_shape,
        scratch_shapes=scratch,
        compiler_params=_params(1, block_bytes),
        name="proj_gates" if gates else ("proj_rope" if any_rope else "proj"),
    )(*args)


ONES_ROWS = 16
OFF_UNROLL = 7


def _flash_scratch(n_maps, n_tiles, tile, dv):
    scores = pltpu.VMEM((n_maps, tile, tile), F32)
    row = pltpu.VMEM((n_maps, 1, tile), F32)
    return [scores, row, scores, row,
            pltpu.VMEM((n_tiles, n_maps, 1, tile), F32),
            pltpu.VMEM((n_tiles, n_maps, dv + ONES_ROWS, tile), F32)]


def _flash_scratch_bytes(n_maps, n_tiles, tile, dv):
    return (2 * _nbytes((n_maps, tile, tile), F32) + 2 * _nbytes((n_maps, 8, tile), F32)
            + _nbytes((n_tiles, n_maps, 8, tile), F32)
            + _nbytes((n_tiles, n_maps, dv + ONES_ROWS, tile), F32))


def _causal_flash(n_tiles, tile, q_ops_of, k_block, vt_block, scratch):
    buf_a, buf_b, m_all, acc_all = scratch[:2], scratch[2:4], scratch[4], scratch[5]
    n_off = n_tiles * (n_tiles - 1) // 2
    assert n_tiles % 2 == 0 and n_off % 2 == 0
    ones = jnp.ones((ONES_ROWS, tile), BF16)
    kk = lax.broadcasted_iota(jnp.int32, (tile, tile), 0)
    qq = lax.broadcasted_iota(jnp.int32, (tile, tile), 1)
    causal = kk <= qq

    def issue(qi, j, buf):
        s_ref, bm_ref = buf
        for i, (ka, qa) in enumerate(zip(k_block(j), q_ops_of(qi))):
            st = lax.dot_general(ka, qa, _NT, preferred_element_type=F32)
            s_ref[i] = st
            bm_ref[i] = jnp.max(st, axis=0, keepdims=True)

    half = tile // 2

    def issue_diag(qi, buf):
        s_ref, _ = buf
        for i, (ka, qa) in enumerate(zip(k_block(qi), q_ops_of(qi))):
            s_ref[i, :half, :half] = lax.dot_general(ka[:half], qa[:half], _NT,
                                                     preferred_element_type=F32)
            s_ref[i, :, half:] = lax.dot_general(ka, qa[half:], _NT, preferred_element_type=F32)

    def process_diag(buf, qi):
        s_ref, _ = buf
        for i, vt in enumerate(vt_block(qi)):
            vta = jnp.concatenate([vt, ones], axis=0)
            for rows, cols in ((slice(0, half), slice(0, half)), (slice(0, tile), slice(half, tile))):
                st = jnp.where(causal[rows, cols], s_ref[i, rows, cols], -jnp.inf)
                m_new = jnp.max(st, axis=0, keepdims=True)
                p = jnp.exp2(st - m_new).astype(BF16)
                acc_all[qi, i, :, cols] = jnp.dot(vta[:, rows], p, preferred_element_type=F32)
                m_all[qi, i, :, cols] = m_new

    def process(buf, qi, j):
        s_ref, bm_ref = buf
        for i, vt in enumerate(vt_block(j)):
            m = m_all[qi, i]
            m_new = jnp.maximum(m, bm_ref[i])
            alpha = jnp.exp2(m - m_new)
            p = jnp.exp2(s_ref[i] - m_new).astype(BF16)
            pv = jnp.dot(jnp.concatenate([vt, ones], axis=0), p, preferred_element_type=F32)
            acc_all[qi, i] = alpha * acc_all[qi, i] + pv
            m_all[qi, i] = m_new

    issue_diag(0, buf_a)

    for qi in range(0, n_tiles, 2):
        issue_diag(qi + 1, buf_b)
        process_diag(buf_a, qi)
        if qi + 2 < n_tiles:
            issue_diag(qi + 2, buf_a)
        else:
            issue(1, 0, buf_a)
        process_diag(buf_b, qi + 1)

    def following(qi, j):
        wrap = j + 1 == qi
        return jnp.where(wrap, qi + 1, qi), jnp.where(wrap, 0, j + 1)

    def off_pair(qi, j):
        qi1, j1 = following(qi, j)
        issue(qi1, j1, buf_b)
        process(buf_a, qi, j)
        qi2, j2 = following(qi1, j1)
        past = qi2 >= n_tiles
        qi2, j2 = jnp.where(past, 1, qi2), jnp.where(past, 0, j2)
        issue(qi2, j2, buf_a)
        process(buf_b, qi1, j1)
        return qi2, j2

    def off_trip(u, carry):
        for _ in range(OFF_UNROLL):
            carry = off_pair(*carry)
        return carry

    assert n_off % (2 * OFF_UNROLL) == 0
    lax.fori_loop(0, n_off // (2 * OFF_UNROLL), off_trip, (jnp.int32(1), jnp.int32(0)))


def _fox_attn_kernel(q_ref, k_ref, vt_ref, cp_ref, o_ref, qa_ref, ka_ref, *flash_scratch, tile,
                     n_heads):
    seq = q_ref.shape[1]
    n_tiles = seq // tile
    hp = pl.program_id(1)
    lane = lax.broadcasted_iota(jnp.int32, (tile, LANES), 1)
    d = HEAD_DIM
    is_head = lane < d
    in_a = (lane == d) | (lane == d + n_heads) | (lane == d + 2 * n_heads)
    in_b = (lane == d + 1) | (lane == d + 1 + n_heads) | (lane == d + 1 + 2 * n_heads)
    k_const = jnp.where(in_b, 1.0, 0.0)
    q_const = jnp.where(in_a, -1.0, 0.0)

    def build(ci, carry):
        r0 = pl.multiple_of(ci * tile, tile)
        qp = q_ref[0, pl.ds(r0, tile), :].astype(F32)
        kp = k_ref[0, pl.ds(r0, tile), :].astype(F32)
        cp = cp_ref[0, pl.ds(r0, tile), :].astype(F32)
        for h2 in range(2):
            head = 2 * hp + h2
            if h2 == 0:
                qx, kx = qp, kp
            else:
                qx = pltpu.roll(qp, d, 1)
                kx = pltpu.roll(kp, d, 1)
            kc = pltpu.roll(cp, d - head, 1)
            qc = pltpu.roll(cp, d + 1 - head, 1)
            k_extra = jnp.where(in_a, kc, k_const)
            q_extra = jnp.where(in_b, qc, q_const)
            ka_ref[h2, pl.ds(r0, tile), :] = jnp.where(is_head, kx, k_extra).astype(BF16)
            qa_ref[h2, pl.ds(r0, tile), :] = jnp.where(is_head, qx, q_extra).astype(BF16)
        return carry

    lax.fori_loop(0, n_tiles, build, 0)

    def q_ops_of(qi):
        q0 = pl.multiple_of(qi * tile, tile)
        return [qa_ref[h2, pl.ds(q0, tile), :] for h2 in range(2)]

    def k_block(j):
        start = pl.multiple_of(j * tile, tile)
        return [ka_ref[h2, pl.ds(start, tile), :] for h2 in range(2)]

    def vt_block(j):
        return [vt_ref[0, 0, j, h2 * d:(h2 + 1) * d, :] for h2 in range(2)]

    _causal_flash(n_tiles, tile, q_ops_of, k_block, vt_block, flash_scratch)
    acc_all = flash_scratch[5]

    def emit(qi, carry):
        a0 = acc_all[qi, 0]
        a1 = acc_all[qi, 1]
        yt = jnp.concatenate([a0[:d] / a0[d:d + 1], a1[:d] / a1[d:d + 1]], axis=0)
        o_ref[0, pl.ds(pl.multiple_of(qi * tile, tile), tile), :] = yt.T.astype(o_ref.dtype)
        return carry

    lax.fori_loop(0, n_tiles, emit, 0)


def _fox_attn(proj, vt, cp, *, n_heads, k_col, out_width):
    assert HEAD_DIM + 2 + 2 * n_heads <= LANES
    B, S, _ = proj.shape
    tile = ATTN_TILE
    n_tiles = S // tile
    n_pairs = n_heads // 2
    seq_block = _nbytes((S, LANES), BF16)
    block_bytes = (2 * 5 * seq_block + 4 * seq_block
                   + _flash_scratch_bytes(2, n_tiles, tile, HEAD_DIM))
    return pl.pallas_call(
        functools.partial(_fox_attn_kernel, tile=tile, n_heads=n_heads),
        grid=(B, n_pairs),
        in_specs=[
            pl.BlockSpec((1, S, LANES), lambda b, h: (b, 0, h)),
            pl.BlockSpec((1, S, LANES), lambda b, h: (b, 0, k_col + h)),
            pl.BlockSpec((1, 1, n_tiles, LANES, tile), lambda b, h: (b, h, 0, 0, 0)),
            pl.BlockSpec((1, S, LANES), lambda b, h: (b, 0, 0)),
        ],
        out_specs=pl.BlockSpec((1, S, LANES), lambda b, h: (b, 0, h)),
        out_shape=jax.ShapeDtypeStruct((B, S, out_width), BF16),
        scratch_shapes=[
            pltpu.VMEM((2, S, LANES), BF16),
            pltpu.VMEM((2, S, LANES), BF16),
        ] + _flash_scratch(2, n_tiles, tile, HEAD_DIM),
        compiler_params=_params(2, block_bytes),
        name="fox_attn",
    )(proj, proj, vt, cp)


def _diff_attn_kernel(q_ref, k_ref, vt_ref, lq1_ref, lk1_ref, lq2_ref, lk2_ref, g_ref, o_ref,
                      *flash_scratch, tile, lambda_init):
    seq = q_ref.shape[1]
    n_tiles = seq // tile
    lane = lax.broadcasted_iota(jnp.int32, (tile, LANES), 1)
    lam = (jnp.exp(jnp.sum(lq1_ref[...] * lk1_ref[...], axis=1, keepdims=True))
           - jnp.exp(jnp.sum(lq2_ref[...] * lk2_ref[...], axis=1, keepdims=True)) + lambda_init)

    dv = 2 * HEAD_DIM

    def q_ops_of(qi):
        qp = q_ref[0, pl.ds(pl.multiple_of(qi * tile, tile), tile), :]
        zero = jnp.zeros_like(qp)
        return [jnp.where(lane < HEAD_DIM, qp, zero), jnp.where(lane >= HEAD_DIM, qp, zero)]

    def k_block(j):
        kb = k_ref[0, pl.ds(pl.multiple_of(j * tile, tile), tile), :]
        return [kb, kb]

    def vt_block(j):
        vb = vt_ref[0, 0, j]
        return [vb, vb]

    _causal_flash(n_tiles, tile, q_ops_of, k_block, vt_block, flash_scratch)
    acc_all = flash_scratch[5]

    def emit(qi, carry):
        a1 = acc_all[qi, 0]
        a2 = acc_all[qi, 1]
        yt = a1[:dv] / a1[dv:dv + 1] - lam * (a2[:dv] / a2[dv:dv + 1])
        ms = jnp.mean(yt * yt, axis=0, keepdims=True)
        y = (yt * lax.rsqrt(ms + NORM_EPS)).T * g_ref[...] * (1.0 - lambda_init)
        o_ref[0, pl.ds(pl.multiple_of(qi * tile, tile), tile), :] = y.astype(o_ref.dtype)
        return carry

    lax.fori_loop(0, n_tiles, emit, 0)


def _diff_attn(qproj, k_sh, vt, lq1, lk1, lq2, lk2, subln_g, *, n_heads, lambda_init):
    B, S, _ = qproj.shape
    tile = ATTN_TILE
    n_tiles = S // tile
    vec = pl.BlockSpec((1, HEAD_DIM), lambda b, h: (0, 0))
    block_bytes = (2 * 4 * _nbytes((S, LANES), BF16)
                   + _flash_scratch_bytes(2, n_tiles, tile, 2 * HEAD_DIM))
    return pl.pallas_call(
        functools.partial(_diff_attn_kernel, tile=tile, lambda_init=lambda_init),
        grid=(B, n_heads),
        in_specs=[
            pl.BlockSpec((1, S, LANES), lambda b, h: (b, 0, h)),
            pl.BlockSpec((1, S, LANES), lambda b, h: (b, 0, h)),
            pl.BlockSpec((1, 1, n_tiles, LANES, tile), lambda b, h: (b, h, 0, 0, 0)),
            vec, vec, vec, vec,
            pl.BlockSpec((1, 2 * HEAD_DIM), lambda b, h: (0, 0)),
        ],
        out_specs=pl.BlockSpec((1, S, LANES), lambda b, h: (b, 0, h)),
        out_shape=jax.ShapeDtypeStruct((B, S, n_heads * 2 * HEAD_DIM), BF16),
        scratch_shapes=_flash_scratch(2, n_tiles, tile, 2 * HEAD_DIM),
        compiler_params=_params(2, block_bytes),
        name="diff_attn",
    )(qproj, k_sh, vt, lq1.reshape(1, -1), lk1.reshape(1, -1), lq2.reshape(1, -1),
      lk2.reshape(1, -1), subln_g.reshape(1, -1))


def _mix_out(x, y_ref, mq_ref, mk_ref, mv_ref, wo_ref):
    tm = x.shape[0]
    mw = mq_ref.shape[1]
    y_width = y_ref.shape[1]
    mq = mq_ref[...]
    mk = mk_ref[0]
    mv = mv_ref[0]
    q_head = lax.broadcasted_iota(jnp.int32, (tm, mw), 1) // HEAD_DIM
    v_head = lax.broadcasted_iota(jnp.int32, mv.shape, 1) // HEAD_DIM
    ymem = jnp.zeros((tm, mw), F32)
    for h in range(mw // HEAD_DIM):
        qh = jnp.where(q_head == h, mq, jnp.zeros_like(mq))
        s = lax.dot_general(qh, mk, _NT, preferred_element_type=F32)
        p = jnp.exp2(s - jnp.max(s, axis=1, keepdims=True))
        l = jnp.sum(p, axis=1, keepdims=True)
        vh = jnp.where(v_head == h, mv, jnp.zeros_like(mv))
        ymem = ymem + jnp.dot(p.astype(BF16), vh, preferred_element_type=F32) / l
    acc = jnp.dot(y_ref[...], wo_ref[0:y_width, :], preferred_element_type=F32)
    acc = acc + jnp.dot(ymem.astype(BF16), wo_ref[y_width:y_width + mw, :],
                        preferred_element_type=F32)
    return x + acc


def _post_kernel(*refs, final, n_chunks):
    x_ref, y_ref, mq_ref, mk_ref, mv_ref, wo_ref, g_ref, wgu_ref, wd_ref = refs[:9]
    wo_ref, wgu_ref, wd_ref = wo_ref.at[0], wgu_ref.at[0], wd_ref.at[0]
    if final:
        fg_ref, o_ref, hn_ref, acc_ref = refs[9:]
    else:
        o_ref, hn_ref, acc_ref = refs[9:]
    x = _mix_out(x_ref[...], y_ref, mq_ref, mk_ref, mv_ref, wo_ref)
    hn_ref[...] = _rms(x, g_ref[...]).astype(BF16)
    acc_ref[...] = x

    d_ff = wd_ref.shape[0]
    tf = d_ff // n_chunks
    for k in range(n_chunks):
        h = hn_ref[...]
        gate = jnp.dot(h, wgu_ref[:, k * tf:(k + 1) * tf], preferred_element_type=F32)
        up = jnp.dot(h, wgu_ref[:, d_ff + k * tf:d_ff + (k + 1) * tf],
                     preferred_element_type=F32)
        a = gate * jax.nn.sigmoid(gate) * up
        acc_ref[...] += jnp.dot(a.astype(BF16), wd_ref[k * tf:(k + 1) * tf, :],
                                preferred_element_type=F32)
    out = acc_ref[...]
    if final:
        out = _rms(out, fg_ref[...])
    o_ref[...] = out


def _post(x2d, y2d, proj2d, mq_col, memkv, wo, g, wgu, wd, *, layer, seq, final_g=None):
    T, D = x2d.shape
    yw = y2d.shape[1]
    n_mem, mw = memkv.shape[1], memkv.shape[2] // 2
    d_ff = wd.shape[1]

    def slab(w):
        return pl.BlockSpec((1,) + w.shape[1:], lambda i: (layer, 0, 0), pipeline_mode=_RESIDENT)

    tm = FFN_TM
    assert T % tm == 0 and seq % tm == 0 and d_ff % FFN_TF == 0
    n_chunks = d_ff // FFN_TF
    n_seq_tiles = seq // tm
    final = final_g is not None
    in_specs = [
        pl.BlockSpec((tm, D), lambda i: (i, 0)),
        pl.BlockSpec((tm, yw), lambda i: (i, 0)),
        pl.BlockSpec((tm, mw), lambda i: (i, mq_col)),
        pl.BlockSpec((1, n_mem, mw), lambda i: (i // n_seq_tiles, 0, 0)),
        pl.BlockSpec((1, n_mem, mw), lambda i: (i // n_seq_tiles, 0, 1)),
        slab(wo),
        pl.BlockSpec((1, D), lambda i: (0, 0)),
        slab(wgu),
        slab(wd),
    ]
    args = [x2d, y2d, proj2d, memkv, memkv, wo, g.reshape(1, D), wgu, wd]
    if final:
        in_specs.append(pl.BlockSpec((1, D), lambda i: (0, 0)))
        args.append(final_g.reshape(1, D))
    block_bytes = (5 * _nbytes((tm, D), F32) + _nbytes((tm, D), BF16)
                   + 2 * _nbytes((tm, yw + mw), BF16) + 4 * _nbytes((n_mem, mw), BF16)
                   + _nbytes(wo.shape[1:], BF16) + _nbytes(wgu.shape[1:], BF16)
                   + _nbytes(wd.shape[1:], BF16))
    return pl.pallas_call(
        functools.partial(_post_kernel, final=final, n_chunks=n_chunks),
        grid=(T // tm,),
        in_specs=in_specs,
        out_specs=pl.BlockSpec((tm, D), lambda i: (i, 0)),
        out_shape=jax.ShapeDtypeStruct((T, D), F32),
        scratch_shapes=[pltpu.VMEM((tm, D), BF16), pltpu.VMEM((tm, D), F32)],
        compiler_params=_params(1, block_bytes),
        name="post_final" if final else "post",
    )(*args)


def _rope_tables(seq):
    half = HEAD_DIM // 2
    inv_freq = jnp.power(ROPE_THETA, -jnp.arange(half, dtype=F32) * (2.0 / HEAD_DIM))
    ang = jnp.arange(seq, dtype=F32)[:, None] * inv_freq[None, :]
    cos, sin = jnp.cos(ang), jnp.sin(ang)
    reps = LANES // HEAD_DIM
    cos_t = jnp.tile(jnp.concatenate([cos, cos], axis=-1), (1, reps))
    sin_t = jnp.tile(jnp.concatenate([-sin, sin], axis=-1), (1, reps))
    return cos_t, sin_t


def kernel(x, mem, attn_norm_g, mem_norm_g, w_mem_kv, w_out, ffn_norm_g, w_gate_up, w_down,
           a_w_in, a_b_f, b_w_in, b_lambda_q1, b_lambda_k1, b_lambda_q2, b_lambda_k2,
           b_subln_g, kv_norm_g, w_kv_shared, final_norm_g):
    B, S, D = x.shape
    depth = attn_norm_g.shape[0]
    n_a = a_w_in.shape[0]
    n_mem = mem.shape[1]
    mem_w = w_mem_kv.shape[2] // 2
    n_fox = a_b_f.shape[1]
    fox_w = n_fox * HEAD_DIM
    diff_w = b_w_in.shape[1] - mem_w
    n_diff = diff_w // (2 * HEAD_DIM)
    scale = HEAD_DIM ** -0.5 * LOG2E
    T = B * S

    x2d = x.reshape(T, D)
    mem2d = mem.reshape(B * n_mem, D)
    rope_tabs = _rope_tables(S)
    k_sh = vt_sh = None
    wo_all = w_out.astype(BF16)
    wgu_all = w_gate_up.astype(BF16)
    wd_all = w_down.astype(BF16)

    for layer in range(depth):
        (memkv,) = _proj(mem2d, [(mem_norm_g[layer], w_mem_kv[layer].astype(BF16),
                                  0, 2 * mem_w, 0)], seq=n_mem)
        memkv = memkv.reshape(B, n_mem, 2 * mem_w)
        if layer < n_a:
            w_in = a_w_in[layer]
            w_main = jnp.concatenate(
                [w_in[:, :fox_w] * scale, w_in[:, fox_w:2 * fox_w],
                 w_in[:, 3 * fox_w + n_fox:] * scale, w_in[:, 2 * fox_w:3 * fox_w]],
                axis=1).astype(BF16)
            gw = jnp.pad(w_in[:, 3 * fox_w:3 * fox_w + n_fox], ((0, 0), (0, LANES - n_fox)))
            gw_hi = gw.astype(BF16)
            gw_lo = (gw - gw_hi.astype(F32)).astype(BF16)
            gate_w = jnp.concatenate([gw_hi, gw_lo], axis=1)
            gate_b = jnp.pad(a_b_f[layer], (0, LANES - n_fox)).reshape(1, LANES)
            row_cols = 2 * fox_w + mem_w
            proj, vt, cp = _proj(x2d, [(attn_norm_g[layer], w_main, 0, row_cols, fox_w)], seq=S,
                                 gate_w=gate_w, gate_b=gate_b, n_gates=n_fox)
            y = _fox_attn(proj.reshape(B, S, row_cols), vt, cp.reshape(B, S, LANES),
                          n_heads=n_fox, k_col=fox_w // LANES, out_width=fox_w)
            mq_col = 2 * fox_w // mem_w
        else:
            j = layer - n_a
            streams = [(attn_norm_g[layer], (b_w_in[j] * scale).astype(BF16),
                        diff_w, diff_w + mem_w, 0)]
            if layer == n_a:
                streams.append((kv_norm_g, w_kv_shared.astype(BF16), diff_w, diff_w,
                                w_kv_shared.shape[1] - diff_w))
                proj, k_sh, vt_sh = _proj(x2d, streams, seq=S, rope_tabs=rope_tabs)
                k_sh = k_sh.reshape(B, S, diff_w)
            else:
                (proj,) = _proj(x2d, streams, seq=S, rope_tabs=rope_tabs)
            lambda_init = 0.8 - 0.6 * math.exp(-0.3 * layer)
            y = _diff_attn(proj.reshape(B, S, -1), k_sh, vt_sh, b_lambda_q1[j], b_lambda_k1[j],
                           b_lambda_q2[j], b_lambda_k2[j], b_subln_g[j], n_heads=n_diff,
                           lambda_init=lambda_init)
            mq_col = diff_w // mem_w
        last = layer == depth - 1
        x2d = _post(x2d, y.reshape(T, -1), proj, mq_col, memkv, wo_all, ffn_norm_g[layer],
                    wgu_all, wd_all, layer=layer, seq=S,
                    final_g=final_norm_g if last else None)
    return x2d.reshape(B, S, D)
```

```python
import functools
import math

import jax
import jax.numpy as jnp
from jax import lax
from jax.experimental import pallas as pl
from jax.experimental.pallas import tpu as pltpu

HEAD_DIM = 64
ROPE_THETA = 10000.0
NORM_EPS = 1e-6
LANES = 128
ATTN_TILE = 512
PROJ_TM = ATTN_TILE
FFN_TM = 1024
FFN_TF = 256
VMEM_CAP = 56 * 1024 * 1024
VMEM_TEMPS = 16 * 1024 * 1024
LOG2E = math.log2(math.e)

F32 = jnp.float32
BF16 = jnp.bfloat16
_NT = (((1,), (1,)), ((), ()))
_RESIDENT = pl.Buffered(1)


def _nbytes(shape, dtype):
    return math.prod(shape) * jnp.dtype(dtype).itemsize


def _params(n_grid, block_bytes):
    limit = min(VMEM_CAP, block_bytes + VMEM_TEMPS)
    return pltpu.CompilerParams(dimension_semantics=("arbitrary",) * n_grid,
                                vmem_limit_bytes=limit)


def _rms(x, g):
    ms = jnp.mean(x * x, axis=-1, keepdims=True)
    return x * lax.rsqrt(ms + NORM_EPS) * g


def _split_bf16(v, n):
    pieces = []
    for _ in range(n - 1):
        p = v.astype(BF16)
        pieces.append(p)
        v = v - p.astype(F32)
    pieces.append(v.astype(BF16))
    return pieces


def _proj_kernel(*refs, streams, n_gates, n_seq_tiles):
    gates = n_gates > 0
    it = iter(refs)
    x_ref = next(it)
    gw = [(next(it), next(it)) for _ in streams]
    any_rope = any(s[0] for s in streams)
    if any_rope:
        cos_ref, sin_ref = next(it), next(it)
    if gates:
        wf_ref, bf_ref = next(it), next(it)
    outs = []
    for (_, _, vt_cols) in streams:
        row_ref = next(it)
        outs.append((row_ref, next(it) if vt_cols else None))
    if gates:
        cp_ref, carry_ref = next(it), next(it)

    tm = x_ref.shape[0]
    x = x_ref[...]
    xn = x * lax.rsqrt(jnp.mean(x * x, axis=-1, keepdims=True) + NORM_EPS)
    if any_rope:
        cos = cos_ref[...]
        sin = sin_ref[...]
        lane = lax.broadcasted_iota(jnp.int32, (tm, LANES), 1)
        first_half = (lane % HEAD_DIM) < (HEAD_DIM // 2)

    for si, ((rope_cols, row_cols, vt_cols), (g_ref, w_ref), (row_ref, vt_ref)) in enumerate(
            zip(streams, gw, outs)):
        hn32 = xn * g_ref[...]
        hn = hn32.astype(BF16)
        res = jnp.dot(hn, w_ref[...], preferred_element_type=F32)
        for c0 in range(0, rope_cols, LANES):
            t = res[:, c0:c0 + LANES]
            up = pltpu.roll(t, HEAD_DIM // 2, 1)
            dn = pltpu.roll(t, LANES - HEAD_DIM // 2, 1)
            sw = jnp.where(first_half, dn, up)
            row_ref[:, c0:c0 + LANES] = (t * cos + sw * sin).astype(BF16)
        if row_cols > rope_cols:
            row_ref[:, rope_cols:row_cols] = res[:, rope_cols:row_cols].astype(BF16)
        for gi in range(vt_cols // LANES):
            c0 = row_cols + gi * LANES
            vt_ref[0, gi, 0] = res[:, c0:c0 + LANES].T.astype(BF16)

        if gates and si == 0:
            lo = (hn32 - hn.astype(F32)).astype(BF16)
            wf = wf_ref[...]
            zz = jnp.dot(hn, wf, preferred_element_type=F32)
            z = (zz[:, :LANES] + zz[:, LANES:]
                 + jnp.dot(lo, wf[:, :LANES], preferred_element_type=F32) + bf_ref[...])
            lf = jnp.minimum(z, 0.0) - jnp.log1p(jnp.exp(-jnp.abs(z)))
            l_hi, l_mid, l_lo = _split_bf16(lf, 3)
            row = lax.broadcasted_iota(jnp.int32, (tm, tm), 0)
            col = lax.broadcasted_iota(jnp.int32, (tm, tm), 1)
            tri = (row >= col).astype(BF16)
            cc = jnp.dot(tri, jnp.concatenate([l_hi, l_mid], axis=1), preferred_element_type=F32)
            cs = cc[:, :LANES] + cc[:, LANES:] + jnp.dot(tri, l_lo, preferred_element_type=F32)

            @pl.when((pl.program_id(0) % n_seq_tiles) == 0)
            def _():
                carry_ref[...] = jnp.zeros_like(carry_ref)

            c = cs + carry_ref[...]
            carry_ref[...] = c[tm - 1:tm, :]
            p_hi, p_mid, p_lo = [p.astype(F32) for p in _split_bf16(c * LOG2E, 3)]
            lane_c = lax.broadcasted_iota(jnp.int32, (tm, LANES), 1)
            packed = jnp.where(lane_c < n_gates, p_hi, jnp.where(
                lane_c < 2 * n_gates, pltpu.roll(p_mid, n_gates, 1),
                pltpu.roll(p_lo, 2 * n_gates, 1)))
            for p in range(n_gates // 2):
                cp_ref[:, p * LANES:(p + 1) * LANES] = pltpu.roll(
                    packed, HEAD_DIM - 2 * p, 1).astype(BF16)


def _proj(x2d, streams, *, seq, rope_tabs=None, gate_w=None, gate_b=None, n_gates=0):
    T, D = x2d.shape
    tm = min(PROJ_TM, T)
    assert T % tm == 0
    gates = n_gates > 0
    assert 3 * n_gates <= LANES
    any_rope = any(s[2] for s in streams)
    any_vt = any(s[4] for s in streams)
    assert seq % tm == 0 or not (gates or any_rope or any_vt)
    n_seq_tiles = max(1, seq // tm)
    n_batch = T // seq

    in_specs = [pl.BlockSpec((tm, D), lambda i: (i, 0))]
    args = [x2d]
    block_bytes = 2 * _nbytes((tm, D), F32)
    for g, w, _, _, _ in streams:
        in_specs += [pl.BlockSpec((1, D), lambda i: (0, 0)),
                     pl.BlockSpec(w.shape, lambda i: (0, 0), pipeline_mode=_RESIDENT)]
        args += [g.reshape(1, D), w]
        block_bytes += _nbytes(w.shape, BF16) + _nbytes((tm, w.shape[1]), F32)
    if any_rope:
        tab_spec = pl.BlockSpec((tm, LANES), lambda i: (i % n_seq_tiles, 0))
        in_specs += [tab_spec, tab_spec]
        args += list(rope_tabs)
        block_bytes += 4 * _nbytes((tm, LANES), F32)
    if gates:
        in_specs += [pl.BlockSpec(gate_w.shape, lambda i: (0, 0), pipeline_mode=_RESIDENT),
                     pl.BlockSpec((1, LANES), lambda i: (0, 0))]
        args += [gate_w, gate_b]
        block_bytes += _nbytes(gate_w.shape, BF16)

    out_shape, out_specs = [], []
    for _, w, _, row_cols, vt_cols in streams:
        out_shape.append(jax.ShapeDtypeStruct((T, row_cols), BF16))
        out_specs.append(pl.BlockSpec((tm, row_cols), lambda i: (i, 0)))
        block_bytes += 2 * _nbytes((tm, w.shape[1]), BF16)
        if vt_cols:
            n_groups = vt_cols // LANES
            out_shape.append(
                jax.ShapeDtypeStruct((n_batch, n_groups, n_seq_tiles, LANES, tm), BF16))
            out_specs.append(pl.BlockSpec(
                (1, n_groups, 1, LANES, tm),
                lambda i: (i // n_seq_tiles, 0, i % n_seq_tiles, 0, 0)))
    scratch = []
    if gates:
        cp_cols = (n_gates // 2) * LANES
        out_shape.append(jax.ShapeDtypeStruct((T, cp_cols), BF16))
        out_specs.append(pl.BlockSpec((tm, cp_cols), lambda i: (i, 0)))
        block_bytes += 2 * _nbytes((tm, cp_cols), BF16)
        scratch.append(pltpu.VMEM((1, LANES), F32))

    cfg = tuple((s[2], s[3], s[4]) for s in streams)
    return pl.pallas_call(
        functools.partial(_proj_kernel, streams=cfg, n_gates=n_gates, n_seq_tiles=n_seq_tiles),
        grid=(T // tm,),
        in_specs=in_specs,
        out_specs=out_specs,
        out_shape=out_shape,
        scratch_shapes=scratch,
        compiler_params=_params(1, block_bytes),
        name="proj_gates" if gates else ("proj_rope" if any_rope else "proj"),
    )(*args)


ONES_ROWS = 16
OFF_UNROLL = 7


def _flash_scratch(n_maps, n_tiles, tile, dv):
    scores = pltpu.VMEM((n_maps, tile, tile), F32)
    row = pltpu.VMEM((n_maps, 1, tile), F32)
    return [scores, row, scores, row,
            pltpu.VMEM((n_tiles, n_maps, 1, tile), F32),
            pltpu.VMEM((n_tiles, n_maps, dv + ONES_ROWS, tile), F32)]


def _flash_scratch_bytes(n_maps, n_tiles, tile, dv):
    return (2 * _nbytes((n_maps, tile, tile), F32) + 2 * _nbytes((n_maps, 8, tile), F32)
            + _nbytes((n_tiles, n_maps, 8, tile), F32)
            + _nbytes((n_tiles, n_maps, dv + ONES_ROWS, tile), F32))


def _causal_flash(n_tiles, tile, q_ops_of, k_block, vt_block, scratch):
    buf_a, buf_b, m_all, acc_all = scratch[:2], scratch[2:4], scratch[4], scratch[5]
    n_off = n_tiles * (n_tiles - 1) // 2
    assert n_tiles % 2 == 0 and n_off % 2 == 0
    ones = jnp.ones((ONES_ROWS, tile), BF16)
    kk = lax.broadcasted_iota(jnp.int32, (tile, tile), 0)
    qq = lax.broadcasted_iota(jnp.int32, (tile, tile), 1)
    causal = kk <= qq

    def issue(qi, j, buf):
        s_ref, bm_ref = buf
        for i, (ka, qa) in enumerate(zip(k_block(j), q_ops_of(qi))):
            st = lax.dot_general(ka, qa, _NT, preferred_element_type=F32)
            s_ref[i] = st
            bm_ref[i] = jnp.max(st, axis=0, keepdims=True)

    half = tile // 2

    def issue_diag(qi, buf):
        s_ref, _ = buf
        for i, (ka, qa) in enumerate(zip(k_block(qi), q_ops_of(qi))):
            s_ref[i, :half, :half] = lax.dot_general(ka[:half], qa[:half], _NT,
                                                     preferred_element_type=F32)
            s_ref[i, :, half:] = lax.dot_general(ka, qa[half:], _NT, preferred_element_type=F32)

    def process_diag(buf, qi):
        s_ref, _ = buf
        for i, vt in enumerate(vt_block(qi)):
            vta = jnp.concatenate([vt, ones], axis=0)
            for rows, cols in ((slice(0, half), slice(0, half)), (slice(0, tile), slice(half, tile))):
                st = jnp.where(causal[rows, cols], s_ref[i, rows, cols], -jnp.inf)
                m_new = jnp.max(st, axis=0, keepdims=True)
                p = jnp.exp2(st - m_new).astype(BF16)
                acc_all[qi, i, :, cols] = jnp.dot(vta[:, rows], p, preferred_element_type=F32)
                m_all[qi, i, :, cols] = m_new

    def process(buf, qi, j):
        s_ref, bm_ref = buf
        for i, vt in enumerate(vt_block(j)):
            m = m_all[qi, i]
            m_new = jnp.maximum(m, bm_ref[i])
            alpha = jnp.exp2(m - m_new)
            p = jnp.exp2(s_ref[i] - m_new).astype(BF16)
            pv = jnp.dot(jnp.concatenate([vt, ones], axis=0), p, preferred_element_type=F32)
            acc_all[qi, i] = alpha * acc_all[qi, i] + pv
            m_all[qi, i] = m_new

    issue_diag(0, buf_a)

    for qi in range(0, n_tiles, 2):
        issue_diag(qi + 1, buf_b)
        process_diag(buf_a, qi)
        if qi + 2 < n_tiles:
            issue_diag(qi + 2, buf_a)
        else:
            issue(1, 0, buf_a)
        process_diag(buf_b, qi + 1)

    def following(qi, j):
        wrap = j + 1 == qi
        return jnp.where(wrap, qi + 1, qi), jnp.where(wrap, 0, j + 1)

    def off_pair(qi, j):
        qi1, j1 = following(qi, j)
        issue(qi1, j1, buf_b)
        process(buf_a, qi, j)
        qi2, j2 = following(qi1, j1)
        past = qi2 >= n_tiles
        qi2, j2 = jnp.where(past, 1, qi2), jnp.where(past, 0, j2)
        issue(qi2, j2, buf_a)
        process(buf_b, qi1, j1)
        return qi2, j2

    def off_trip(u, carry):
        for _ in range(OFF_UNROLL):
            carry = off_pair(*carry)
        return carry

    assert n_off % (2 * OFF_UNROLL) == 0
    lax.fori_loop(0, n_off // (2 * OFF_UNROLL), off_trip, (jnp.int32(1), jnp.int32(0)))


def _fox_attn_kernel(q_ref, k_ref, vt_ref, cp_ref, o_ref, qa_ref, ka_ref, *flash_scratch, tile,
                     n_heads):
    seq = q_ref.shape[1]
    n_tiles = seq // tile
    lane = lax.broadcasted_iota(jnp.int32, (tile, LANES), 1)
    d = HEAD_DIM

    def lanes_of(base):
        return (lane == base) | (lane == base + n_heads) | (lane == base + 2 * n_heads)

    def build(ci, carry):
        r0 = pl.multiple_of(ci * tile, tile)
        qp = q_ref[0, pl.ds(r0, tile), :].astype(F32)
        kp = k_ref[0, pl.ds(r0, tile), :].astype(F32)
        cp = cp_ref[0, pl.ds(r0, tile), :].astype(F32)
        for h2 in range(2):
            base = (1 - h2) * d
            is_head = (lane >= h2 * d) & (lane < h2 * d + d)
            in_a, in_b = lanes_of(base), lanes_of(base + 1)
            kc = cp if h2 == 0 else pltpu.roll(cp, d - 1, 1)
            qc = pltpu.roll(cp, 1 if h2 == 0 else d, 1)
            k_extra = jnp.where(in_a, kc, jnp.where(in_b, 1.0, 0.0))
            q_extra = jnp.where(in_b, qc, jnp.where(in_a, -1.0, 0.0))
            ka_ref[h2, pl.ds(r0, tile), :] = jnp.where(is_head, kp, k_extra).astype(BF16)
            qa_ref[h2, pl.ds(r0, tile), :] = jnp.where(is_head, qp, q_extra).astype(BF16)
        return carry

    lax.fori_loop(0, n_tiles, build, 0)

    def q_ops_of(qi):
        q0 = pl.multiple_of(qi * tile, tile)
        return [qa_ref[h2, pl.ds(q0, tile), :] for h2 in range(2)]

    def k_block(j):
        start = pl.multiple_of(j * tile, tile)
        return [ka_ref[h2, pl.ds(start, tile), :] for h2 in range(2)]

    def vt_block(j):
        return [vt_ref[0, 0, j, h2 * d:(h2 + 1) * d, :] for h2 in range(2)]

    _causal_flash(n_tiles, tile, q_ops_of, k_block, vt_block, flash_scratch)
    acc_all = flash_scratch[5]

    def emit(qi, carry):
        a0 = acc_all[qi, 0]
        a1 = acc_all[qi, 1]
        yt = jnp.concatenate([a0[:d] / a0[d:d + 1], a1[:d] / a1[d:d + 1]], axis=0)
        o_ref[0, pl.ds(pl.multiple_of(qi * tile, tile), tile), :] = yt.T.astype(o_ref.dtype)
        return carry

    lax.fori_loop(0, n_tiles, emit, 0)


def _fox_attn(proj, vt, cp, *, n_heads, k_col, out_width):
    assert HEAD_DIM + 2 + 2 * n_heads <= LANES
    B, S, _ = proj.shape
    tile = ATTN_TILE
    n_tiles = S // tile
    n_pairs = n_heads // 2
    seq_block = _nbytes((S, LANES), BF16)
    block_bytes = (2 * 5 * seq_block + 4 * seq_block
                   + _flash_scratch_bytes(2, n_tiles, tile, HEAD_DIM))
    return pl.pallas_call(
        functools.partial(_fox_attn_kernel, tile=tile, n_heads=n_heads),
        grid=(B, n_pairs),
        in_specs=[
            pl.BlockSpec((1, S, LANES), lambda b, h: (b, 0, h)),
            pl.BlockSpec((1, S, LANES), lambda b, h: (b, 0, k_col + h)),
            pl.BlockSpec((1, 1, n_tiles, LANES, tile), lambda b, h: (b, h, 0, 0, 0)),
            pl.BlockSpec((1, S, LANES), lambda b, h: (b, 0, h)),
        ],
        out_specs=pl.BlockSpec((1, S, LANES), lambda b, h: (b, 0, h)),
        out_shape=jax.ShapeDtypeStruct((B, S, out_width), BF16),
        scratch_shapes=[
            pltpu.VMEM((2, S, LANES), BF16),
            pltpu.VMEM((2, S, LANES), BF16),
        ] + _flash_scratch(2, n_tiles, tile, HEAD_DIM),
        compiler_params=_params(2, block_bytes),
        name="fox_attn",
    )(proj, proj, vt, cp)


def _diff_attn_kernel(q_ref, k_ref, vt_ref, lq1_ref, lk1_ref, lq2_ref, lk2_ref, g_ref, o_ref,
                      *flash_scratch, tile, lambda_init):
    seq = q_ref.shape[1]
    n_tiles = seq // tile
    lane = lax.broadcasted_iota(jnp.int32, (tile, LANES), 1)
    lam = (jnp.exp(jnp.sum(lq1_ref[...] * lk1_ref[...], axis=1, keepdims=True))
           - jnp.exp(jnp.sum(lq2_ref[...] * lk2_ref[...], axis=1, keepdims=True)) + lambda_init)

    dv = 2 * HEAD_DIM

    def q_ops_of(qi):
        qp = q_ref[0, pl.ds(pl.multiple_of(qi * tile, tile), tile), :]
        zero = jnp.zeros_like(qp)
        return [jnp.where(lane < HEAD_DIM, qp, zero), jnp.where(lane >= HEAD_DIM, qp, zero)]

    def k_block(j):
        kb = k_ref[0, pl.ds(pl.multiple_of(j * tile, tile), tile), :]
        return [kb, kb]

    def vt_block(j):
        vb = vt_ref[0, 0, j]
        return [vb, vb]

    _causal_flash(n_tiles, tile, q_ops_of, k_block, vt_block, flash_scratch)
    acc_all = flash_scratch[5]

    def emit(qi, carry):
        a1 = acc_all[qi, 0]
        a2 = acc_all[qi, 1]
        yt = a1[:dv] / a1[dv:dv + 1] - lam * (a2[:dv] / a2[dv:dv + 1])
        ms = jnp.mean(yt * yt, axis=0, keepdims=True)
        y = (yt * lax.rsqrt(ms + NORM_EPS)).T * g_ref[...] * (1.0 - lambda_init)
        o_ref[0, pl.ds(pl.multiple_of(qi * tile, tile), tile), :] = y.astype(o_ref.dtype)
        return carry

    lax.fori_loop(0, n_tiles, emit, 0)


def _diff_attn(qproj, k_sh, vt, lq1, lk1, lq2, lk2, subln_g, *, n_heads, lambda_init):
    B, S, _ = qproj.shape
    tile = ATTN_TILE
    n_tiles = S // tile
    vec = pl.BlockSpec((1, HEAD_DIM), lambda b, h: (0, 0))
    block_bytes = (2 * 4 * _nbytes((S, LANES), BF16)
                   + _flash_scratch_bytes(2, n_tiles, tile, 2 * HEAD_DIM))
    return pl.pallas_call(
        functools.partial(_diff_attn_kernel, tile=tile, lambda_init=lambda_init),
        grid=(B, n_heads),
        in_specs=[
            pl.BlockSpec((1, S, LANES), lambda b, h: (b, 0, h)),
            pl.BlockSpec((1, S, LANES), lambda b, h: (b, 0, h)),
            pl.BlockSpec((1, 1, n_tiles, LANES, tile), lambda b, h: (b, h, 0, 0, 0)),
            vec, vec, vec, vec,
            pl.BlockSpec((1, 2 * HEAD_DIM), lambda b, h: (0, 0)),
        ],
        out_specs=pl.BlockSpec((1, S, LANES), lambda b, h: (b, 0, h)),
        out_shape=jax.ShapeDtypeStruct((B, S, n_heads * 2 * HEAD_DIM), BF16),
        scratch_shapes=_flash_scratch(2, n_tiles, tile, 2 * HEAD_DIM),
        compiler_params=_params(2, block_bytes),
        name="diff_attn",
    )(qproj, k_sh, vt, lq1.reshape(1, -1), lk1.reshape(1, -1), lq2.reshape(1, -1),
      lk2.reshape(1, -1), subln_g.reshape(1, -1))


def _mix_out(x, y_ref, mq_ref, mk_ref, mv_ref, wo_ref):
    tm = x.shape[0]
    mw = mq_ref.shape[1]
    y_width = y_ref.shape[1]
    mq = mq_ref[...]
    mk = mk_ref[0]
    mv = mv_ref[0]
    q_head = lax.broadcasted_iota(jnp.int32, (tm, mw), 1) // HEAD_DIM
    v_head = lax.broadcasted_iota(jnp.int32, mv.shape, 1) // HEAD_DIM
    ymem = jnp.zeros((tm, mw), F32)
    for h in range(mw // HEAD_DIM):
        qh = jnp.where(q_head == h, mq, jnp.zeros_like(mq))
        s = lax.dot_general(qh, mk, _NT, preferred_element_type=F32)
        p = jnp.exp2(s - jnp.max(s, axis=1, keepdims=True))
        l = jnp.sum(p, axis=1, keepdims=True)
        vh = jnp.where(v_head == h, mv, jnp.zeros_like(mv))
        ymem = ymem + jnp.dot(p.astype(BF16), vh, preferred_element_type=F32) / l
    acc = jnp.dot(y_ref[...], wo_ref[0:y_width, :], preferred_element_type=F32)
    acc = acc + jnp.dot(ymem.astype(BF16), wo_ref[y_width:y_width + mw, :],
                        preferred_element_type=F32)
    return x + acc


def _post_kernel(*refs, final, n_chunks):
    x_ref, y_ref, mq_ref, mk_ref, mv_ref, wo_ref, g_ref, wgu_ref, wd_ref = refs[:9]
    wo_ref, wgu_ref, wd_ref = wo_ref.at[0], wgu_ref.at[0], wd_ref.at[0]
    if final:
        fg_ref, o_ref, hn_ref, acc_ref = refs[9:]
    else:
        o_ref, hn_ref, acc_ref = refs[9:]
    x = _mix_out(x_ref[...], y_ref, mq_ref, mk_ref, mv_ref, wo_ref)
    hn_ref[...] = _rms(x, g_ref[...]).astype(BF16)
    acc_ref[...] = x

    d_ff = wd_ref.shape[0]
    tf = d_ff // n_chunks
    for k in range(n_chunks):
        h = hn_ref[...]
        gate = jnp.dot(h, wgu_ref[:, k * tf:(k + 1) * tf], preferred_element_type=F32)
        up = jnp.dot(h, wgu_ref[:, d_ff + k * tf:d_ff + (k + 1) * tf],
                     preferred_element_type=F32)
        a = gate * jax.nn.sigmoid(gate) * up
        acc_ref[...] += jnp.dot(a.astype(BF16), wd_ref[k * tf:(k + 1) * tf, :],
                                preferred_element_type=F32)
    out = acc_ref[...]
    if final:
        out = _rms(out, fg_ref[...])
    o_ref[...] = out


def _post(x2d, y2d, proj2d, mq_col, memkv, wo, g, wgu, wd, *, layer, seq, final_g=None):
    T, D = x2d.shape
    yw = y2d.shape[1]
    n_mem, mw = memkv.shape[1], memkv.shape[2] // 2
    d_ff = wd.shape[1]

    def slab(w):
        return pl.BlockSpec((1,) + w.shape[1:], lambda i: (layer, 0, 0), pipeline_mode=_RESIDENT)

    tm = FFN_TM
    assert T % tm == 0 and seq % tm == 0 and d_ff % FFN_TF == 0
    n_chunks = d_ff // FFN_TF
    n_seq_tiles = seq // tm
    final = final_g is not None
    in_specs = [
        pl.BlockSpec((tm, D), lambda i: (i, 0)),
        pl.BlockSpec((tm, yw), lambda i: (i, 0)),
        pl.BlockSpec((tm, mw), lambda i: (i, mq_col)),
        pl.BlockSpec((1, n_mem, mw), lambda i: (i // n_seq_tiles, 0, 0)),
        pl.BlockSpec((1, n_mem, mw), lambda i: (i // n_seq_tiles, 0, 1)),
        slab(wo),
        pl.BlockSpec((1, D), lambda i: (0, 0)),
        slab(wgu),
        slab(wd),
    ]
    args = [x2d, y2d, proj2d, memkv, memkv, wo, g.reshape(1, D), wgu, wd]
    if final:
        in_specs.append(pl.BlockSpec((1, D), lambda i: (0, 0)))
        args.append(final_g.reshape(1, D))
    block_bytes = (5 * _nbytes((tm, D), F32) + _nbytes((tm, D), BF16)
                   + 2 * _nbytes((tm, yw + mw), BF16) + 4 * _nbytes((n_mem, mw), BF16)
                   + _nbytes(wo.shape[1:], BF16) + _nbytes(wgu.shape[1:], BF16)
                   + _nbytes(wd.shape[1:], BF16))
    return pl.pallas_call(
        functools.partial(_post_kernel, final=final, n_chunks=n_chunks),
        grid=(T // tm,),
        in_specs=in_specs,
        out_specs=pl.BlockSpec((tm, D), lambda i: (i, 0)),
        out_shape=jax.ShapeDtypeStruct((T, D), F32),
        scratch_shapes=[pltpu.VMEM((tm, D), BF16), pltpu.VMEM((tm, D), F32)],
        compiler_params=_params(1, block_bytes),
        name="post_final" if final else "post",
    )(*args)


def _rope_tables(seq):
    half = HEAD_DIM // 2
    inv_freq = jnp.power(ROPE_THETA, -jnp.arange(half, dtype=F32) * (2.0 / HEAD_DIM))
    ang = jnp.arange(seq, dtype=F32)[:, None] * inv_freq[None, :]
    cos, sin = jnp.cos(ang), jnp.sin(ang)
    reps = LANES // HEAD_DIM
    cos_t = jnp.tile(jnp.concatenate([cos, cos], axis=-1), (1, reps))
    sin_t = jnp.tile(jnp.concatenate([-sin, sin], axis=-1), (1, reps))
    return cos_t, sin_t


def kernel(x, mem, attn_norm_g, mem_norm_g, w_mem_kv, w_out, ffn_norm_g, w_gate_up, w_down,
           a_w_in, a_b_f, b_w_in, b_lambda_q1, b_lambda_k1, b_lambda_q2, b_lambda_k2,
           b_subln_g, kv_norm_g, w_kv_shared, final_norm_g):
    B, S, D = x.shape
    depth = attn_norm_g.shape[0]
    n_a = a_w_in.shape[0]
    n_mem = mem.shape[1]
    mem_w = w_mem_kv.shape[2] // 2
    n_fox = a_b_f.shape[1]
    fox_w = n_fox * HEAD_DIM
    diff_w = b_w_in.shape[1] - mem_w
    n_diff = diff_w // (2 * HEAD_DIM)
    scale = HEAD_DIM ** -0.5 * LOG2E
    T = B * S

    x2d = x.reshape(T, D)
    mem2d = mem.reshape(B * n_mem, D)
    rope_tabs = _rope_tables(S)
    k_sh = vt_sh = None
    wo_all = w_out.astype(BF16)
    wgu_all = w_gate_up.astype(BF16)
    wd_all = w_down.astype(BF16)

    for layer in range(depth):
        (memkv,) = _proj(mem2d, [(mem_norm_g[layer], w_mem_kv[layer].astype(BF16),
                                  0, 2 * mem_w, 0)], seq=n_mem)
        memkv = memkv.reshape(B, n_mem, 2 * mem_w)
        if layer < n_a:
            w_in = a_w_in[layer]
            w_main = jnp.concatenate(
                [w_in[:, :fox_w] * scale, w_in[:, fox_w:2 * fox_w],
                 w_in[:, 3 * fox_w + n_fox:] * scale, w_in[:, 2 * fox_w:3 * fox_w]],
                axis=1).astype(BF16)
            gw = jnp.pad(w_in[:, 3 * fox_w:3 * fox_w + n_fox], ((0, 0), (0, LANES - n_fox)))
            gw_hi = gw.astype(BF16)
            gw_lo = (gw - gw_hi.astype(F32)).astype(BF16)
            gate_w = jnp.concatenate([gw_hi, gw_lo], axis=1)
            gate_b = jnp.pad(a_b_f[layer], (0, LANES - n_fox)).reshape(1, LANES)
            row_cols = 2 * fox_w + mem_w
            proj, vt, cp = _proj(x2d, [(attn_norm_g[layer], w_main, 0, row_cols, fox_w)], seq=S,
                                 gate_w=gate_w, gate_b=gate_b, n_gates=n_fox)
            y = _fox_attn(proj.reshape(B, S, row_cols), vt, cp.reshape(B, S, -1),
                          n_heads=n_fox, k_col=fox_w // LANES, out_width=fox_w)
            mq_col = 2 * fox_w // mem_w
        else:
            j = layer - n_a
            streams = [(attn_norm_g[layer], (b_w_in[j] * scale).astype(BF16),
                        diff_w, diff_w + mem_w, 0)]
            if layer == n_a:
                streams.append((kv_norm_g, w_kv_shared.astype(BF16), diff_w, diff_w,
                                w_kv_shared.shape[1] - diff_w))
                proj, k_sh, vt_sh = _proj(x2d, streams, seq=S, rope_tabs=rope_tabs)
                k_sh = k_sh.reshape(B, S, diff_w)
            else:
                (proj,) = _proj(x2d, streams, seq=S, rope_tabs=rope_tabs)
            lambda_init = 0.8 - 0.6 * math.exp(-0.3 * layer)
            y = _diff_attn(proj.reshape(B, S, -1), k_sh, vt_sh, b_lambda_q1[j], b_lambda_k1[j],
                           b_lambda_q2[j], b_lambda_k2[j], b_subln_g[j], n_heads=n_diff,
                           lambda_init=lambda_init)
            mq_col = diff_w // mem_w
        last = layer == depth - 1
        x2d = _post(x2d, y.reshape(T, -1), proj, mq_col, memkv, wo_all, ffn_norm_g[layer],
                    wgu_all, wd_all, layer=layer, seq=S,
                    final_g=final_norm_g if last else None)
    return x2d.reshape(B, S, D)
```

```python
import functools
import math

import jax
import jax.numpy as jnp
from jax import lax
from jax.experimental import pallas as pl
from jax.experimental.pallas import tpu as pltpu

HEAD_DIM = 64
ROPE_THETA = 10000.0
NORM_EPS = 1e-6
LANES = 128
ATTN_TILE = 512
PROJ_TM = ATTN_TILE
FFN_TM = 1024
FFN_TF = 256
VMEM_CAP = 56 * 1024 * 1024
VMEM_TEMPS = 16 * 1024 * 1024
LOG2E = math.log2(math.e)

F32 = jnp.float32
BF16 = jnp.bfloat16
_NT = (((1,), (1,)), ((), ()))
_RESIDENT = pl.Buffered(1)


def _nbytes(shape, dtype):
    return math.prod(shape) * jnp.dtype(dtype).itemsize


def _params(n_grid, block_bytes):
    limit = min(VMEM_CAP, block_bytes + VMEM_TEMPS)
    return pltpu.CompilerParams(dimension_semantics=("arbitrary",) * n_grid,
                                vmem_limit_bytes=limit)


def _rms(x, g):
    ms = jnp.mean(x * x, axis=-1, keepdims=True)
    return x * lax.rsqrt(ms + NORM_EPS) * g


def _split_bf16(v, n):
    pieces = []
    for _ in range(n - 1):
        p = v.astype(BF16)
        pieces.append(p)
        v = v - p.astype(F32)
    pieces.append(v.astype(BF16))
    return pieces


def _proj_kernel(*refs, streams, n_gates, n_seq_tiles):
    gates = n_gates > 0
    it = iter(refs)
    x_ref = next(it)
    gw = [(next(it), next(it)) for _ in streams]
    any_rope = any(s[0] for s in streams)
    if any_rope:
        cos_ref, sin_ref = next(it), next(it)
    if gates:
        wf_ref, bf_ref = next(it), next(it)
    outs = []
    for (_, _, vt_cols) in streams:
        row_ref = next(it)
        outs.append((row_ref, next(it) if vt_cols else None))
    if gates:
        cp_ref, carry_ref = next(it), next(it)

    tm = x_ref.shape[0]
    x = x_ref[...]
    xn = x * lax.rsqrt(jnp.mean(x * x, axis=-1, keepdims=True) + NORM_EPS)
    if any_rope:
        cos = cos_ref[...]
        sin = sin_ref[...]
        lane = lax.broadcasted_iota(jnp.int32, (tm, LANES), 1)
        first_half = (lane % HEAD_DIM) < (HEAD_DIM // 2)

    for si, ((rope_cols, row_cols, vt_cols), (g_ref, w_ref), (row_ref, vt_ref)) in enumerate(
            zip(streams, gw, outs)):
        hn32 = xn * g_ref[...]
        hn = hn32.astype(BF16)
        res = jnp.dot(hn, w_ref[...], preferred_element_type=F32)
        for c0 in range(0, rope_cols, LANES):
            t = res[:, c0:c0 + LANES]
            up = pltpu.roll(t, HEAD_DIM // 2, 1)
            dn = pltpu.roll(t, LANES - HEAD_DIM // 2, 1)
            sw = jnp.where(first_half, dn, up)
            row_ref[:, c0:c0 + LANES] = (t * cos + sw * sin).astype(BF16)
        if row_cols > rope_cols:
            row_ref[:, rope_cols:row_cols] = res[:, rope_cols:row_cols].astype(BF16)
        for gi in range(vt_cols // LANES):
            c0 = row_cols + gi * LANES
            vt_ref[0, gi, 0] = res[:, c0:c0 + LANES].T.astype(BF16)

        if gates and si == 0:
            lo = (hn32 - hn.astype(F32)).astype(BF16)
            wf = wf_ref[...]
            zz = jnp.dot(hn, wf, preferred_element_type=F32)
            z = (zz[:, :LANES] + zz[:, LANES:]
                 + jnp.dot(lo, wf[:, :LANES], preferred_element_type=F32) + bf_ref[...])
            lf = jnp.minimum(z, 0.0) - jnp.log1p(jnp.exp(-jnp.abs(z)))
            l_hi, l_mid, l_lo = _split_bf16(lf, 3)
            row = lax.broadcasted_iota(jnp.int32, (tm, tm), 0)
            col = lax.broadcasted_iota(jnp.int32, (tm, tm), 1)
            tri = (row >= col).astype(BF16)
            cc = jnp.dot(tri, jnp.concatenate([l_hi, l_mid], axis=1), preferred_element_type=F32)
            cs = cc[:, :LANES] + cc[:, LANES:] + jnp.dot(tri, l_lo, preferred_element_type=F32)

            @pl.when((pl.program_id(0) % n_seq_tiles) == 0)
            def _():
                carry_ref[...] = jnp.zeros_like(carry_ref)

            c = cs + carry_ref[...]
            carry_ref[...] = c[tm - 1:tm, :]
            p_hi, p_mid, p_lo = [p.astype(F32) for p in _split_bf16(c * LOG2E, 3)]
            lane_c = lax.broadcasted_iota(jnp.int32, (tm, LANES), 1)
            packed = jnp.where(lane_c < n_gates, p_hi, jnp.where(
                lane_c < 2 * n_gates, pltpu.roll(p_mid, n_gates, 1),
                pltpu.roll(p_lo, 2 * n_gates, 1)))
            for p in range(n_gates // 2):
                cp_ref[:, p * LANES:(p + 1) * LANES] = pltpu.roll(
                    packed, HEAD_DIM - 2 * p, 1).astype(BF16)


def _proj(x2d, streams, *, seq, rope_tabs=None, gate_w=None, gate_b=None, n_gates=0):
    T, D = x2d.shape
    tm = min(PROJ_TM, T)
    assert T % tm == 0
    gates = n_gates > 0
    assert 3 * n_gates <= LANES
    any_rope = any(s[2] for s in streams)
    any_vt = any(s[4] for s in streams)
    assert seq % tm == 0 or not (gates or any_rope or any_vt)
    n_seq_tiles = max(1, seq // tm)
    n_batch = T // seq

    in_specs = [pl.BlockSpec((tm, D), lambda i: (i, 0))]
    args = [x2d]
    block_bytes = 2 * _nbytes((tm, D), F32)
    for g, w, _, _, _ in streams:
        in_specs += [pl.BlockSpec((1, D), lambda i: (0, 0)),
                     pl.BlockSpec(w.shape, lambda i: (0, 0), pipeline_mode=_RESIDENT)]
        args += [g.reshape(1, D), w]
        block_bytes += _nbytes(w.shape, BF16) + _nbytes((tm, w.shape[1]), F32)
    if any_rope:
        tab_spec = pl.BlockSpec((tm, LANES), lambda i: (i % n_seq_tiles, 0))
        in_specs += [tab_spec, tab_spec]
        args += list(rope_tabs)
        block_bytes += 4 * _nbytes((tm, LANES), F32)
    if gates:
        in_specs += [pl.BlockSpec(gate_w.shape, lambda i: (0, 0), pipeline_mode=_RESIDENT),
                     pl.BlockSpec((1, LANES), lambda i: (0, 0))]
        args += [gate_w, gate_b]
        block_bytes += _nbytes(gate_w.shape, BF16)

    out_shape, out_specs = [], []
    for _, w, _, row_cols, vt_cols in streams:
        out_shape.append(jax.ShapeDtypeStruct((T, row_cols), BF16))
        out_specs.append(pl.BlockSpec((tm, row_cols), lambda i: (i, 0)))
        block_bytes += 2 * _nbytes((tm, w.shape[1]), BF16)
        if vt_cols:
            n_groups = vt_cols // LANES
            out_shape.append(
                jax.ShapeDtypeStruct((n_batch, n_groups, n_seq_tiles, LANES, tm), BF16))
            out_specs.append(pl.BlockSpec(
                (1, n_groups, 1, LANES, tm),
                lambda i: (i // n_seq_tiles, 0, i % n_seq_tiles, 0, 0)))
    scratch = []
    if gates:
        cp_cols = (n_gates // 2) * LANES
        out_shape.append(jax.ShapeDtypeStruct((T, cp_cols), BF16))
        out_specs.append(pl.BlockSpec((tm, cp_cols), lambda i: (i, 0)))
        block_bytes += 2 * _nbytes((tm, cp_cols), BF16)
        scratch.append(pltpu.VMEM((1, LANES), F32))

    cfg = tuple((s[2], s[3], s[4]) for s in streams)
    return pl.pallas_call(
        functools.partial(_proj_kernel, streams=cfg, n_gates=n_gates, n_seq_tiles=n_seq_tiles),
        grid=(T // tm,),
        in_specs=in_specs,
        out_specs=out_specs,
        out_shape=out_shape,
        scratch_shapes=scratch,
        compiler_params=_params(1, block_bytes),
        name="proj_gates" if gates else ("proj_rope" if any_rope else "proj"),
    )(*args)


ONES_ROWS = 16
OFF_UNROLL = 7


def _flash_scratch(n_maps, n_tiles, tile, dv):
    scores = pltpu.VMEM((n_maps, tile, tile), F32)
    row = pltpu.VMEM((n_maps, 1, tile), F32)
    return [scores, row, scores, row,
            pltpu.VMEM((n_tiles, n_maps, 1, tile), F32),
            pltpu.VMEM((n_tiles, n_maps, dv + ONES_ROWS, tile), F32)]


def _flash_scratch_bytes(n_maps, n_tiles, tile, dv):
    return (2 * _nbytes((n_maps, tile, tile), F32) + 2 * _nbytes((n_maps, 8, tile), F32)
            + _nbytes((n_tiles, n_maps, 8, tile), F32)
            + _nbytes((n_tiles, n_maps, dv + ONES_ROWS, tile), F32))


def _causal_flash(n_tiles, tile, q_ops_of, k_block, vt_block, scratch):
    buf_a, buf_b, m_all, acc_all = scratch[:2], scratch[2:4], scratch[4], scratch[5]
    n_off = n_tiles * (n_tiles - 1) // 2
    assert n_tiles % 2 == 0 and n_off % 2 == 0
    ones = jnp.ones((ONES_ROWS, tile), BF16)
    kk = lax.broadcasted_iota(jnp.int32, (tile, tile), 0)
    qq = lax.broadcasted_iota(jnp.int32, (tile, tile), 1)
    causal = kk <= qq

    def issue(qi, j, buf):
        s_ref, bm_ref = buf
        for i, (ka, qa) in enumerate(zip(k_block(j), q_ops_of(qi))):
            st = lax.dot_general(ka, qa, _NT, preferred_element_type=F32)
            s_ref[i] = st
            bm_ref[i] = jnp.max(st, axis=0, keepdims=True)

    half = tile // 2

    def issue_diag(qi, buf):
        s_ref, _ = buf
        for i, (ka, qa) in enumerate(zip(k_block(qi), q_ops_of(qi))):
            s_ref[i, :half, :half] = lax.dot_general(ka[:half], qa[:half], _NT,
                                                     preferred_element_type=F32)
            s_ref[i, :, half:] = lax.dot_general(ka, qa[half:], _NT, preferred_element_type=F32)

    def process_diag(buf, qi):
        s_ref, _ = buf
        for i, vt in enumerate(vt_block(qi)):
            vta = jnp.concatenate([vt, ones], axis=0)
            for rows, cols in ((slice(0, half), slice(0, half)), (slice(0, tile), slice(half, tile))):
                st = jnp.where(causal[rows, cols], s_ref[i, rows, cols], -jnp.inf)
                m_new = jnp.max(st, axis=0, keepdims=True)
                p = jnp.exp2(st - m_new).astype(BF16)
                acc_all[qi, i, :, cols] = jnp.dot(vta[:, rows], p, preferred_element_type=F32)
                m_all[qi, i, :, cols] = m_new

    def process(buf, qi, j):
        s_ref, bm_ref = buf
        for i, vt in enumerate(vt_block(j)):
            m = m_all[qi, i]
            m_new = jnp.maximum(m, bm_ref[i])
            alpha = jnp.exp2(m - m_new)
            p = jnp.exp2(s_ref[i] - m_new).astype(BF16)
            pv = jnp.dot(jnp.concatenate([vt, ones], axis=0), p, preferred_element_type=F32)
            acc_all[qi, i] = alpha * acc_all[qi, i] + pv
            m_all[qi, i] = m_new

    issue_diag(0, buf_a)

    for qi in range(0, n_tiles, 2):
        issue_diag(qi + 1, buf_b)
        process_diag(buf_a, qi)
        if qi + 2 < n_tiles:
            issue_diag(qi + 2, buf_a)
        else:
            issue(1, 0, buf_a)
        process_diag(buf_b, qi + 1)

    def following(qi, j):
        wrap = j + 1 == qi
        return jnp.where(wrap, qi + 1, qi), jnp.where(wrap, 0, j + 1)

    def off_pair(qi, j):
        qi1, j1 = following(qi, j)
        issue(qi1, j1, buf_b)
        process(buf_a, qi, j)
        qi2, j2 = following(qi1, j1)
        past = qi2 >= n_tiles
        qi2, j2 = jnp.where(past, 1, qi2), jnp.where(past, 0, j2)
        issue(qi2, j2, buf_a)
        process(buf_b, qi1, j1)
        return qi2, j2

    def off_trip(u, carry):
        for _ in range(OFF_UNROLL):
            carry = off_pair(*carry)
        return carry

    assert n_off % (2 * OFF_UNROLL) == 0
    lax.fori_loop(0, n_off // (2 * OFF_UNROLL), off_trip, (jnp.int32(1), jnp.int32(0)))


def _fox_attn_kernel(q_ref, k_ref, vt_ref, cp_ref, o_ref, qa_ref, ka_ref, *flash_scratch, tile,
                     n_heads):
    seq = q_ref.shape[1]
    n_tiles = seq // tile
    lane = lax.broadcasted_iota(jnp.int32, (tile, LANES), 1)
    d = HEAD_DIM

    def lanes_of(base):
        return (lane == base) | (lane == base + n_heads) | (lane == base + 2 * n_heads)

    def build(ci, carry):
        r0 = pl.multiple_of(ci * tile, tile)
        qp = q_ref[0, pl.ds(r0, tile), :].astype(F32)
        kp = k_ref[0, pl.ds(r0, tile), :].astype(F32)
        cp = cp_ref[0, pl.ds(r0, tile), :].astype(F32)
        for h2 in range(2):
            base = (1 - h2) * d
            is_head = (lane >= h2 * d) & (lane < h2 * d + d)
            in_a, in_b = lanes_of(base), lanes_of(base + 1)
            kc = cp if h2 == 0 else pltpu.roll(cp, d - 1, 1)
            qc = pltpu.roll(cp, 1 if h2 == 0 else d, 1)
            k_extra = jnp.where(in_a, kc, jnp.where(in_b, 1.0, 0.0))
            q_extra = jnp.where(in_b, qc, jnp.where(in_a, -1.0, 0.0))
            ka_ref[h2, pl.ds(r0, tile), :] = jnp.where(is_head, kp, k_extra).astype(BF16)
            qa_ref[h2, pl.ds(r0, tile), :] = jnp.where(is_head, qp, q_extra).astype(BF16)
        return carry

    lax.fori_loop(0, n_tiles, build, 0)

    def q_ops_of(qi):
        q0 = pl.multiple_of(qi * tile, tile)
        return [qa_ref[h2, pl.ds(q0, tile), :] for h2 in range(2)]

    def k_block(j):
        start = pl.multiple_of(j * tile, tile)
        return [ka_ref[h2, pl.ds(start, tile), :] for h2 in range(2)]

    def vt_block(j):
        vb = vt_ref[0, 0, j]
        return [vb, vb]

    _causal_flash(n_tiles, tile, q_ops_of, k_block, vt_block, flash_scratch)
    acc_all = flash_scratch[5]

    def emit(qi, carry):
        a0 = acc_all[qi, 0]
        a1 = acc_all[qi, 1]
        yt = jnp.concatenate([a0[:d] / a0[2 * d:2 * d + 1], a1[d:2 * d] / a1[2 * d:2 * d + 1]],
                             axis=0)
        o_ref[0, pl.ds(pl.multiple_of(qi * tile, tile), tile), :] = yt.T.astype(o_ref.dtype)
        return carry

    lax.fori_loop(0, n_tiles, emit, 0)


def _fox_attn(proj, vt, cp, *, n_heads, k_col, out_width):
    assert HEAD_DIM + 2 + 2 * n_heads <= LANES
    B, S, _ = proj.shape
    tile = ATTN_TILE
    n_tiles = S // tile
    n_pairs = n_heads // 2
    seq_block = _nbytes((S, LANES), BF16)
    block_bytes = (2 * 5 * seq_block + 4 * seq_block
                   + _flash_scratch_bytes(2, n_tiles, tile, 2 * HEAD_DIM))
    return pl.pallas_call(
        functools.partial(_fox_attn_kernel, tile=tile, n_heads=n_heads),
        grid=(B, n_pairs),
        in_specs=[
            pl.BlockSpec((1, S, LANES), lambda b, h: (b, 0, h)),
            pl.BlockSpec((1, S, LANES), lambda b, h: (b, 0, k_col + h)),
            pl.BlockSpec((1, 1, n_tiles, LANES, tile), lambda b, h: (b, h, 0, 0, 0)),
            pl.BlockSpec((1, S, LANES), lambda b, h: (b, 0, h)),
        ],
        out_specs=pl.BlockSpec((1, S, LANES), lambda b, h: (b, 0, h)),
        out_shape=jax.ShapeDtypeStruct((B, S, out_width), BF16),
        scratch_shapes=[
            pltpu.VMEM((2, S, LANES), BF16),
            pltpu.VMEM((2, S, LANES), BF16),
        ] + _flash_scratch(2, n_tiles, tile, 2 * HEAD_DIM),
        compiler_params=_params(2, block_bytes),
        name="fox_attn",
    )(proj, proj, vt, cp)


def _diff_attn_kernel(q_ref, k_ref, vt_ref, lq1_ref, lk1_ref, lq2_ref, lk2_ref, g_ref, o_ref,
                      *flash_scratch, tile, lambda_init):
    seq = q_ref.shape[1]
    n_tiles = seq // tile
    lane = lax.broadcasted_iota(jnp.int32, (tile, LANES), 1)
    lam = (jnp.exp(jnp.sum(lq1_ref[...] * lk1_ref[...], axis=1, keepdims=True))
           - jnp.exp(jnp.sum(lq2_ref[...] * lk2_ref[...], axis=1, keepdims=True)) + lambda_init)

    dv = 2 * HEAD_DIM

    def q_ops_of(qi):
        qp = q_ref[0, pl.ds(pl.multiple_of(qi * tile, tile), tile), :]
        zero = jnp.zeros_like(qp)
        return [jnp.where(lane < HEAD_DIM, qp, zero), jnp.where(lane >= HEAD_DIM, qp, zero)]

    def k_block(j):
        kb = k_ref[0, pl.ds(pl.multiple_of(j * tile, tile), tile), :]
        return [kb, kb]

    def vt_block(j):
        vb = vt_ref[0, 0, j]
        return [vb, vb]

    _causal_flash(n_tiles, tile, q_ops_of, k_block, vt_block, flash_scratch)
    acc_all = flash_scratch[5]

    def emit(qi, carry):
        a1 = acc_all[qi, 0]
        a2 = acc_all[qi, 1]
        yt = a1[:dv] / a1[dv:dv + 1] - lam * (a2[:dv] / a2[dv:dv + 1])
        ms = jnp.mean(yt * yt, axis=0, keepdims=True)
        y = (yt * lax.rsqrt(ms + NORM_EPS)).T * g_ref[...] * (1.0 - lambda_init)
        o_ref[0, pl.ds(pl.multiple_of(qi * tile, tile), tile), :] = y.astype(o_ref.dtype)
        return carry

    lax.fori_loop(0, n_tiles, emit, 0)


def _diff_attn(qproj, k_sh, vt, lq1, lk1, lq2, lk2, subln_g, *, n_heads, lambda_init):
    B, S, _ = qproj.shape
    tile = ATTN_TILE
    n_tiles = S // tile
    vec = pl.BlockSpec((1, HEAD_DIM), lambda b, h: (0, 0))
    block_bytes = (2 * 4 * _nbytes((S, LANES), BF16)
                   + _flash_scratch_bytes(2, n_tiles, tile, 2 * HEAD_DIM))
    return pl.pallas_call(
        functools.partial(_diff_attn_kernel, tile=tile, lambda_init=lambda_init),
        grid=(B, n_heads),
        in_specs=[
            pl.BlockSpec((1, S, LANES), lambda b, h: (b, 0, h)),
            pl.BlockSpec((1, S, LANES), lambda b, h: (b, 0, h)),
            pl.BlockSpec((1, 1, n_tiles, LANES, tile), lambda b, h: (b, h, 0, 0, 0)),
            vec, vec, vec, vec,
            pl.BlockSpec((1, 2 * HEAD_DIM), lambda b, h: (0, 0)),
        ],
        out_specs=pl.BlockSpec((1, S, LANES), lambda b, h: (b, 0, h)),
        out_shape=jax.ShapeDtypeStruct((B, S, n_heads * 2 * HEAD_DIM), BF16),
        scratch_shapes=_flash_scratch(2, n_tiles, tile, 2 * HEAD_DIM),
        compiler_params=_params(2, block_bytes),
        name="diff_attn",
    )(qproj, k_sh, vt, lq1.reshape(1, -1), lk1.reshape(1, -1), lq2.reshape(1, -1),
      lk2.reshape(1, -1), subln_g.reshape(1, -1))


def _mix_out(x, y_ref, mq_ref, mk_ref, mv_ref, wo_ref):
    tm = x.shape[0]
    mw = mq_ref.shape[1]
    y_width = y_ref.shape[1]
    mq = mq_ref[...]
    mk = mk_ref[0]
    mv = mv_ref[0]
    q_head = lax.broadcasted_iota(jnp.int32, (tm, mw), 1) // HEAD_DIM
    v_head = lax.broadcasted_iota(jnp.int32, mv.shape, 1) // HEAD_DIM
    ymem = jnp.zeros((tm, mw), F32)
    for h in range(mw // HEAD_DIM):
        qh = jnp.where(q_head == h, mq, jnp.zeros_like(mq))
        s = lax.dot_general(qh, mk, _NT, preferred_element_type=F32)
        p = jnp.exp2(s - jnp.max(s, axis=1, keepdims=True))
        l = jnp.sum(p, axis=1, keepdims=True)
        vh = jnp.where(v_head == h, mv, jnp.zeros_like(mv))
        ymem = ymem + jnp.dot(p.astype(BF16), vh, preferred_element_type=F32) / l
    acc = jnp.dot(y_ref[...], wo_ref[0:y_width, :], preferred_element_type=F32)
    acc = acc + jnp.dot(ymem.astype(BF16), wo_ref[y_width:y_width + mw, :],
                        preferred_element_type=F32)
    return x + acc


def _post_kernel(*refs, final, n_chunks):
    x_ref, y_ref, mq_ref, mk_ref, mv_ref, wo_ref, g_ref, wgu_ref, wd_ref = refs[:9]
    wo_ref, wgu_ref, wd_ref = wo_ref.at[0], wgu_ref.at[0], wd_ref.at[0]
    if final:
        fg_ref, o_ref, hn_ref, acc_ref = refs[9:]
    else:
        o_ref, hn_ref, acc_ref = refs[9:]
    x = _mix_out(x_ref[...], y_ref, mq_ref, mk_ref, mv_ref, wo_ref)
    hn_ref[...] = _rms(x, g_ref[...]).astype(BF16)
    acc_ref[...] = x

    d_ff = wd_ref.shape[0]
    tf = d_ff // n_chunks
    for k in range(n_chunks):
        h = hn_ref[...]
        gate = jnp.dot(h, wgu_ref[:, k * tf:(k + 1) * tf], preferred_element_type=F32)
        up = jnp.dot(h, wgu_ref[:, d_ff + k * tf:d_ff + (k + 1) * tf],
                     preferred_element_type=F32)
        a = gate * jax.nn.sigmoid(gate) * up
        acc_ref[...] += jnp.dot(a.astype(BF16), wd_ref[k * tf:(k + 1) * tf, :],
                                preferred_element_type=F32)
    out = acc_ref[...]
    if final:
        out = _rms(out, fg_ref[...])
    o_ref[...] = out


def _post(x2d, y2d, proj2d, mq_col, memkv, wo, g, wgu, wd, *, layer, seq, final_g=None):
    T, D = x2d.shape
    yw = y2d.shape[1]
    n_mem, mw = memkv.shape[1], memkv.shape[2] // 2
    d_ff = wd.shape[1]

    def slab(w):
        return pl.BlockSpec((1,) + w.shape[1:], lambda i: (layer, 0, 0), pipeline_mode=_RESIDENT)

    tm = FFN_TM
    assert T % tm == 0 and seq % tm == 0 and d_ff % FFN_TF == 0
    n_chunks = d_ff // FFN_TF
    n_seq_tiles = seq // tm
    final = final_g is not None
    in_specs = [
        pl.BlockSpec((tm, D), lambda i: (i, 0)),
        pl.BlockSpec((tm, yw), lambda i: (i, 0)),
        pl.BlockSpec((tm, mw), lambda i: (i, mq_col)),
        pl.BlockSpec((1, n_mem, mw), lambda i: (i // n_seq_tiles, 0, 0)),
        pl.BlockSpec((1, n_mem, mw), lambda i: (i // n_seq_tiles, 0, 1)),
        slab(wo),
        pl.BlockSpec((1, D), lambda i: (0, 0)),
        slab(wgu),
        slab(wd),
    ]
    args = [x2d, y2d, proj2d, memkv, memkv, wo, g.reshape(1, D), wgu, wd]
    if final:
        in_specs.append(pl.BlockSpec((1, D), lambda i: (0, 0)))
        args.append(final_g.reshape(1, D))
    block_bytes = (5 * _nbytes((tm, D), F32) + _nbytes((tm, D), BF16)
                   + 2 * _nbytes((tm, yw + mw), BF16) + 4 * _nbytes((n_mem, mw), BF16)
                   + _nbytes(wo.shape[1:], BF16) + _nbytes(wgu.shape[1:], BF16)
                   + _nbytes(wd.shape[1:], BF16))
    return pl.pallas_call(
        functools.partial(_post_kernel, final=final, n_chunks=n_chunks),
        grid=(T // tm,),
        in_specs=in_specs,
        out_specs=pl.BlockSpec((tm, D), lambda i: (i, 0)),
        out_shape=jax.ShapeDtypeStruct((T, D), F32),
        scratch_shapes=[pltpu.VMEM((tm, D), BF16), pltpu.VMEM((tm, D), F32)],
        compiler_params=_params(1, block_bytes),
        name="post_final" if final else "post",
    )(*args)


def _rope_tables(seq):
    half = HEAD_DIM // 2
    inv_freq = jnp.power(ROPE_THETA, -jnp.arange(half, dtype=F32) * (2.0 / HEAD_DIM))
    ang = jnp.arange(seq, dtype=F32)[:, None] * inv_freq[None, :]
    cos, sin = jnp.cos(ang), jnp.sin(ang)
    reps = LANES // HEAD_DIM
    cos_t = jnp.tile(jnp.concatenate([cos, cos], axis=-1), (1, reps))
    sin_t = jnp.tile(jnp.concatenate([-sin, sin], axis=-1), (1, reps))
    return cos_t, sin_t


def kernel(x, mem, attn_norm_g, mem_norm_g, w_mem_kv, w_out, ffn_norm_g, w_gate_up, w_down,
           a_w_in, a_b_f, b_w_in, b_lambda_q1, b_lambda_k1, b_lambda_q2, b_lambda_k2,
           b_subln_g, kv_norm_g, w_kv_shared, final_norm_g):
    B, S, D = x.shape
    depth = attn_norm_g.shape[0]
    n_a = a_w_in.shape[0]
    n_mem = mem.shape[1]
    mem_w = w_mem_kv.shape[2] // 2
    n_fox = a_b_f.shape[1]
    fox_w = n_fox * HEAD_DIM
    diff_w = b_w_in.shape[1] - mem_w
    n_diff = diff_w // (2 * HEAD_DIM)
    scale = HEAD_DIM ** -0.5 * LOG2E
    T = B * S

    x2d = x.reshape(T, D)
    mem2d = mem.reshape(B * n_mem, D)
    rope_tabs = _rope_tables(S)
    k_sh = vt_sh = None
    wo_all = w_out.astype(BF16)
    wgu_all = w_gate_up.astype(BF16)
    wd_all = w_down.astype(BF16)

    for layer in range(depth):
        (memkv,) = _proj(mem2d, [(mem_norm_g[layer], w_mem_kv[layer].astype(BF16),
                                  0, 2 * mem_w, 0)], seq=n_mem)
        memkv = memkv.reshape(B, n_mem, 2 * mem_w)
        if layer < n_a:
            w_in = a_w_in[layer]
            w_main = jnp.concatenate(
                [w_in[:, :fox_w] * scale, w_in[:, fox_w:2 * fox_w],
                 w_in[:, 3 * fox_w + n_fox:] * scale, w_in[:, 2 * fox_w:3 * fox_w]],
                axis=1).astype(BF16)
            gw = jnp.pad(w_in[:, 3 * fox_w:3 * fox_w + n_fox], ((0, 0), (0, LANES - n_fox)))
            gw_hi = gw.astype(BF16)
            gw_lo = (gw - gw_hi.astype(F32)).astype(BF16)
            gate_w = jnp.concatenate([gw_hi, gw_lo], axis=1)
            gate_b = jnp.pad(a_b_f[layer], (0, LANES - n_fox)).reshape(1, LANES)
            row_cols = 2 * fox_w + mem_w
            proj, vt, cp = _proj(x2d, [(attn_norm_g[layer], w_main, 0, row_cols, fox_w)], seq=S,
                                 gate_w=gate_w, gate_b=gate_b, n_gates=n_fox)
            y = _fox_attn(proj.reshape(B, S, row_cols), vt, cp.reshape(B, S, -1),
                          n_heads=n_fox, k_col=fox_w // LANES, out_width=fox_w)
            mq_col = 2 * fox_w // mem_w
        else:
            j = layer - n_a
            streams = [(attn_norm_g[layer], (b_w_in[j] * scale).astype(BF16),
                        diff_w, diff_w + mem_w, 0)]
            if layer == n_a:
                streams.append((kv_norm_g, w_kv_shared.astype(BF16), diff_w, diff_w,
                                w_kv_shared.shape[1] - diff_w))
                proj, k_sh, vt_sh = _proj(x2d, streams, seq=S, rope_tabs=rope_tabs)
                k_sh = k_sh.reshape(B, S, diff_w)
            else:
                (proj,) = _proj(x2d, streams, seq=S, rope_tabs=rope_tabs)
            lambda_init = 0.8 - 0.6 * math.exp(-0.3 * layer)
            y = _diff_attn(proj.reshape(B, S, -1), k_sh, vt_sh, b_lambda_q1[j], b_lambda_k1[j],
                           b_lambda_q2[j], b_lambda_k2[j], b_subln_g[j], n_heads=n_diff,
                           lambda_init=lambda_init)
            mq_col = diff_w // mem_w
        last = layer == depth - 1
        x2d = _post(x2d, y.reshape(T, -1), proj, mq_col, memkv, wo_all, ffn_norm_g[layer],
                    wgu_all, wd_all, layer=layer, seq=S,
                    final_g=final_norm_g if last else None)
    return x2d.reshape(B, S, D)
```

```python
import functools
import math

import jax
import jax.numpy as jnp
from jax import lax
from jax.experimental import pallas as pl
from jax.experimental.pallas import tpu as pltpu

HEAD_DIM = 64
ROPE_THETA = 10000.0
NORM_EPS = 1e-6
LANES = 128
ATTN_TILE = 512
PROJ_TM = ATTN_TILE
FFN_TM = 1024
FFN_TF = 256
VMEM_CAP = 56 * 1024 * 1024
VMEM_TEMPS = 16 * 1024 * 1024
LOG2E = math.log2(math.e)

F32 = jnp.float32
BF16 = jnp.bfloat16
_NT = (((1,), (1,)), ((), ()))
_RESIDENT = pl.Buffered(1)


def _nbytes(shape, dtype):
    return math.prod(shape) * jnp.dtype(dtype).itemsize


def _params(n_grid, block_bytes):
    limit = min(VMEM_CAP, block_bytes + VMEM_TEMPS)
    return pltpu.CompilerParams(dimension_semantics=("arbitrary",) * n_grid,
                                vmem_limit_bytes=limit)


def _rms(x, g):
    ms = jnp.mean(x * x, axis=-1, keepdims=True)
    return x * lax.rsqrt(ms + NORM_EPS) * g


def _split_bf16(v, n):
    pieces = []
    for _ in range(n - 1):
        p = v.astype(BF16)
        pieces.append(p)
        v = v - p.astype(F32)
    pieces.append(v.astype(BF16))
    return pieces


def _proj_kernel(*refs, streams, n_gates, n_seq_tiles):
    gates = n_gates > 0
    it = iter(refs)
    x_ref = next(it)
    gw = [(next(it), next(it)) for _ in streams]
    any_rope = any(s[0] for s in streams)
    if any_rope:
        cos_ref, sin_ref = next(it), next(it)
    if gates:
        wf_ref, bf_ref = next(it), next(it)
    outs = []
    for (_, _, vt_cols) in streams:
        row_ref = next(it)
        outs.append((row_ref, next(it) if vt_cols else None))
    if gates:
        cp_ref, carry_ref = next(it), next(it)

    tm = x_ref.shape[0]
    x = x_ref[...]
    xn = x * lax.rsqrt(jnp.mean(x * x, axis=-1, keepdims=True) + NORM_EPS)
    if any_rope:
        cos = cos_ref[...]
        sin = sin_ref[...]
        lane = lax.broadcasted_iota(jnp.int32, (tm, LANES), 1)
        first_half = (lane % HEAD_DIM) < (HEAD_DIM // 2)

    for si, ((rope_cols, row_cols, vt_cols), (g_ref, w_ref), (row_ref, vt_ref)) in enumerate(
            zip(streams, gw, outs)):
        hn32 = xn * g_ref[...]
        hn = hn32.astype(BF16)
        res = jnp.dot(hn, w_ref[...], preferred_element_type=F32)
        for c0 in range(0, rope_cols, LANES):
            t = res[:, c0:c0 + LANES]
            up = pltpu.roll(t, HEAD_DIM // 2, 1)
            dn = pltpu.roll(t, LANES - HEAD_DIM // 2, 1)
            sw = jnp.where(first_half, dn, up)
            row_ref[:, c0:c0 + LANES] = (t * cos + sw * sin).astype(BF16)
        if row_cols > rope_cols:
            row_ref[:, rope_cols:row_cols] = res[:, rope_cols:row_cols].astype(BF16)
        for gi in range(vt_cols // LANES):
            c0 = row_cols + gi * LANES
            vt_ref[0, gi, 0] = res[:, c0:c0 + LANES].T.astype(BF16)

        if gates and si == 0:
            lo = (hn32 - hn.astype(F32)).astype(BF16)
            wf = wf_ref[...]
            zz = jnp.dot(hn, wf, preferred_element_type=F32)
            z = (zz[:, :LANES] + zz[:, LANES:]
                 + jnp.dot(lo, wf[:, :LANES], preferred_element_type=F32) + bf_ref[...])
            lf = jnp.minimum(z, 0.0) - jnp.log1p(jnp.exp(-jnp.abs(z)))
            l_hi, l_mid, l_lo = _split_bf16(lf, 3)
            row = lax.broadcasted_iota(jnp.int32, (tm, tm), 0)
            col = lax.broadcasted_iota(jnp.int32, (tm, tm), 1)
            tri = (row >= col).astype(BF16)
            cc = jnp.dot(tri, jnp.concatenate([l_hi, l_mid], axis=1), preferred_element_type=F32)
            cs = cc[:, :LANES] + cc[:, LANES:] + jnp.dot(tri, l_lo, preferred_element_type=F32)

            @pl.when((pl.program_id(0) % n_seq_tiles) == 0)
            def _():
                carry_ref[...] = jnp.zeros_like(carry_ref)

            c = cs + carry_ref[...]
            carry_ref[...] = c[tm - 1:tm, :]
            p_hi, p_mid, p_lo = [p.astype(F32) for p in _split_bf16(c * LOG2E, 3)]
            lane_c = lax.broadcasted_iota(jnp.int32, (tm, LANES), 1)
            packed = jnp.where(lane_c < n_gates, p_hi, jnp.where(
                lane_c < 2 * n_gates, pltpu.roll(p_mid, n_gates, 1),
                pltpu.roll(p_lo, 2 * n_gates, 1)))
            for p in range(n_gates // 2):
                cp_ref[:, p * LANES:(p + 1) * LANES] = pltpu.roll(
                    packed, HEAD_DIM - 2 * p, 1).astype(BF16)


def _proj(x2d, streams, *, seq, rope_tabs=None, gate_w=None, gate_b=None, n_gates=0):
    T, D = x2d.shape
    tm = min(PROJ_TM, T)
    assert T % tm == 0
    gates = n_gates > 0
    assert 3 * n_gates <= LANES
    any_rope = any(s[2] for s in streams)
    any_vt = any(s[4] for s in streams)
    assert seq % tm == 0 or not (gates or any_rope or any_vt)
    n_seq_tiles = max(1, seq // tm)
    n_batch = T // seq

    in_specs = [pl.BlockSpec((tm, D), lambda i: (i, 0))]
    args = [x2d]
    block_bytes = 2 * _nbytes((tm, D), F32)
    for g, w, _, _, _ in streams:
        in_specs += [pl.BlockSpec((1, D), lambda i: (0, 0)),
                     pl.BlockSpec(w.shape, lambda i: (0, 0), pipeline_mode=_RESIDENT)]
        args += [g.reshape(1, D), w]
        block_bytes += _nbytes(w.shape, BF16) + _nbytes((tm, w.shape[1]), F32)
    if any_rope:
        tab_spec = pl.BlockSpec((tm, LANES), lambda i: (i % n_seq_tiles, 0))
        in_specs += [tab_spec, tab_spec]
        args += list(rope_tabs)
        block_bytes += 4 * _nbytes((tm, LANES), F32)
    if gates:
        in_specs += [pl.BlockSpec(gate_w.shape, lambda i: (0, 0), pipeline_mode=_RESIDENT),
                     pl.BlockSpec((1, LANES), lambda i: (0, 0))]
        args += [gate_w, gate_b]
        block_bytes += _nbytes(gate_w.shape, BF16)

    out_shape, out_specs = [], []
    for _, w, _, row_cols, vt_cols in streams:
        out_shape.append(jax.ShapeDtypeStruct((T, row_cols), BF16))
        out_specs.append(pl.BlockSpec((tm, row_cols), lambda i: (i, 0)))
        block_bytes += 2 * _nbytes((tm, w.shape[1]), BF16)
        if vt_cols:
            n_groups = vt_cols // LANES
            out_shape.append(
                jax.ShapeDtypeStruct((n_batch, n_groups, n_seq_tiles, LANES, tm), BF16))
            out_specs.append(pl.BlockSpec(
                (1, n_groups, 1, LANES, tm),
                lambda i: (i // n_seq_tiles, 0, i % n_seq_tiles, 0, 0)))
    scratch = []
    if gates:
        cp_cols = (n_gates // 2) * LANES
        out_shape.append(jax.ShapeDtypeStruct((T, cp_cols), BF16))
        out_specs.append(pl.BlockSpec((tm, cp_cols), lambda i: (i, 0)))
        block_bytes += 2 * _nbytes((tm, cp_cols), BF16)
        scratch.append(pltpu.VMEM((1, LANES), F32))

    cfg = tuple((s[2], s[3], s[4]) for s in streams)
    return pl.pallas_call(
        functools.partial(_proj_kernel, streams=cfg, n_gates=n_gates, n_seq_tiles=n_seq_tiles),
        grid=(T // tm,),
        in_specs=in_specs,
        out_specs=out_specs,
        out_shape=out_shape,
        scratch_shapes=scratch,
        compiler_params=_params(1, block_bytes),
        name="proj_gates" if gates else ("proj_rope" if any_rope else "proj"),
    )(*args)


ONES_ROWS = 16
OFF_UNROLL = 7


def _flash_scratch(n_maps, n_tiles, tile, dv):
    scores = pltpu.VMEM((n_maps, tile, tile), F32)
    row = pltpu.VMEM((n_maps, 1, tile), F32)
    return [scores, row, scores, row,
            pltpu.VMEM((n_tiles, n_maps, 1, tile), F32),
            pltpu.VMEM((n_tiles, n_maps, dv, tile), F32),
            pltpu.VMEM((n_tiles, n_maps, 1, tile), F32)]


def _flash_scratch_bytes(n_maps, n_tiles, tile, dv):
    return (2 * _nbytes((n_maps, tile, tile), F32) + 2 * _nbytes((n_maps, 8, tile), F32)
            + 2 * _nbytes((n_tiles, n_maps, 8, tile), F32)
            + _nbytes((n_tiles, n_maps, dv, tile), F32))


def _causal_flash(n_tiles, tile, q_ops_of, k_block, vt_block, scratch):
    buf_a, buf_b, m_all, acc_all, l_all = scratch[:2], scratch[2:4], scratch[4], scratch[5], scratch[6]
    n_off = n_tiles * (n_tiles - 1) // 2
    assert n_tiles % 2 == 0 and n_off % 2 == 0
    kk = lax.broadcasted_iota(jnp.int32, (tile, tile), 0)
    qq = lax.broadcasted_iota(jnp.int32, (tile, tile), 1)
    causal = kk <= qq

    def issue(qi, j, buf):
        s_ref, bm_ref = buf
        for i, (ka, qa) in enumerate(zip(k_block(j), q_ops_of(qi))):
            st = lax.dot_general(ka, qa, _NT, preferred_element_type=F32)
            s_ref[i] = st
            bm_ref[i] = jnp.max(st, axis=0, keepdims=True)

    half = tile // 2

    def issue_diag(qi, buf):
        s_ref, _ = buf
        for i, (ka, qa) in enumerate(zip(k_block(qi), q_ops_of(qi))):
            s_ref[i, :half, :half] = lax.dot_general(ka[:half], qa[:half], _NT,
                                                     preferred_element_type=F32)
            s_ref[i, :, half:] = lax.dot_general(ka, qa[half:], _NT, preferred_element_type=F32)

    def process_diag(buf, qi):
        s_ref, _ = buf
        for i, vt in enumerate(vt_block(qi)):
            for rows, cols in ((slice(0, half), slice(0, half)), (slice(0, tile), slice(half, tile))):
                st = jnp.where(causal[rows, cols], s_ref[i, rows, cols], -jnp.inf)
                m_new = jnp.max(st, axis=0, keepdims=True)
                e = jnp.exp2(st - m_new)
                acc_all[qi, i, :, cols] = jnp.dot(vt[:, rows], e.astype(BF16),
                                                  preferred_element_type=F32)
                l_all[qi, i, :, cols] = jnp.sum(e, axis=0, keepdims=True)
                m_all[qi, i, :, cols] = m_new

    def process(buf, qi, j):
        s_ref, bm_ref = buf
        for i, vt in enumerate(vt_block(j)):
            m = m_all[qi, i]
            m_new = jnp.maximum(m, bm_ref[i])
            alpha = jnp.exp2(m - m_new)
            e = jnp.exp2(s_ref[i] - m_new)
            pv = jnp.dot(vt, e.astype(BF16), preferred_element_type=F32)
            acc_all[qi, i] = alpha * acc_all[qi, i] + pv
            l_all[qi, i] = alpha * l_all[qi, i] + jnp.sum(e, axis=0, keepdims=True)
            m_all[qi, i] = m_new

    issue_diag(0, buf_a)

    for qi in range(0, n_tiles, 2):
        issue_diag(qi + 1, buf_b)
        process_diag(buf_a, qi)
        if qi + 2 < n_tiles:
            issue_diag(qi + 2, buf_a)
        else:
            issue(1, 0, buf_a)
        process_diag(buf_b, qi + 1)

    def following(qi, j):
        wrap = j + 1 == qi
        return jnp.where(wrap, qi + 1, qi), jnp.where(wrap, 0, j + 1)

    def off_pair(qi, j):
        qi1, j1 = following(qi, j)
        issue(qi1, j1, buf_b)
        process(buf_a, qi, j)
        qi2, j2 = following(qi1, j1)
        past = qi2 >= n_tiles
        qi2, j2 = jnp.where(past, 1, qi2), jnp.where(past, 0, j2)
        issue(qi2, j2, buf_a)
        process(buf_b, qi1, j1)
        return qi2, j2

    def off_trip(u, carry):
        for _ in range(OFF_UNROLL):
            carry = off_pair(*carry)
        return carry

    assert n_off % (2 * OFF_UNROLL) == 0
    lax.fori_loop(0, n_off // (2 * OFF_UNROLL), off_trip, (jnp.int32(1), jnp.int32(0)))


def _fox_attn_kernel(q_ref, k_ref, vt_ref, cp_ref, o_ref, qa_ref, ka_ref, *flash_scratch, tile,
                     n_heads):
    seq = q_ref.shape[1]
    n_tiles = seq // tile
    lane = lax.broadcasted_iota(jnp.int32, (tile, LANES), 1)
    d = HEAD_DIM

    def lanes_of(base):
        return (lane == base) | (lane == base + n_heads) | (lane == base + 2 * n_heads)

    def build(ci, carry):
        r0 = pl.multiple_of(ci * tile, tile)
        qp = q_ref[0, pl.ds(r0, tile), :].astype(F32)
        kp = k_ref[0, pl.ds(r0, tile), :].astype(F32)
        cp = cp_ref[0, pl.ds(r0, tile), :].astype(F32)
        for h2 in range(2):
            base = (1 - h2) * d
            is_head = (lane >= h2 * d) & (lane < h2 * d + d)
            in_a, in_b = lanes_of(base), lanes_of(base + 1)
            kc = cp if h2 == 0 else pltpu.roll(cp, d - 1, 1)
            qc = pltpu.roll(cp, 1 if h2 == 0 else d, 1)
            k_extra = jnp.where(in_a, kc, jnp.where(in_b, 1.0, 0.0))
            q_extra = jnp.where(in_b, qc, jnp.where(in_a, -1.0, 0.0))
            ka_ref[h2, pl.ds(r0, tile), :] = jnp.where(is_head, kp, k_extra).astype(BF16)
            qa_ref[h2, pl.ds(r0, tile), :] = jnp.where(is_head, qp, q_extra).astype(BF16)
        return carry

    lax.fori_loop(0, n_tiles, build, 0)

    def q_ops_of(qi):
        q0 = pl.multiple_of(qi * tile, tile)
        return [qa_ref[h2, pl.ds(q0, tile), :] for h2 in range(2)]

    def k_block(j):
        start = pl.multiple_of(j * tile, tile)
        return [ka_ref[h2, pl.ds(start, tile), :] for h2 in range(2)]

    def vt_block(j):
        vb = vt_ref[0, 0, j]
        return [vb, vb]

    _causal_flash(n_tiles, tile, q_ops_of, k_block, vt_block, flash_scratch)
    acc_all, l_all = flash_scratch[5], flash_scratch[6]

    def emit(qi, carry):
        a0 = acc_all[qi, 0]
        a1 = acc_all[qi, 1]
        yt = jnp.concatenate([a0[:d] / l_all[qi, 0], a1[d:2 * d] / l_all[qi, 1]], axis=0)
        o_ref[0, pl.ds(pl.multiple_of(qi * tile, tile), tile), :] = yt.T.astype(o_ref.dtype)
        return carry

    lax.fori_loop(0, n_tiles, emit, 0)


def _fox_attn(proj, vt, cp, *, n_heads, k_col, out_width):
    assert HEAD_DIM + 2 + 2 * n_heads <= LANES
    B, S, _ = proj.shape
    tile = ATTN_TILE
    n_tiles = S // tile
    n_pairs = n_heads // 2
    seq_block = _nbytes((S, LANES), BF16)
    block_bytes = (2 * 5 * seq_block + 4 * seq_block
                   + _flash_scratch_bytes(2, n_tiles, tile, 2 * HEAD_DIM))
    return pl.pallas_call(
        functools.partial(_fox_attn_kernel, tile=tile, n_heads=n_heads),
        grid=(B, n_pairs),
        in_specs=[
            pl.BlockSpec((1, S, LANES), lambda b, h: (b, 0, h)),
            pl.BlockSpec((1, S, LANES), lambda b, h: (b, 0, k_col + h)),
            pl.BlockSpec((1, 1, n_tiles, LANES, tile), lambda b, h: (b, h, 0, 0, 0)),
            pl.BlockSpec((1, S, LANES), lambda b, h: (b, 0, h)),
        ],
        out_specs=pl.BlockSpec((1, S, LANES), lambda b, h: (b, 0, h)),
        out_shape=jax.ShapeDtypeStruct((B, S, out_width), BF16),
        scratch_shapes=[
            pltpu.VMEM((2, S, LANES), BF16),
            pltpu.VMEM((2, S, LANES), BF16),
        ] + _flash_scratch(2, n_tiles, tile, 2 * HEAD_DIM),
        compiler_params=_params(2, block_bytes),
        name="fox_attn",
    )(proj, proj, vt, cp)


def _diff_attn_kernel(q_ref, k_ref, vt_ref, lq1_ref, lk1_ref, lq2_ref, lk2_ref, g_ref, o_ref,
                      *flash_scratch, tile, lambda_init):
    seq = q_ref.shape[1]
    n_tiles = seq // tile
    lane = lax.broadcasted_iota(jnp.int32, (tile, LANES), 1)
    lam = (jnp.exp(jnp.sum(lq1_ref[...] * lk1_ref[...], axis=1, keepdims=True))
           - jnp.exp(jnp.sum(lq2_ref[...] * lk2_ref[...], axis=1, keepdims=True)) + lambda_init)

    dv = 2 * HEAD_DIM

    def q_ops_of(qi):
        qp = q_ref[0, pl.ds(pl.multiple_of(qi * tile, tile), tile), :]
        zero = jnp.zeros_like(qp)
        return [jnp.where(lane < HEAD_DIM, qp, zero), jnp.where(lane >= HEAD_DIM, qp, zero)]

    def k_block(j):
        kb = k_ref[0, pl.ds(pl.multiple_of(j * tile, tile), tile), :]
        return [kb, kb]

    def vt_block(j):
        vb = vt_ref[0, 0, j]
        return [vb, vb]

    _causal_flash(n_tiles, tile, q_ops_of, k_block, vt_block, flash_scratch)
    acc_all, l_all = flash_scratch[5], flash_scratch[6]

    def emit(qi, carry):
        a1 = acc_all[qi, 0]
        a2 = acc_all[qi, 1]
        yt = a1 / l_all[qi, 0] - lam * (a2 / l_all[qi, 1])
        ms = jnp.mean(yt * yt, axis=0, keepdims=True)
        y = (yt * lax.rsqrt(ms + NORM_EPS)).T * g_ref[...] * (1.0 - lambda_init)
        o_ref[0, pl.ds(pl.multiple_of(qi * tile, tile), tile), :] = y.astype(o_ref.dtype)
        return carry

    lax.fori_loop(0, n_tiles, emit, 0)


def _diff_attn(qproj, k_sh, vt, lq1, lk1, lq2, lk2, subln_g, *, n_heads, lambda_init):
    B, S, _ = qproj.shape
    tile = ATTN_TILE
    n_tiles = S // tile
    vec = pl.BlockSpec((1, HEAD_DIM), lambda b, h: (0, 0))
    block_bytes = (2 * 4 * _nbytes((S, LANES), BF16)
                   + _flash_scratch_bytes(2, n_tiles, tile, 2 * HEAD_DIM))
    return pl.pallas_call(
        functools.partial(_diff_attn_kernel, tile=tile, lambda_init=lambda_init),
        grid=(B, n_heads),
        in_specs=[
            pl.BlockSpec((1, S, LANES), lambda b, h: (b, 0, h)),
            pl.BlockSpec((1, S, LANES), lambda b, h: (b, 0, h)),
            pl.BlockSpec((1, 1, n_tiles, LANES, tile), lambda b, h: (b, h, 0, 0, 0)),
            vec, vec, vec, vec,
            pl.BlockSpec((1, 2 * HEAD_DIM), lambda b, h: (0, 0)),
        ],
        out_specs=pl.BlockSpec((1, S, LANES), lambda b, h: (b, 0, h)),
        out_shape=jax.ShapeDtypeStruct((B, S, n_heads * 2 * HEAD_DIM), BF16),
        scratch_shapes=_flash_scratch(2, n_tiles, tile, 2 * HEAD_DIM),
        compiler_params=_params(2, block_bytes),
        name="diff_attn",
    )(qproj, k_sh, vt, lq1.reshape(1, -1), lk1.reshape(1, -1), lq2.reshape(1, -1),
      lk2.reshape(1, -1), subln_g.reshape(1, -1))


def _mix_out(x, y_ref, mq_ref, mk_ref, mv_ref, wo_ref):
    tm = x.shape[0]
    mw = mq_ref.shape[1]
    y_width = y_ref.shape[1]
    mq = mq_ref[...]
    mk = mk_ref[0]
    mv = mv_ref[0]
    q_head = lax.broadcasted_iota(jnp.int32, (tm, mw), 1) // HEAD_DIM
    v_head = lax.broadcasted_iota(jnp.int32, mv.shape, 1) // HEAD_DIM
    ymem = jnp.zeros((tm, mw), F32)
    for h in range(mw // HEAD_DIM):
        qh = jnp.where(q_head == h, mq, jnp.zeros_like(mq))
        s = lax.dot_general(qh, mk, _NT, preferred_element_type=F32)
        p = jnp.exp2(s - jnp.max(s, axis=1, keepdims=True))
        l = jnp.sum(p, axis=1, keepdims=True)
        vh = jnp.where(v_head == h, mv, jnp.zeros_like(mv))
        ymem = ymem + jnp.dot(p.astype(BF16), vh, preferred_element_type=F32) / l
    acc = jnp.dot(y_ref[...], wo_ref[0:y_width, :], preferred_element_type=F32)
    acc = acc + jnp.dot(ymem.astype(BF16), wo_ref[y_width:y_width + mw, :],
                        preferred_element_type=F32)
    return x + acc


def _post_kernel(*refs, final, n_chunks):
    x_ref, y_ref, mq_ref, mk_ref, mv_ref, wo_ref, g_ref, wgu_ref, wd_ref = refs[:9]
    wo_ref, wgu_ref, wd_ref = wo_ref.at[0], wgu_ref.at[0], wd_ref.at[0]
    if final:
        fg_ref, o_ref, hn_ref, acc_ref = refs[9:]
    else:
        o_ref, hn_ref, acc_ref = refs[9:]
    x = _mix_out(x_ref[...], y_ref, mq_ref, mk_ref, mv_ref, wo_ref)
    hn_ref[...] = _rms(x, g_ref[...]).astype(BF16)
    acc_ref[...] = x

    d_ff = wd_ref.shape[0]
    tf = d_ff // n_chunks
    for k in range(n_chunks):
        h = hn_ref[...]
        gate = jnp.dot(h, wgu_ref[:, k * tf:(k + 1) * tf], preferred_element_type=F32)
        up = jnp.dot(h, wgu_ref[:, d_ff + k * tf:d_ff + (k + 1) * tf],
                     preferred_element_type=F32)
        a = gate * jax.nn.sigmoid(gate) * up
        acc_ref[...] += jnp.dot(a.astype(BF16), wd_ref[k * tf:(k + 1) * tf, :],
                                preferred_element_type=F32)
    out = acc_ref[...]
    if final:
        out = _rms(out, fg_ref[...])
    o_ref[...] = out


def _post(x2d, y2d, proj2d, mq_col, memkv, wo, g, wgu, wd, *, layer, seq, final_g=None):
    T, D = x2d.shape
    yw = y2d.shape[1]
    n_mem, mw = memkv.shape[1], memkv.shape[2] // 2
    d_ff = wd.shape[1]

    def slab(w):
        return pl.BlockSpec((1,) + w.shape[1:], lambda i: (layer, 0, 0), pipeline_mode=_RESIDENT)

    tm = FFN_TM
    assert T % tm == 0 and seq % tm == 0 and d_ff % FFN_TF == 0
    n_chunks = d_ff // FFN_TF
    n_seq_tiles = seq // tm
    final = final_g is not None
    in_specs = [
        pl.BlockSpec((tm, D), lambda i: (i, 0)),
        pl.BlockSpec((tm, yw), lambda i: (i, 0)),
        pl.BlockSpec((tm, mw), lambda i: (i, mq_col)),
        pl.BlockSpec((1, n_mem, mw), lambda i: (i // n_seq_tiles, 0, 0)),
        pl.BlockSpec((1, n_mem, mw), lambda i: (i // n_seq_tiles, 0, 1)),
        slab(wo),
        pl.BlockSpec((1, D), lambda i: (0, 0)),
        slab(wgu),
        slab(wd),
    ]
    args = [x2d, y2d, proj2d, memkv, memkv, wo, g.reshape(1, D), wgu, wd]
    if final:
        in_specs.append(pl.BlockSpec((1, D), lambda i: (0, 0)))
        args.append(final_g.reshape(1, D))
    block_bytes = (5 * _nbytes((tm, D), F32) + _nbytes((tm, D), BF16)
                   + 2 * _nbytes((tm, yw + mw), BF16) + 4 * _nbytes((n_mem, mw), BF16)
                   + _nbytes(wo.shape[1:], BF16) + _nbytes(wgu.shape[1:], BF16)
                   + _nbytes(wd.shape[1:], BF16))
    return pl.pallas_call(
        functools.partial(_post_kernel, final=final, n_chunks=n_chunks),
        grid=(T // tm,),
        in_specs=in_specs,
        out_specs=pl.BlockSpec((tm, D), lambda i: (i, 0)),
        out_shape=jax.ShapeDtypeStruct((T, D), F32),
        scratch_shapes=[pltpu.VMEM((tm, D), BF16), pltpu.VMEM((tm, D), F32)],
        compiler_params=_params(1, block_bytes),
        name="post_final" if final else "post",
    )(*args)


def _rope_tables(seq):
    half = HEAD_DIM // 2
    inv_freq = jnp.power(ROPE_THETA, -jnp.arange(half, dtype=F32) * (2.0 / HEAD_DIM))
    ang = jnp.arange(seq, dtype=F32)[:, None] * inv_freq[None, :]
    cos, sin = jnp.cos(ang), jnp.sin(ang)
    reps = LANES // HEAD_DIM
    cos_t = jnp.tile(jnp.concatenate([cos, cos], axis=-1), (1, reps))
    sin_t = jnp.tile(jnp.concatenate([-sin, sin], axis=-1), (1, reps))
    return cos_t, sin_t


def kernel(x, mem, attn_norm_g, mem_norm_g, w_mem_kv, w_out, ffn_norm_g, w_gate_up, w_down,
           a_w_in, a_b_f, b_w_in, b_lambda_q1, b_lambda_k1, b_lambda_q2, b_lambda_k2,
           b_subln_g, kv_norm_g, w_kv_shared, final_norm_g):
    B, S, D = x.shape
    depth = attn_norm_g.shape[0]
    n_a = a_w_in.shape[0]
    n_mem = mem.shape[1]
    mem_w = w_mem_kv.shape[2] // 2
    n_fox = a_b_f.shape[1]
    fox_w = n_fox * HEAD_DIM
    diff_w = b_w_in.shape[1] - mem_w
    n_diff = diff_w // (2 * HEAD_DIM)
    scale = HEAD_DIM ** -0.5 * LOG2E
    T = B * S

    x2d = x.reshape(T, D)
    mem2d = mem.reshape(B * n_mem, D)
    rope_tabs = _rope_tables(S)
    k_sh = vt_sh = None
    wo_all = w_out.astype(BF16)
    wgu_all = w_gate_up.astype(BF16)
    wd_all = w_down.astype(BF16)

    for layer in range(depth):
        (memkv,) = _proj(mem2d, [(mem_norm_g[layer], w_mem_kv[layer].astype(BF16),
                                  0, 2 * mem_w, 0)], seq=n_mem)
        memkv = memkv.reshape(B, n_mem, 2 * mem_w)
        if layer < n_a:
            w_in = a_w_in[layer]
            w_main = jnp.concatenate(
                [w_in[:, :fox_w] * scale, w_in[:, fox_w:2 * fox_w],
                 w_in[:, 3 * fox_w + n_fox:] * scale, w_in[:, 2 * fox_w:3 * fox_w]],
                axis=1).astype(BF16)
            gw = jnp.pad(w_in[:, 3 * fox_w:3 * fox_w + n_fox], ((0, 0), (0, LANES - n_fox)))
            gw_hi = gw.astype(BF16)
            gw_lo = (gw - gw_hi.astype(F32)).astype(BF16)
            gate_w = jnp.concatenate([gw_hi, gw_lo], axis=1)
            gate_b = jnp.pad(a_b_f[layer], (0, LANES - n_fox)).reshape(1, LANES)
            row_cols = 2 * fox_w + mem_w
            proj, vt, cp = _proj(x2d, [(attn_norm_g[layer], w_main, 0, row_cols, fox_w)], seq=S,
                                 gate_w=gate_w, gate_b=gate_b, n_gates=n_fox)
            y = _fox_attn(proj.reshape(B, S, row_cols), vt, cp.reshape(B, S, -1),
                          n_heads=n_fox, k_col=fox_w // LANES, out_width=fox_w)
            mq_col = 2 * fox_w // mem_w
        else:
            j = layer - n_a
            streams = [(attn_norm_g[layer], (b_w_in[j] * scale).astype(BF16),
                        diff_w, diff_w + mem_w, 0)]
            if layer == n_a:
                streams.append((kv_norm_g, w_kv_shared.astype(BF16), diff_w, diff_w,
                                w_kv_shared.shape[1] - diff_w))
                proj, k_sh, vt_sh = _proj(x2d, streams, seq=S, rope_tabs=rope_tabs)
                k_sh = k_sh.reshape(B, S, diff_w)
            else:
                (proj,) = _proj(x2d, streams, seq=S, rope_tabs=rope_tabs)
            lambda_init = 0.8 - 0.6 * math.exp(-0.3 * layer)
            y = _diff_attn(proj.reshape(B, S, -1), k_sh, vt_sh, b_lambda_q1[j], b_lambda_k1[j],
                           b_lambda_q2[j], b_lambda_k2[j], b_subln_g[j], n_heads=n_diff,
                           lambda_init=lambda_init)
            mq_col = diff_w // mem_w
        last = layer == depth - 1
        x2d = _post(x2d, y.reshape(T, -1), proj, mq_col, memkv, wo_all, ffn_norm_g[layer],
                    wgu_all, wd_all, layer=layer, seq=S,
                    final_g=final_norm_g if last else None)
    return x2d.reshape(B, S, D)
```

```python
import functools
import math

import jax
import jax.numpy as jnp
from jax import lax
from jax.experimental import pallas as pl
from jax.experimental.pallas import tpu as pltpu

HEAD_DIM = 64
ROPE_THETA = 10000.0
NORM_EPS = 1e-6
LANES = 128
ATTN_TILE = 512
PROJ_TM = ATTN_TILE
FFN_TM = 1024
FFN_TF = 256
VMEM_CAP = 56 * 1024 * 1024
VMEM_TEMPS = 16 * 1024 * 1024
LOG2E = math.log2(math.e)

F32 = jnp.float32
BF16 = jnp.bfloat16
_NT = (((1,), (1,)), ((), ()))
_RESIDENT = pl.Buffered(1)


def _nbytes(shape, dtype):
    return math.prod(shape) * jnp.dtype(dtype).itemsize


def _params(n_grid, block_bytes):
    limit = min(VMEM_CAP, block_bytes + VMEM_TEMPS)
    return pltpu.CompilerParams(dimension_semantics=("arbitrary",) * n_grid,
                                vmem_limit_bytes=limit)


def _rms(x, g):
    ms = jnp.mean(x * x, axis=-1, keepdims=True)
    return x * lax.rsqrt(ms + NORM_EPS) * g


def _split_bf16(v, n):
    pieces = []
    for _ in range(n - 1):
        p = v.astype(BF16)
        pieces.append(p)
        v = v - p.astype(F32)
    pieces.append(v.astype(BF16))
    return pieces


def _proj_kernel(*refs, streams, n_gates, n_seq_tiles):
    gates = n_gates > 0
    it = iter(refs)
    x_ref = next(it)
    gw = [(next(it), next(it)) for _ in streams]
    any_rope = any(s[0] for s in streams)
    if any_rope:
        cos_ref, sin_ref = next(it), next(it)
    if gates:
        wf_ref, bf_ref = next(it), next(it)
    outs = []
    for (_, _, vt_cols) in streams:
        row_ref = next(it)
        outs.append((row_ref, next(it) if vt_cols else None))
    if gates:
        cp_ref, carry_ref = next(it), next(it)

    tm = x_ref.shape[0]
    x = x_ref[...]
    xn = x * lax.rsqrt(jnp.mean(x * x, axis=-1, keepdims=True) + NORM_EPS)
    if any_rope:
        cos = cos_ref[...]
        sin = sin_ref[...]
        lane = lax.broadcasted_iota(jnp.int32, (tm, LANES), 1)
        first_half = (lane % HEAD_DIM) < (HEAD_DIM // 2)

    for si, ((rope_cols, row_cols, vt_cols), (g_ref, w_ref), (row_ref, vt_ref)) in enumerate(
            zip(streams, gw, outs)):
        hn32 = xn * g_ref[...]
        hn = hn32.astype(BF16)
        res = jnp.dot(hn, w_ref[...], preferred_element_type=F32)
        for c0 in range(0, rope_cols, LANES):
            t = res[:, c0:c0 + LANES]
            up = pltpu.roll(t, HEAD_DIM // 2, 1)
            dn = pltpu.roll(t, LANES - HEAD_DIM // 2, 1)
            sw = jnp.where(first_half, dn, up)
            row_ref[:, c0:c0 + LANES] = (t * cos + sw * sin).astype(BF16)
        if row_cols > rope_cols:
            row_ref[:, rope_cols:row_cols] = res[:, rope_cols:row_cols].astype(BF16)
        for gi in range(vt_cols // LANES):
            c0 = row_cols + gi * LANES
            vt_ref[0, gi, 0] = res[:, c0:c0 + LANES].T.astype(BF16)

        if gates and si == 0:
            lo = (hn32 - hn.astype(F32)).astype(BF16)
            wf = wf_ref[...]
            zz = jnp.dot(hn, wf, preferred_element_type=F32)
            z = (zz[:, :LANES] + zz[:, LANES:]
                 + jnp.dot(lo, wf[:, :LANES], preferred_element_type=F32) + bf_ref[...])
            lf = jnp.minimum(z, 0.0) - jnp.log1p(jnp.exp(-jnp.abs(z)))
            l_hi, l_mid, l_lo = _split_bf16(lf, 3)
            row = lax.broadcasted_iota(jnp.int32, (tm, tm), 0)
            col = lax.broadcasted_iota(jnp.int32, (tm, tm), 1)
            tri = (row >= col).astype(BF16)
            cc = jnp.dot(tri, jnp.concatenate([l_hi, l_mid], axis=1), preferred_element_type=F32)
            cs = cc[:, :LANES] + cc[:, LANES:] + jnp.dot(tri, l_lo, preferred_element_type=F32)

            @pl.when((pl.program_id(0) % n_seq_tiles) == 0)
            def _():
                carry_ref[...] = jnp.zeros_like(carry_ref)

            c = cs + carry_ref[...]
            carry_ref[...] = c[tm - 1:tm, :]
            p_hi, p_mid, p_lo = [p.astype(F32) for p in _split_bf16(c * LOG2E, 3)]
            lane_c = lax.broadcasted_iota(jnp.int32, (tm, LANES), 1)
            packed = jnp.where(lane_c < n_gates, p_hi, jnp.where(
                lane_c < 2 * n_gates, pltpu.roll(p_mid, n_gates, 1),
                pltpu.roll(p_lo, 2 * n_gates, 1)))
            for p in range(n_gates // 2):
                cp_ref[:, p * LANES:(p + 1) * LANES] = pltpu.roll(
                    packed, HEAD_DIM - 2 * p, 1).astype(BF16)


def _proj(x2d, streams, *, seq, rope_tabs=None, gate_w=None, gate_b=None, n_gates=0):
    T, D = x2d.shape
    tm = min(PROJ_TM, T)
    assert T % tm == 0
    gates = n_gates > 0
    assert 3 * n_gates <= LANES
    any_rope = any(s[2] for s in streams)
    any_vt = any(s[4] for s in streams)
    assert seq % tm == 0 or not (gates or any_rope or any_vt)
    n_seq_tiles = max(1, seq // tm)
    n_batch = T // seq

    in_specs = [pl.BlockSpec((tm, D), lambda i: (i, 0))]
    args = [x2d]
    block_bytes = 2 * _nbytes((tm, D), F32)
    for g, w, _, _, _ in streams:
        in_specs += [pl.BlockSpec((1, D), lambda i: (0, 0)),
                     pl.BlockSpec(w.shape, lambda i: (0, 0), pipeline_mode=_RESIDENT)]
        args += [g.reshape(1, D), w]
        block_bytes += _nbytes(w.shape, BF16) + _nbytes((tm, w.shape[1]), F32)
    if any_rope:
        tab_spec = pl.BlockSpec((tm, LANES), lambda i: (i % n_seq_tiles, 0))
        in_specs += [tab_spec, tab_spec]
        args += list(rope_tabs)
        block_bytes += 4 * _nbytes((tm, LANES), F32)
    if gates:
        in_specs += [pl.BlockSpec(gate_w.shape, lambda i: (0, 0), pipeline_mode=_RESIDENT),
                     pl.BlockSpec((1, LANES), lambda i: (0, 0))]
        args += [gate_w, gate_b]
        block_bytes += _nbytes(gate_w.shape, BF16)

    out_shape, out_specs = [], []
    for _, w, _, row_cols, vt_cols in streams:
        out_shape.append(jax.ShapeDtypeStruct((T, row_cols), BF16))
        out_specs.append(pl.BlockSpec((tm, row_cols), lambda i: (i, 0)))
        block_bytes += 2 * _nbytes((tm, w.shape[1]), BF16)
        if vt_cols:
            n_groups = vt_cols // LANES
            out_shape.append(
                jax.ShapeDtypeStruct((n_batch, n_groups, n_seq_tiles, LANES, tm), BF16))
            out_specs.append(pl.BlockSpec(
                (1, n_groups, 1, LANES, tm),
                lambda i: (i // n_seq_tiles, 0, i % n_seq_tiles, 0, 0)))
    scratch = []
    if gates:
        cp_cols = (n_gates // 2) * LANES
        out_shape.append(jax.ShapeDtypeStruct((T, cp_cols), BF16))
        out_specs.append(pl.BlockSpec((tm, cp_cols), lambda i: (i, 0)))
        block_bytes += 2 * _nbytes((tm, cp_cols), BF16)
        scratch.append(pltpu.VMEM((1, LANES), F32))

    cfg = tuple((s[2], s[3], s[4]) for s in streams)
    return pl.pallas_call(
        functools.partial(_proj_kernel, streams=cfg, n_gates=n_gates, n_seq_tiles=n_seq_tiles),
        grid=(T // tm,),
        in_specs=in_specs,
        out_specs=out_specs,
        out_shape=out_shape,
        scratch_shapes=scratch,
        compiler_params=_params(1, block_bytes),
        name="proj_gates" if gates else ("proj_rope" if any_rope else "proj"),
    )(*args)


ONES_ROWS = 16
OFF_UNROLL = 7


def _flash_scratch(n_maps, n_tiles, tile, dv):
    scores = pltpu.VMEM((n_maps, tile, tile), F32)
    row = pltpu.VMEM((n_maps, 1, tile), F32)
    return [scores, row, scores, row,
            pltpu.VMEM((n_tiles, n_maps, 1, tile), F32),
            pltpu.VMEM((n_tiles, n_maps, dv, tile), F32)]


def _flash_scratch_bytes(n_maps, n_tiles, tile, dv):
    return (2 * _nbytes((n_maps, tile, tile), F32) + 2 * _nbytes((n_maps, 8, tile), F32)
            + _nbytes((n_tiles, n_maps, 8, tile), F32)
            + _nbytes((n_tiles, n_maps, dv, tile), F32))


def _causal_flash(n_tiles, tile, q_ops_of, k_block, vt_block, scratch):
    buf_a, buf_b, m_all, acc_all = scratch[:2], scratch[2:4], scratch[4], scratch[5]
    n_off = n_tiles * (n_tiles - 1) // 2
    assert n_tiles % 2 == 0 and n_off % 2 == 0
    kk =lax.broadcasted_iota(jnp.int32, (tile, tile), 0)
    qq = lax.broadcasted_iota(jnp.int32, (tile, tile), 1)
    causal = kk <= qq

    def issue(qi, j, buf):
        s_ref, bm_ref = buf
        for i, (ka, qa) in enumerate(zip(k_block(j), q_ops_of(qi))):
            st = lax.dot_general(ka, qa, _NT, preferred_element_type=F32)
            s_ref[i] = st
            bm_ref[i] = jnp.max(st, axis=0, keepdims=True)

    half = tile // 2

    def issue_diag(qi, buf):
        s_ref, _ = buf
        for i, (ka, qa) in enumerate(zip(k_block(qi), q_ops_of(qi))):
            s_ref[i, :half, :half] = lax.dot_general(ka[:half], qa[:half], _NT,
                                                     preferred_element_type=F32)
            s_ref[i, :, half:] = lax.dot_general(ka, qa[half:], _NT, preferred_element_type=F32)

    def process_diag(buf, qi):
        s_ref, _ = buf
        for i, vt in enumerate(vt_block(qi)):
            vta = vt
            for rows, cols in ((slice(0, half), slice(0, half)), (slice(0, tile), slice(half, tile))):
                st = jnp.where(causal[rows, cols], s_ref[i, rows, cols], -jnp.inf)
                m_new = jnp.max(st, axis=0, keepdims=True)
                p = jnp.exp2(st - m_new).astype(BF16)
                acc_all[qi, i, :, cols] = jnp.dot(vta[:, rows], p, preferred_element_type=F32)
                m_all[qi, i, :, cols] = m_new

    def process(buf, qi, j):
        s_ref, bm_ref = buf
        for i, vt in enumerate(vt_block(j)):
            m = m_all[qi, i]
            m_new = jnp.maximum(m, bm_ref[i])
            alpha = jnp.exp2(m - m_new)
            p = jnp.exp2(s_ref[i] - m_new).astype(BF16)
            pv = jnp.dot(vt, p, preferred_element_type=F32)
            acc_all[qi, i] = alpha * acc_all[qi, i] + pv
            m_all[qi, i] = m_new

    issue_diag(0, buf_a)

    for qi in range(0, n_tiles, 2):
        issue_diag(qi + 1, buf_b)
        process_diag(buf_a, qi)
        if qi + 2 < n_tiles:
            issue_diag(qi + 2, buf_a)
        else:
            issue(1, 0, buf_a)
        process_diag(buf_b, qi + 1)

    def following(qi, j):
        wrap = j + 1 == qi
        return jnp.where(wrap, qi + 1, qi), jnp.where(wrap, 0, j + 1)

    def off_pair(qi, j):
        qi1, j1 = following(qi, j)
        issue(qi1, j1, buf_b)
        process(buf_a, qi, j)
        qi2, j2 = following(qi1, j1)
        past = qi2 >= n_tiles
        qi2, j2 = jnp.where(past, 1, qi2), jnp.where(past, 0, j2)
        issue(qi2, j2, buf_a)
        process(buf_b, qi1, j1)
        return qi2, j2

    def off_trip(u, carry):
        for _ in range(OFF_UNROLL):
            carry = off_pair(*carry)
        return carry

    assert n_off % (2 * OFF_UNROLL) == 0
    lax.fori_loop(0, n_off // (2 * OFF_UNROLL), off_trip, (jnp.int32(1), jnp.int32(0)))


def _fox_attn_kernel(q_ref, k_ref, vt_ref, cp_ref, o_ref, qa_ref, ka_ref, *flash_scratch, tile,
                     n_heads):
    seq = q_ref.shape[1]
    n_tiles = seq // tile
    lane = lax.broadcasted_iota(jnp.int32, (tile, LANES), 1)
    d = HEAD_DIM

    def lanes_of(base):
        return (lane == base) | (lane == base + n_heads) | (lane == base + 2 * n_heads)

    def build(ci, carry):
        r0 = pl.multiple_of(ci * tile, tile)
        qp = q_ref[0, pl.ds(r0, tile), :].astype(F32)
        kp = k_ref[0, pl.ds(r0, tile), :].astype(F32)
        cp = cp_ref[0, pl.ds(r0, tile), :].astype(F32)
        for h2 in range(2):
            base = (1 - h2) * d
            is_head = (lane >= h2 * d) & (lane < h2 * d + d)
            in_a, in_b = lanes_of(base), lanes_of(base + 1)
            kc = cp if h2 == 0 else pltpu.roll(cp, d - 1, 1)
            qc = pltpu.roll(cp, 1 if h2 == 0 else d, 1)
            k_extra = jnp.where(in_a, kc, jnp.where(in_b, 1.0, 0.0))
            q_extra = jnp.where(in_b, qc, jnp.where(in_a, -1.0, 0.0))
            ka_ref[h2, pl.ds(r0, tile), :] = jnp.where(is_head, kp, k_extra).astype(BF16)
            qa_ref[h2, pl.ds(r0, tile), :] = jnp.where(is_head, qp, q_extra).astype(BF16)
        return carry

    lax.fori_loop(0, n_tiles, build, 0)

    def q_ops_of(qi):
        q0 = pl.multiple_of(qi * tile, tile)
        return [qa_ref[h2, pl.ds(q0, tile), :] for h2 in range(2)]

    def k_block(j):
        start = pl.multiple_of(j * tile, tile)
        return [ka_ref[h2, pl.ds(start, tile), :] for h2 in range(2)]

    fill = jnp.concatenate([jnp.ones((ONES_ROWS, tile), BF16),
                            jnp.zeros((d - ONES_ROWS, tile), BF16)], axis=0)

    def vt_block(j):
        return [jnp.concatenate([vt_ref[0, 0, j, :d, :], fill], axis=0),
                jnp.concatenate([fill, vt_ref[0, 0, j, d:, :]], axis=0)]

    _causal_flash(n_tiles, tile, q_ops_of, k_block, vt_block, flash_scratch)
    acc_all = flash_scratch[5]

    def emit(qi, carry):
        a0 = acc_all[qi, 0]
        a1 = acc_all[qi, 1]
        yt = jnp.concatenate([a0[:d] / a0[d:d + 1], a1[d:2 * d] / a1[0:1]], axis=0)
        o_ref[0, pl.ds(pl.multiple_of(qi * tile, tile), tile), :] = yt.T.astype(o_ref.dtype)
        return carry

    lax.fori_loop(0, n_tiles, emit, 0)


def _fox_attn(proj, vt, cp, *, n_heads, k_col, out_width):
    assert HEAD_DIM + 2 + 2 * n_heads <= LANES
    B, S, _ = proj.shape
    tile = ATTN_TILE
    n_tiles = S // tile
    n_pairs = n_heads // 2
    seq_block = _nbytes((S, LANES), BF16)
    block_bytes = (2 * 5 * seq_block + 4 * seq_block
                   + _flash_scratch_bytes(2, n_tiles, tile, 2 * HEAD_DIM))
    return pl.pallas_call(
        functools.partial(_fox_attn_kernel, tile=tile, n_heads=n_heads),
        grid=(B, n_pairs),
        in_specs=[
            pl.BlockSpec((1, S, LANES), lambda b, h: (b, 0, h)),
            pl.BlockSpec((1, S, LANES), lambda b, h: (b, 0, k_col + h)),
            pl.BlockSpec((1, 1, n_tiles, LANES, tile), lambda b, h: (b, h, 0, 0, 0)),
            pl.BlockSpec((1, S, LANES), lambda b, h: (b, 0, h)),
        ],
        out_specs=pl.BlockSpec((1, S, LANES), lambda b, h: (b, 0, h)),
        out_shape=jax.ShapeDtypeStruct((B, S, out_width), BF16),
        scratch_shapes=[
            pltpu.VMEM((2, S, LANES), BF16),
            pltpu.VMEM((2, S, LANES), BF16),
        ] + _flash_scratch(2, n_tiles, tile, 2 * HEAD_DIM),
        compiler_params=_params(2, block_bytes),
        name="fox_attn",
    )(proj, proj, vt, cp)


def _diff_attn_kernel(q_ref, k_ref, vt_ref, lq1_ref, lk1_ref, lq2_ref, lk2_ref, g_ref, o_ref,
                      *flash_scratch, tile, lambda_init):
    seq = q_ref.shape[1]
    n_tiles = seq // tile
    lane = lax.broadcasted_iota(jnp.int32, (tile, LANES), 1)
    lam = (jnp.exp(jnp.sum(lq1_ref[...] * lk1_ref[...], axis=1, keepdims=True))
           - jnp.exp(jnp.sum(lq2_ref[...] * lk2_ref[...], axis=1, keepdims=True)) + lambda_init)

    dv = 2 * HEAD_DIM

    def q_ops_of(qi):
        qp = q_ref[0, pl.ds(pl.multiple_of(qi * tile, tile), tile), :]
        zero = jnp.zeros_like(qp)
        return [jnp.where(lane < HEAD_DIM, qp, zero), jnp.where(lane >= HEAD_DIM, qp, zero)]

    def k_block(j):
        kb = k_ref[0, pl.ds(pl.multiple_of(j * tile, tile), tile), :]
        return [kb, kb]

    ones = jnp.ones((ONES_ROWS, tile), BF16)

    def vt_block(j):
        vb = jnp.concatenate([vt_ref[0, 0, j], ones], axis=0)
        return [vb, vb]

    _causal_flash(n_tiles, tile, q_ops_of, k_block, vt_block, flash_scratch)
    acc_all = flash_scratch[5]

    def emit(qi, carry):
        a1 = acc_all[qi, 0]
        a2 = acc_all[qi, 1]
        yt = a1[:dv] / a1[dv:dv + 1] - lam * (a2[:dv] / a2[dv:dv + 1])
        ms = jnp.mean(yt * yt, axis=0, keepdims=True)
        y = (yt * lax.rsqrt(ms + NORM_EPS)).T * g_ref[...] * (1.0 - lambda_init)
        o_ref[0, pl.ds(pl.multiple_of(qi * tile, tile), tile), :] = y.astype(o_ref.dtype)
        return carry

    lax.fori_loop(0, n_tiles, emit, 0)


def _diff_attn(qproj, k_sh, vt, lq1, lk1, lq2, lk2, subln_g, *, n_heads, lambda_init):
    B, S, _ = qproj.shape
    tile = ATTN_TILE
    n_tiles = S // tile
    vec = pl.BlockSpec((1, HEAD_DIM), lambda b, h: (0, 0))
    acc_rows = 2 * HEAD_DIM + ONES_ROWS
    block_bytes = (2 * 4 * _nbytes((S, LANES), BF16)
                   + _flash_scratch_bytes(2, n_tiles, tile, acc_rows))
    return pl.pallas_call(
        functools.partial(_diff_attn_kernel, tile=tile, lambda_init=lambda_init),
        grid=(B, n_heads),
        in_specs=[
            pl.BlockSpec((1, S, LANES), lambda b, h: (b, 0, h)),
            pl.BlockSpec((1, S, LANES), lambda b, h: (b, 0, h)),
            pl.BlockSpec((1, 1, n_tiles, LANES, tile), lambda b, h: (b, h, 0, 0, 0)),
            vec, vec, vec, vec,
            pl.BlockSpec((1, 2 * HEAD_DIM), lambda b, h: (0, 0)),
        ],
        out_specs=pl.BlockSpec((1, S, LANES), lambda b, h: (b, 0, h)),
        out_shape=jax.ShapeDtypeStruct((B, S, n_heads * 2 * HEAD_DIM), BF16),
        scratch_shapes=_flash_scratch(2, n_tiles, tile, acc_rows),
        compiler_params=_params(2, block_bytes),
        name="diff_attn",
    )(qproj, k_sh, vt, lq1.reshape(1, -1), lk1.reshape(1, -1), lq2.reshape(1, -1),
      lk2.reshape(1, -1), subln_g.reshape(1, -1))


def _mix_out(x, y_ref, mq_ref, mk_ref, mv_ref, wo_ref):
    tm = x.shape[0]
    mw = mq_ref.shape[1]
    y_width = y_ref.shape[1]
    mq = mq_ref[...]
    mk = mk_ref[0]
    mv = mv_ref[0]
    q_head = lax.broadcasted_iota(jnp.int32, (tm, mw), 1) // HEAD_DIM
    v_head = lax.broadcasted_iota(jnp.int32, mv.shape, 1) // HEAD_DIM
    ymem = jnp.zeros((tm, mw), F32)
    for h in range(mw // HEAD_DIM):
        qh = jnp.where(q_head == h, mq, jnp.zeros_like(mq))
        s = lax.dot_general(qh, mk, _NT, preferred_element_type=F32)
        p = jnp.exp2(s - jnp.max(s, axis=1, keepdims=True))
        l = jnp.sum(p, axis=1, keepdims=True)
        vh = jnp.where(v_head == h, mv, jnp.zeros_like(mv))
        ymem = ymem + jnp.dot(p.astype(BF16), vh, preferred_element_type=F32) / l
    acc = jnp.dot(y_ref[...], wo_ref[0:y_width, :], preferred_element_type=F32)
    acc = acc + jnp.dot(ymem.astype(BF16), wo_ref[y_width:y_width + mw, :],
                        preferred_element_type=F32)
    return x + acc


def _post_kernel(*refs, final, n_chunks):
    x_ref, y_ref, mq_ref, mk_ref, mv_ref, wo_ref, g_ref, wgu_ref, wd_ref = refs[:9]
    wo_ref, wgu_ref, wd_ref = wo_ref.at[0], wgu_ref.at[0], wd_ref.at[0]
    if final:
        fg_ref, o_ref, hn_ref, acc_ref = refs[9:]
    else:
        o_ref, hn_ref, acc_ref = refs[9:]
    x = _mix_out(x_ref[...], y_ref, mq_ref, mk_ref, mv_ref, wo_ref)
    hn_ref[...] = _rms(x, g_ref[...]).astype(BF16)
    acc_ref[...] = x

    d_ff = wd_ref.shape[0]
    tf = d_ff // n_chunks
    for k in range(n_chunks):
        h = hn_ref[...]
        gate = jnp.dot(h, wgu_ref[:, k * tf:(k + 1) * tf], preferred_element_type=F32)
        up = jnp.dot(h, wgu_ref[:, d_ff + k * tf:d_ff + (k + 1) * tf],
                     preferred_element_type=F32)
        a = gate * jax.nn.sigmoid(gate) * up
        acc_ref[...] += jnp.dot(a.astype(BF16), wd_ref[k * tf:(k + 1) * tf, :],
                                preferred_element_type=F32)
    out = acc_ref[...]
    if final:
        out = _rms(out, fg_ref[...])
    o_ref[...] = out


def _post(x2d, y2d, proj2d, mq_col, memkv, wo, g, wgu, wd, *, layer, seq, final_g=None):
    T, D = x2d.shape
    yw = y2d.shape[1]
    n_mem, mw = memkv.shape[1], memkv.shape[2] // 2
    d_ff = wd.shape[1]

    def slab(w):
        return pl.BlockSpec((1,) + w.shape[1:], lambda i: (layer, 0, 0), pipeline_mode=_RESIDENT)

    tm = FFN_TM
    assert T % tm == 0 and seq % tm == 0 and d_ff % FFN_TF == 0
    n_chunks = d_ff // FFN_TF
    n_seq_tiles = seq // tm
    final = final_g is not None
    in_specs = [
        pl.BlockSpec((tm, D), lambda i: (i, 0)),
        pl.BlockSpec((tm, yw), lambda i: (i, 0)),
        pl.BlockSpec((tm, mw), lambda i: (i, mq_col)),
        pl.BlockSpec((1, n_mem, mw), lambda i: (i // n_seq_tiles, 0, 0)),
        pl.BlockSpec((1, n_mem, mw), lambda i: (i // n_seq_tiles, 0, 1)),
        slab(wo),
        pl.BlockSpec((1, D), lambda i: (0, 0)),
        slab(wgu),
        slab(wd),
    ]
    args = [x2d, y2d, proj2d, memkv, memkv, wo, g.reshape(1, D), wgu, wd]
    if final:
        in_specs.append(pl.BlockSpec((1, D), lambda i: (0, 0)))
        args.append(final_g.reshape(1, D))
    block_bytes = (5 * _nbytes((tm, D), F32) + _nbytes((tm, D), BF16)
                   + 2 * _nbytes((tm, yw + mw), BF16) + 4 * _nbytes((n_mem, mw), BF16)
                   + _nbytes(wo.shape[1:], BF16) + _nbytes(wgu.shape[1:], BF16)
                   + _nbytes(wd.shape[1:], BF16))
    return pl.pallas_call(
        functools.partial(_post_kernel, final=final, n_chunks=n_chunks),
        grid=(T // tm,),
        in_specs=in_specs,
        out_specs=pl.BlockSpec((tm, D), lambda i: (i, 0)),
        out_shape=jax.ShapeDtypeStruct((T, D), F32),
        scratch_shapes=[pltpu.VMEM((tm, D), BF16), pltpu.VMEM((tm, D), F32)],
        compiler_params=_params(1, block_bytes),
        name="post_final" if final else "post",
    )(*args)


def _rope_tables(seq):
    half = HEAD_DIM // 2
    inv_freq = jnp.power(ROPE_THETA, -jnp.arange(half, dtype=F32) * (2.0 / HEAD_DIM))
    ang = jnp.arange(seq, dtype=F32)[:, None] * inv_freq[None, :]
    cos, sin = jnp.cos(ang), jnp.sin(ang)
    reps = LANES // HEAD_DIM
    cos_t = jnp.tile(jnp.concatenate([cos, cos], axis=-1), (1, reps))
    sin_t = jnp.tile(jnp.concatenate([-sin, sin], axis=-1), (1, reps))
    return cos_t, sin_t


def kernel(x, mem, attn_norm_g, mem_norm_g, w_mem_kv, w_out, ffn_norm_g, w_gate_up, w_down,
           a_w_in, a_b_f, b_w_in, b_lambda_q1, b_lambda_k1, b_lambda_q2, b_lambda_k2,
           b_subln_g, kv_norm_g, w_kv_shared, final_norm_g):
    B, S, D = x.shape
    depth = attn_norm_g.shape[0]
    n_a = a_w_in.shape[0]
    n_mem = mem.shape[1]
    mem_w = w_mem_kv.shape[2] // 2
    n_fox = a_b_f.shape[1]
    fox_w = n_fox * HEAD_DIM
    diff_w = b_w_in.shape[1] - mem_w
    n_diff = diff_w // (2 * HEAD_DIM)
    scale = HEAD_DIM ** -0.5 * LOG2E
    T = B * S

    x2d = x.reshape(T, D)
    mem2d = mem.reshape(B * n_mem, D)
    rope_tabs = _rope_tables(S)
    k_sh = vt_sh = None
    wo_all = w_out.astype(BF16)
    wgu_all = w_gate_up.astype(BF16)
    wd_all = w_down.astype(BF16)

    for layer in range(depth):
        (memkv,) = _proj(mem2d, [(mem_norm_g[layer], w_mem_kv[layer].astype(BF16),
                                  0, 2 * mem_w, 0)], seq=n_mem)
        memkv = memkv.reshape(B, n_mem, 2 * mem_w)
        if layer < n_a:
            w_in = a_w_in[layer]
            w_main = jnp.concatenate(
                [w_in[:, :fox_w] * scale, w_in[:, fox_w:2 * fox_w],
                 w_in[:, 3 * fox_w + n_fox:] * scale, w_in[:, 2 * fox_w:3 * fox_w]],
                axis=1).astype(BF16)
            gw = jnp.pad(w_in[:, 3 * fox_w:3 * fox_w + n_fox], ((0, 0), (0, LANES - n_fox)))
            gw_hi = gw.astype(BF16)
            gw_lo = (gw - gw_hi.astype(F32)).astype(BF16)
            gate_w = jnp.concatenate([gw_hi, gw_lo], axis=1)
            gate_b = jnp.pad(a_b_f[layer], (0, LANES - n_fox)).reshape(1, LANES)
            row_cols = 2 * fox_w + mem_w
            proj, vt, cp = _proj(x2d, [(attn_norm_g[layer], w_main, 0, row_cols, fox_w)], seq=S,
                                 gate_w=gate_w, gate_b=gate_b, n_gates=n_fox)
            y = _fox_attn(proj.reshape(B, S, row_cols), vt, cp.reshape(B, S, -1),
                          n_heads=n_fox, k_col=fox_w // LANES, out_width=fox_w)
            mq_col = 2 * fox_w // mem_w
        else:
            j = layer - n_a
            streams = [(attn_norm_g[layer], (b_w_in[j] * scale).astype(BF16),
                        diff_w, diff_w + mem_w, 0)]
            if layer == n_a:
                streams.append((kv_norm_g, w_kv_shared.astype(BF16), diff_w, diff_w,
                                w_kv_shared.shape[1] - diff_w))
                proj, k_sh, vt_sh = _proj(x2d, streams, seq=S, rope_tabs=rope_tabs)
                k_sh = k_sh.reshape(B, S, diff_w)
            else:
                (proj,) = _proj(x2d, streams, seq=S, rope_tabs=rope_tabs)
            lambda_init = 0.8 - 0.6 * math.exp(-0.3 * layer)
            y = _diff_attn(proj.reshape(B, S, -1), k_sh, vt_sh, b_lambda_q1[j], b_lambda_k1[j],
                           b_lambda_q2[j], b_lambda_k2[j], b_subln_g[j], n_heads=n_diff,
                           lambda_init=lambda_init)
            mq_col = diff_w // mem_w
        last = layer == depth - 1
        x2d = _post(x2d, y.reshape(T, -1), proj, mq_col, memkv, wo_all, ffn_norm_g[layer],
                    wgu_all, wd_all, layer=layer, seq=S,
                    final_g=final_norm_g if last else None)
    return x2d.reshape(B, S, D)
```

```python
import functools
import math

import jax
import jax.numpy as jnp
from jax import lax
from jax.experimental import pallas as pl
from jax.experimental.pallas import tpu as pltpu

HEAD_DIM = 64
ROPE_THETA = 10000.0
NORM_EPS = 1e-6
LANES = 128
ATTN_TILE = 512
PROJ_TM = ATTN_TILE
FFN_TM = 1024
FFN_TF = 256
VMEM_CAP = 56 * 1024 * 1024
VMEM_TEMPS = 16 * 1024 * 1024
LOG2E = math.log2(math.e)

F32 = jnp.float32
BF16 = jnp.bfloat16
_NT = (((1,), (1,)), ((), ()))
_RESIDENT = pl.Buffered(1)


def _nbytes(shape, dtype):
    return math.prod(shape) * jnp.dtype(dtype).itemsize


def _params(n_grid, block_bytes):
    limit = min(VMEM_CAP, block_bytes + VMEM_TEMPS)
    return pltpu.CompilerParams(dimension_semantics=("arbitrary",) * n_grid,
                                vmem_limit_bytes=limit)


def _rms(x, g):
    ms = jnp.mean(x * x, axis=-1, keepdims=True)
    return x * lax.rsqrt(ms + NORM_EPS) * g


def _split_bf16(v, n):
    pieces = []
    for _ in range(n - 1):
        p = v.astype(BF16)
        pieces.append(p)
        v = v - p.astype(F32)
    pieces.append(v.astype(BF16))
    return pieces


def _proj_kernel(*refs, streams, n_gates, n_seq_tiles):
    gates = n_gates > 0
    it = iter(refs)
    x_ref = next(it)
    gw = [(next(it), next(it)) for _ in streams]
    any_rope = any(s[0] for s in streams)
    if any_rope:
        cos_ref, sin_ref = next(it), next(it)
    if gates:
        wf_ref, bf_ref = next(it), next(it)
    outs = []
    for (_, _, vt_cols) in streams:
        row_ref = next(it)
        outs.append((row_ref, next(it) if vt_cols else None))
    if gates:
        cp_ref, carry_ref = next(it), next(it)

    tm = x_ref.shape[0]
    x = x_ref[...]
    xn = x * lax.rsqrt(jnp.mean(x * x, axis=-1, keepdims=True) + NORM_EPS)
    if any_rope:
        cos = cos_ref[...]
        sin = sin_ref[...]
        lane = lax.broadcasted_iota(jnp.int32, (tm, LANES), 1)
        first_half = (lane % HEAD_DIM) < (HEAD_DIM // 2)

    for si, ((rope_cols, row_cols, vt_cols), (g_ref, w_ref), (row_ref, vt_ref)) in enumerate(
            zip(streams, gw, outs)):
        hn32 = xn * g_ref[...]
        hn = hn32.astype(BF16)
        res = jnp.dot(hn, w_ref[...], preferred_element_type=F32)
        for c0 in range(0, rope_cols, LANES):
            t = res[:, c0:c0 + LANES]
            up = pltpu.roll(t, HEAD_DIM // 2, 1)
            dn = pltpu.roll(t, LANES - HEAD_DIM // 2, 1)
            sw = jnp.where(first_half, dn, up)
            row_ref[:, c0:c0 + LANES] = (t * cos + sw * sin).astype(BF16)
        if row_cols > rope_cols:
            row_ref[:, rope_cols:row_cols] = res[:, rope_cols:row_cols].astype(BF16)
        for gi in range(vt_cols // LANES):
            c0 = row_cols + gi * LANES
            vt_ref[0, gi, 0] = res[:, c0:c0 + LANES].T.astype(BF16)

        if gates and si == 0:
            lo = (hn32 - hn.astype(F32)).astype(BF16)
            wf = wf_ref[...]
            zz = jnp.dot(hn, wf, preferred_element_type=F32)
            z = (zz[:, :LANES] + zz[:, LANES:]
                 + jnp.dot(lo, wf[:, :LANES], preferred_element_type=F32) + bf_ref[...])
            lf = jnp.minimum(z, 0.0) - jnp.log1p(jnp.exp(-jnp.abs(z)))
            l_hi, l_mid, l_lo = _split_bf16(lf, 3)
            row = lax.broadcasted_iota(jnp.int32, (tm, tm), 0)
            col = lax.broadcasted_iota(jnp.int32, (tm, tm), 1)
            tri = (row >= col).astype(BF16)
            cc = jnp.dot(tri, jnp.concatenate([l_hi, l_mid], axis=1), preferred_element_type=F32)
            cs = cc[:, :LANES] + cc[:, LANES:] + jnp.dot(tri, l_lo, preferred_element_type=F32)

            @pl.when((pl.program_id(0) % n_seq_tiles) == 0)
            def _():
                carry_ref[...] = jnp.zeros_like(carry_ref)

            c = cs + carry_ref[...]
            carry_ref[...] = c[tm - 1:tm, :]
            p_hi, p_mid, p_lo = [p.astype(F32) for p in _split_bf16(c * LOG2E, 3)]
            lane_c = lax.broadcasted_iota(jnp.int32, (tm, LANES), 1)
            packed = jnp.where(lane_c < n_gates, p_hi, jnp.where(
                lane_c < 2 * n_gates, pltpu.roll(p_mid, n_gates, 1),
                pltpu.roll(p_lo, 2 * n_gates, 1)))
            for p in range(n_gates // 2):
                cp_ref[:, p * LANES:(p + 1) * LANES] = pltpu.roll(
                    packed, HEAD_DIM - 2 * p, 1).astype(BF16)


def _proj(x2d, streams, *, seq, rope_tabs=None, gate_w=None, gate_b=None, n_gates=0):
    T, D = x2d.shape
    tm = min(PROJ_TM, T)
    assert T % tm == 0
    gates = n_gates > 0
    assert 3 * n_gates <= LANES
    any_rope = any(s[2] for s in streams)
    any_vt = any(s[4] for s in streams)
    assert seq % tm == 0 or not (gates or any_rope or any_vt)
    n_seq_tiles = max(1, seq // tm)
    n_batch = T // seq

    in_specs = [pl.BlockSpec((tm, D), lambda i: (i, 0))]
    args = [x2d]
    block_bytes = 2 * _nbytes((tm, D), F32)
    for g, w, _, _, _ in streams:
        in_specs += [pl.BlockSpec((1, D), lambda i: (0, 0)),
                     pl.BlockSpec(w.shape, lambda i: (0, 0), pipeline_mode=_RESIDENT)]
        args += [g.reshape(1, D), w]
        block_bytes += _nbytes(w.shape, BF16) + _nbytes((tm, w.shape[1]), F32)
    if any_rope:
        tab_spec = pl.BlockSpec((tm, LANES), lambda i: (i % n_seq_tiles, 0))
        in_specs += [tab_spec, tab_spec]
        args += list(rope_tabs)
        block_bytes += 4 * _nbytes((tm, LANES), F32)
    if gates:
        in_specs += [pl.BlockSpec(gate_w.shape, lambda i: (0, 0), pipeline_mode=_RESIDENT),
                     pl.BlockSpec((1, LANES), lambda i: (0, 0))]
        args += [gate_w, gate_b]
        block_bytes += _nbytes(gate_w.shape, BF16)

    out_shape, out_specs = [], []
    for _, w, _, row_cols, vt_cols in streams:
        out_shape.append(jax.ShapeDtypeStruct((T, row_cols), BF16))
        out_specs.append(pl.BlockSpec((tm, row_cols), lambda i: (i, 0)))
        block_bytes += 2 * _nbytes((tm, w.shape[1]), BF16)
        if vt_cols:
            n_groups = vt_cols // LANES
            out_shape.append(
                jax.ShapeDtypeStruct((n_batch, n_groups, n_seq_tiles, LANES, tm), BF16))
            out_specs.append(pl.BlockSpec(
                (1, n_groups, 1, LANES, tm),
                lambda i: (i // n_seq_tiles, 0, i % n_seq_tiles, 0, 0)))
    scratch = []
    if gates:
        cp_cols = (n_gates // 2) * LANES
        out_shape.append(jax.ShapeDtypeStruct((T, cp_cols), BF16))
        out_specs.append(pl.BlockSpec((tm, cp_cols), lambda i: (i, 0)))
        block_bytes += 2 * _nbytes((tm, cp_cols), BF16)
        scratch.append(pltpu.VMEM((1, LANES), F32))

    cfg = tuple((s[2], s[3], s[4]) for s in streams)
    return pl.pallas_call(
        functools.partial(_proj_kernel, streams=cfg, n_gates=n_gates, n_seq_tiles=n_seq_tiles),
        grid=(T // tm,),
        in_specs=in_specs,
        out_specs=out_specs,
        out_shape=out_shape,
        scratch_shapes=scratch,
        compiler_params=_params(1, block_bytes),
        name="proj_gates" if gates else ("proj_rope" if any_rope else "proj"),
    )(*args)


ONES_ROWS = 16
OFF_UNROLL = 7


def _flash_scratch(n_maps, n_tiles, tile, dv):
    scores = pltpu.VMEM((n_maps, tile, tile), F32)
    row = pltpu.VMEM((n_maps, 1, tile), F32)
    return [scores, row, scores, row,
            pltpu.VMEM((n_tiles, n_maps, 1, tile), F32),
            pltpu.VMEM((n_tiles, n_maps, dv + ONES_ROWS, tile), F32)]


def _flash_scratch_bytes(n_maps, n_tiles, tile, dv):
    return (2 * _nbytes((n_maps, tile, tile), F32) + 2 * _nbytes((n_maps, 8, tile), F32)
            + _nbytes((n_tiles, n_maps, 8, tile), F32)
            + _nbytes((n_tiles, n_maps, dv + ONES_ROWS, tile), F32))


def _causal_flash(n_tiles, tile, q_ops_of, k_block, vt_block, scratch):
    buf_a, buf_b, m_all, acc_all = scratch[:2], scratch[2:4], scratch[4], scratch[5]
    n_off = n_tiles * (n_tiles - 1) // 2
    assert n_tiles % 2 == 0 and n_off % 2 == 0
    ones = jnp.ones((ONES_ROWS, tile), BF16)
    kk = lax.broadcasted_iota(jnp.int32, (tile, tile), 0)
    qq = lax.broadcasted_iota(jnp.int32, (tile, tile), 1)
    causal = kk <= qq

    def issue(qi, j, buf):
        s_ref, bm_ref = buf
        for i, (ka, qa) in enumerate(zip(k_block(j), q_ops_of(qi))):
            st = lax.dot_general(ka, qa, _NT, preferred_element_type=F32)
            s_ref[i] = st
            bm_ref[i] = jnp.max(st, axis=0, keepdims=True)

    half = tile // 2

    def issue_diag(qi, buf):
        s_ref, _ = buf
        for i, (ka, qa) in enumerate(zip(k_block(qi), q_ops_of(qi))):
            s_ref[i, :half, :half] = lax.dot_general(ka[:half], qa[:half], _NT,
                                                     preferred_element_type=F32)
            s_ref[i, :, half:] = lax.dot_general(ka, qa[half:], _NT, preferred_element_type=F32)

    def process_diag(buf, qi):
        s_ref, _ = buf
        for i, vt in enumerate(vt_block(qi)):
            vta = jnp.concatenate([vt, ones], axis=0)
            for rows, cols in ((slice(0, half), slice(0, half)), (slice(0, tile), slice(half, tile))):
                st = jnp.where(causal[rows, cols], s_ref[i, rows, cols], -jnp.inf)
                m_new = jnp.max(st, axis=0, keepdims=True)
                p = jnp.exp2(st - m_new).astype(BF16)
                acc_all[qi, i, :, cols] = jnp.dot(vta[:, rows], p, preferred_element_type=F32)
                m_all[qi, i, :, cols] = m_new

    def process(buf, qi, j):
        s_ref, bm_ref = buf
        for i, vt in enumerate(vt_block(j)):
            m = m_all[qi, i]
            m_new = jnp.maximum(m, bm_ref[i])
            alpha = jnp.exp2(m - m_new)
            p = jnp.exp2(s_ref[i] - m_new).astype(BF16)
            pv = jnp.dot(jnp.concatenate([vt, ones], axis=0), p, preferred_element_type=F32)
            acc_all[qi, i] = alpha * acc_all[qi, i] + pv
            m_all[qi, i] = m_new

    issue_diag(0, buf_a)

    for qi in range(0, n_tiles, 2):
        issue_diag(qi + 1, buf_b)
        process_diag(buf_a, qi)
        if qi + 2 < n_tiles:
            issue_diag(qi + 2, buf_a)
        else:
            issue(1, 0, buf_a)
        process_diag(buf_b, qi + 1)

    def following(qi, j):
        wrap = j + 1 == qi
        return jnp.where(wrap, qi + 1, qi), jnp.where(wrap, 0, j + 1)

    def off_pair(qi, j):
        qi1, j1 = following(qi, j)
        issue(qi1, j1, buf_b)
        process(buf_a, qi, j)
        qi2, j2 = following(qi1, j1)
        past = qi2 >= n_tiles
        qi2, j2 = jnp.where(past, 1, qi2), jnp.where(past, 0, j2)
        issue(qi2, j2, buf_a)
        process(buf_b, qi1, j1)
        return qi2, j2

    def off_trip(u, carry):
        for _ in range(OFF_UNROLL):
            carry = off_pair(*carry)
        return carry

    assert n_off % (2 * OFF_UNROLL) == 0
    lax.fori_loop(0, n_off // (2 * OFF_UNROLL), off_trip, (jnp.int32(1), jnp.int32(0)))


def _fox_attn_kernel(q_ref, k_ref, vt_ref, cp_ref, o_ref, qa_ref, ka_ref, *flash_scratch, tile,
                     n_heads):
    seq = q_ref.shape[1]
    n_tiles = seq // tile
    lane = lax.broadcasted_iota(jnp.int32, (tile, LANES), 1)
    d = HEAD_DIM

    def lanes_of(base):
        return (lane == base) | (lane == base + n_heads) | (lane == base + 2 * n_heads)

    def build(ci, carry):
        r0 = pl.multiple_of(ci * tile, tile)
        qp = q_ref[0, pl.ds(r0, tile), :].astype(F32)
        kp = k_ref[0, pl.ds(r0, tile), :].astype(F32)
        cp = cp_ref[0, pl.ds(r0, tile), :].astype(F32)
        for h2 in range(2):
            base = (1 - h2) * d
            is_head = (lane >= h2 * d) & (lane < h2 * d + d)
            in_a, in_b = lanes_of(base), lanes_of(base + 1)
            kc = cp if h2 == 0 else pltpu.roll(cp, d - 1, 1)
            qc = pltpu.roll(cp, 1 if h2 == 0 else d, 1)
            k_extra = jnp.where(in_a, kc, jnp.where(in_b, 1.0, 0.0))
            q_extra = jnp.where(in_b, qc, jnp.where(in_a, -1.0, 0.0))
            ka_ref[h2, pl.ds(r0, tile), :] = jnp.where(is_head, kp, k_extra).astype(BF16)
            qa_ref[h2, pl.ds(r0, tile), :] = jnp.where(is_head, qp, q_extra).astype(BF16)
        return carry

    lax.fori_loop(0, n_tiles, build, 0)

    def q_ops_of(qi):
        q0 = pl.multiple_of(qi * tile, tile)
        return [qa_ref[h2, pl.ds(q0, tile), :] for h2 in range(2)]

    def k_block(j):
        start = pl.multiple_of(j * tile, tile)
        return [ka_ref[h2, pl.ds(start, tile), :] for h2 in range(2)]

    def vt_block(j):
        vb = vt_ref[0, 0, j]
        return [vb, vb]

    _causal_flash(n_tiles, tile, q_ops_of, k_block, vt_block, flash_scratch)
    acc_all = flash_scratch[5]

    def emit(qi, carry):
        a0 = acc_all[qi, 0]
        a1 = acc_all[qi, 1]
        yt = jnp.concatenate([a0[:d] / a0[2 * d:2 * d + 1], a1[d:2 * d] / a1[2 * d:2 * d + 1]],
                             axis=0)
        o_ref[0, 0, qi] = yt.astype(o_ref.dtype)
        return carry

    lax.fori_loop(0, n_tiles, emit, 0)


def _fox_attn(proj, vt, cp, *, n_heads, k_col):
    assert HEAD_DIM + 2 + 2 * n_heads <= LANES
    B, S, _ = proj.shape
    tile = ATTN_TILE
    n_tiles = S // tile
    n_pairs = n_heads // 2
    seq_block = _nbytes((S, LANES), BF16)
    block_bytes = (2 * 5 * seq_block + 4 * seq_block
                   + _flash_scratch_bytes(2, n_tiles, tile, 2 * HEAD_DIM))
    return pl.pallas_call(
        functools.partial(_fox_attn_kernel, tile=tile, n_heads=n_heads),
        grid=(B, n_pairs),
        in_specs=[
            pl.BlockSpec((1, S, LANES), lambda b, h: (b, 0, h)),
            pl.BlockSpec((1, S, LANES), lambda b, h: (b, 0, k_col + h)),
            pl.BlockSpec((1, 1, n_tiles, LANES, tile), lambda b, h: (b, h, 0, 0, 0)),
            pl.BlockSpec((1, S, LANES), lambda b, h: (b, 0, h)),
        ],
        out_specs=pl.BlockSpec((1, 1, n_tiles, LANES, tile), lambda b, h: (b, h, 0, 0, 0)),
        out_shape=jax.ShapeDtypeStruct((B, n_pairs, n_tiles, LANES, tile), BF16),
        scratch_shapes=[
            pltpu.VMEM((2, S, LANES), BF16),
            pltpu.VMEM((2, S, LANES), BF16),
        ] + _flash_scratch(2, n_tiles, tile, 2 * HEAD_DIM),
        compiler_params=_params(2, block_bytes),
        name="fox_attn",
    )(proj, proj, vt, cp)


def _diff_attn_kernel(q_ref, k_ref, vt_ref, lq1_ref, lk1_ref, lq2_ref, lk2_ref, g_ref, o_ref,
                      *flash_scratch, tile, lambda_init):
    seq = q_ref.shape[1]
    n_tiles = seq // tile
    lane = lax.broadcasted_iota(jnp.int32, (tile, LANES), 1)
    lam = (jnp.exp(jnp.sum(lq1_ref[...] * lk1_ref[...], axis=1, keepdims=True))
           - jnp.exp(jnp.sum(lq2_ref[...] * lk2_ref[...], axis=1, keepdims=True)) + lambda_init)

    dv = 2 * HEAD_DIM

    def q_ops_of(qi):
        qp = q_ref[0, pl.ds(pl.multiple_of(qi * tile, tile), tile), :]
        zero = jnp.zeros_like(qp)
        return [jnp.where(lane < HEAD_DIM, qp, zero), jnp.where(lane >= HEAD_DIM, qp, zero)]

    def k_block(j):
        kb = k_ref[0, pl.ds(pl.multiple_of(j * tile, tile), tile), :]
        return [kb, kb]

    def vt_block(j):
        vb = vt_ref[0, 0, j]
        return [vb, vb]

    _causal_flash(n_tiles, tile, q_ops_of, k_block, vt_block, flash_scratch)
    acc_all = flash_scratch[5]

    def emit(qi, carry):
        a1 = acc_all[qi, 0]
        a2 = acc_all[qi, 1]
        yt = a1[:dv] / a1[dv:dv + 1] - lam * (a2[:dv] / a2[dv:dv + 1])
        ms = jnp.mean(yt * yt, axis=0, keepdims=True)
        y = yt * lax.rsqrt(ms + NORM_EPS) * g_ref[...] * (1.0 - lambda_init)
        o_ref[0, 0, qi] = y.astype(o_ref.dtype)
        return carry

    lax.fori_loop(0, n_tiles, emit, 0)


def _diff_attn(qproj, k_sh, vt, lq1, lk1, lq2, lk2, subln_g, *, n_heads, lambda_init):
    B, S, _ = qproj.shape
    tile = ATTN_TILE
    n_tiles = S // tile
    vec = pl.BlockSpec((1, HEAD_DIM), lambda b, h: (0, 0))
    block_bytes = (2 * 4 * _nbytes((S, LANES), BF16)
                   + _flash_scratch_bytes(2, n_tiles, tile, 2 * HEAD_DIM))
    return pl.pallas_call(
        functools.partial(_diff_attn_kernel, tile=tile, lambda_init=lambda_init),
        grid=(B, n_heads),
        in_specs=[
            pl.BlockSpec((1, S, LANES), lambda b, h: (b, 0, h)),
            pl.BlockSpec((1, S, LANES), lambda b, h: (b, 0, h)),
            pl.BlockSpec((1, 1, n_tiles, LANES, tile), lambda b, h: (b, h, 0, 0, 0)),
            vec, vec, vec, vec,
            pl.BlockSpec((2 * HEAD_DIM, 1), lambda b, h: (0, 0)),
        ],
        out_specs=pl.BlockSpec((1, 1, n_tiles, LANES, tile), lambda b, h: (b, h, 0, 0, 0)),
        out_shape=jax.ShapeDtypeStruct((B, n_heads, n_tiles, LANES, tile), BF16),
        scratch_shapes=_flash_scratch(2, n_tiles, tile, 2 * HEAD_DIM),
        compiler_params=_params(2, block_bytes),
        name="diff_attn",
    )(qproj, k_sh, vt, lq1.reshape(1, -1), lk1.reshape(1, -1), lq2.reshape(1, -1),
      lk2.reshape(1, -1), subln_g.reshape(-1, 1))


def _mix_out(x, yt_ref, mq_ref, mk_ref, mv_ref, wo_ref):
    tm = x.shape[0]
    mw = mq_ref.shape[1]
    n_groups, n_sub = yt_ref.shape[1], yt_ref.shape[2]
    y_width = n_groups * LANES
    y = jnp.concatenate(
        [jnp.concatenate([yt_ref[0, g, t].astype(F32).T.astype(BF16) for t in range(n_sub)], axis=0)
         for g in range(n_groups)], axis=1)
    mq = mq_ref[...]
    mk = mk_ref[0]
    mv = mv_ref[0]
    q_head = lax.broadcasted_iota(jnp.int32, (tm, mw), 1) // HEAD_DIM
    v_head = lax.broadcasted_iota(jnp.int32, mv.shape, 1) // HEAD_DIM
    ymem = jnp.zeros((tm, mw), F32)
    for h in range(mw // HEAD_DIM):
        qh = jnp.where(q_head == h, mq, jnp.zeros_like(mq))
        s = lax.dot_general(qh, mk, _NT, preferred_element_type=F32)
        p = jnp.exp2(s - jnp.max(s, axis=1, keepdims=True))
        l = jnp.sum(p, axis=1, keepdims=True)
        vh = jnp.where(v_head == h, mv, jnp.zeros_like(mv))
        ymem = ymem + jnp.dot(p.astype(BF16), vh, preferred_element_type=F32) / l
    acc = jnp.dot(y, wo_ref[0:y_width, :], preferred_element_type=F32)
    acc = acc + jnp.dot(ymem.astype(BF16), wo_ref[y_width:y_width + mw, :],
                        preferred_element_type=F32)
    return x + acc


def _post_kernel(*refs, final, n_chunks):
    x_ref, y_ref, mq_ref, mk_ref, mv_ref, wo_ref, g_ref, wgu_ref, wd_ref = refs[:9]
    wo_ref, wgu_ref, wd_ref = wo_ref.at[0], wgu_ref.at[0], wd_ref.at[0]
    if final:
        fg_ref, o_ref, hn_ref, acc_ref = refs[9:]
    else:
        o_ref, hn_ref, acc_ref = refs[9:]
    x = _mix_out(x_ref[...], y_ref, mq_ref, mk_ref, mv_ref, wo_ref)
    hn_ref[...] = _rms(x, g_ref[...]).astype(BF16)
    acc_ref[...] = x

    d_ff = wd_ref.shape[0]
    tf = d_ff // n_chunks
    for k in range(n_chunks):
        h = hn_ref[...]
        gate = jnp.dot(h, wgu_ref[:, k * tf:(k + 1) * tf], preferred_element_type=F32)
        up = jnp.dot(h, wgu_ref[:, d_ff + k * tf:d_ff + (k + 1) * tf],
                     preferred_element_type=F32)
        a = gate * jax.nn.sigmoid(gate) * up
        acc_ref[...] += jnp.dot(a.astype(BF16), wd_ref[k * tf:(k + 1) * tf, :],
                                preferred_element_type=F32)
    out = acc_ref[...]
    if final:
        out = _rms(out, fg_ref[...])
    o_ref[...] = out


def _post(x2d, yt, proj2d, mq_col, memkv, wo, g, wgu, wd, *, layer, seq, final_g=None):
    T, D = x2d.shape
    n_groups, tile = yt.shape[1], yt.shape[4]
    yw = n_groups * LANES
    n_mem, mw = memkv.shape[1], memkv.shape[2] // 2
    d_ff = wd.shape[1]

    def slab(w):
        return pl.BlockSpec((1,) + w.shape[1:], lambda i: (layer, 0, 0), pipeline_mode=_RESIDENT)

    tm = FFN_TM
    assert T % tm == 0 and seq % tm == 0 and tm % tile == 0 and d_ff % FFN_TF == 0
    n_chunks = d_ff // FFN_TF
    n_seq_tiles = seq // tm
    final = final_g is not None
    in_specs = [
        pl.BlockSpec((tm, D), lambda i: (i, 0)),
        pl.BlockSpec((1, n_groups, tm // tile, LANES, tile),
                     lambda i: (i // n_seq_tiles, 0, i % n_seq_tiles, 0, 0)),
        pl.BlockSpec((tm, mw), lambda i: (i, mq_col)),
        pl.BlockSpec((1, n_mem, mw), lambda i: (i // n_seq_tiles, 0, 0)),
        pl.BlockSpec((1, n_mem, mw), lambda i: (i // n_seq_tiles, 0, 1)),
        slab(wo),
        pl.BlockSpec((1, D), lambda i: (0, 0)),
        slab(wgu),
        slab(wd),
    ]
    args = [x2d, yt, proj2d, memkv, memkv, wo, g.reshape(1, D), wgu, wd]
    if final:
        in_specs.append(pl.BlockSpec((1, D), lambda i: (0, 0)))
        args.append(final_g.reshape(1, D))
    block_bytes = (5 * _nbytes((tm, D), F32) + _nbytes((tm, D), BF16)
                   + 2 * _nbytes((tm, yw + mw), BF16) + 4 * _nbytes((n_mem, mw), BF16)
                   + _nbytes(wo.shape[1:], BF16) + _nbytes(wgu.shape[1:], BF16)
                   + _nbytes(wd.shape[1:], BF16))
    return pl.pallas_call(
        functools.partial(_post_kernel, final=final, n_chunks=n_chunks),
        grid=(T // tm,),
        in_specs=in_specs,
        out_specs=pl.BlockSpec((tm, D), lambda i: (i, 0)),
        out_shape=jax.ShapeDtypeStruct((T, D), F32),
        scratch_shapes=[pltpu.VMEM((tm, D), BF16), pltpu.VMEM((tm, D), F32)],
        compiler_params=_params(1, block_bytes),
        name="post_final" if final else "post",
    )(*args)


def _rope_tables(seq):
    half = HEAD_DIM // 2
    inv_freq = jnp.power(ROPE_THETA, -jnp.arange(half, dtype=F32) * (2.0 / HEAD_DIM))
    ang = jnp.arange(seq, dtype=F32)[:, None] * inv_freq[None, :]
    cos, sin = jnp.cos(ang), jnp.sin(ang)
    reps = LANES // HEAD_DIM
    cos_t = jnp.tile(jnp.concatenate([cos, cos], axis=-1), (1, reps))
    sin_t = jnp.tile(jnp.concatenate([-sin, sin], axis=-1), (1, reps))
    return cos_t, sin_t


def kernel(x, mem, attn_norm_g, mem_norm_g, w_mem_kv, w_out, ffn_norm_g, w_gate_up, w_down,
           a_w_in, a_b_f, b_w_in, b_lambda_q1, b_lambda_k1, b_lambda_q2, b_lambda_k2,
           b_subln_g, kv_norm_g, w_kv_shared, final_norm_g):
    B, S, D = x.shape
    depth = attn_norm_g.shape[0]
    n_a = a_w_in.shape[0]
    n_mem = mem.shape[1]
    mem_w = w_mem_kv.shape[2] // 2
    n_fox = a_b_f.shape[1]
    fox_w = n_fox * HEAD_DIM
    diff_w = b_w_in.shape[1] - mem_w
    n_diff = diff_w // (2 * HEAD_DIM)
    scale = HEAD_DIM ** -0.5 * LOG2E
    T = B * S

    x2d = x.reshape(T, D)
    mem2d = mem.reshape(B * n_mem, D)
    rope_tabs = _rope_tables(S)
    k_sh = vt_sh = None
    wo_all = w_out.astype(BF16)
    wgu_all = w_gate_up.astype(BF16)
    wd_all = w_down.astype(BF16)

    for layer in range(depth):
        (memkv,) = _proj(mem2d, [(mem_norm_g[layer], w_mem_kv[layer].astype(BF16),
                                  0, 2 * mem_w, 0)], seq=n_mem)
        memkv = memkv.reshape(B, n_mem, 2 * mem_w)
        if layer < n_a:
            w_in = a_w_in[layer]
            w_main = jnp.concatenate(
                [w_in[:, :fox_w] * scale, w_in[:, fox_w:2 * fox_w],
                 w_in[:, 3 * fox_w + n_fox:] * scale, w_in[:, 2 * fox_w:3 * fox_w]],
                axis=1).astype(BF16)
            gw = jnp.pad(w_in[:, 3 * fox_w:3 * fox_w + n_fox], ((0, 0), (0, LANES - n_fox)))
            gw_hi = gw.astype(BF16)
            gw_lo = (gw - gw_hi.astype(F32)).astype(BF16)
            gate_w = jnp.concatenate([gw_hi, gw_lo], axis=1)
            gate_b = jnp.pad(a_b_f[layer], (0, LANES - n_fox)).reshape(1, LANES)
            row_cols = 2 * fox_w + mem_w
            proj, vt, cp = _proj(x2d, [(attn_norm_g[layer], w_main, 0, row_cols, fox_w)], seq=S,
                                 gate_w=gate_w, gate_b=gate_b, n_gates=n_fox)
            y = _fox_attn(proj.reshape(B, S, row_cols), vt, cp.reshape(B, S, -1),
                          n_heads=n_fox, k_col=fox_w // LANES)
            mq_col = 2 * fox_w // mem_w
        else:
            j = layer - n_a
            streams = [(attn_norm_g[layer], (b_w_in[j] * scale).astype(BF16),
                        diff_w, diff_w + mem_w, 0)]
            if layer == n_a:
                streams.append((kv_norm_g, w_kv_shared.astype(BF16), diff_w, diff_w,
                                w_kv_shared.shape[1] - diff_w))
                proj, k_sh, vt_sh = _proj(x2d, streams, seq=S, rope_tabs=rope_tabs)
                k_sh = k_sh.reshape(B, S, diff_w)
            else:
                (proj,) = _proj(x2d, streams, seq=S, rope_tabs=rope_tabs)
            lambda_init = 0.8 - 0.6 * math.exp(-0.3 * layer)
            y = _diff_attn(proj.reshape(B, S, -1), k_sh, vt_sh, b_lambda_q1[j], b_lambda_k1[j],
                           b_lambda_q2[j], b_lambda_k2[j], b_subln_g[j], n_heads=n_diff,
                           lambda_init=lambda_init)
            mq_col = diff_w // mem_w
        last = layer == depth - 1
        x2d = _post(x2d, y, proj, mq_col, memkv, wo_all, ffn_norm_g[layer],
                    wgu_all, wd_all, layer=layer, seq=S,
                    final_g=final_norm_g if last else None)
    return x2d.reshape(B, S, D)
```

```python
import functools
import math

import jax
import jax.numpy as jnp
from jax import lax
from jax.experimental import pallas as pl
from jax.experimental.pallas import tpu as pltpu

HEAD_DIM = 64
ROPE_THETA = 10000.0
NORM_EPS = 1e-6
LANES = 128
ATTN_TILE = 512
PROJ_TM = ATTN_TILE
FFN_TM = 1024
FFN_TF = 256
VMEM_CAP = 56 * 1024 * 1024
VMEM_TEMPS = 16 * 1024 * 1024
LOG2E = math.log2(math.e)

F32 = jnp.float32
BF16 = jnp.bfloat16
_NT = (((1,), (1,)), ((), ()))
_RESIDENT = pl.Buffered(1)


def _nbytes(shape, dtype):
    return math.prod(shape) * jnp.dtype(dtype).itemsize


def _params(n_grid, block_bytes):
    limit = min(VMEM_CAP, block_bytes + VMEM_TEMPS)
    return pltpu.CompilerParams(dimension_semantics=("arbitrary",) * n_grid,
                                vmem_limit_bytes=limit)


def _rms(x, g):
    ms = jnp.mean(x * x, axis=-1, keepdims=True)
    return x * lax.rsqrt(ms + NORM_EPS) * g


def _split_bf16(v, n):
    pieces = []
    for _ in range(n - 1):
        p = v.astype(BF16)
        pieces.append(p)
        v = v - p.astype(F32)
    pieces.append(v.astype(BF16))
    return pieces


def _proj_kernel(*refs, streams, n_gates, n_seq_tiles):
    gates = n_gates > 0
    it = iter(refs)
    x_ref = next(it)
    gw = [(next(it), next(it)) for _ in streams]
    any_rope = any(s[0] for s in streams)
    if any_rope:
        cos_ref, sin_ref = next(it), next(it)
    if gates:
        wf_ref, bf_ref = next(it), next(it)
    outs = []
    for (_, _, vt_cols) in streams:
        row_ref = next(it)
        outs.append((row_ref, next(it) if vt_cols else None))
    if gates:
        qa_ref, ka_ref, carry_ref = next(it), next(it), next(it)

    tm = x_ref.shape[0]
    x = x_ref[...]
    xn = x * lax.rsqrt(jnp.mean(x * x, axis=-1, keepdims=True) + NORM_EPS)
    if any_rope:
        cos = cos_ref[...]
        sin = sin_ref[...]
        lane = lax.broadcasted_iota(jnp.int32, (tm, LANES), 1)
        first_half = (lane % HEAD_DIM) < (HEAD_DIM // 2)

    def gate_pieces(hn32, hn):
        lo = (hn32 - hn.astype(F32)).astype(BF16)
        wf = wf_ref[...]
        zz = jnp.dot(hn, wf, preferred_element_type=F32)
        z = (zz[:, :LANES] + zz[:, LANES:]
             + jnp.dot(lo, wf[:, :LANES], preferred_element_type=F32) + bf_ref[...])
        lf = jnp.minimum(z, 0.0) - jnp.log1p(jnp.exp(-jnp.abs(z)))
        l_hi, l_mid, l_lo = _split_bf16(lf, 3)
        row = lax.broadcasted_iota(jnp.int32, (tm, tm), 0)
        col = lax.broadcasted_iota(jnp.int32, (tm, tm), 1)
        tri = (row >= col).astype(BF16)
        cc = jnp.dot(tri, jnp.concatenate([l_hi, l_mid], axis=1), preferred_element_type=F32)
        cs = cc[:, :LANES] + cc[:, LANES:] + jnp.dot(tri, l_lo, preferred_element_type=F32)

        @pl.when((pl.program_id(0) % n_seq_tiles) == 0)
        def _():
            carry_ref[...] = jnp.zeros_like(carry_ref)

        c = cs + carry_ref[...]
        carry_ref[...] = c[tm - 1:tm, :]
        p_hi, p_mid, p_lo = [p.astype(F32) for p in _split_bf16(c * LOG2E, 3)]
        lane_c = lax.broadcasted_iota(jnp.int32, (tm, LANES), 1)
        return jnp.where(lane_c < n_gates, p_hi, jnp.where(
            lane_c < 2 * n_gates, pltpu.roll(p_mid, n_gates, 1),
            pltpu.roll(p_lo, 2 * n_gates, 1)))

    for si, ((rope_cols, row_cols, vt_cols), (g_ref, w_ref), (row_ref, vt_ref)) in enumerate(
            zip(streams, gw, outs)):
        hn32 = xn * g_ref[...]
        hn = hn32.astype(BF16)
        if gates and si == 0:
            packed = gate_pieces(hn32, hn)
        res = jnp.dot(hn, w_ref[...], preferred_element_type=F32)
        for c0 in range(0, rope_cols, LANES):
            t = res[:, c0:c0 + LANES]
            up = pltpu.roll(t, HEAD_DIM // 2, 1)
            dn = pltpu.roll(t, LANES - HEAD_DIM // 2, 1)
            sw = jnp.where(first_half, dn, up)
            row_ref[:, c0:c0 + LANES] = (t * cos + sw * sin).astype(BF16)
        row_from = 2 * n_gates * HEAD_DIM if (gates and si == 0) else 0
        plain = max(rope_cols, row_from)
        if row_cols > plain:
            row_ref[:, plain - row_from:row_cols - row_from] = res[:, plain:row_cols].astype(BF16)
        for gi in range(vt_cols // LANES):
            c0 = row_cols + gi * LANES
            vt_ref[0, gi, 0] = res[:, c0:c0 + LANES].T.astype(BF16)

        if gates and si == 0:
            d = HEAD_DIM
            lane_c = lax.broadcasted_iota(jnp.int32, (tm, LANES), 1)

            def lanes_of(base):
                return ((lane_c == base) | (lane_c == base + n_gates)
                        | (lane_c == base + 2 * n_gates))

            for p in range(n_gates // 2):
                cp = pltpu.roll(packed, d - 2 * p, 1)
                qp = res[:, p * LANES:(p + 1) * LANES]
                kp = res[:, n_gates * d + p * LANES:n_gates * d + (p + 1) * LANES]
                for h2 in range(2):
                    base = (1 - h2) * d
                    is_head = (lane_c >= h2 * d) & (lane_c < h2 * d + d)
                    in_a, in_b = lanes_of(base), lanes_of(base + 1)
                    kc = cp if h2 == 0 else pltpu.roll(cp, d - 1, 1)
                    qc = pltpu.roll(cp, 1 if h2 == 0 else d, 1)
                    k_extra = jnp.where(in_a, kc, jnp.where(in_b, 1.0, 0.0))
                    q_extra = jnp.where(in_b, qc, jnp.where(in_a, -1.0, 0.0))
                    c0 = (2 * p + h2) * LANES
                    ka_ref[:, c0:c0 + LANES] = jnp.where(is_head, kp, k_extra).astype(BF16)
                    qa_ref[:, c0:c0 + LANES] = jnp.where(is_head, qp, q_extra).astype(BF16)


def _proj(x2d, streams, *, seq, rope_tabs=None, gate_w=None, gate_b=None, n_gates=0):
    T, D = x2d.shape
    tm = min(PROJ_TM, T)
    assert T % tm == 0
    gates = n_gates > 0
    assert n_gates % 2 == 0 and HEAD_DIM + 2 + 2 * n_gates <= LANES
    any_rope = any(s[2] for s in streams)
    any_vt = any(s[4] for s in streams)
    assert seq % tm == 0 or not (gates or any_rope or any_vt)
    n_seq_tiles = max(1, seq // tm)
    n_batch = T // seq

    in_specs = [pl.BlockSpec((tm, D), lambda i: (i, 0))]
    args = [x2d]
    block_bytes = 2 * _nbytes((tm, D), F32)
    for g, w, _, _, _ in streams:
        in_specs += [pl.BlockSpec((1, D), lambda i: (0, 0)),
                     pl.BlockSpec(w.shape, lambda i: (0, 0), pipeline_mode=_RESIDENT)]
        args += [g.reshape(1, D), w]
        block_bytes += _nbytes(w.shape, BF16) + _nbytes((tm, w.shape[1]), F32)
    if any_rope:
        tab_spec = pl.BlockSpec((tm, LANES), lambda i: (i % n_seq_tiles, 0))
        in_specs += [tab_spec, tab_spec]
        args += list(rope_tabs)
        block_bytes += 4 * _nbytes((tm, LANES), F32)
    if gates:
        in_specs += [pl.BlockSpec(gate_w.shape, lambda i: (0, 0), pipeline_mode=_RESIDENT),
                     pl.BlockSpec((1, LANES), lambda i: (0, 0))]
        args += [gate_w, gate_b]
        block_bytes += _nbytes(gate_w.shape, BF16)

    out_shape, out_specs = [], []
    for si, (_, w, _, row_cols, vt_cols) in enumerate(streams):
        if gates and si == 0:
            row_cols -= 2 * n_gates * HEAD_DIM
        out_shape.append(jax.ShapeDtypeStruct((T, row_cols), BF16))
        out_specs.append(pl.BlockSpec((tm, row_cols), lambda i: (i, 0)))
        block_bytes += 2 * _nbytes((tm, w.shape[1]), BF16)
        if vt_cols:
            n_groups = vt_cols // LANES
            out_shape.append(
                jax.ShapeDtypeStruct((n_batch, n_groups, n_seq_tiles, LANES, tm), BF16))
            out_specs.append(pl.BlockSpec(
                (1, n_groups, 1, LANES, tm),
                lambda i: (i // n_seq_tiles, 0, i % n_seq_tiles, 0, 0)))
    scratch = []
    if gates:
        aug_cols = n_gates * LANES
        for _ in range(2):
            out_shape.append(jax.ShapeDtypeStruct((T, aug_cols), BF16))
            out_specs.append(pl.BlockSpec((tm, aug_cols), lambda i: (i, 0)))
            block_bytes += 2 * _nbytes((tm, aug_cols), BF16)
        scratch.append(pltpu.VMEM((1, LANES), F32))

    cfg = tuple((s[2], s[3], s[4]) for s in streams)
    return pl.pallas_call(
        functools.partial(_proj_kernel, streams=cfg, n_gates=n_gates, n_seq_tiles=n_seq_tiles),
        grid=(T // tm,),
        in_specs=in_specs,
        out_specs=out_specs,
        out_shape=out_shape,
        scratch_shapes=scratch,
        compiler_params=_params(1, block_bytes),
        name="proj_gates" if gates else ("proj_rope" if any_rope else "proj"),
    )(*args)


ONES_ROWS = 16
OFF_UNROLL = 7


def _flash_scratch(n_maps, n_tiles, tile, dv):
    scores = pltpu.VMEM((n_maps, tile, tile), F32)
    row = pltpu.VMEM((n_maps, 1, tile), F32)
    return [scores, row, scores, row,
            pltpu.VMEM((n_tiles, n_maps, 1, tile), F32),
            pltpu.VMEM((n_tiles, n_maps, dv + ONES_ROWS, tile), F32)]


def _flash_scratch_bytes(n_maps, n_tiles, tile, dv):
    return (2 * _nbytes((n_maps, tile, tile), F32) + 2 * _nbytes((n_maps, 8, tile), F32)
            + _nbytes((n_tiles, n_maps, 8, tile), F32)
            + _nbytes((n_tiles, n_maps, dv + ONES_ROWS, tile), F32))


def _causal_flash(n_tiles, tile, q_ops_of, k_block, vt_block, scratch):
    buf_a, buf_b, m_all, acc_all = scratch[:2], scratch[2:4], scratch[4], scratch[5]
    n_off = n_tiles * (n_tiles - 1) // 2
    assert n_tiles % 2 == 0 and n_off % 2 == 0
    ones = jnp.ones((ONES_ROWS, tile), BF16)
    kk = lax.broadcasted_iota(jnp.int32, (tile, tile), 0)
    qq = lax.broadcasted_iota(jnp.int32, (tile, tile), 1)
    causal = kk <= qq

    def issue(qi, j, buf):
        s_ref, bm_ref = buf
        for i, (ka, qa) in enumerate(zip(k_block(j), q_ops_of(qi))):
            st = lax.dot_general(ka, qa, _NT, preferred_element_type=F32)
            s_ref[i] = st
            bm_ref[i] = jnp.max(st, axis=0, keepdims=True)

    half = tile // 2

    def issue_diag(qi, buf):
        s_ref, _ = buf
        for i, (ka, qa) in enumerate(zip(k_block(qi), q_ops_of(qi))):
            s_ref[i, :half, :half] = lax.dot_general(ka[:half], qa[:half], _NT,
                                                     preferred_element_type=F32)
            s_ref[i, :, half:] = lax.dot_general(ka, qa[half:], _NT, preferred_element_type=F32)

    def process_diag(buf, qi):
        s_ref, _ = buf
        for i, vt in enumerate(vt_block(qi)):
            vta = jnp.concatenate([vt, ones], axis=0)
            for rows, cols in ((slice(0, half), slice(0, half)), (slice(0, tile), slice(half, tile))):
                st = jnp.where(causal[rows, cols], s_ref[i, rows, cols], -jnp.inf)
                m_new = jnp.max(st, axis=0, keepdims=True)
                p = jnp.exp2(st - m_new).astype(BF16)
                acc_all[qi, i, :, cols] = jnp.dot(vta[:, rows], p, preferred_element_type=F32)
                m_all[qi, i, :, cols] = m_new

    def process(buf, qi, j):
        s_ref, bm_ref = buf
        for i, vt in enumerate(vt_block(j)):
            m = m_all[qi, i]
            m_new = jnp.maximum(m, bm_ref[i])
            alpha = jnp.exp2(m - m_new)
            p = jnp.exp2(s_ref[i] - m_new).astype(BF16)
            pv = jnp.dot(jnp.concatenate([vt, ones], axis=0), p, preferred_element_type=F32)
            acc_all[qi, i] = alpha * acc_all[qi, i] + pv
            m_all[qi, i] = m_new

    issue_diag(0, buf_a)

    for qi in range(0, n_tiles, 2):
        issue_diag(qi + 1, buf_b)
        process_diag(buf_a, qi)
        if qi + 2 < n_tiles:
            issue_diag(qi + 2, buf_a)
        else:
            issue(1, 0, buf_a)
        process_diag(buf_b, qi + 1)

    def following(qi, j):
        wrap = j + 1 == qi
        return jnp.where(wrap, qi + 1, qi), jnp.where(wrap, 0, j + 1)

    def off_pair(qi, j):
        qi1, j1 = following(qi, j)
        issue(qi1, j1, buf_b)
        process(buf_a, qi, j)
        qi2, j2 = following(qi1, j1)
        past = qi2 >= n_tiles
        qi2, j2 = jnp.where(past, 1, qi2), jnp.where(past, 0, j2)
        issue(qi2, j2, buf_a)
        process(buf_b, qi1, j1)
        return qi2, j2

    def off_trip(u, carry):
        for _ in range(OFF_UNROLL):
            carry = off_pair(*carry)
        return carry

    assert n_off % (2 * OFF_UNROLL) == 0
    lax.fori_loop(0, n_off // (2 * OFF_UNROLL), off_trip, (jnp.int32(1), jnp.int32(0)))


def _fox_attn_kernel(qa_ref, ka_ref, vt_ref, o_ref, *flash_scratch, tile):
    seq = qa_ref.shape[1]
    n_tiles = seq // tile
    d = HEAD_DIM

    def q_ops_of(qi):
        q0 = pl.multiple_of(qi * tile, tile)
        return [qa_ref[0, pl.ds(q0, tile), h2 * LANES:(h2 + 1) * LANES] for h2 in range(2)]

    def k_block(j):
        start = pl.multiple_of(j * tile, tile)
        return [ka_ref[0, pl.ds(start, tile), h2 * LANES:(h2 + 1) * LANES] for h2 in range(2)]

    def vt_block(j):
        vb = vt_ref[0, 0, j]
        return [vb, vb]

    _causal_flash(n_tiles, tile, q_ops_of, k_block, vt_block, flash_scratch)
    acc_all = flash_scratch[5]

    def emit(qi, carry):
        a0 = acc_all[qi, 0]
        a1 = acc_all[qi, 1]
        yt = jnp.concatenate([a0[:d] / a0[2 * d:2 * d + 1], a1[d:2 * d] / a1[2 * d:2 * d + 1]],
                             axis=0)
        o_ref[0, 0, qi] = yt.astype(o_ref.dtype)
        return carry

    lax.fori_loop(0, n_tiles, emit, 0)


def _fox_attn(qa, ka, vt):
    B, S, width = qa.shape
    tile = ATTN_TILE
    n_tiles = S // tile
    n_pairs = width // (2 * LANES)
    pair_block = pl.BlockSpec((1, S, 2 * LANES), lambda b, h: (b, 0, h))
    tiles_block = pl.BlockSpec((1, 1, n_tiles, LANES, tile), lambda b, h: (b, h, 0, 0, 0))
    block_bytes = (2 * 6 * _nbytes((S, LANES), BF16)
                   + _flash_scratch_bytes(2, n_tiles, tile, 2 * HEAD_DIM))
    return pl.pallas_call(
        functools.partial(_fox_attn_kernel, tile=tile),
        grid=(B, n_pairs),
        in_specs=[pair_block, pair_block, tiles_block],
        out_specs=tiles_block,
        out_shape=jax.ShapeDtypeStruct((B, n_pairs, n_tiles, LANES, tile), BF16),
        scratch_shapes=_flash_scratch(2, n_tiles, tile, 2 * HEAD_DIM),
        compiler_params=_params(2, block_bytes),
        name="fox_attn",
    )(qa, ka, vt)


def _diff_attn_kernel(q_ref, k_ref, vt_ref, lq1_ref, lk1_ref, lq2_ref, lk2_ref, g_ref, o_ref,
                      *flash_scratch, tile, lambda_init):
    seq = q_ref.shape[1]
    n_tiles = seq // tile
    lane = lax.broadcasted_iota(jnp.int32, (tile, LANES), 1)
    lam = (jnp.exp(jnp.sum(lq1_ref[...] * lk1_ref[...], axis=1, keepdims=True))
           - jnp.exp(jnp.sum(lq2_ref[...] * lk2_ref[...], axis=1, keepdims=True)) + lambda_init)

    dv = 2 * HEAD_DIM

    def q_ops_of(qi):
        qp = q_ref[0, pl.ds(pl.multiple_of(qi * tile, tile), tile), :]
        zero = jnp.zeros_like(qp)
        return [jnp.where(lane < HEAD_DIM, qp, zero), jnp.where(lane >= HEAD_DIM, qp, zero)]

    def k_block(j):
        kb = k_ref[0, pl.ds(pl.multiple_of(j * tile, tile), tile), :]
        return [kb, kb]

    def vt_block(j):
        vb = vt_ref[0, 0, j]
        return [vb, vb]

    _causal_flash(n_tiles, tile, q_ops_of, k_block, vt_block, flash_scratch)
    acc_all = flash_scratch[5]

    def emit(qi, carry):
        a1 = acc_all[qi, 0]
        a2 = acc_all[qi, 1]
        yt = a1[:dv] / a1[dv:dv + 1] - lam * (a2[:dv] / a2[dv:dv + 1])
        ms = jnp.mean(yt * yt, axis=0, keepdims=True)
        y = yt * lax.rsqrt(ms + NORM_EPS) * g_ref[...] * (1.0 - lambda_init)
        o_ref[0, 0, qi] = y.astype(o_ref.dtype)
        return carry

    lax.fori_loop(0, n_tiles, emit, 0)


def _diff_attn(qproj, k_sh, vt, lq1, lk1, lq2, lk2, subln_g, *, n_heads, lambda_init):
    B, S, _ = qproj.shape
    tile = ATTN_TILE
    n_tiles = S // tile
    vec = pl.BlockSpec((1, HEAD_DIM), lambda b, h: (0, 0))
    block_bytes = (2 * 4 * _nbytes((S, LANES), BF16)
                   + _flash_scratch_bytes(2, n_tiles, tile, 2 * HEAD_DIM))
    return pl.pallas_call(
        functools.partial(_diff_attn_kernel, tile=tile, lambda_init=lambda_init),
        grid=(B, n_heads),
        in_specs=[
            pl.BlockSpec((1, S, LANES), lambda b, h: (b, 0, h)),
            pl.BlockSpec((1, S, LANES), lambda b, h: (b, 0, h)),
            pl.BlockSpec((1, 1, n_tiles, LANES, tile), lambda b, h: (b, h, 0, 0, 0)),
            vec, vec, vec, vec,
            pl.BlockSpec((2 * HEAD_DIM, 1), lambda b, h: (0, 0)),
        ],
        out_specs=pl.BlockSpec((1, 1, n_tiles, LANES, tile), lambda b, h: (b, h, 0, 0, 0)),
        out_shape=jax.ShapeDtypeStruct((B, n_heads, n_tiles, LANES, tile), BF16),
        scratch_shapes=_flash_scratch(2, n_tiles, tile, 2 * HEAD_DIM),
        compiler_params=_params(2, block_bytes),
        name="diff_attn",
    )(qproj, k_sh, vt, lq1.reshape(1, -1), lk1.reshape(1, -1), lq2.reshape(1, -1),
      lk2.reshape(1, -1), subln_g.reshape(-1, 1))


def _mix_out(x, yt_ref, mq_ref, mk_ref, mv_ref, wo_ref):
    tm = x.shape[0]
    mw = mq_ref.shape[1]
    n_groups, n_sub = yt_ref.shape[1], yt_ref.shape[2]
    y_width = n_groups * LANES
    y = jnp.concatenate(
        [jnp.concatenate([yt_ref[0, g, t].astype(F32).T.astype(BF16) for t in range(n_sub)], axis=0)
         for g in range(n_groups)], axis=1)
    mq = mq_ref[...]
    mk = mk_ref[0]
    mv = mv_ref[0]
    q_head = lax.broadcasted_iota(jnp.int32, (tm, mw), 1) // HEAD_DIM
    v_head = lax.broadcasted_iota(jnp.int32, mv.shape, 1) // HEAD_DIM
    ymem = jnp.zeros((tm, mw), F32)
    for h in range(mw // HEAD_DIM):
        qh = jnp.where(q_head == h, mq, jnp.zeros_like(mq))
        s = lax.dot_general(qh, mk, _NT, preferred_element_type=F32)
        p = jnp.exp2(s - jnp.max(s, axis=1, keepdims=True))
        l = jnp.sum(p, axis=1, keepdims=True)
        vh = jnp.where(v_head == h, mv, jnp.zeros_like(mv))
        ymem = ymem + jnp.dot(p.astype(BF16), vh, preferred_element_type=F32) / l
    acc = jnp.dot(y, wo_ref[0:y_width, :], preferred_element_type=F32)
    acc = acc + jnp.dot(ymem.astype(BF16), wo_ref[y_width:y_width + mw, :],
                        preferred_element_type=F32)
    return x + acc


def _post_kernel(*refs, final, n_chunks):
    x_ref, y_ref, mq_ref, mk_ref, mv_ref, wo_ref, g_ref, wgu_ref, wd_ref = refs[:9]
    wo_ref, wgu_ref, wd_ref = wo_ref.at[0], wgu_ref.at[0], wd_ref.at[0]
    if final:
        fg_ref, o_ref, hn_ref, acc_ref = refs[9:]
    else:
        o_ref, hn_ref, acc_ref = refs[9:]
    x = _mix_out(x_ref[...], y_ref, mq_ref, mk_ref, mv_ref, wo_ref)
    hn_ref[...] = _rms(x, g_ref[...]).astype(BF16)
    acc_ref[...] = x

    d_ff = wd_ref.shape[0]
    tf = d_ff // n_chunks
    for k in range(n_chunks):
        h = hn_ref[...]
        gate = jnp.dot(h, wgu_ref[:, k * tf:(k + 1) * tf], preferred_element_type=F32)
        up = jnp.dot(h, wgu_ref[:, d_ff + k * tf:d_ff + (k + 1) * tf],
                     preferred_element_type=F32)
        a = gate * jax.nn.sigmoid(gate) * up
        acc_ref[...] += jnp.dot(a.astype(BF16), wd_ref[k * tf:(k + 1) * tf, :],
                                preferred_element_type=F32)
    out = acc_ref[...]
    if final:
        out = _rms(out, fg_ref[...])
    o_ref[...] = out


def _post(x2d, yt, proj2d, mq_col, memkv, wo, g, wgu, wd, *, layer, seq, final_g=None):
    T, D = x2d.shape
    n_groups, tile = yt.shape[1], yt.shape[4]
    yw = n_groups * LANES
    n_mem, mw = memkv.shape[1], memkv.shape[2] // 2
    d_ff = wd.shape[1]

    def slab(w):
        return pl.BlockSpec((1,) + w.shape[1:], lambda i: (layer, 0, 0), pipeline_mode=_RESIDENT)

    tm = FFN_TM
    assert T % tm == 0 and seq % tm == 0 and tm % tile == 0 and d_ff % FFN_TF == 0
    n_chunks = d_ff // FFN_TF
    n_seq_tiles = seq // tm
    final = final_g is not None
    in_specs = [
        pl.BlockSpec((tm, D), lambda i: (i, 0)),
        pl.BlockSpec((1, n_groups, tm // tile, LANES, tile),
                     lambda i: (i // n_seq_tiles, 0, i % n_seq_tiles, 0, 0)),
        pl.BlockSpec((tm, mw), lambda i: (i, mq_col)),
        pl.BlockSpec((1, n_mem, mw), lambda i: (i // n_seq_tiles, 0, 0)),
        pl.BlockSpec((1, n_mem, mw), lambda i: (i // n_seq_tiles, 0, 1)),
        slab(wo),
        pl.BlockSpec((1, D), lambda i: (0, 0)),
        slab(wgu),
        slab(wd),
    ]
    args = [x2d, yt, proj2d, memkv, memkv, wo, g.reshape(1, D), wgu, wd]
    if final:
        in_specs.append(pl.BlockSpec((1, D), lambda i: (0, 0)))
        args.append(final_g.reshape(1, D))
    block_bytes = (5 * _nbytes((tm, D), F32) + _nbytes((tm, D), BF16)
                   + 2 * _nbytes((tm, yw + mw), BF16) + 4 * _nbytes((n_mem, mw), BF16)
                   + _nbytes(wo.shape[1:], BF16) + _nbytes(wgu.shape[1:], BF16)
                   + _nbytes(wd.shape[1:], BF16))
    return pl.pallas_call(
        functools.partial(_post_kernel, final=final, n_chunks=n_chunks),
        grid=(T // tm,),
        in_specs=in_specs,
        out_specs=pl.BlockSpec((tm, D), lambda i: (i, 0)),
        out_shape=jax.ShapeDtypeStruct((T, D), F32),
        scratch_shapes=[pltpu.VMEM((tm, D), BF16), pltpu.VMEM((tm, D), F32)],
        compiler_params=_params(1, block_bytes),
        name="post_final" if final else "post",
    )(*args)


def _rope_tables(seq):
    half = HEAD_DIM // 2
    inv_freq = jnp.power(ROPE_THETA, -jnp.arange(half, dtype=F32) * (2.0 / HEAD_DIM))
    ang = jnp.arange(seq, dtype=F32)[:, None] * inv_freq[None, :]
    cos, sin = jnp.cos(ang), jnp.sin(ang)
    reps = LANES // HEAD_DIM
    cos_t = jnp.tile(jnp.concatenate([cos, cos], axis=-1), (1, reps))
    sin_t = jnp.tile(jnp.concatenate([-sin, sin], axis=-1), (1, reps))
    return cos_t, sin_t


def kernel(x, mem, attn_norm_g, mem_norm_g, w_mem_kv, w_out, ffn_norm_g, w_gate_up, w_down,
           a_w_in, a_b_f, b_w_in, b_lambda_q1, b_lambda_k1, b_lambda_q2, b_lambda_k2,
           b_subln_g, kv_norm_g, w_kv_shared, final_norm_g):
    B, S, D = x.shape
    depth = attn_norm_g.shape[0]
    n_a = a_w_in.shape[0]
    n_mem = mem.shape[1]
    mem_w = w_mem_kv.shape[2] // 2
    n_fox = a_b_f.shape[1]
    fox_w = n_fox * HEAD_DIM
    diff_w = b_w_in.shape[1] - mem_w
    n_diff = diff_w // (2 * HEAD_DIM)
    scale = HEAD_DIM ** -0.5 * LOG2E
    T = B * S

    x2d = x.reshape(T, D)
    mem2d = mem.reshape(B * n_mem, D)
    rope_tabs = _rope_tables(S)
    k_sh = vt_sh = None
    wo_all = w_out.astype(BF16)
    wgu_all = w_gate_up.astype(BF16)
    wd_all = w_down.astype(BF16)

    for layer in range(depth):
        (memkv,) = _proj(mem2d, [(mem_norm_g[layer], w_mem_kv[layer].astype(BF16),
                                  0, 2 * mem_w, 0)], seq=n_mem)
        memkv = memkv.reshape(B, n_mem, 2 * mem_w)
        if layer < n_a:
            w_in = a_w_in[layer]
            w_main = jnp.concatenate(
                [w_in[:, :fox_w] * scale, w_in[:, fox_w:2 * fox_w],
                 w_in[:, 3 * fox_w + n_fox:] * scale, w_in[:, 2 * fox_w:3 * fox_w]],
                axis=1).astype(BF16)
            gw = jnp.pad(w_in[:, 3 * fox_w:3 * fox_w + n_fox], ((0, 0), (0, LANES - n_fox)))
            gw_hi = gw.astype(BF16)
            gw_lo = (gw - gw_hi.astype(F32)).astype(BF16)
            gate_w = jnp.concatenate([gw_hi, gw_lo], axis=1)
            gate_b = jnp.pad(a_b_f[layer], (0, LANES - n_fox)).reshape(1, LANES)
            proj, vt, qa, ka = _proj(
                x2d, [(attn_norm_g[layer], w_main, 0, 2 * fox_w + mem_w, fox_w)], seq=S,
                gate_w=gate_w, gate_b=gate_b, n_gates=n_fox)
            y = _fox_attn(qa.reshape(B, S, -1), ka.reshape(B, S, -1), vt)
            mq_col = 0
        else:
            j = layer - n_a
            streams = [(attn_norm_g[layer], (b_w_in[j] * scale).astype(BF16),
                        diff_w, diff_w + mem_w, 0)]
            if layer == n_a:
                streams.append((kv_norm_g, w_kv_shared.astype(BF16), diff_w, diff_w,
                                w_kv_shared.shape[1] - diff_w))
                proj, k_sh, vt_sh = _proj(x2d, streams, seq=S, rope_tabs=rope_tabs)
                k_sh = k_sh.reshape(B, S, diff_w)
            else:
                (proj,) = _proj(x2d, streams, seq=S, rope_tabs=rope_tabs)
            lambda_init = 0.8 - 0.6 * math.exp(-0.3 * layer)
            y = _diff_attn(proj.reshape(B, S, -1), k_sh, vt_sh, b_lambda_q1[j], b_lambda_k1[j],
                           b_lambda_q2[j], b_lambda_k2[j], b_subln_g[j], n_heads=n_diff,
                           lambda_init=lambda_init)
            mq_col = diff_w // mem_w
        last = layer == depth - 1
        x2d = _post(x2d, y, proj, mq_col, memkv, wo_all, ffn_norm_g[layer],
                    wgu_all, wd_all, layer=layer, seq=S,
                    final_g=final_norm_g if last else None)
    return x2d.reshape(B, S, D)
```

```python
import functools
import math

import jax
import jax.numpy as jnp
from jax import lax
from jax.experimental import pallas as pl
from jax.experimental.pallas import tpu as pltpu

HEAD_DIM = 64
ROPE_THETA = 10000.0
NORM_EPS = 1e-6
LANES = 128
ATTN_TILE = 512
PROJ_TM = ATTN_TILE
FFN_TM = 1024
FFN_TF = 256
VMEM_CAP = 56 * 1024 * 1024
VMEM_TEMPS = 16 * 1024 * 1024
LOG2E = math.log2(math.e)

F32 = jnp.float32
BF16 = jnp.bfloat16
_NT = (((1,), (1,)), ((), ()))
_RESIDENT = pl.Buffered(1)


def _nbytes(shape, dtype):
    return math.prod(shape) * jnp.dtype(dtype).itemsize


def _params(n_grid, block_bytes):
    limit = min(VMEM_CAP, block_bytes + VMEM_TEMPS)
    return pltpu.CompilerParams(dimension_semantics=("arbitrary",) * n_grid,
                                vmem_limit_bytes=limit)


def _rms(x, g):
    ms = jnp.mean(x * x, axis=-1, keepdims=True)
    return x * lax.rsqrt(ms + NORM_EPS) * g


def _split_bf16(v, n):
    pieces = []
    for _ in range(n - 1):
        p = v.astype(BF16)
        pieces.append(p)
        v = v - p.astype(F32)
    pieces.append(v.astype(BF16))
    return pieces


def _proj_kernel(*refs, streams, n_gates, n_seq_tiles):
    gates = n_gates > 0
    it = iter(refs)
    x_ref = next(it)
    gw = [(next(it), next(it)) for _ in streams]
    any_rope = any(s[0] for s in streams)
    if any_rope:
        cos_ref, sin_ref = next(it), next(it)
    if gates:
        wf_ref, bf_ref = next(it), next(it)
    outs = []
    for (_, _, vt_cols) in streams:
        row_ref = next(it)
        outs.append((row_ref, next(it) if vt_cols else None))
    if gates:
        qa_ref, ka_ref, carry_ref = next(it), next(it), next(it)

    tm = x_ref.shape[0]
    x = x_ref[...]
    xn = x * lax.rsqrt(jnp.mean(x * x, axis=-1, keepdims=True) + NORM_EPS)
    if any_rope:
        cos = cos_ref[...]
        sin = sin_ref[...]
        lane = lax.broadcasted_iota(jnp.int32, (tm, LANES), 1)
        first_half = (lane % HEAD_DIM) < (HEAD_DIM // 2)

    def gate_pieces(hn32, hn):
        lo = (hn32 - hn.astype(F32)).astype(BF16)
        wf = wf_ref[...]
        zz = jnp.dot(hn, wf, preferred_element_type=F32)
        z = (zz[:, :LANES] + zz[:, LANES:]
             + jnp.dot(lo, wf[:, :LANES], preferred_element_type=F32) + bf_ref[...])
        lf = jnp.minimum(z, 0.0) - jnp.log1p(jnp.exp(-jnp.abs(z)))
        l_hi, l_mid, l_lo = _split_bf16(lf, 3)
        row = lax.broadcasted_iota(jnp.int32, (tm, tm), 0)
        col = lax.broadcasted_iota(jnp.int32, (tm, tm), 1)
        tri = (row >= col).astype(BF16)
        cc = jnp.dot(tri, jnp.concatenate([l_hi, l_mid], axis=1), preferred_element_type=F32)
        cs = cc[:, :LANES] + cc[:, LANES:] + jnp.dot(tri, l_lo, preferred_element_type=F32)

        @pl.when((pl.program_id(0) % n_seq_tiles) == 0)
        def _():
            carry_ref[...] = jnp.zeros_like(carry_ref)

        c = cs + carry_ref[...]
        carry_ref[...] = c[tm - 1:tm, :]
        p_hi, p_mid, p_lo = [p.astype(F32) for p in _split_bf16(c * LOG2E, 3)]
        lane_c = lax.broadcasted_iota(jnp.int32, (tm, LANES), 1)
        return jnp.where(lane_c < n_gates, p_hi, jnp.where(
            lane_c < 2 * n_gates, pltpu.roll(p_mid, n_gates, 1),
            pltpu.roll(p_lo, 2 * n_gates, 1)))

    for si, ((rope_cols, row_cols, vt_cols), (g_ref, w_ref), (row_ref, vt_ref)) in enumerate(
            zip(streams, gw, outs)):
        hn32 = xn * g_ref[...]
        hn = hn32.astype(BF16)
        if gates and si == 0:
            packed = gate_pieces(hn32, hn)
        res = jnp.dot(hn, w_ref[...], preferred_element_type=F32)
        for c0 in range(0, rope_cols, LANES):
            t = res[:, c0:c0 + LANES]
            up = pltpu.roll(t, HEAD_DIM // 2, 1)
            dn = pltpu.roll(t, LANES - HEAD_DIM // 2, 1)
            sw = jnp.where(first_half, dn, up)
            row_ref[:, c0:c0 + LANES] = (t * cos + sw * sin).astype(BF16)
        row_from = 2 * n_gates * HEAD_DIM if (gates and si == 0) else 0
        plain = max(rope_cols, row_from)
        if row_cols > plain:
            row_ref[:, plain - row_from:row_cols - row_from] = res[:, plain:row_cols].astype(BF16)
        for gi in range(vt_cols // LANES):
            c0 = row_cols + gi * LANES
            vt_ref[0, gi, 0] = res[:, c0:c0 + LANES].T.astype(BF16)

        if gates and si == 0:
            d = HEAD_DIM
            lane_c = lax.broadcasted_iota(jnp.int32, (tm, LANES), 1)

            def lanes_of(base):
                return ((lane_c == base) | (lane_c == base + n_gates)
                        | (lane_c == base + 2 * n_gates))

            for p in range(n_gates // 2):
                cp = pltpu.roll(packed, d - 2 * p, 1)
                qp = res[:, p * LANES:(p + 1) * LANES]
                kp = res[:, n_gates * d + p * LANES:n_gates * d + (p + 1) * LANES]
                for h2 in range(2):
                    base = (1 - h2) * d
                    is_head = (lane_c >= h2 * d) & (lane_c < h2 * d + d)
                    in_a, in_b = lanes_of(base), lanes_of(base + 1)
                    kc = cp if h2 == 0 else pltpu.roll(cp, d - 1, 1)
                    qc = pltpu.roll(cp, 1 if h2 == 0 else d, 1)
                    k_extra = jnp.where(in_a, kc, jnp.where(in_b, 1.0, 0.0))
                    q_extra = jnp.where(in_b, qc, jnp.where(in_a, -1.0, 0.0))
                    c0 = (2 * p + h2) * LANES
                    ka_ref[:, c0:c0 + LANES] = jnp.where(is_head, kp, k_extra).astype(BF16)
                    qa_ref[:, c0:c0 + LANES] = jnp.where(is_head, qp, q_extra).astype(BF16)


def _proj(x2d, streams, *, seq, rope_tabs=None, gate_w=None, gate_b=None, n_gates=0):
    T, D = x2d.shape
    tm = min(PROJ_TM, T)
    assert T % tm == 0
    gates = n_gates > 0
    assert n_gates % 2 == 0 and HEAD_DIM + 2 + 2 * n_gates <= LANES
    any_rope = any(s[2] for s in streams)
    any_vt = any(s[4] for s in streams)
    assert seq % tm == 0 or not (gates or any_rope or any_vt)
    n_seq_tiles = max(1, seq // tm)
    n_batch = T // seq

    in_specs = [pl.BlockSpec((tm, D), lambda i: (i, 0))]
    args = [x2d]
    block_bytes = 2 * _nbytes((tm, D), F32)
    for g, w, _, _, _ in streams:
        in_specs += [pl.BlockSpec((1, D), lambda i: (0, 0)),
                     pl.BlockSpec(w.shape, lambda i: (0, 0), pipeline_mode=_RESIDENT)]
        args += [g.reshape(1, D), w]
        block_bytes += _nbytes(w.shape, BF16) + _nbytes((tm, w.shape[1]), F32)
    if any_rope:
        tab_spec = pl.BlockSpec((tm, LANES), lambda i: (i % n_seq_tiles, 0))
        in_specs += [tab_spec, tab_spec]
        args += list(rope_tabs)
        block_bytes += 4 * _nbytes((tm, LANES), F32)
    if gates:
        in_specs += [pl.BlockSpec(gate_w.shape, lambda i: (0, 0), pipeline_mode=_RESIDENT),
                     pl.BlockSpec((1, LANES), lambda i: (0, 0))]
        args += [gate_w, gate_b]
        block_bytes += _nbytes(gate_w.shape, BF16)

    out_shape, out_specs = [], []
    for si, (_, w, _, row_cols, vt_cols) in enumerate(streams):
        if gates and si == 0:
            row_cols -= 2 * n_gates * HEAD_DIM
        out_shape.append(jax.ShapeDtypeStruct((T, row_cols), BF16))
        out_specs.append(pl.BlockSpec((tm, row_cols), lambda i: (i, 0)))
        block_bytes += 2 * _nbytes((tm, w.shape[1]), BF16)
        if vt_cols:
            n_groups = vt_cols // LANES
            out_shape.append(
                jax.ShapeDtypeStruct((n_batch, n_groups, n_seq_tiles, LANES, tm), BF16))
            out_specs.append(pl.BlockSpec(
                (1, n_groups, 1, LANES, tm),
                lambda i: (i // n_seq_tiles, 0, i % n_seq_tiles, 0, 0)))
    scratch = []
    if gates:
        aug_cols = n_gates * LANES
        for _ in range(2):
            out_shape.append(jax.ShapeDtypeStruct((T, aug_cols), BF16))
            out_specs.append(pl.BlockSpec((tm, aug_cols), lambda i: (i, 0)))
            block_bytes += 2 * _nbytes((tm, aug_cols), BF16)
        scratch.append(pltpu.VMEM((1, LANES), F32))

    cfg = tuple((s[2], s[3], s[4]) for s in streams)
    return pl.pallas_call(
        functools.partial(_proj_kernel, streams=cfg, n_gates=n_gates, n_seq_tiles=n_seq_tiles),
        grid=(T // tm,),
        in_specs=in_specs,
        out_specs=out_specs,
        out_shape=out_shape,
        scratch_shapes=scratch,
        compiler_params=_params(1, block_bytes),
        name="proj_gates" if gates else ("proj_rope" if any_rope else "proj"),
    )(*args)


ONES_ROWS = 16
OFF_UNROLL = 7


def _flash_scratch(n_maps, n_tiles, tile, dv):
    scores = pltpu.VMEM((n_maps, tile, tile), F32)
    row = pltpu.VMEM((n_maps, 1, tile), F32)
    return [scores, row, scores, row,
            pltpu.VMEM((n_tiles, n_maps, 1, tile), F32),
            pltpu.VMEM((n_tiles, n_maps, dv + ONES_ROWS, tile), F32)]


def _flash_scratch_bytes(n_maps, n_tiles, tile, dv):
    return (2 * _nbytes((n_maps, tile, tile), F32) + 2 * _nbytes((n_maps, 8, tile), F32)
            + _nbytes((n_tiles, n_maps, 8, tile), F32)
            + _nbytes((n_tiles, n_maps, dv + ONES_ROWS, tile), F32))


def _causal_flash(n_tiles, tile, q_ops_of, k_block, vt_block, scratch):
    buf_a, buf_b, m_all, acc_all = scratch[:2], scratch[2:4], scratch[4], scratch[5]
    n_off = n_tiles * (n_tiles - 1) // 2
    assert n_tiles % 2 == 0 and n_off % 2 == 0
    ones = jnp.ones((ONES_ROWS, tile), BF16)
    kk = lax.broadcasted_iota(jnp.int32, (tile, tile), 0)
    qq = lax.broadcasted_iota(jnp.int32, (tile, tile), 1)
    causal = kk <= qq

    def issue(qi, j, buf):
        s_ref, bm_ref = buf
        for i, (ka, qa) in enumerate(zip(k_block(j), q_ops_of(qi))):
            st = lax.dot_general(ka, qa, _NT, preferred_element_type=F32)
            s_ref[i] = st
            bm_ref[i] = jnp.max(st, axis=0, keepdims=True)

    half = tile // 2

    def issue_diag(qi, buf):
        s_ref, _ = buf
        for i, (ka, qa) in enumerate(zip(k_block(qi), q_ops_of(qi))):
            s_ref[i, :half, :half] = lax.dot_general(ka[:half], qa[:half], _NT,
                                                     preferred_element_type=F32)
            s_ref[i, :, half:] = lax.dot_general(ka, qa[half:], _NT, preferred_element_type=F32)

    def process_diag(buf, qi):
        s_ref, _ = buf
        for i, vt in enumerate(vt_block(qi)):
            vta = jnp.concatenate([vt, ones], axis=0)
            for rows, cols in ((slice(0, half), slice(0, half)), (slice(0, tile), slice(half, tile))):
                st = jnp.where(causal[rows, cols], s_ref[i, rows, cols], -jnp.inf)
                m_new = jnp.max(st, axis=0, keepdims=True)
                p = jnp.exp2(st - m_new).astype(BF16)
                acc_all[qi, i, :, cols] = jnp.dot(vta[:, rows], p, preferred_element_type=F32)
                m_all[qi, i, :, cols] = m_new

    def process(buf, qi, j):
        s_ref, bm_ref = buf
        for i, vt in enumerate(vt_block(j)):
            m = m_all[qi, i]
            m_new = jnp.maximum(m, bm_ref[i])
            alpha = jnp.exp2(m - m_new)
            p = jnp.exp2(s_ref[i] - m_new).astype(BF16)
            pv = jnp.dot(jnp.concatenate([vt, ones], axis=0), p, preferred_element_type=F32)
            acc_all[qi, i] = alpha * acc_all[qi, i] + pv
            m_all[qi, i] = m_new

    issue_diag(0, buf_a)

    for qi in range(0, n_tiles, 2):
        issue_diag(qi + 1, buf_b)
        process_diag(buf_a, qi)
        if qi + 2 < n_tiles:
            issue_diag(qi + 2, buf_a)
        else:
            issue(1, 0, buf_a)
        process_diag(buf_b, qi + 1)

    def following(qi, j):
        wrap = j + 1 == qi
        return jnp.where(wrap, qi + 1, qi), jnp.where(wrap, 0, j + 1)

    def off_pair(qi, j):
        qi1, j1 = following(qi, j)
        issue(qi1, j1, buf_b)
        process(buf_a, qi, j)
        qi2, j2 = following(qi1, j1)
        past = qi2 >= n_tiles
        qi2, j2 = jnp.where(past, 1, qi2), jnp.where(past, 0, j2)
        issue(qi2, j2, buf_a)
        process(buf_b, qi1, j1)
        return qi2, j2

    def off_trip(u, carry):
        for _ in range(OFF_UNROLL):
            carry = off_pair(*carry)
        return carry

    assert n_off % (2 * OFF_UNROLL) == 0
    lax.fori_loop(0, n_off // (2 * OFF_UNROLL), off_trip, (jnp.int32(1), jnp.int32(0)))


def _fox_attn_kernel(qa_ref, ka_ref, vt_ref, *rest, tile, n_casts):
    cast_in, o_ref = rest[:n_casts], rest[n_casts]
    cast_out, flash_scratch = rest[n_casts + 1:2 * n_casts + 1], rest[2 * n_casts + 1:]
    for src, dst in zip(cast_in, cast_out):
        dst[...] = src[...].astype(dst.dtype)
    seq = qa_ref.shape[1]
    n_tiles = seq // tile
    d = HEAD_DIM

    def q_ops_of(qi):
        q0 = pl.multiple_of(qi * tile, tile)
        return [qa_ref[0, pl.ds(q0, tile), h2 * LANES:(h2 + 1) * LANES] for h2 in range(2)]

    def k_block(j):
        start = pl.multiple_of(j * tile, tile)
        return [ka_ref[0, pl.ds(start, tile), h2 * LANES:(h2 + 1) * LANES] for h2 in range(2)]

    def vt_block(j):
        vb = vt_ref[0, 0, j]
        return [vb, vb]

    _causal_flash(n_tiles, tile, q_ops_of, k_block, vt_block, flash_scratch)
    acc_all = flash_scratch[5]

    def emit(qi, carry):
        a0 = acc_all[qi, 0]
        a1 = acc_all[qi, 1]
        yt = jnp.concatenate([a0[:d] * (1.0 / a0[2 * d:2 * d + 1]),
                              a1[d:2 * d] * (1.0 / a1[2 * d:2 * d + 1])], axis=0)
        o_ref[0, 0, qi] = yt.astype(o_ref.dtype)
        return carry

    lax.fori_loop(0, n_tiles, emit, 0)


def _fox_attn(qa, ka, vt, side_casts=()):
    B, S, width = qa.shape
    tile = ATTN_TILE
    n_tiles = S // tile
    n_pairs = width // (2 * LANES)
    pair_block = pl.BlockSpec((1, S, 2 * LANES), lambda b, h: (b, 0, h))
    tiles_block = pl.BlockSpec((1, 1, n_tiles, LANES, tile), lambda b, h: (b, h, 0, 0, 0))
    block_bytes = (2 * 6 * _nbytes((S, LANES), BF16)
                   + _flash_scratch_bytes(2, n_tiles, tile, 2 * HEAD_DIM))
    cast_specs, cast_shapes = [], []
    for w, axis, chunk in side_casts:
        n_chunks = w.shape[axis] // chunk
        assert w.shape[axis] % chunk == 0 and n_chunks <= B * n_pairs
        shape = tuple(chunk if a == axis else s for a, s in enumerate(w.shape))

        def index_map(b, h, axis=axis, last=n_chunks - 1):
            step = jnp.minimum(b * n_pairs + h, last)
            return tuple(step if a == axis else 0 for a in range(2))

        cast_specs.append(pl.BlockSpec(shape, index_map))
        cast_shapes.append(jax.ShapeDtypeStruct(w.shape, BF16))
        block_bytes += 2 * (_nbytes(shape, F32) + _nbytes(shape, BF16))
    outs = pl.pallas_call(
        functools.partial(_fox_attn_kernel, tile=tile, n_casts=len(side_casts)),
        grid=(B, n_pairs),
        in_specs=[pair_block, pair_block, tiles_block] + cast_specs,
        out_specs=[tiles_block] + cast_specs,
        out_shape=[jax.ShapeDtypeStruct((B, n_pairs, n_tiles, LANES, tile), BF16)] + cast_shapes,
        scratch_shapes=_flash_scratch(2, n_tiles, tile, 2 * HEAD_DIM),
        compiler_params=_params(2, block_bytes),
        name="fox_attn",
    )(qa, ka, vt, *[w for w, _, _ in side_casts])
    return outs[0], outs[1:]


def _diff_attn_kernel(q_ref, k_ref, vt_ref, lq1_ref, lk1_ref, lq2_ref, lk2_ref, g_ref, o_ref,
                      *flash_scratch, tile, lambda_init):
    seq = q_ref.shape[1]
    n_tiles = seq // tile
    lane = lax.broadcasted_iota(jnp.int32, (tile, LANES), 1)
    lam = (jnp.exp(jnp.sum(lq1_ref[...] * lk1_ref[...], axis=1, keepdims=True))
           - jnp.exp(jnp.sum(lq2_ref[...] * lk2_ref[...], axis=1, keepdims=True)) + lambda_init)

    dv = 2 * HEAD_DIM

    def q_ops_of(qi):
        qp = q_ref[0, pl.ds(pl.multiple_of(qi * tile, tile), tile), :]
        zero = jnp.zeros_like(qp)
        return [jnp.where(lane < HEAD_DIM, qp, zero), jnp.where(lane >= HEAD_DIM, qp, zero)]

    def k_block(j):
        kb = k_ref[0, pl.ds(pl.multiple_of(j * tile, tile), tile), :]
        return [kb, kb]

    def vt_block(j):
        vb = vt_ref[0, 0, j]
        return [vb, vb]

    _causal_flash(n_tiles, tile, q_ops_of, k_block, vt_block, flash_scratch)
    acc_all = flash_scratch[5]

    def emit(qi, carry):
        a1 = acc_all[qi, 0]
        a2 = acc_all[qi, 1]
        yt = a1[:dv] * (1.0 / a1[dv:dv + 1]) - a2[:dv] * (lam / a2[dv:dv + 1])
        ms = jnp.mean(yt * yt, axis=0, keepdims=True)
        y = yt * lax.rsqrt(ms + NORM_EPS) * g_ref[...] * (1.0 - lambda_init)
        o_ref[0, 0, qi] = y.astype(o_ref.dtype)
        return carry

    lax.fori_loop(0, n_tiles, emit, 0)


def _diff_attn(qproj, k_sh, vt, lq1, lk1, lq2, lk2, subln_g, *, n_heads, lambda_init):
    B, S, _ = qproj.shape
    tile = ATTN_TILE
    n_tiles = S // tile
    vec = pl.BlockSpec((1, HEAD_DIM), lambda b, h: (0, 0))
    block_bytes = (2 * 4 * _nbytes((S, LANES), BF16)
                   + _flash_scratch_bytes(2, n_tiles, tile, 2 * HEAD_DIM))
    return pl.pallas_call(
        functools.partial(_diff_attn_kernel, tile=tile, lambda_init=lambda_init),
        grid=(B, n_heads),
        in_specs=[
            pl.BlockSpec((1, S, LANES), lambda b, h: (b, 0, h)),
            pl.BlockSpec((1, S, LANES), lambda b, h: (b, 0, h)),
            pl.BlockSpec((1, 1, n_tiles, LANES, tile), lambda b, h: (b, h, 0, 0, 0)),
            vec, vec, vec, vec,
            pl.BlockSpec((2 * HEAD_DIM, 1), lambda b, h: (0, 0)),
        ],
        out_specs=pl.BlockSpec((1, 1, n_tiles, LANES, tile), lambda b, h: (b, h, 0, 0, 0)),
        out_shape=jax.ShapeDtypeStruct((B, n_heads, n_tiles, LANES, tile), BF16),
        scratch_shapes=_flash_scratch(2, n_tiles, tile, 2 * HEAD_DIM),
        compiler_params=_params(2, block_bytes),
        name="diff_attn",
    )(qproj, k_sh, vt, lq1.reshape(1, -1), lk1.reshape(1, -1), lq2.reshape(1, -1),
      lk2.reshape(1, -1), subln_g.reshape(-1, 1))


def _mix_out(x, yt_ref, mq_ref, mk_ref, mv_ref, wo_ref):
    tm = x.shape[0]
    mw = mq_ref.shape[1]
    n_groups, n_sub = yt_ref.shape[1], yt_ref.shape[2]
    y_width = n_groups * LANES
    y = jnp.concatenate(
        [jnp.concatenate([yt_ref[0, g, t].astype(F32).T.astype(BF16) for t in range(n_sub)], axis=0)
         for g in range(n_groups)], axis=1)
    mq = mq_ref[...]
    mk = mk_ref[0]
    mv = mv_ref[0]
    q_head = lax.broadcasted_iota(jnp.int32, (tm, mw), 1) // HEAD_DIM
    v_head = lax.broadcasted_iota(jnp.int32, mv.shape, 1) // HEAD_DIM
    ymem = jnp.zeros((tm, mw), F32)
    for h in range(mw // HEAD_DIM):
        qh = jnp.where(q_head == h, mq, jnp.zeros_like(mq))
        s = lax.dot_general(qh, mk, _NT, preferred_element_type=F32)
        p = jnp.exp2(s - jnp.max(s, axis=1, keepdims=True))
        l = jnp.sum(p, axis=1, keepdims=True)
        vh = jnp.where(v_head == h, mv, jnp.zeros_like(mv))
        ymem = ymem + jnp.dot(p.astype(BF16), vh, preferred_element_type=F32) / l
    acc = jnp.dot(y, wo_ref[0:y_width, :], preferred_element_type=F32)
    acc = acc + jnp.dot(ymem.astype(BF16), wo_ref[y_width:y_width + mw, :],
                        preferred_element_type=F32)
    return x + acc


def _post_kernel(*refs, final, n_chunks):
    x_ref, y_ref, mq_ref, mk_ref, mv_ref, wo_ref, g_ref, wgu_ref, wd_ref = refs[:9]
    wo_ref, wgu_ref, wd_ref = wo_ref.at[0], wgu_ref.at[0], wd_ref.at[0]
    if final:
        fg_ref, o_ref, hn_ref, acc_ref = refs[9:]
    else:
        o_ref, hn_ref, acc_ref = refs[9:]
    x = _mix_out(x_ref[...], y_ref, mq_ref, mk_ref, mv_ref, wo_ref)
    hn_ref[...] = _rms(x, g_ref[...]).astype(BF16)
    acc_ref[...] = x

    d_ff = wd_ref.shape[0]
    tf = d_ff // n_chunks
    for k in range(n_chunks):
        h = hn_ref[...]
        gate = jnp.dot(h, wgu_ref[:, k * tf:(k + 1) * tf], preferred_element_type=F32)
        up = jnp.dot(h, wgu_ref[:, d_ff + k * tf:d_ff + (k + 1) * tf],
                     preferred_element_type=F32)
        a = gate * jax.nn.sigmoid(gate) * up
        acc_ref[...] += jnp.dot(a.astype(BF16), wd_ref[k * tf:(k + 1) * tf, :],
                                preferred_element_type=F32)
    out = acc_ref[...]
    if final:
        out = _rms(out, fg_ref[...])
    o_ref[...] = out


def _post(x2d, yt, proj2d, mq_col, memkv, wo, g, wgu, wd, *, layer, seq, final_g=None):
    T, D = x2d.shape
    n_groups, tile = yt.shape[1], yt.shape[4]
    yw = n_groups * LANES
    n_mem, mw = memkv.shape[1], memkv.shape[2] // 2
    d_ff = wd.shape[1]

    def slab(w):
        return pl.BlockSpec((1,) + w.shape[1:], lambda i: (layer, 0, 0), pipeline_mode=_RESIDENT)

    tm = FFN_TM
    assert T % tm == 0 and seq % tm == 0 and tm % tile == 0 and d_ff % FFN_TF == 0
    n_chunks = d_ff // FFN_TF
    n_seq_tiles = seq // tm
    final = final_g is not None
    in_specs = [
        pl.BlockSpec((tm, D), lambda i: (i, 0)),
        pl.BlockSpec((1, n_groups, tm // tile, LANES, tile),
                     lambda i: (i // n_seq_tiles, 0, i % n_seq_tiles, 0, 0)),
        pl.BlockSpec((tm, mw), lambda i: (i, mq_col)),
        pl.BlockSpec((1, n_mem, mw), lambda i: (i // n_seq_tiles, 0, 0)),
        pl.BlockSpec((1, n_mem, mw), lambda i: (i // n_seq_tiles, 0, 1)),
        slab(wo),
        pl.BlockSpec((1, D), lambda i: (0, 0)),
        slab(wgu),
        slab(wd),
    ]
    args = [x2d, yt, proj2d, memkv, memkv, wo, g.reshape(1, D), wgu, wd]
    if final:
        in_specs.append(pl.BlockSpec((1, D), lambda i: (0, 0)))
        args.append(final_g.reshape(1, D))
    block_bytes = (5 * _nbytes((tm, D), F32) + _nbytes((tm, D), BF16)
                   + 2 * _nbytes((tm, yw + mw), BF16) + 4 * _nbytes((n_mem, mw), BF16)
                   + _nbytes(wo.shape[1:], BF16) + _nbytes(wgu.shape[1:], BF16)
                   + _nbytes(wd.shape[1:], BF16))
    return pl.pallas_call(
        functools.partial(_post_kernel, final=final, n_chunks=n_chunks),
        grid=(T // tm,),
        in_specs=in_specs,
        out_specs=pl.BlockSpec((tm, D), lambda i: (i, 0)),
        out_shape=jax.ShapeDtypeStruct((T, D), F32),
        scratch_shapes=[pltpu.VMEM((tm, D), BF16), pltpu.VMEM((tm, D), F32)],
        compiler_params=_params(1, block_bytes),
        name="post_final" if final else "post",
    )(*args)


def _rope_tables(seq):
    half = HEAD_DIM // 2
    inv_freq = jnp.power(ROPE_THETA, -jnp.arange(half, dtype=F32) * (2.0 / HEAD_DIM))
    ang = jnp.arange(seq, dtype=F32)[:, None] * inv_freq[None, :]
    cos, sin = jnp.cos(ang), jnp.sin(ang)
    reps = LANES // HEAD_DIM
    cos_t = jnp.tile(jnp.concatenate([cos, cos], axis=-1), (1, reps))
    sin_t = jnp.tile(jnp.concatenate([-sin, sin], axis=-1), (1, reps))
    return cos_t, sin_t


def _cast_chunk(n, steps, align):
    return next(c for c in range(align, n + 1, align) if n % c == 0 and n // c <= steps)


def kernel(x, mem, attn_norm_g, mem_norm_g, w_mem_kv, w_out, ffn_norm_g, w_gate_up, w_down,
           a_w_in, a_b_f, b_w_in, b_lambda_q1, b_lambda_k1, b_lambda_q2, b_lambda_k2,
           b_subln_g, kv_norm_g, w_kv_shared, final_norm_g):
    B, S, D = x.shape
    depth = attn_norm_g.shape[0]
    n_a = a_w_in.shape[0]
    n_mem = mem.shape[1]
    mem_w = w_mem_kv.shape[2] // 2
    n_fox = a_b_f.shape[1]
    fox_w = n_fox * HEAD_DIM
    diff_w = b_w_in.shape[1] - mem_w
    n_diff = diff_w // (2 * HEAD_DIM)
    scale = HEAD_DIM ** -0.5 * LOG2E
    T = B * S

    x2d = x.reshape(T, D)
    mem2d = mem.reshape(B * n_mem, D)
    rope_tabs = _rope_tables(S)
    k_sh = vt_sh = None
    wo_all = w_out.astype(BF16)
    wgu_all = wd_all = None
    d_ff = w_down.shape[1]

    for layer in range(depth):
        (memkv,) = _proj(mem2d, [(mem_norm_g[layer], w_mem_kv[layer].astype(BF16),
                                  0, 2 * mem_w, 0)], seq=n_mem)
        memkv = memkv.reshape(B, n_mem, 2 * mem_w)
        if layer < n_a:
            w_in = a_w_in[layer]
            w_main = jnp.concatenate(
                [w_in[:, :fox_w] * scale, w_in[:, fox_w:2 * fox_w],
                 w_in[:, 3 * fox_w + n_fox:] * scale, w_in[:, 2 * fox_w:3 * fox_w]],
                axis=1).astype(BF16)
            gw = jnp.pad(w_in[:, 3 * fox_w:3 * fox_w + n_fox], ((0, 0), (0, LANES - n_fox)))
            gw_hi = gw.astype(BF16)
            gw_lo = (gw - gw_hi.astype(F32)).astype(BF16)
            gate_w = jnp.concatenate([gw_hi, gw_lo], axis=1)
            gate_b = jnp.pad(a_b_f[layer], (0, LANES - n_fox)).reshape(1, LANES)
            proj, vt, qa, ka = _proj(
                x2d, [(attn_norm_g[layer], w_main, 0, 2 * fox_w + mem_w, fox_w)], seq=S,
                gate_w=gate_w, gate_b=gate_b, n_gates=n_fox)
            side = ()
            if wgu_all is None:
                steps = B * (n_fox // 2)
                side = ((w_gate_up.reshape(depth * D, 2 * d_ff), 1,
                         _cast_chunk(2 * d_ff, steps, LANES)),
                        (w_down.reshape(depth * d_ff, D), 0, _cast_chunk(depth * d_ff, steps, 16)))
            y, casts = _fox_attn(qa.reshape(B, S, -1), ka.reshape(B, S, -1), vt, side)
            if side:
                wgu_all = casts[0].reshape(depth, D, 2 * d_ff)
                wd_all = casts[1].reshape(depth, d_ff, D)
            mq_col = 0
        else:
            j = layer - n_a
            streams = [(attn_norm_g[layer], (b_w_in[j] * scale).astype(BF16),
                        diff_w, diff_w + mem_w, 0)]
            if layer == n_a:
                streams.append((kv_norm_g, w_kv_shared.astype(BF16), diff_w, diff_w,
                                w_kv_shared.shape[1] - diff_w))
                proj, k_sh, vt_sh = _proj(x2d, streams, seq=S, rope_tabs=rope_tabs)
                k_sh = k_sh.reshape(B, S, diff_w)
            else:
                (proj,) = _proj(x2d, streams, seq=S, rope_tabs=rope_tabs)
            lambda_init = 0.8 - 0.6 * math.exp(-0.3 * layer)
            y = _diff_attn(proj.reshape(B, S, -1), k_sh, vt_sh, b_lambda_q1[j], b_lambda_k1[j],
                           b_lambda_q2[j], b_lambda_k2[j], b_subln_g[j], n_heads=n_diff,
                           lambda_init=lambda_init)
            mq_col = diff_w // mem_w
        last = layer == depth - 1
        if wgu_all is None:
            wgu_all, wd_all = w_gate_up.astype(BF16), w_down.astype(BF16)
        x2d = _post(x2d, y, proj, mq_col, memkv, wo_all, ffn_norm_g[layer],
                    wgu_all, wd_all, layer=layer, seq=S,
                    final_g=final_norm_g if last else None)
    return x2d.reshape(B, S, D)
```

```python
import functools
import math

import jax
import jax.numpy as jnp
from jax import lax
from jax.experimental import pallas as pl
from jax.experimental.pallas import tpu as pltpu

HEAD_DIM = 64
ROPE_THETA = 10000.0
NORM_EPS = 1e-6
LANES = 128
ATTN_TILE = 512
PROJ_TM = ATTN_TILE
FFN_TM = 1024
FFN_TF = 256
VMEM_CAP = 56 * 1024 * 1024
VMEM_TEMPS = 16 * 1024 * 1024
LOG2E = math.log2(math.e)

F32 = jnp.float32
BF16 = jnp.bfloat16
_NT = (((1,), (1,)), ((), ()))
_RESIDENT = pl.Buffered(1)


def _nbytes(shape, dtype):
    return math.prod(shape) * jnp.dtype(dtype).itemsize


def _params(n_grid, block_bytes):
    limit = min(VMEM_CAP, block_bytes + VMEM_TEMPS)
    return pltpu.CompilerParams(dimension_semantics=("arbitrary",) * n_grid,
                                vmem_limit_bytes=limit)


def _rms(x, g):
    ms = jnp.mean(x * x, axis=-1, keepdims=True)
    return x * lax.rsqrt(ms + NORM_EPS) * g


def _split_bf16(v, n):
    pieces = []
    for _ in range(n - 1):
        p = v.astype(BF16)
        pieces.append(p)
        v = v - p.astype(F32)
    pieces.append(v.astype(BF16))
    return pieces


def _proj_kernel(*refs, streams, n_gates, n_seq_tiles):
    gates = n_gates > 0
    it = iter(refs)
    x_ref = next(it)
    gw = [(next(it), next(it)) for _ in streams]
    any_rope = any(s[0] for s in streams)
    if any_rope:
        cos_ref, sin_ref = next(it), next(it)
    if gates:
        wf_ref, bf_ref = next(it), next(it)
    outs = []
    for (_, _, vt_cols) in streams:
        row_ref = next(it)
        outs.append((row_ref, next(it) if vt_cols else None))
    if gates:
        qa_ref, ka_ref, carry_ref = next(it), next(it), next(it)

    tm = x_ref.shape[0]
    x = x_ref[...]
    xn = x * lax.rsqrt(jnp.mean(x * x, axis=-1, keepdims=True) + NORM_EPS)
    if any_rope:
        cos = cos_ref[...]
        sin = sin_ref[...]
        lane = lax.broadcasted_iota(jnp.int32, (tm, LANES), 1)
        first_half = (lane % HEAD_DIM) < (HEAD_DIM // 2)

    def gate_pieces(hn32, hn):
        lo = (hn32 - hn.astype(F32)).astype(BF16)
        wf = wf_ref[...]
        zz = jnp.dot(hn, wf, preferred_element_type=F32)
        z = (zz[:, :LANES] + zz[:, LANES:]
             + jnp.dot(lo, wf[:, :LANES], preferred_element_type=F32) + bf_ref[...])
        lf = jnp.minimum(z, 0.0) - jnp.log1p(jnp.exp(-jnp.abs(z)))
        l_hi, l_mid, l_lo = _split_bf16(lf, 3)
        row = lax.broadcasted_iota(jnp.int32, (tm, tm), 0)
        col = lax.broadcasted_iota(jnp.int32, (tm, tm), 1)
        tri = (row >= col).astype(BF16)
        cc = jnp.dot(tri, jnp.concatenate([l_hi, l_mid], axis=1), preferred_element_type=F32)
        cs = cc[:, :LANES] + cc[:, LANES:] + jnp.dot(tri, l_lo, preferred_element_type=F32)

        @pl.when((pl.program_id(0) % n_seq_tiles) == 0)
        def _():
            carry_ref[...] = jnp.zeros_like(carry_ref)

        c = cs + carry_ref[...]
        carry_ref[...] = c[tm - 1:tm, :]
        p_hi, p_mid, p_lo = [p.astype(F32) for p in _split_bf16(c * LOG2E, 3)]
        lane_c = lax.broadcasted_iota(jnp.int32, (tm, LANES), 1)
        return jnp.where(lane_c < n_gates, p_hi, jnp.where(
            lane_c < 2 * n_gates, pltpu.roll(p_mid, n_gates, 1),
            pltpu.roll(p_lo, 2 * n_gates, 1)))

    for si, ((rope_cols, row_cols, vt_cols), (g_ref, w_ref), (row_ref, vt_ref)) in enumerate(
            zip(streams, gw, outs)):
        hn32 = xn * g_ref[...]
        hn = hn32.astype(BF16)
        if gates and si == 0:
            packed = gate_pieces(hn32, hn)
        res = jnp.dot(hn, w_ref[...], preferred_element_type=F32)
        for c0 in range(0, rope_cols, LANES):
            t = res[:, c0:c0 + LANES]
            up = pltpu.roll(t, HEAD_DIM // 2, 1)
            dn = pltpu.roll(t, LANES - HEAD_DIM // 2, 1)
            sw = jnp.where(first_half, dn, up)
            row_ref[:, c0:c0 + LANES] = (t * cos + sw * sin).astype(BF16)
        row_from = 2 * n_gates * HEAD_DIM if (gates and si == 0) else 0
        plain = max(rope_cols, row_from)
        if row_cols > plain:
            row_ref[:, plain - row_from:row_cols - row_from] = res[:, plain:row_cols].astype(BF16)
        for gi in range(vt_cols // LANES):
            c0 = row_cols + gi * LANES
            vt_ref[0, gi, 0] = res[:, c0:c0 + LANES].T.astype(BF16)

        if gates and si == 0:
            d = HEAD_DIM
            lane_c = lax.broadcasted_iota(jnp.int32, (tm, LANES), 1)

            def lanes_of(base):
                return ((lane_c == base) | (lane_c == base + n_gates)
                        | (lane_c == base + 2 * n_gates))

            for p in range(n_gates // 2):
                cp = pltpu.roll(packed, d - 2 * p, 1)
                qp = res[:, p * LANES:(p + 1) * LANES]
                kp = res[:, n_gates * d + p * LANES:n_gates * d + (p + 1) * LANES]
                for h2 in range(2):
                    base = (1 - h2) * d
                    is_head = (lane_c >= h2 * d) & (lane_c < h2 * d + d)
                    in_a, in_b = lanes_of(base), lanes_of(base + 1)
                    kc = cp if h2 == 0 else pltpu.roll(cp, d - 1, 1)
                    qc = pltpu.roll(cp, 1 if h2 == 0 else d, 1)
                    k_extra = jnp.where(in_a, kc, jnp.where(in_b, 1.0, 0.0))
                    q_extra = jnp.where(in_b, qc, jnp.where(in_a, -1.0, 0.0))
                    c0 = (2 * p + h2) * LANES
                    ka_ref[:, c0:c0 + LANES] = jnp.where(is_head, kp, k_extra).astype(BF16)
                    qa_ref[:, c0:c0 + LANES] = jnp.where(is_head, qp, q_extra).astype(BF16)


def _proj(x2d, streams, *, seq, rope_tabs=None, gate_w=None, gate_b=None, n_gates=0):
    T, D = x2d.shape
    tm = min(PROJ_TM, T)
    assert T % tm == 0
    gates = n_gates > 0
    assert n_gates % 2 == 0 and HEAD_DIM + 2 + 2 * n_gates <= LANES
    any_rope = any(s[2] for s in streams)
    any_vt = any(s[4] for s in streams)
    assert seq % tm == 0 or not (gates or any_rope or any_vt)
    n_seq_tiles = max(1, seq // tm)
    n_batch = T // seq

    in_specs = [pl.BlockSpec((tm, D), lambda i: (i, 0))]
    args = [x2d]
    block_bytes = 2 * _nbytes((tm, D), F32)
    for g, w, _, _, _ in streams:
        in_specs += [pl.BlockSpec((1, D), lambda i: (0, 0)),
                     pl.BlockSpec(w.shape, lambda i: (0, 0), pipeline_mode=_RESIDENT)]
        args += [g.reshape(1, D), w]
        block_bytes += _nbytes(w.shape, BF16) + _nbytes((tm, w.shape[1]), F32)
    if any_rope:
        tab_spec = pl.BlockSpec((tm, LANES), lambda i: (i % n_seq_tiles, 0))
        in_specs += [tab_spec, tab_spec]
        args += list(rope_tabs)
        block_bytes += 4 * _nbytes((tm, LANES), F32)
    if gates:
        in_specs += [pl.BlockSpec(gate_w.shape, lambda i: (0, 0), pipeline_mode=_RESIDENT),
                     pl.BlockSpec((1, LANES), lambda i: (0, 0))]
        args += [gate_w, gate_b]
        block_bytes += _nbytes(gate_w.shape, BF16)

    out_shape, out_specs = [], []
    for si, (_, w, _, row_cols, vt_cols) in enumerate(streams):
        if gates and si == 0:
            row_cols -= 2 * n_gates * HEAD_DIM
        out_shape.append(jax.ShapeDtypeStruct((T, row_cols), BF16))
        out_specs.append(pl.BlockSpec((tm, row_cols), lambda i: (i, 0)))
        block_bytes += 2 * _nbytes((tm, w.shape[1]), BF16)
        if vt_cols:
            n_groups = vt_cols // LANES
            out_shape.append(
                jax.ShapeDtypeStruct((n_batch, n_groups, n_seq_tiles, LANES, tm), BF16))
            out_specs.append(pl.BlockSpec(
                (1, n_groups, 1, LANES, tm),
                lambda i: (i // n_seq_tiles, 0, i % n_seq_tiles, 0, 0)))
    scratch = []
    if gates:
        aug_cols = n_gates * LANES
        for _ in range(2):
            out_shape.append(jax.ShapeDtypeStruct((T, aug_cols), BF16))
            out_specs.append(pl.BlockSpec((tm, aug_cols), lambda i: (i, 0)))
            block_bytes += 2 * _nbytes((tm, aug_cols), BF16)
        scratch.append(pltpu.VMEM((1, LANES), F32))

    cfg = tuple((s[2], s[3], s[4]) for s in streams)
    return pl.pallas_call(
        functools.partial(_proj_kernel, streams=cfg, n_gates=n_gates, n_seq_tiles=n_seq_tiles),
        grid=(T // tm,),
        in_specs=in_specs,
        out_specs=out_specs,
        out_shape=out_shape,
        scratch_shapes=scratch,
        compiler_params=_params(1, block_bytes),
        name="proj_gates" if gates else ("proj_rope" if any_rope else "proj"),
    )(*args)


ONES_ROWS = 16
OFF_UNROLL = 7


def _flash_scratch(n_maps, n_tiles, tile, dv):
    scores = pltpu.VMEM((n_maps, tile, tile), F32)
    row = pltpu.VMEM((n_maps, 1, tile), F32)
    return [scores, row, scores, row,
            pltpu.VMEM((n_tiles, n_maps, 1, tile), F32),
            pltpu.VMEM((n_tiles, n_maps, dv + ONES_ROWS, tile), F32)]


def _flash_scratch_bytes(n_maps, n_tiles, tile, dv):
    return (2 * _nbytes((n_maps, tile, tile), F32) + 2 * _nbytes((n_maps, 8, tile), F32)
            + _nbytes((n_tiles, n_maps, 8, tile), F32)
            + _nbytes((n_tiles, n_maps, dv + ONES_ROWS, tile), F32))


def _causal_flash(n_tiles, tile, q_ops_of, k_block, vt_block, scratch):
    buf_a, buf_b, m_all, acc_all = scratch[:2], scratch[2:4], scratch[4], scratch[5]
    n_off = n_tiles * (n_tiles - 1) // 2
    assert n_tiles % 2 == 0 and n_off % 2 == 0
    ones = jnp.ones((ONES_ROWS, tile), BF16)
    kk = lax.broadcasted_iota(jnp.int32, (tile, tile), 0)
    qq = lax.broadcasted_iota(jnp.int32, (tile, tile), 1)
    causal = kk <= qq

    def issue(qi, j, buf):
        s_ref, bm_ref = buf
        for i, (ka, qa) in enumerate(zip(k_block(j), q_ops_of(qi))):
            st = lax.dot_general(ka, qa, _NT, preferred_element_type=F32)
            s_ref[i] = st
            bm_ref[i] = jnp.max(st, axis=0, keepdims=True)

    half = tile // 2

    def issue_diag(qi, buf):
        s_ref, _ = buf
        for i, (ka, qa) in enumerate(zip(k_block(qi), q_ops_of(qi))):
            s_ref[i, :half, :half] = lax.dot_general(ka[:half], qa[:half], _NT,
                                                     preferred_element_type=F32)
            s_ref[i, :, half:] = lax.dot_general(ka, qa[half:], _NT, preferred_element_type=F32)

    def process_diag(buf, qi):
        s_ref, _ = buf
        for i, vt in enumerate(vt_block(qi)):
            vta = jnp.concatenate([vt, ones], axis=0)
            for rows, cols in ((slice(0, half), slice(0, half)), (slice(0, tile), slice(half, tile))):
                st = jnp.where(causal[rows, cols], s_ref[i, rows, cols], -jnp.inf)
                m_new = jnp.max(st, axis=0, keepdims=True)
                p = jnp.exp2(st - m_new).astype(BF16)
                acc_all[qi, i, :, cols] = jnp.dot(vta[:, rows], p, preferred_element_type=F32)
                m_all[qi, i, :, cols] = m_new

    def process(buf, qi, j):
        s_ref, bm_ref = buf
        for i, vt in enumerate(vt_block(j)):
            m = m_all[qi, i]
            m_new = jnp.maximum(m, bm_ref[i])
            alpha = jnp.exp2(m - m_new)
            p = jnp.exp2(s_ref[i] - m_new).astype(BF16)
            pv = jnp.dot(jnp.concatenate([vt, ones], axis=0), p, preferred_element_type=F32)
            acc_all[qi, i] = alpha * acc_all[qi, i] + pv
            m_all[qi, i] = m_new

    issue_diag(0, buf_a)

    for qi in range(0, n_tiles, 2):
        issue_diag(qi + 1, buf_b)
        process_diag(buf_a, qi)
        if qi + 2 < n_tiles:
            issue_diag(qi + 2, buf_a)
        else:
            issue(1, 0, buf_a)
        process_diag(buf_b, qi + 1)

    def following(qi, j):
        wrap = j + 1 == qi
        return jnp.where(wrap, qi + 1, qi), jnp.where(wrap, 0, j + 1)

    def off_pair(qi, j):
        qi1, j1 = following(qi, j)
        issue(qi1, j1, buf_b)
        process(buf_a, qi, j)
        qi2, j2 = following(qi1, j1)
        past = qi2 >= n_tiles
        qi2, j2 = jnp.where(past, 1, qi2), jnp.where(past, 0, j2)
        issue(qi2, j2, buf_a)
        process(buf_b, qi1, j1)
        return qi2, j2

    def off_trip(u, carry):
        for _ in range(OFF_UNROLL):
            carry = off_pair(*carry)
        return carry

    assert n_off % (2 * OFF_UNROLL) == 0
    lax.fori_loop(0, n_off // (2 * OFF_UNROLL), off_trip, (jnp.int32(1), jnp.int32(0)))


def _fox_attn_kernel(qa_ref, ka_ref, vt_ref, *rest, tile, n_casts):
    cast_in, o_ref = rest[:n_casts], rest[n_casts]
    cast_out, flash_scratch = rest[n_casts + 1:2 * n_casts + 1], rest[2 * n_casts + 1:]
    for src, dst in zip(cast_in, cast_out):
        dst[...] = src[...].astype(dst.dtype)
    seq = qa_ref.shape[1]
    n_tiles = seq // tile
    d = HEAD_DIM

    def q_ops_of(qi):
        q0 = pl.multiple_of(qi * tile, tile)
        return [qa_ref[0, pl.ds(q0, tile), h2 * LANES:(h2 + 1) * LANES] for h2 in range(2)]

    def k_block(j):
        start = pl.multiple_of(j * tile, tile)
        return [ka_ref[0, pl.ds(start, tile), h2 * LANES:(h2 + 1) * LANES] for h2 in range(2)]

    def vt_block(j):
        vb = vt_ref[0, 0, j]
        return [vb, vb]

    _causal_flash(n_tiles, tile, q_ops_of, k_block, vt_block, flash_scratch)
    acc_all = flash_scratch[5]

    def emit(qi, carry):
        a0 = acc_all[qi, 0]
        a1 = acc_all[qi, 1]
        yt = jnp.concatenate([a0[:d] / a0[2 * d:2 * d + 1], a1[d:2 * d] / a1[2 * d:2 * d + 1]],
                             axis=0)
        o_ref[0, 0, qi] = yt.astype(o_ref.dtype)
        return carry

    lax.fori_loop(0, n_tiles, emit, 0)


def _fox_attn(qa, ka, vt, side_casts=()):
    B, S, width = qa.shape
    tile = ATTN_TILE
    n_tiles = S // tile
    n_pairs = width // (2 * LANES)
    pair_block = pl.BlockSpec((1, S, 2 * LANES), lambda b, h: (b, 0, h))
    tiles_block = pl.BlockSpec((1, 1, n_tiles, LANES, tile), lambda b, h: (b, h, 0, 0, 0))
    block_bytes = (2 * 6 * _nbytes((S, LANES), BF16)
                   + _flash_scratch_bytes(2, n_tiles, tile, 2 * HEAD_DIM))
    cast_specs, cast_shapes = [], []
    for w, axis, chunk in side_casts:
        n_chunks = w.shape[axis] // chunk
        assert w.shape[axis] % chunk == 0 and n_chunks <= B * n_pairs
        shape = tuple(chunk if a == axis else s for a, s in enumerate(w.shape))

        def index_map(b, h, axis=axis, last=n_chunks - 1):
            step = jnp.minimum(b * n_pairs + h, last)
            return tuple(step if a == axis else 0 for a in range(2))

        cast_specs.append(pl.BlockSpec(shape, index_map))
        cast_shapes.append(jax.ShapeDtypeStruct(w.shape, BF16))
        block_bytes += 2 * (_nbytes(shape, F32) + _nbytes(shape, BF16))
    outs = pl.pallas_call(
        functools.partial(_fox_attn_kernel, tile=tile, n_casts=len(side_casts)),
        grid=(B, n_pairs),
        in_specs=[pair_block, pair_block, tiles_block] + cast_specs,
        out_specs=[tiles_block] + cast_specs,
        out_shape=[jax.ShapeDtypeStruct((B, n_pairs, n_tiles, LANES, tile), BF16)] + cast_shapes,
        scratch_shapes=_flash_scratch(2, n_tiles, tile, 2 * HEAD_DIM),
        compiler_params=_params(2, block_bytes),
        name="fox_attn",
    )(qa, ka, vt, *[w for w, _, _ in side_casts])
    return outs[0], outs[1:]


def _diff_attn_kernel(q_ref, k_ref, vt_ref, lq1_ref, lk1_ref, lq2_ref, lk2_ref, g_ref, o_ref,
                      *flash_scratch, tile, lambda_init):
    seq = q_ref.shape[1]
    n_tiles = seq // tile
    lane = lax.broadcasted_iota(jnp.int32, (tile, LANES), 1)
    lam = (jnp.exp(jnp.sum(lq1_ref[...] * lk1_ref[...], axis=1, keepdims=True))
           - jnp.exp(jnp.sum(lq2_ref[...] * lk2_ref[...], axis=1, keepdims=True)) + lambda_init)

    dv = 2 * HEAD_DIM

    def q_ops_of(qi):
        qp = q_ref[0, pl.ds(pl.multiple_of(qi * tile, tile), tile), :]
        zero = jnp.zeros_like(qp)
        return [jnp.where(lane < HEAD_DIM, qp, zero), jnp.where(lane >= HEAD_DIM, qp, zero)]

    def k_block(j):
        kb = k_ref[0, pl.ds(pl.multiple_of(j * tile, tile), tile), :]
        return [kb, kb]

    def vt_block(j):
        vb = vt_ref[0, 0, j]
        return [vb, vb]

    _causal_flash(n_tiles, tile, q_ops_of, k_block, vt_block, flash_scratch)
    acc_all = flash_scratch[5]

    def emit(qi, carry):
        a1 = acc_all[qi, 0]
        a2 = acc_all[qi, 1]
        yt = a1[:dv] / a1[dv:dv + 1] - lam * (a2[:dv] / a2[dv:dv + 1])
        ms = jnp.mean(yt * yt, axis=0, keepdims=True)
        y = yt * lax.rsqrt(ms + NORM_EPS) * g_ref[...] * (1.0 - lambda_init)
        o_ref[0, 0, qi] = y.astype(o_ref.dtype)
        return carry

    lax.fori_loop(0, n_tiles, emit, 0)


def _diff_attn(qproj, k_sh, vt, lq1, lk1, lq2, lk2, subln_g, *, n_heads, lambda_init):
    B, S, _ = qproj.shape
    tile = ATTN_TILE
    n_tiles = S // tile
    vec = pl.BlockSpec((1, HEAD_DIM), lambda b, h: (0, 0))
    block_bytes = (2 * 4 * _nbytes((S, LANES), BF16)
                   + _flash_scratch_bytes(2, n_tiles, tile, 2 * HEAD_DIM))
    return pl.pallas_call(
        functools.partial(_diff_attn_kernel, tile=tile, lambda_init=lambda_init),
        grid=(B, n_heads),
        in_specs=[
            pl.BlockSpec((1, S, LANES), lambda b, h: (b, 0, h)),
            pl.BlockSpec((1, S, LANES), lambda b, h: (b, 0, h)),
            pl.BlockSpec((1, 1, n_tiles, LANES, tile), lambda b, h: (b, h, 0, 0, 0)),
            vec, vec, vec, vec,
            pl.BlockSpec((2 * HEAD_DIM, 1), lambda b, h: (0, 0)),
        ],
        out_specs=pl.BlockSpec((1, 1, n_tiles, LANES, tile), lambda b, h: (b, h, 0, 0, 0)),
        out_shape=jax.ShapeDtypeStruct((B, n_heads, n_tiles, LANES, tile), BF16),
        scratch_shapes=_flash_scratch(2, n_tiles, tile, 2 * HEAD_DIM),
        compiler_params=_params(2, block_bytes),
        name="diff_attn",
    )(qproj, k_sh, vt, lq1.reshape(1, -1), lk1.reshape(1, -1), lq2.reshape(1, -1),
      lk2.reshape(1, -1), subln_g.reshape(-1, 1))


def _mix_out(x, yt_ref, mq_ref, mk_ref, mv_ref, wo_ref):
    tm = x.shape[0]
    mw = mq_ref.shape[1]
    n_groups, n_sub = yt_ref.shape[1], yt_ref.shape[2]
    y_width = n_groups * LANES
    y = jnp.concatenate(
        [jnp.concatenate([yt_ref[0, g, t].astype(F32).T.astype(BF16) for t in range(n_sub)], axis=0)
         for g in range(n_groups)], axis=1)
    mq = mq_ref[...]
    mk = mk_ref[0]
    mv = mv_ref[0]
    q_head = lax.broadcasted_iota(jnp.int32, (tm, mw), 1) // HEAD_DIM
    v_head = lax.broadcasted_iota(jnp.int32, mv.shape, 1) // HEAD_DIM
    ymem = jnp.zeros((tm, mw), F32)
    for h in range(mw // HEAD_DIM):
        qh = jnp.where(q_head == h, mq, jnp.zeros_like(mq))
        s = lax.dot_general(qh, mk, _NT, preferred_element_type=F32)
        p = jnp.exp2(s - jnp.max(s, axis=1, keepdims=True))
        l = jnp.sum(p, axis=1, keepdims=True)
        vh = jnp.where(v_head == h, mv, jnp.zeros_like(mv))
        ymem = ymem + jnp.dot(p.astype(BF16), vh, preferred_element_type=F32) / l
    acc = jnp.dot(y, wo_ref[0:y_width, :], preferred_element_type=F32)
    acc = acc + jnp.dot(ymem.astype(BF16), wo_ref[y_width:y_width + mw, :],
                        preferred_element_type=F32)
    return x + acc


def _post_kernel(*refs, final, n_chunks):
    x_ref, y_ref, mq_ref, mk_ref, mv_ref, wo_ref, g_ref, wgu_ref, wd_ref = refs[:9]
    wo_ref, wgu_ref, wd_ref = wo_ref.at[0], wgu_ref.at[0], wd_ref.at[0]
    if final:
        fg_ref, o_ref, hn_ref, acc_ref = refs[9:]
    else:
        o_ref, hn_ref, acc_ref = refs[9:]
    x = _mix_out(x_ref[...], y_ref, mq_ref, mk_ref, mv_ref, wo_ref)
    hn_ref[...] = _rms(x, g_ref[...]).astype(BF16)
    acc_ref[...] = x

    d_ff = wd_ref.shape[0]
    tf = d_ff // n_chunks
    for k in range(n_chunks):
        h = hn_ref[...]
        gate = jnp.dot(h, wgu_ref[:, k * tf:(k + 1) * tf], preferred_element_type=F32)
        up = jnp.dot(h, wgu_ref[:, d_ff + k * tf:d_ff + (k + 1) * tf],
                     preferred_element_type=F32)
        a = gate * jax.nn.sigmoid(gate) * up
        acc_ref[...] += jnp.dot(a.astype(BF16), wd_ref[k * tf:(k + 1) * tf, :],
                                preferred_element_type=F32)
    out = acc_ref[...]
    if final:
        out = _rms(out, fg_ref[...])
    o_ref[...] = out


def _post(x2d, yt, proj2d, mq_col, memkv, wo, g, wgu, wd, *, layer, seq, final_g=None):
    T, D = x2d.shape
    n_groups, tile = yt.shape[1], yt.shape[4]
    yw = n_groups * LANES
    n_mem, mw = memkv.shape[1], memkv.shape[2] // 2
    d_ff = wd.shape[1]

    def slab(w):
        return pl.BlockSpec((1,) + w.shape[1:], lambda i: (layer, 0, 0), pipeline_mode=_RESIDENT)

    tm = FFN_TM
    assert T % tm == 0 and seq % tm == 0 and tm % tile == 0 and d_ff % FFN_TF == 0
    n_chunks = d_ff // FFN_TF
    n_seq_tiles = seq // tm
    final = final_g is not None
    in_specs = [
        pl.BlockSpec((tm, D), lambda i: (i, 0)),
        pl.BlockSpec((1, n_groups, tm // tile, LANES, tile),
                     lambda i: (i // n_seq_tiles, 0, i % n_seq_tiles, 0, 0)),
        pl.BlockSpec((tm, mw), lambda i: (i, mq_col)),
        pl.BlockSpec((1, n_mem, mw), lambda i: (i // n_seq_tiles, 0, 0)),
        pl.BlockSpec((1, n_mem, mw), lambda i: (i // n_seq_tiles, 0, 1)),
        slab(wo),
        pl.BlockSpec((1, D), lambda i: (0, 0)),
        slab(wgu),
        slab(wd),
    ]
    args = [x2d, yt, proj2d, memkv, memkv, wo, g.reshape(1, D), wgu, wd]
    if final:
        in_specs.append(pl.BlockSpec((1, D), lambda i: (0, 0)))
        args.append(final_g.reshape(1, D))
    block_bytes = (5 * _nbytes((tm, D), F32) + _nbytes((tm, D), BF16)
                   + 2 * _nbytes((tm, yw + mw), BF16) + 4 * _nbytes((n_mem, mw), BF16)
                   + _nbytes(wo.shape[1:], BF16) + _nbytes(wgu.shape[1:], BF16)
                   + _nbytes(wd.shape[1:], BF16))
    return pl.pallas_call(
        functools.partial(_post_kernel, final=final, n_chunks=n_chunks),
        grid=(T // tm,),
        in_specs=in_specs,
        out_specs=pl.BlockSpec((tm, D), lambda i: (i, 0)),
        out_shape=jax.ShapeDtypeStruct((T, D), F32),
        scratch_shapes=[pltpu.VMEM((tm, D), BF16), pltpu.VMEM((tm, D), F32)],
        compiler_params=_params(1, block_bytes),
        name="post_final" if final else "post",
    )(*args)


def _rope_tables(seq):
    half = HEAD_DIM // 2
    inv_freq = jnp.power(ROPE_THETA, -jnp.arange(half, dtype=F32) * (2.0 / HEAD_DIM))
    ang = jnp.arange(seq, dtype=F32)[:, None] * inv_freq[None, :]
    cos, sin = jnp.cos(ang), jnp.sin(ang)
    reps = LANES // HEAD_DIM
    cos_t = jnp.tile(jnp.concatenate([cos, cos], axis=-1), (1, reps))
    sin_t = jnp.tile(jnp.concatenate([-sin, sin], axis=-1), (1, reps))
    return cos_t, sin_t


def _cast_chunk(n, steps, align):
    return next(c for c in range(align, n + 1, align) if n % c == 0 and n // c <= steps)


def kernel(x, mem, attn_norm_g, mem_norm_g, w_mem_kv, w_out, ffn_norm_g, w_gate_up, w_down,
           a_w_in, a_b_f, b_w_in, b_lambda_q1, b_lambda_k1, b_lambda_q2, b_lambda_k2,
           b_subln_g, kv_norm_g, w_kv_shared, final_norm_g):
    B, S, D = x.shape
    depth = attn_norm_g.shape[0]
    n_a = a_w_in.shape[0]
    n_mem = mem.shape[1]
    mem_w = w_mem_kv.shape[2] // 2
    n_fox = a_b_f.shape[1]
    fox_w = n_fox * HEAD_DIM
    diff_w = b_w_in.shape[1] - mem_w
    n_diff = diff_w // (2 * HEAD_DIM)
    scale = HEAD_DIM ** -0.5 * LOG2E
    T = B * S

    x2d = x.reshape(T, D)
    mem2d = mem.reshape(B * n_mem, D)
    rope_tabs = _rope_tables(S)
    k_sh = vt_sh = None
    wo_all = w_out.astype(BF16)
    wgu_all = wd_all = None
    d_ff = w_down.shape[1]

    for layer in range(depth):
        (memkv,) = _proj(mem2d, [(mem_norm_g[layer], w_mem_kv[layer].astype(BF16),
                                  0, 2 * mem_w, 0)], seq=n_mem)
        memkv = memkv.reshape(B, n_mem, 2 * mem_w)
        if layer < n_a:
            w_in = a_w_in[layer]
            w_main = jnp.concatenate(
                [w_in[:, :fox_w] * scale, w_in[:, fox_w:2 * fox_w],
                 w_in[:, 3 * fox_w + n_fox:] * scale, w_in[:, 2 * fox_w:3 * fox_w]],
                axis=1).astype(BF16)
            gw = jnp.pad(w_in[:, 3 * fox_w:3 * fox_w + n_fox], ((0, 0), (0, LANES - n_fox)))
            gw_hi = gw.astype(BF16)
            gw_lo = (gw - gw_hi.astype(F32)).astype(BF16)
            gate_w = jnp.concatenate([gw_hi, gw_lo], axis=1)
            gate_b = jnp.pad(a_b_f[layer], (0, LANES - n_fox)).reshape(1, LANES)
            proj, vt, qa, ka = _proj(
                x2d, [(attn_norm_g[layer], w_main, 0, 2 * fox_w + mem_w, fox_w)], seq=S,
                gate_w=gate_w, gate_b=gate_b, n_gates=n_fox)
            side = ()
            if wgu_all is None:
                steps = B * (n_fox // 2)
                side = ((w_gate_up.reshape(depth * D, 2 * d_ff), 1,
                         _cast_chunk(2 * d_ff, steps, LANES)),
                        (w_down.reshape(depth * d_ff, D), 0, _cast_chunk(depth * d_ff, steps, 16)))
            y, casts = _fox_attn(qa.reshape(B, S, -1), ka.reshape(B, S, -1), vt, side)
            if side:
                wgu_all = casts[0].reshape(depth, D, 2 * d_ff)
                wd_all = casts[1].reshape(depth, d_ff, D)
            mq_col = 0
        else:
            j = layer - n_a
            streams = [(attn_norm_g[layer], (b_w_in[j] * scale).astype(BF16),
                        diff_w, diff_w + mem_w, 0)]
            if layer == n_a:
                streams.append((kv_norm_g, w_kv_shared.astype(BF16), diff_w, diff_w,
                                w_kv_shared.shape[1] - diff_w))
                proj, k_sh, vt_sh = _proj(x2d, streams, seq=S, rope_tabs=rope_tabs)
                k_sh = k_sh.reshape(B, S, diff_w)
            else:
                (proj,) = _proj(x2d, streams, seq=S, rope_tabs=rope_tabs)
            lambda_init = 0.8 - 0.6 * math.exp(-0.3 * layer)
            y = _diff_attn(proj.reshape(B, S, -1), k_sh, vt_sh, b_lambda_q1[j], b_lambda_k1[j],
                           b_lambda_q2[j], b_lambda_k2[j], b_subln_g[j], n_heads=n_diff,
                           lambda_init=lambda_init)
            mq_col = diff_w // mem_w
        last = layer == depth - 1
        if wgu_all is None:
            wgu_all, wd_all = w_gate_up.astype(BF16), w_down.astype(BF16)
        x2d = _post(x2d, y, proj, mq_col, memkv, wo_all, ffn_norm_g[layer],
                    wgu_all, wd_all, layer=layer, seq=S,
                    final_g=final_norm_g if last else None)
    return x2d.reshape(B, S, D)
```

```python
import functools
import math

import jax
import jax.numpy as jnp
from jax import lax
from jax.experimental import pallas as pl
from jax.experimental.pallas import tpu as pltpu

HEAD_DIM = 64
ROPE_THETA = 10000.0
NORM_EPS = 1e-6
LANES = 128
ATTN_TILE = 512
PROJ_TM = ATTN_TILE
FFN_TM = 1024
FFN_TF = 256
VMEM_CAP = 56 * 1024 * 1024
VMEM_TEMPS = 16 * 1024 * 1024
LOG2E = math.log2(math.e)

F32 = jnp.float32
BF16 = jnp.bfloat16
_NT = (((1,), (1,)), ((), ()))
_RESIDENT = pl.Buffered(1)


def _nbytes(shape, dtype):
    return math.prod(shape) * jnp.dtype(dtype).itemsize


def _params(n_grid, block_bytes):
    limit = min(VMEM_CAP, block_bytes + VMEM_TEMPS)
    return pltpu.CompilerParams(dimension_semantics=("arbitrary",) * n_grid,
                                vmem_limit_bytes=limit)


def _rms(x, g):
    ms = jnp.mean(x * x, axis=-1, keepdims=True)
    return x * lax.rsqrt(ms + NORM_EPS) * g


def _split_bf16(v, n):
    pieces = []
    for _ in range(n - 1):
        p = v.astype(BF16)
        pieces.append(p)
        v = v - p.astype(F32)
    pieces.append(v.astype(BF16))
    return pieces


def _proj_kernel(*refs, streams, n_gates, n_seq_tiles):
    gates = n_gates > 0
    it = iter(refs)
    x_ref = next(it)
    gw = [(next(it), next(it)) for _ in streams]
    any_rope = any(s[0] for s in streams)
    if any_rope:
        cos_ref, sin_ref = next(it), next(it)
    if gates:
        wf_ref, bf_ref = next(it), next(it)
    outs = []
    for (_, _, vt_cols) in streams:
        row_ref = next(it)
        outs.append((row_ref, next(it) if vt_cols else None))
    if gates:
        qa_ref, ka_ref, carry_ref = next(it), next(it), next(it)

    tm = x_ref.shape[0]
    x = x_ref[...]
    xn = x * lax.rsqrt(jnp.mean(x * x, axis=-1, keepdims=True) + NORM_EPS)
    if any_rope:
        cos = cos_ref[...]
        sin = sin_ref[...]
        lane = lax.broadcasted_iota(jnp.int32, (tm, LANES), 1)
        first_half = (lane % HEAD_DIM) < (HEAD_DIM // 2)

    def gate_pieces(hn32, hn):
        lo = (hn32 - hn.astype(F32)).astype(BF16)
        wf = wf_ref[...]
        zz = jnp.dot(hn, wf, preferred_element_type=F32)
        z = (zz[:, :LANES] + zz[:, LANES:]
             + jnp.dot(lo, wf[:, :LANES], preferred_element_type=F32) + bf_ref[...])
        lf = jnp.minimum(z, 0.0) - jnp.log1p(jnp.exp(-jnp.abs(z)))
        l_hi, l_mid, l_lo = _split_bf16(lf, 3)
        row = lax.broadcasted_iota(jnp.int32, (tm, tm), 0)
        col = lax.broadcasted_iota(jnp.int32, (tm, tm), 1)
        tri = (row >= col).astype(BF16)
        cc = jnp.dot(tri, jnp.concatenate([l_hi, l_mid], axis=1), preferred_element_type=F32)
        cs = cc[:, :LANES] + cc[:, LANES:] + jnp.dot(tri, l_lo, preferred_element_type=F32)

        @pl.when((pl.program_id(0) % n_seq_tiles) == 0)
        def _():
            carry_ref[...] = jnp.zeros_like(carry_ref)

        c = cs + carry_ref[...]
        carry_ref[...] = c[tm - 1:tm, :]
        p_hi, p_mid, p_lo = [p.astype(F32) for p in _split_bf16(c * LOG2E, 3)]
        lane_c = lax.broadcasted_iota(jnp.int32, (tm, LANES), 1)
        return jnp.where(lane_c < n_gates, p_hi, jnp.where(
            lane_c < 2 * n_gates, pltpu.roll(p_mid, n_gates, 1),
            pltpu.roll(p_lo, 2 * n_gates, 1)))

    for si, ((rope_cols, row_cols, vt_cols), (g_ref, w_ref), (row_ref, vt_ref)) in enumerate(
            zip(streams, gw, outs)):
        hn32 = xn * g_ref[...]
        hn = hn32.astype(BF16)
        if gates and si == 0:
            packed = gate_pieces(hn32, hn)
        res = jnp.dot(hn, w_ref[...], preferred_element_type=F32)
        for c0 in range(0, rope_cols, LANES):
            t = res[:, c0:c0 + LANES]
            up = pltpu.roll(t, HEAD_DIM // 2, 1)
            dn = pltpu.roll(t, LANES - HEAD_DIM // 2, 1)
            sw = jnp.where(first_half, dn, up)
            row_ref[:, c0:c0 + LANES] = (t * cos + sw * sin).astype(BF16)
        row_from = 2 * n_gates * HEAD_DIM if (gates and si == 0) else 0
        plain = max(rope_cols, row_from)
        if row_cols > plain:
            row_ref[:, plain - row_from:row_cols - row_from] = res[:, plain:row_cols].astype(BF16)
        for gi in range(vt_cols // LANES):
            c0 = row_cols + gi * LANES
            vt_ref[0, gi, 0] = res[:, c0:c0 + LANES].T.astype(BF16)

        if gates and si == 0:
            d = HEAD_DIM
            lane_c = lax.broadcasted_iota(jnp.int32, (tm, LANES), 1)

            def lanes_of(base):
                return ((lane_c == base) | (lane_c == base + n_gates)
                        | (lane_c == base + 2 * n_gates))

            for p in range(n_gates // 2):
                cp = pltpu.roll(packed, d - 2 * p, 1)
                qp = res[:, p * LANES:(p + 1) * LANES]
                kp = res[:, n_gates * d + p * LANES:n_gates * d + (p + 1) * LANES]
                for h2 in range(2):
                    base = (1 - h2) * d
                    is_head = (lane_c >= h2 * d) & (lane_c < h2 * d + d)
                    in_a, in_b = lanes_of(base), lanes_of(base + 1)
                    kc = cp if h2 == 0 else pltpu.roll(cp, d - 1, 1)
                    qc = pltpu.roll(cp, 1 if h2 == 0 else d, 1)
                    k_extra = jnp.where(in_a, kc, jnp.where(in_b, 1.0, 0.0))
                    q_extra = jnp.where(in_b, qc, jnp.where(in_a, -1.0, 0.0))
                    c0 = (2 * p + h2) * LANES
                    ka_ref[:, c0:c0 + LANES] = jnp.where(is_head, kp, k_extra).astype(BF16)
                    qa_ref[:, c0:c0 + LANES] = jnp.where(is_head, qp, q_extra).astype(BF16)


def _proj(x2d, streams, *, seq, rope_tabs=None, gate_w=None, gate_b=None, n_gates=0):
    T, D = x2d.shape
    tm = min(PROJ_TM, T)
    assert T % tm == 0
    gates = n_gates > 0
    assert n_gates % 2 == 0 and HEAD_DIM + 2 + 2 * n_gates <= LANES
    any_rope = any(s[2] for s in streams)
    any_vt = any(s[4] for s in streams)
    assert seq % tm == 0 or not (gates or any_rope or any_vt)
    n_seq_tiles = max(1, seq // tm)
    n_batch = T // seq

    in_specs = [pl.BlockSpec((tm, D), lambda i: (i, 0))]
    args = [x2d]
    block_bytes = 2 * _nbytes((tm, D), F32)
    for g, w, _, _, _ in streams:
        in_specs += [pl.BlockSpec((1, D), lambda i: (0, 0)),
                     pl.BlockSpec(w.shape, lambda i: (0, 0), pipeline_mode=_RESIDENT)]
        args += [g.reshape(1, D), w]
        block_bytes += _nbytes(w.shape, BF16) + _nbytes((tm, w.shape[1]), F32)
    if any_rope:
        tab_spec = pl.BlockSpec((tm, LANES), lambda i: (i % n_seq_tiles, 0))
        in_specs += [tab_spec, tab_spec]
        args += list(rope_tabs)
        block_bytes += 4 * _nbytes((tm, LANES), F32)
    if gates:
        in_specs += [pl.BlockSpec(gate_w.shape, lambda i: (0, 0), pipeline_mode=_RESIDENT),
                     pl.BlockSpec((1, LANES), lambda i: (0, 0))]
        args += [gate_w, gate_b]
        block_bytes += _nbytes(gate_w.shape, BF16)

    out_shape, out_specs = [], []
    for si, (_, w, _, row_cols, vt_cols) in enumerate(streams):
        if gates and si == 0:
            row_cols -= 2 * n_gates * HEAD_DIM
        out_shape.append(jax.ShapeDtypeStruct((T, row_cols), BF16))
        out_specs.append(pl.BlockSpec((tm, row_cols), lambda i: (i, 0)))
        block_bytes += 2 * _nbytes((tm, w.shape[1]), BF16)
        if vt_cols:
            n_groups = vt_cols // LANES
            out_shape.append(
                jax.ShapeDtypeStruct((n_batch, n_groups, n_seq_tiles, LANES, tm), BF16))
            out_specs.append(pl.BlockSpec(
                (1, n_groups, 1, LANES, tm),
                lambda i: (i // n_seq_tiles, 0, i % n_seq_tiles, 0, 0)))
    scratch = []
    if gates:
        aug_cols = n_gates * LANES
        for _ in range(2):
            out_shape.append(jax.ShapeDtypeStruct((T, aug_cols), BF16))
            out_specs.append(pl.BlockSpec((tm, aug_cols), lambda i: (i, 0)))
            block_bytes += 2 * _nbytes((tm, aug_cols), BF16)
        scratch.append(pltpu.VMEM((1, LANES), F32))

    cfg = tuple((s[2], s[3], s[4]) for s in streams)
    return pl.pallas_call(
        functools.partial(_proj_kernel, streams=cfg, n_gates=n_gates, n_seq_tiles=n_seq_tiles),
        grid=(T // tm,),
        in_specs=in_specs,
        out_specs=out_specs,
        out_shape=out_shape,
        scratch_shapes=scratch,
        compiler_params=_params(1, block_bytes),
        name="proj_gates" if gates else ("proj_rope" if any_rope else "proj"),
    )(*args)


ONES_ROWS = 16
OFF_UNROLL = 7


def _flash_scratch(n_maps, n_tiles, tile, dv):
    scores = pltpu.VMEM((n_maps, tile, tile), F32)
    row = pltpu.VMEM((n_maps, 1, tile), F32)
    return [scores, row, scores, row,
            pltpu.VMEM((n_tiles, n_maps, 1, tile), F32),
            pltpu.VMEM((n_tiles, n_maps, dv + ONES_ROWS, tile), F32)]


def _flash_scratch_bytes(n_maps, n_tiles, tile, dv):
    return (2 * _nbytes((n_maps, tile, tile), F32) + 2 * _nbytes((n_maps, 8, tile), F32)
            + _nbytes((n_tiles, n_maps, 8, tile), F32)
            + _nbytes((n_tiles, n_maps, dv + ONES_ROWS, tile), F32))


def _causal_flash(n_tiles, tile, q_ops_of, k_block, vt_block, scratch):
    buf_a, buf_b, m_all, acc_all = scratch[:2], scratch[2:4], scratch[4], scratch[5]
    n_off = n_tiles * (n_tiles - 1) // 2
    assert n_tiles % 2 == 0 and n_off % 2 == 0
    ones = jnp.ones((ONES_ROWS, tile), BF16)
    kk = lax.broadcasted_iota(jnp.int32, (tile, tile), 0)
    qq = lax.broadcasted_iota(jnp.int32, (tile, tile), 1)
    causal = kk <= qq

    def issue(qi, j, buf):
        s_ref, bm_ref = buf
        for i, (ka, qa) in enumerate(zip(k_block(j), q_ops_of(qi))):
            st = lax.dot_general(ka, qa, _NT, preferred_element_type=F32)
            s_ref[i] = st
            bm_ref[i] = jnp.max(st, axis=0, keepdims=True)

    half = tile // 2

    def issue_diag(qi, buf):
        s_ref, _ = buf
        for i, (ka, qa) in enumerate(zip(k_block(qi), q_ops_of(qi))):
            s_ref[i, :half, :half] = lax.dot_general(ka[:half], qa[:half], _NT,
                                                     preferred_element_type=F32)
            s_ref[i, :, half:] = lax.dot_general(ka, qa[half:], _NT, preferred_element_type=F32)

    def process_diag(buf, qi):
        s_ref, _ = buf
        for i, vt in enumerate(vt_block(qi)):
            vta = jnp.concatenate([vt, ones], axis=0)
            for rows, cols in ((slice(0, half), slice(0, half)), (slice(0, tile), slice(half, tile))):
                st = jnp.where(causal[rows, cols], s_ref[i, rows, cols], -jnp.inf)
                m_new = jnp.max(st, axis=0, keepdims=True)
                p = jnp.exp2(st - m_new).astype(BF16)
                acc_all[qi, i, :, cols] = jnp.dot(vta[:, rows], p, preferred_element_type=F32)
                m_all[qi, i, :, cols] = m_new

    def process(buf, qi, j):
        s_ref, bm_ref = buf
        for i, vt in enumerate(vt_block(j)):
            m = m_all[qi, i]
            m_new = jnp.maximum(m, bm_ref[i])
            alpha = jnp.exp2(m - m_new)
            p = jnp.exp2(s_ref[i] - m_new).astype(BF16)
            pv = jnp.dot(jnp.concatenate([vt, ones], axis=0), p, preferred_element_type=F32)
            acc_all[qi, i] = alpha * acc_all[qi, i] + pv
            m_all[qi, i] = m_new

    issue_diag(0, buf_a)

    for qi in range(0, n_tiles, 2):
        issue_diag(qi + 1, buf_b)
        process_diag(buf_a, qi)
        if qi + 2 < n_tiles:
            issue_diag(qi + 2, buf_a)
        else:
            issue(1, 0, buf_a)
        process_diag(buf_b, qi + 1)

    def following(qi, j):
        wrap = j + 1 == qi
        return jnp.where(wrap, qi + 1, qi), jnp.where(wrap, 0, j + 1)

    def off_pair(qi, j):
        qi1, j1 = following(qi, j)
        issue(qi1, j1, buf_b)
        process(buf_a, qi, j)
        qi2, j2 = following(qi1, j1)
        past = qi2 >= n_tiles
        qi2, j2 = jnp.where(past, 1, qi2), jnp.where(past, 0, j2)
        issue(qi2, j2, buf_a)
        process(buf_b, qi1, j1)
        return qi2, j2

    def off_trip(u, carry):
        for _ in range(OFF_UNROLL):
            carry = off_pair(*carry)
        return carry

    assert n_off % (2 * OFF_UNROLL) == 0
    lax.fori_loop(0, n_off // (2 * OFF_UNROLL), off_trip, (jnp.int32(1), jnp.int32(0)))


def _fox_attn_kernel(qa_ref, ka_ref, vt_ref, o_ref, *flash_scratch, tile):
    seq = qa_ref.shape[1]
    n_tiles = seq // tile
    d = HEAD_DIM

    def q_ops_of(qi):
        q0 = pl.multiple_of(qi * tile, tile)
        return [qa_ref[0, pl.ds(q0, tile), h2 * LANES:(h2 + 1) * LANES] for h2 in range(2)]

    def k_block(j):
        start = pl.multiple_of(j * tile, tile)
        return [ka_ref[0, pl.ds(start, tile), h2 * LANES:(h2 + 1) * LANES] for h2 in range(2)]

    def vt_block(j):
        vb = vt_ref[0, 0, j]
        return [vb, vb]

    _causal_flash(n_tiles, tile, q_ops_of, k_block, vt_block, flash_scratch)
    acc_all = flash_scratch[5]

    def emit(qi, carry):
        a0 = acc_all[qi, 0]
        a1 = acc_all[qi, 1]
        yt = jnp.concatenate([a0[:d] / a0[2 * d:2 * d + 1], a1[d:2 * d] / a1[2 * d:2 * d + 1]],
                             axis=0)
        o_ref[0, 0, qi] = yt.astype(o_ref.dtype)
        return carry

    lax.fori_loop(0, n_tiles, emit, 0)


def _fox_attn(qa, ka, vt):
    B, S, width = qa.shape
    tile = ATTN_TILE
    n_tiles = S // tile
    n_pairs = width // (2 * LANES)
    pair_block = pl.BlockSpec((1, S, 2 * LANES), lambda b, h: (b, 0, h))
    tiles_block = pl.BlockSpec((1, 1, n_tiles, LANES, tile), lambda b, h: (b, h, 0, 0, 0))
    block_bytes = (2 * 6 * _nbytes((S, LANES), BF16)
                   + _flash_scratch_bytes(2, n_tiles, tile, 2 * HEAD_DIM))
    return pl.pallas_call(
        functools.partial(_fox_attn_kernel, tile=tile),
        grid=(B, n_pairs),
        in_specs=[pair_block, pair_block, tiles_block],
        out_specs=tiles_block,
        out_shape=jax.ShapeDtypeStruct((B, n_pairs, n_tiles, LANES, tile), BF16),
        scratch_shapes=_flash_scratch(2, n_tiles, tile, 2 * HEAD_DIM),
        compiler_params=_params(2, block_bytes),
        name="fox_attn",
    )(qa, ka, vt)


def _diff_attn_kernel(q_ref, k_ref, vt_ref, lq1_ref, lk1_ref, lq2_ref, lk2_ref, g_ref, o_ref,
                      *flash_scratch, tile, lambda_init):
    seq = q_ref.shape[1]
    n_tiles = seq // tile
    lane = lax.broadcasted_iota(jnp.int32, (tile, LANES), 1)
    lam = (jnp.exp(jnp.sum(lq1_ref[...] * lk1_ref[...], axis=1, keepdims=True))
           - jnp.exp(jnp.sum(lq2_ref[...] * lk2_ref[...], axis=1, keepdims=True)) + lambda_init)

    dv = 2 * HEAD_DIM

    def q_ops_of(qi):
        qp = q_ref[0, pl.ds(pl.multiple_of(qi * tile, tile), tile), :]
        zero = jnp.zeros_like(qp)
        return [jnp.where(lane < HEAD_DIM, qp, zero), jnp.where(lane >= HEAD_DIM, qp, zero)]

    def k_block(j):
        kb = k_ref[0, pl.ds(pl.multiple_of(j * tile, tile), tile), :]
        return [kb, kb]

    def vt_block(j):
        vb = vt_ref[0, 0, j]
        return [vb, vb]

    _causal_flash(n_tiles, tile, q_ops_of, k_block, vt_block, flash_scratch)
    acc_all = flash_scratch[5]

    def emit(qi, carry):
        a1 = acc_all[qi, 0]
        a2 = acc_all[qi, 1]
        yt = a1[:dv] / a1[dv:dv + 1] - lam * (a2[:dv] / a2[dv:dv + 1])
        ms = jnp.mean(yt * yt, axis=0, keepdims=True)
        y = yt * lax.rsqrt(ms + NORM_EPS) * g_ref[...] * (1.0 - lambda_init)
        o_ref[0, 0, qi] = y.astype(o_ref.dtype)
        return carry

    lax.fori_loop(0, n_tiles, emit, 0)


def _diff_attn(qproj, k_sh, vt, lq1, lk1, lq2, lk2, subln_g, *, n_heads, lambda_init):
    B, S, _ = qproj.shape
    tile = ATTN_TILE
    n_tiles = S // tile
    vec = pl.BlockSpec((1, HEAD_DIM), lambda b, h: (0, 0))
    block_bytes = (2 * 4 * _nbytes((S, LANES), BF16)
                   + _flash_scratch_bytes(2, n_tiles, tile, 2 * HEAD_DIM))
    return pl.pallas_call(
        functools.partial(_diff_attn_kernel, tile=tile, lambda_init=lambda_init),
        grid=(B, n_heads),
        in_specs=[
            pl.BlockSpec((1, S, LANES), lambda b, h: (b, 0, h)),
            pl.BlockSpec((1, S, LANES), lambda b, h: (b, 0, h)),
            pl.BlockSpec((1, 1, n_tiles, LANES, tile), lambda b, h: (b, h, 0, 0, 0)),
            vec, vec, vec, vec,
            pl.BlockSpec((2 * HEAD_DIM, 1), lambda b, h: (0, 0)),
        ],
        out_specs=pl.BlockSpec((1, 1, n_tiles, LANES, tile), lambda b, h: (b, h, 0, 0, 0)),
        out_shape=jax.ShapeDtypeStruct((B, n_heads, n_tiles, LANES, tile), BF16),
        scratch_shapes=_flash_scratch(2, n_tiles, tile, 2 * HEAD_DIM),
        compiler_params=_params(2, block_bytes),
        name="diff_attn",
    )(qproj, k_sh, vt, lq1.reshape(1, -1), lk1.reshape(1, -1), lq2.reshape(1, -1),
      lk2.reshape(1, -1), subln_g.reshape(-1, 1))


def _mix_out(x, yt_ref, mq_ref, mk_ref, mv_ref, wo_ref):
    tm = x.shape[0]
    mw = mq_ref.shape[1]
    n_groups, n_sub = yt_ref.shape[1], yt_ref.shape[2]
    y_width = n_groups * LANES
    y = jnp.concatenate(
        [jnp.concatenate([yt_ref[0, g, t].astype(F32).T.astype(BF16) for t in range(n_sub)], axis=0)
         for g in range(n_groups)], axis=1)
    mq = mq_ref[...]
    mk = mk_ref[0]
    mv = mv_ref[0]
    q_head = lax.broadcasted_iota(jnp.int32, (tm, mw), 1) // HEAD_DIM
    v_head = lax.broadcasted_iota(jnp.int32, mv.shape, 1) // HEAD_DIM
    ymem = jnp.zeros((tm, mw), F32)
    for h in range(mw // HEAD_DIM):
        qh = jnp.where(q_head == h, mq, jnp.zeros_like(mq))
        s = lax.dot_general(qh, mk, _NT, preferred_element_type=F32)
        p = jnp.exp2(s - jnp.max(s, axis=1, keepdims=True))
        l = jnp.sum(p, axis=1, keepdims=True)
        vh = jnp.where(v_head == h, mv, jnp.zeros_like(mv))
        ymem = ymem + jnp.dot(p.astype(BF16), vh, preferred_element_type=F32) / l
    acc = jnp.dot(y, wo_ref[0:y_width, :], preferred_element_type=F32)
    acc = acc + jnp.dot(ymem.astype(BF16), wo_ref[y_width:y_width + mw, :],
                        preferred_element_type=F32)
    return x + acc


def _post_kernel(*refs, final, n_chunks):
    x_ref, y_ref, mq_ref, mk_ref, mv_ref, wo_ref, g_ref, wgu_ref, wd_ref = refs[:9]
    wo_ref, wgu_ref, wd_ref = wo_ref.at[0], wgu_ref.at[0], wd_ref.at[0]
    if final:
        fg_ref, o_ref, hn_ref, acc_ref = refs[9:]
    else:
        o_ref, hn_ref, acc_ref = refs[9:]
    x = _mix_out(x_ref[...], y_ref, mq_ref, mk_ref, mv_ref, wo_ref)
    hn_ref[...] = _rms(x, g_ref[...]).astype(BF16)
    acc_ref[...] = x

    d_ff = wd_ref.shape[0]
    tf = d_ff // n_chunks
    for k in range(n_chunks):
        h = hn_ref[...]
        gate = jnp.dot(h, wgu_ref[:, k * tf:(k + 1) * tf], preferred_element_type=F32)
        up = jnp.dot(h, wgu_ref[:, d_ff + k * tf:d_ff + (k + 1) * tf],
                     preferred_element_type=F32)
        a = gate * jax.nn.sigmoid(gate) * up
        acc_ref[...] += jnp.dot(a.astype(BF16), wd_ref[k * tf:(k + 1) * tf, :],
                                preferred_element_type=F32)
    out = acc_ref[...]
    if final:
        out = _rms(out, fg_ref[...])
    o_ref[...] = out


def _post(x2d, yt, proj2d, mq_col, memkv, wo, g, wgu, wd, *, layer, seq, final_g=None):
    T, D = x2d.shape
    n_groups, tile = yt.shape[1], yt.shape[4]
    yw = n_groups * LANES
    n_mem, mw = memkv.shape[1], memkv.shape[2] // 2
    d_ff = wd.shape[1]

    def slab(w):
        return pl.BlockSpec((1,) + w.shape[1:], lambda i: (layer, 0, 0), pipeline_mode=_RESIDENT)

    tm = FFN_TM
    assert T % tm == 0 and seq % tm == 0 and tm % tile == 0 and d_ff % FFN_TF == 0
    n_chunks = d_ff // FFN_TF
    n_seq_tiles = seq // tm
    final = final_g is not None
    in_specs = [
        pl.BlockSpec((tm, D), lambda i: (i, 0)),
        pl.BlockSpec((1, n_groups, tm // tile, LANES, tile),
                     lambda i: (i // n_seq_tiles, 0, i % n_seq_tiles, 0, 0)),
        pl.BlockSpec((tm, mw), lambda i: (i, mq_col)),
        pl.BlockSpec((1, n_mem, mw), lambda i: (i // n_seq_tiles, 0, 0)),
        pl.BlockSpec((1, n_mem, mw), lambda i: (i // n_seq_tiles, 0, 1)),
        slab(wo),
        pl.BlockSpec((1, D), lambda i: (0, 0)),
        slab(wgu),
        slab(wd),
    ]
    args = [x2d, yt, proj2d, memkv, memkv, wo, g.reshape(1, D), wgu, wd]
    if final:
        in_specs.append(pl.BlockSpec((1, D), lambda i: (0, 0)))
        args.append(final_g.reshape(1, D))
    block_bytes = (5 * _nbytes((tm, D), F32) + _nbytes((tm, D), BF16)
                   + 2 * _nbytes((tm, yw + mw), BF16) + 4 * _nbytes((n_mem, mw), BF16)
                   + _nbytes(wo.shape[1:], BF16) + _nbytes(wgu.shape[1:], BF16)
                   + _nbytes(wd.shape[1:], BF16))
    return pl.pallas_call(
        functools.partial(_post_kernel, final=final, n_chunks=n_chunks),
        grid=(T // tm,),
        in_specs=in_specs,
        out_specs=pl.BlockSpec((tm, D), lambda i: (i, 0)),
        out_shape=jax.ShapeDtypeStruct((T, D), F32),
        scratch_shapes=[pltpu.VMEM((tm, D), BF16), pltpu.VMEM((tm, D), F32)],
        compiler_params=_params(1, block_bytes),
        name="post_final" if final else "post",
    )(*args)


def _rope_tables(seq):
    half = HEAD_DIM // 2
    inv_freq = jnp.power(ROPE_THETA, -jnp.arange(half, dtype=F32) * (2.0 / HEAD_DIM))
    ang = jnp.arange(seq, dtype=F32)[:, None] * inv_freq[None, :]
    cos, sin = jnp.cos(ang), jnp.sin(ang)
    reps = LANES // HEAD_DIM
    cos_t = jnp.tile(jnp.concatenate([cos, cos], axis=-1), (1, reps))
    sin_t = jnp.tile(jnp.concatenate([-sin, sin], axis=-1), (1, reps))
    return cos_t, sin_t


def _take_layer_kernel(w_ref, o_ref):
    o_ref[...] = w_ref[0]


def _take_layer(w_stack, layer):
    _, rows, cols = w_stack.shape
    return pl.pallas_call(
        _take_layer_kernel,
        grid=(1,),
        in_specs=[pl.BlockSpec((1, rows, cols), lambda i: (layer, 0, 0))],
        out_specs=pl.BlockSpec((rows, cols), lambda i: (0, 0)),
        out_shape=jax.ShapeDtypeStruct((rows, cols), w_stack.dtype),
        compiler_params=_params(1, 4 * _nbytes((rows, cols), w_stack.dtype)),
        name="take_layer",
    )(w_stack)


def kernel(x, mem, attn_norm_g, mem_norm_g, w_mem_kv, w_out, ffn_norm_g, w_gate_up, w_down,
           a_w_in, a_b_f, b_w_in, b_lambda_q1, b_lambda_k1, b_lambda_q2, b_lambda_k2,
           b_subln_g, kv_norm_g, w_kv_shared, final_norm_g):
    B, S, D = x.shape
    depth = attn_norm_g.shape[0]
    n_a = a_w_in.shape[0]
    n_mem = mem.shape[1]
    mem_w = w_mem_kv.shape[2] // 2
    n_fox = a_b_f.shape[1]
    fox_w = n_fox * HEAD_DIM
    diff_w = b_w_in.shape[1] - mem_w
    n_diff = diff_w // (2 * HEAD_DIM)
    scale = HEAD_DIM ** -0.5 * LOG2E
    T = B * S

    x2d = x.reshape(T, D)
    mem2d = mem.reshape(B * n_mem, D)
    rope_tabs = _rope_tables(S)
    k_sh = vt_sh = None
    wo_all = w_out.astype(BF16)
    wgu_all = w_gate_up.astype(BF16)
    wd_all = w_down.astype(BF16)

    for layer in range(depth):
        (memkv,) = _proj(mem2d, [(mem_norm_g[layer], w_mem_kv[layer].astype(BF16),
                                  0, 2 * mem_w, 0)], seq=n_mem)
        memkv = memkv.reshape(B, n_mem, 2 * mem_w)
        if layer < n_a:
            w_in = _take_layer(a_w_in, layer)
            w_main = jnp.concatenate(
                [w_in[:, :fox_w] * scale, w_in[:, fox_w:2 * fox_w],
                 w_in[:, 3 * fox_w + n_fox:] * scale, w_in[:, 2 * fox_w:3 * fox_w]],
                axis=1).astype(BF16)
            gw = jnp.pad(w_in[:, 3 * fox_w:3 * fox_w + n_fox], ((0, 0), (0, LANES - n_fox)))
            gw_hi = gw.astype(BF16)
            gw_lo = (gw - gw_hi.astype(F32)).astype(BF16)
            gate_w = jnp.concatenate([gw_hi, gw_lo], axis=1)
            gate_b = jnp.pad(a_b_f[layer], (0, LANES - n_fox)).reshape(1, LANES)
            proj, vt, qa, ka = _proj(
                x2d, [(attn_norm_g[layer], w_main, 0, 2 * fox_w + mem_w, fox_w)], seq=S,
                gate_w=gate_w, gate_b=gate_b, n_gates=n_fox)
            y = _fox_attn(qa.reshape(B, S, -1), ka.reshape(B, S, -1), vt)
            mq_col = 0
        else:
            j = layer - n_a
            streams = [(attn_norm_g[layer], (b_w_in[j] * scale).astype(BF16),
                        diff_w, diff_w + mem_w, 0)]
            if layer == n_a:
                streams.append((kv_norm_g, w_kv_shared.astype(BF16), diff_w, diff_w,
                                w_kv_shared.shape[1] - diff_w))
                proj, k_sh, vt_sh = _proj(x2d, streams, seq=S, rope_tabs=rope_tabs)
                k_sh = k_sh.reshape(B, S, diff_w)
            else:
                (proj,) = _proj(x2d, streams, seq=S, rope_tabs=rope_tabs)
            lambda_init = 0.8 - 0.6 * math.exp(-0.3 * layer)
            y = _diff_attn(proj.reshape(B, S, -1), k_sh, vt_sh, b_lambda_q1[j], b_lambda_k1[j],
                           b_lambda_q2[j], b_lambda_k2[j], b_subln_g[j], n_heads=n_diff,
                           lambda_init=lambda_init)
            mq_col = diff_w // mem_w
        last = layer == depth - 1
        x2d = _post(x2d, y, proj, mq_col, memkv, wo_all, ffn_norm_g[layer],
                    wgu_all, wd_all, layer=layer, seq=S,
                    final_g=final_norm_g if last else None)
    return x2d.reshape(B, S, D)
```

```python
import functools
import math

import jax
import jax.numpy as jnp
import numpy as np
from jax import lax
from jax.experimental import pallas as pl
from jax.experimental.pallas import tpu as pltpu

HEAD_DIM = 64
ROPE_THETA = 10000.0
NORM_EPS = 1e-6
LANES = 128
ATTN_TILE = 512
PROJ_TM = ATTN_TILE
FFN_TM = 1024
FFN_TF = 256
VMEM_CAP = 56 * 1024 * 1024
VMEM_TEMPS = 16 * 1024 * 1024
LOG2E = math.log2(math.e)

F32 = jnp.float32
BF16 = jnp.bfloat16
_NT = (((1,), (1,)), ((), ()))
_RESIDENT = pl.Buffered(1)


def _nbytes(shape, dtype):
    return math.prod(shape) * jnp.dtype(dtype).itemsize


def _params(n_grid, block_bytes):
    limit = min(VMEM_CAP, block_bytes + VMEM_TEMPS)
    return pltpu.CompilerParams(dimension_semantics=("arbitrary",) * n_grid,
                                vmem_limit_bytes=limit)


def _rms(x, g):
    ms = jnp.mean(x * x, axis=-1, keepdims=True)
    return x * lax.rsqrt(ms + NORM_EPS) * g


def _split_bf16(v, n):
    pieces = []
    for _ in range(n - 1):
        p = v.astype(BF16)
        pieces.append(p)
        v = v - p.astype(F32)
    pieces.append(v.astype(BF16))
    return pieces


def _proj_kernel(*refs, streams, n_gates, n_seq_tiles):
    gates = n_gates > 0
    it = iter(refs)
    x_ref = next(it)
    gw = [(next(it), next(it)) for _ in streams]
    any_rope = any(s[0] for s in streams)
    if any_rope:
        cos_ref, sin_ref = next(it), next(it)
    if gates:
        wf_ref, bf_ref = next(it), next(it)
    outs = []
    for (_, _, vt_cols) in streams:
        row_ref = next(it)
        outs.append((row_ref, next(it) if vt_cols else None))
    if gates:
        qa_ref, ka_ref, carry_ref = next(it), next(it), next(it)

    tm = x_ref.shape[0]
    x = x_ref[...]
    xn = x * lax.rsqrt(jnp.mean(x * x, axis=-1, keepdims=True) + NORM_EPS)
    if any_rope:
        cos = cos_ref[...]
        sin = sin_ref[...]
        lane = lax.broadcasted_iota(jnp.int32, (tm, LANES), 1)
        first_half = (lane % HEAD_DIM) < (HEAD_DIM // 2)

    def gate_pieces(hn32, hn):
        lo = (hn32 - hn.astype(F32)).astype(BF16)
        wf = wf_ref[...]
        zz = jnp.dot(hn, wf, preferred_element_type=F32)
        z = (zz[:, :LANES] + zz[:, LANES:]
             + jnp.dot(lo, wf[:, :LANES], preferred_element_type=F32) + bf_ref[...])
        lf = jnp.minimum(z, 0.0) - jnp.log1p(jnp.exp(-jnp.abs(z)))
        l_hi, l_mid, l_lo = _split_bf16(lf, 3)
        row = lax.broadcasted_iota(jnp.int32, (tm, tm), 0)
        col = lax.broadcasted_iota(jnp.int32, (tm, tm), 1)
        tri = (row >= col).astype(BF16)
        cc = jnp.dot(tri, jnp.concatenate([l_hi, l_mid], axis=1), preferred_element_type=F32)
        cs = cc[:, :LANES] + cc[:, LANES:] + jnp.dot(tri, l_lo, preferred_element_type=F32)

        @pl.when((pl.program_id(0) % n_seq_tiles) == 0)
        def _():
            carry_ref[...] = jnp.zeros_like(carry_ref)

        c = cs + carry_ref[...]
        carry_ref[...] = c[tm - 1:tm, :]
        p_hi, p_mid, p_lo = [p.astype(F32) for p in _split_bf16(c * LOG2E, 3)]
        lane_c = lax.broadcasted_iota(jnp.int32, (tm, LANES), 1)
        return jnp.where(lane_c < n_gates, p_hi, jnp.where(
            lane_c < 2 * n_gates, pltpu.roll(p_mid, n_gates, 1),
            pltpu.roll(p_lo, 2 * n_gates, 1)))

    for si, ((rope_cols, row_cols, vt_cols), (g_ref, w_ref), (row_ref, vt_ref)) in enumerate(
            zip(streams, gw, outs)):
        hn32 = xn * g_ref[...]
        hn = hn32.astype(BF16)
        if gates and si == 0:
            packed = gate_pieces(hn32, hn)
        res = jnp.dot(hn, w_ref[...], preferred_element_type=F32)
        for c0 in range(0, rope_cols, LANES):
            t = res[:, c0:c0 + LANES]
            up = pltpu.roll(t, HEAD_DIM // 2, 1)
            dn = pltpu.roll(t, LANES - HEAD_DIM // 2, 1)
            sw = jnp.where(first_half, dn, up)
            row_ref[:, c0:c0 + LANES] = (t * cos + sw * sin).astype(BF16)
        row_from = 2 * n_gates * HEAD_DIM if (gates and si == 0) else 0
        plain = max(rope_cols, row_from)
        if row_cols > plain:
            row_ref[:, plain - row_from:row_cols - row_from] = res[:, plain:row_cols].astype(BF16)
        for gi in range(vt_cols // LANES):
            c0 = row_cols + gi * LANES
            vt_ref[0, gi, 0] = res[:, c0:c0 + LANES].T.astype(BF16)

        if gates and si == 0:
            d = HEAD_DIM
            lane_c = lax.broadcasted_iota(jnp.int32, (tm, LANES), 1)

            def lanes_of(base):
                return ((lane_c == base) | (lane_c == base + n_gates)
                        | (lane_c == base + 2 * n_gates))

            for p in range(n_gates // 2):
                cp = pltpu.roll(packed, d - 2 * p, 1)
                qp = res[:, p * LANES:(p + 1) * LANES]
                kp = res[:, n_gates * d + p * LANES:n_gates * d + (p + 1) * LANES]
                for h2 in range(2):
                    base = (1 - h2) * d
                    is_head = (lane_c >= h2 * d) & (lane_c < h2 * d + d)
                    in_a, in_b = lanes_of(base), lanes_of(base + 1)
                    kc = cp if h2 == 0 else pltpu.roll(cp, d - 1, 1)
                    qc = pltpu.roll(cp, 1 if h2 == 0 else d, 1)
                    k_extra = jnp.where(in_a, kc, jnp.where(in_b, 1.0, 0.0))
                    q_extra = jnp.where(in_b, qc, jnp.where(in_a, -1.0, 0.0))
                    c0 = (2 * p + h2) * LANES
                    ka_ref[:, c0:c0 + LANES] = jnp.where(is_head, kp, k_extra).astype(BF16)
                    qa_ref[:, c0:c0 + LANES] = jnp.where(is_head, qp, q_extra).astype(BF16)


def _proj(x2d, streams, *, seq, rope_tabs=None, gate_w=None, gate_b=None, n_gates=0):
    T, D = x2d.shape
    tm = min(PROJ_TM, T)
    assert T % tm == 0
    gates = n_gates > 0
    assert n_gates % 2 == 0 and HEAD_DIM + 2 + 2 * n_gates <= LANES
    any_rope = any(s[2] for s in streams)
    any_vt = any(s[4] for s in streams)
    assert seq % tm == 0 or not (gates or any_rope or any_vt)
    n_seq_tiles = max(1, seq // tm)
    n_batch = T // seq

    in_specs = [pl.BlockSpec((tm, D), lambda i: (i, 0))]
    args = [x2d]
    block_bytes = 2 * _nbytes((tm, D), F32)
    for g, w, _, _, _ in streams:
        in_specs += [pl.BlockSpec((1, D), lambda i: (0, 0)),
                     pl.BlockSpec(w.shape, lambda i: (0, 0), pipeline_mode=_RESIDENT)]
        args += [g.reshape(1, D), w]
        block_bytes += _nbytes(w.shape, BF16) + _nbytes((tm, w.shape[1]), F32)
    if any_rope:
        tab_spec = pl.BlockSpec((tm, LANES), lambda i: (i % n_seq_tiles, 0))
        in_specs += [tab_spec, tab_spec]
        args += list(rope_tabs)
        block_bytes += 4 * _nbytes((tm, LANES), F32)
    if gates:
        in_specs += [pl.BlockSpec(gate_w.shape, lambda i: (0, 0), pipeline_mode=_RESIDENT),
                     pl.BlockSpec((1, LANES), lambda i: (0, 0))]
        args += [gate_w, gate_b]
        block_bytes += _nbytes(gate_w.shape, BF16)

    out_shape, out_specs = [], []
    for si, (_, w, _, row_cols, vt_cols) in enumerate(streams):
        if gates and si == 0:
            row_cols -= 2 * n_gates * HEAD_DIM
        out_shape.append(jax.ShapeDtypeStruct((T, row_cols), BF16))
        out_specs.append(pl.BlockSpec((tm, row_cols), lambda i: (i, 0)))
        block_bytes += 2 * _nbytes((tm, w.shape[1]), BF16)
        if vt_cols:
            n_groups = vt_cols // LANES
            out_shape.append(
                jax.ShapeDtypeStruct((n_batch, n_groups, n_seq_tiles, LANES, tm), BF16))
            out_specs.append(pl.BlockSpec(
                (1, n_groups, 1, LANES, tm),
                lambda i: (i // n_seq_tiles, 0, i % n_seq_tiles, 0, 0)))
    scratch = []
    if gates:
        aug_cols = n_gates * LANES
        for _ in range(2):
            out_shape.append(jax.ShapeDtypeStruct((T, aug_cols), BF16))
            out_specs.append(pl.BlockSpec((tm, aug_cols), lambda i: (i, 0)))
            block_bytes += 2 * _nbytes((tm, aug_cols), BF16)
        scratch.append(pltpu.VMEM((1, LANES), F32))

    cfg = tuple((s[2], s[3], s[4]) for s in streams)
    return pl.pallas_call(
        functools.partial(_proj_kernel, streams=cfg, n_gates=n_gates, n_seq_tiles=n_seq_tiles),
        grid=(T // tm,),
        in_specs=in_specs,
        out_specs=out_specs,
        out_shape=out_shape,
        scratch_shapes=scratch,
        compiler_params=_params(1, block_bytes),
        name="proj_gates" if gates else ("proj_rope" if any_rope else "proj"),
    )(*args)


ONES_ROWS = 16
OFF_UNROLL = 7


def _flash_scratch(n_maps, n_tiles, tile, dv):
    scores = pltpu.VMEM((n_maps, tile, tile), F32)
    row = pltpu.VMEM((n_maps, 1, tile), F32)
    return [scores, row, scores, row,
            pltpu.VMEM((n_tiles, n_maps, 1, tile), F32),
            pltpu.VMEM((n_tiles, n_maps, dv + ONES_ROWS, tile), F32)]


def _flash_scratch_bytes(n_maps, n_tiles, tile, dv):
    return (2 * _nbytes((n_maps, tile, tile), F32) + 2 * _nbytes((n_maps, 8, tile), F32)
            + _nbytes((n_tiles, n_maps, 8, tile), F32)
            + _nbytes((n_tiles, n_maps, dv + ONES_ROWS, tile), F32))


def _causal_flash(n_tiles, tile, q_ops_of, k_block, vt_block, scratch):
    buf_a, buf_b, m_all, acc_all = scratch[:2], scratch[2:4], scratch[4], scratch[5]
    n_off = n_tiles * (n_tiles - 1) // 2
    assert n_tiles % 2 == 0 and n_off % 2 == 0
    ones = jnp.ones((ONES_ROWS, tile), BF16)
    kk = lax.broadcasted_iota(jnp.int32, (tile, tile), 0)
    qq = lax.broadcasted_iota(jnp.int32, (tile, tile), 1)
    causal = kk <= qq

    def issue(qi, j, buf):
        s_ref, bm_ref = buf
        for i, (ka, qa) in enumerate(zip(k_block(j), q_ops_of(qi))):
            st = lax.dot_general(ka, qa, _NT, preferred_element_type=F32)
            s_ref[i] = st
            bm_ref[i] = jnp.max(st, axis=0, keepdims=True)

    half = tile // 2

    quadrants = ((slice(0, half), slice(0, half)), (slice(0, tile), slice(half, tile)))

    def issue_diag(qi, buf):
        s_ref, bm_ref = buf
        for i, (ka, qa) in enumerate(zip(k_block(qi), q_ops_of(qi))):
            for rows, cols in quadrants:
                st = lax.dot_general(ka[rows], qa[cols], _NT, preferred_element_type=F32)
                st = jnp.where(causal[rows, cols], st, -jnp.inf)
                s_ref[i, rows, cols] = st
                bm_ref[i, :, cols] = jnp.max(st, axis=0, keepdims=True)

    def process_diag(buf, qi):
        s_ref, bm_ref = buf
        for i, vt in enumerate(vt_block(qi)):
            vta = jnp.concatenate([vt, ones], axis=0)
            for rows, cols in quadrants:
                m_new = bm_ref[i, :, cols]
                p = jnp.exp2(s_ref[i, rows, cols] - m_new).astype(BF16)
                acc_all[qi, i, :, cols] = jnp.dot(vta[:, rows], p, preferred_element_type=F32)
                m_all[qi, i, :, cols] = m_new

    def process(buf, qi, j):
        s_ref, bm_ref = buf
        for i, vt in enumerate(vt_block(j)):
            m = m_all[qi, i]
            m_new = jnp.maximum(m, bm_ref[i])
            alpha = jnp.exp2(m - m_new)
            p = jnp.exp2(s_ref[i] - m_new).astype(BF16)
            pv = jnp.dot(jnp.concatenate([vt, ones], axis=0), p, preferred_element_type=F32)
            acc_all[qi, i] = alpha * acc_all[qi, i] + pv
            m_all[qi, i] = m_new

    issue_diag(0, buf_a)

    for qi in range(0, n_tiles, 2):
        issue_diag(qi + 1, buf_b)
        process_diag(buf_a, qi)
        if qi + 2 < n_tiles:
            issue_diag(qi + 2, buf_a)
        else:
            issue(1, 0, buf_a)
        process_diag(buf_b, qi + 1)

    def following(qi, j):
        wrap = j + 1 == qi
        return jnp.where(wrap, qi + 1, qi), jnp.where(wrap, 0, j + 1)

    def off_pair(qi, j):
        qi1, j1 = following(qi, j)
        issue(qi1, j1, buf_b)
        process(buf_a, qi, j)
        qi2, j2 = following(qi1, j1)
        past = qi2 >= n_tiles
        qi2, j2 = jnp.where(past, 1, qi2), jnp.where(past, 0, j2)
        issue(qi2, j2, buf_a)
        process(buf_b, qi1, j1)
        return qi2, j2

    def off_trip(u, carry):
        for _ in range(OFF_UNROLL):
            carry = off_pair(*carry)
        return carry

    assert n_off % (2 * OFF_UNROLL) == 0
    lax.fori_loop(0, n_off // (2 * OFF_UNROLL), off_trip, (jnp.int32(1), jnp.int32(0)))


def _fox_attn_kernel(qa_ref, ka_ref, vt_ref, o_ref, *flash_scratch, tile):
    seq = qa_ref.shape[1]
    n_tiles = seq // tile
    d = HEAD_DIM

    def q_ops_of(qi):
        q0 = pl.multiple_of(qi * tile, tile)
        return [qa_ref[0, pl.ds(q0, tile), h2 * LANES:(h2 + 1) * LANES] for h2 in range(2)]

    def k_block(j):
        start = pl.multiple_of(j * tile, tile)
        return [ka_ref[0, pl.ds(start, tile), h2 * LANES:(h2 + 1) * LANES] for h2 in range(2)]

    def vt_block(j):
        vb = vt_ref[0, 0, j]
        return [vb, vb]

    _causal_flash(n_tiles, tile, q_ops_of, k_block, vt_block, flash_scratch)
    acc_all = flash_scratch[5]

    def emit(qi, carry):
        a0 = acc_all[qi, 0]
        a1 = acc_all[qi, 1]
        yt = jnp.concatenate([a0[:d] / a0[2 * d:2 * d + 1], a1[d:2 * d] / a1[2 * d:2 * d + 1]],
                             axis=0)
        o_ref[0, 0, qi] = yt.astype(o_ref.dtype)
        return carry

    lax.fori_loop(0, n_tiles, emit, 0)


def _fox_attn(qa, ka, vt):
    B, S, width = qa.shape
    tile = ATTN_TILE
    n_tiles = S // tile
    n_pairs = width // (2 * LANES)
    pair_block = pl.BlockSpec((1, S, 2 * LANES), lambda b, h: (b, 0, h))
    tiles_block = pl.BlockSpec((1, 1, n_tiles, LANES, tile), lambda b, h: (b, h, 0, 0, 0))
    block_bytes = (2 * 6 * _nbytes((S, LANES), BF16)
                   + _flash_scratch_bytes(2, n_tiles, tile, 2 * HEAD_DIM))
    return pl.pallas_call(
        functools.partial(_fox_attn_kernel, tile=tile),
        grid=(B, n_pairs),
        in_specs=[pair_block, pair_block, tiles_block],
        out_specs=tiles_block,
        out_shape=jax.ShapeDtypeStruct((B, n_pairs, n_tiles, LANES, tile), BF16),
        scratch_shapes=_flash_scratch(2, n_tiles, tile, 2 * HEAD_DIM),
        compiler_params=_params(2, block_bytes),
        name="fox_attn",
    )(qa, ka, vt)


def _diff_attn_kernel(q_ref, k_ref, vt_ref, lq1_ref, lk1_ref, lq2_ref, lk2_ref, g_ref, o_ref,
                      *flash_scratch, tile, lambda_init):
    seq = q_ref.shape[1]
    n_tiles = seq // tile
    lane = lax.broadcasted_iota(jnp.int32, (tile, LANES), 1)
    lam = (jnp.exp(jnp.sum(lq1_ref[...] * lk1_ref[...], axis=1, keepdims=True))
           - jnp.exp(jnp.sum(lq2_ref[...] * lk2_ref[...], axis=1, keepdims=True)) + lambda_init)

    dv = 2 * HEAD_DIM

    def q_ops_of(qi):
        qp = q_ref[0, pl.ds(pl.multiple_of(qi * tile, tile), tile), :]
        zero = jnp.zeros_like(qp)
        return [jnp.where(lane < HEAD_DIM, qp, zero), jnp.where(lane >= HEAD_DIM, qp, zero)]

    def k_block(j):
        kb = k_ref[0, pl.ds(pl.multiple_of(j * tile, tile), tile), :]
        return [kb, kb]

    def vt_block(j):
        vb = vt_ref[0, 0, j]
        return [vb, vb]

    _causal_flash(n_tiles, tile, q_ops_of, k_block, vt_block, flash_scratch)
    acc_all = flash_scratch[5]

    def emit(qi, carry):
        a1 = acc_all[qi, 0]
        a2 = acc_all[qi, 1]
        yt = a1[:dv] / a1[dv:dv + 1] - lam * (a2[:dv] / a2[dv:dv + 1])
        ms = jnp.mean(yt * yt, axis=0, keepdims=True)
        y = yt * lax.rsqrt(ms + NORM_EPS) * g_ref[...] * (1.0 - lambda_init)
        o_ref[0, 0, qi] = y.astype(o_ref.dtype)
        return carry

    lax.fori_loop(0, n_tiles, emit, 0)


def _diff_attn(qproj, k_sh, vt, lq1, lk1, lq2, lk2, subln_g, *, n_heads, lambda_init):
    B, S, _ = qproj.shape
    tile = ATTN_TILE
    n_tiles = S // tile
    vec = pl.BlockSpec((1, HEAD_DIM), lambda b, h: (0, 0))
    block_bytes = (2 * 4 * _nbytes((S, LANES), BF16)
                   + _flash_scratch_bytes(2, n_tiles, tile, 2 * HEAD_DIM))
    return pl.pallas_call(
        functools.partial(_diff_attn_kernel, tile=tile, lambda_init=lambda_init),
        grid=(B, n_heads),
        in_specs=[
            pl.BlockSpec((1, S, LANES), lambda b, h: (b, 0, h)),
            pl.BlockSpec((1, S, LANES), lambda b, h: (b, 0, h)),
            pl.BlockSpec((1, 1, n_tiles, LANES, tile), lambda b, h: (b, h, 0, 0, 0)),
            vec, vec, vec, vec,
            pl.BlockSpec((2 * HEAD_DIM, 1), lambda b, h: (0, 0)),
        ],
        out_specs=pl.BlockSpec((1, 1, n_tiles, LANES, tile), lambda b, h: (b, h, 0, 0, 0)),
        out_shape=jax.ShapeDtypeStruct((B, n_heads, n_tiles, LANES, tile), BF16),
        scratch_shapes=_flash_scratch(2, n_tiles, tile, 2 * HEAD_DIM),
        compiler_params=_params(2, block_bytes),
        name="diff_attn",
    )(qproj, k_sh, vt, lq1.reshape(1, -1), lk1.reshape(1, -1), lq2.reshape(1, -1),
      lk2.reshape(1, -1), subln_g.reshape(-1, 1))


def _mix_out(x, yt_ref, mq_ref, mk_ref, mv_ref, wo_ref):
    tm = x.shape[0]
    mw = mq_ref.shape[1]
    n_groups, n_sub = yt_ref.shape[1], yt_ref.shape[2]
    y_width = n_groups * LANES
    y = jnp.concatenate(
        [jnp.concatenate([yt_ref[0, g, t].astype(F32).T.astype(BF16) for t in range(n_sub)], axis=0)
         for g in range(n_groups)], axis=1)
    mq = mq_ref[...]
    mk = mk_ref[0]
    mv = mv_ref[0]
    q_head = lax.broadcasted_iota(jnp.int32, (tm, mw), 1) // HEAD_DIM
    v_head = lax.broadcasted_iota(jnp.int32, mv.shape, 1) // HEAD_DIM
    ymem = jnp.zeros((tm, mw), F32)
    for h in range(mw // HEAD_DIM):
        qh = jnp.where(q_head == h, mq, jnp.zeros_like(mq))
        s = lax.dot_general(qh, mk, _NT, preferred_element_type=F32)
        p = jnp.exp2(s - jnp.max(s, axis=1, keepdims=True))
        l = jnp.sum(p, axis=1, keepdims=True)
        vh = jnp.where(v_head == h, mv, jnp.zeros_like(mv))
        ymem = ymem + jnp.dot(p.astype(BF16), vh, preferred_element_type=F32) / l
    acc = jnp.dot(y, wo_ref[0:y_width, :], preferred_element_type=F32)
    acc = acc + jnp.dot(ymem.astype(BF16), wo_ref[y_width:y_width + mw, :],
                        preferred_element_type=F32)
    return x + acc


def _post_kernel(*refs, final, n_chunks):
    x_ref, y_ref, mq_ref, mk_ref, mv_ref, wo_ref, g_ref, wgu_ref, wd_ref = refs[:9]
    wo_ref, wgu_ref, wd_ref = wo_ref.at[0], wgu_ref.at[0], wd_ref.at[0]
    if final:
        fg_ref, o_ref, hn_ref, acc_ref = refs[9:]
    else:
        o_ref, hn_ref, acc_ref = refs[9:]
    x = _mix_out(x_ref[...], y_ref, mq_ref, mk_ref, mv_ref, wo_ref)
    hn_ref[...] = _rms(x, g_ref[...]).astype(BF16)
    acc_ref[...] = x

    d_ff = wd_ref.shape[0]
    tf = d_ff // n_chunks
    for k in range(n_chunks):
        h = hn_ref[...]
        gate = jnp.dot(h, wgu_ref[:, k * tf:(k + 1) * tf], preferred_element_type=F32)
        up = jnp.dot(h, wgu_ref[:, d_ff + k * tf:d_ff + (k + 1) * tf],
                     preferred_element_type=F32)
        a = gate * jax.nn.sigmoid(gate) * up
        acc_ref[...] += jnp.dot(a.astype(BF16), wd_ref[k * tf:(k + 1) * tf, :],
                                preferred_element_type=F32)
    out = acc_ref[...]
    if final:
        out = _rms(out, fg_ref[...])
    o_ref[...] = out


def _post(x2d, yt, proj2d, mq_col, memkv, wo, g, wgu, wd, *, layer, seq, final_g=None):
    T, D = x2d.shape
    n_groups, tile = yt.shape[1], yt.shape[4]
    yw = n_groups * LANES
    n_mem, mw = memkv.shape[1], memkv.shape[2] // 2
    d_ff = wd.shape[1]

    def slab(w):
        return pl.BlockSpec((1,) + w.shape[1:], lambda i: (layer, 0, 0), pipeline_mode=_RESIDENT)

    tm = FFN_TM
    assert T % tm == 0 and seq % tm == 0 and tm % tile == 0 and d_ff % FFN_TF == 0
    n_chunks = d_ff // FFN_TF
    n_seq_tiles = seq // tm
    final = final_g is not None
    in_specs = [
        pl.BlockSpec((tm, D), lambda i: (i, 0)),
        pl.BlockSpec((1, n_groups, tm // tile, LANES, tile),
                     lambda i: (i // n_seq_tiles, 0, i % n_seq_tiles, 0, 0)),
        pl.BlockSpec((tm, mw), lambda i: (i, mq_col)),
        pl.BlockSpec((1, n_mem, mw), lambda i: (i // n_seq_tiles, 0, 0)),
        pl.BlockSpec((1, n_mem, mw), lambda i: (i // n_seq_tiles, 0, 1)),
        slab(wo),
        pl.BlockSpec((1, D), lambda i: (0, 0)),
        slab(wgu),
        slab(wd),
    ]
    args = [x2d, yt, proj2d, memkv, memkv, wo, g.reshape(1, D), wgu, wd]
    if final:
        in_specs.append(pl.BlockSpec((1, D), lambda i: (0, 0)))
        args.append(final_g.reshape(1, D))
    block_bytes = (5 * _nbytes((tm, D), F32) + _nbytes((tm, D), BF16)
                   + 2 * _nbytes((tm, yw + mw), BF16) + 4 * _nbytes((n_mem, mw), BF16)
                   + _nbytes(wo.shape[1:], BF16) + _nbytes(wgu.shape[1:], BF16)
                   + _nbytes(wd.shape[1:], BF16))
    return pl.pallas_call(
        functools.partial(_post_kernel, final=final, n_chunks=n_chunks),
        grid=(T // tm,),
        in_specs=in_specs,
        out_specs=pl.BlockSpec((tm, D), lambda i: (i, 0)),
        out_shape=jax.ShapeDtypeStruct((T, D), F32),
        scratch_shapes=[pltpu.VMEM((tm, D), BF16), pltpu.VMEM((tm, D), F32)],
        compiler_params=_params(1, block_bytes),
        name="post_final" if final else "post",
    )(*args)


def _rope_tables(seq):
    half = HEAD_DIM // 2
    f32 = np.float32
    inv_freq = np.power(f32(ROPE_THETA), -np.arange(half, dtype=f32) * f32(2.0 / HEAD_DIM))
    ang = np.arange(seq, dtype=f32)[:, None] * inv_freq[None, :]
    cos, sin = np.cos(ang).astype(f32), np.sin(ang).astype(f32)
    reps = LANES // HEAD_DIM
    cos_t = np.tile(np.concatenate([cos, cos], axis=-1), (1, reps))
    sin_t = np.tile(np.concatenate([-sin, sin], axis=-1), (1, reps))
    return jnp.asarray(cos_t), jnp.asarray(sin_t)


def kernel(x, mem, attn_norm_g, mem_norm_g, w_mem_kv, w_out, ffn_norm_g, w_gate_up, w_down,
           a_w_in, a_b_f, b_w_in, b_lambda_q1, b_lambda_k1, b_lambda_q2, b_lambda_k2,
           b_subln_g, kv_norm_g, w_kv_shared, final_norm_g):
    B, S, D = x.shape
    depth = attn_norm_g.shape[0]
    n_a = a_w_in.shape[0]
    n_mem = mem.shape[1]
    mem_w = w_mem_kv.shape[2] // 2
    n_fox = a_b_f.shape[1]
    fox_w = n_fox * HEAD_DIM
    diff_w = b_w_in.shape[1] - mem_w
    n_diff = diff_w // (2 * HEAD_DIM)
    scale = HEAD_DIM ** -0.5 * LOG2E
    T = B * S

    x2d = x.reshape(T, D)
    mem2d = mem.reshape(B * n_mem, D)
    rope_tabs = _rope_tables(S)
    k_sh = vt_sh = None
    wo_all = w_out.astype(BF16)
    wgu_all = w_gate_up.astype(BF16)
    wd_all = w_down.astype(BF16)

    for layer in range(depth):
        (memkv,) = _proj(mem2d, [(mem_norm_g[layer], w_mem_kv[layer].astype(BF16),
                                  0, 2 * mem_w, 0)], seq=n_mem)
        memkv = memkv.reshape(B, n_mem, 2 * mem_w)
        if layer < n_a:
            w_in = a_w_in[layer]
            w_main = jnp.concatenate(
                [w_in[:, :fox_w] * scale, w_in[:, fox_w:2 * fox_w],
                 w_in[:, 3 * fox_w + n_fox:] * scale, w_in[:, 2 * fox_w:3 * fox_w]],
                axis=1).astype(BF16)
            gw = jnp.pad(w_in[:, 3 * fox_w:3 * fox_w + n_fox], ((0, 0), (0, LANES - n_fox)))
            gw_hi = gw.astype(BF16)
            gw_lo = (gw - gw_hi.astype(F32)).astype(BF16)
            gate_w = jnp.concatenate([gw_hi, gw_lo], axis=1)
            gate_b = jnp.pad(a_b_f[layer], (0, LANES - n_fox)).reshape(1, LANES)
            proj, vt, qa, ka = _proj(
                x2d, [(attn_norm_g[layer], w_main, 0, 2 * fox_w + mem_w, fox_w)], seq=S,
                gate_w=gate_w, gate_b=gate_b, n_gates=n_fox)
            y = _fox_attn(qa.reshape(B, S, -1), ka.reshape(B, S, -1), vt)
            mq_col = 0
        else:
            j = layer - n_a
            streams = [(attn_norm_g[layer], (b_w_in[j] * scale).astype(BF16),
                        diff_w, diff_w + mem_w, 0)]
            if layer == n_a:
                streams.append((kv_norm_g, w_kv_shared.astype(BF16), diff_w, diff_w,
                                w_kv_shared.shape[1] - diff_w))
                proj, k_sh, vt_sh = _proj(x2d, streams, seq=S, rope_tabs=rope_tabs)
                k_sh = k_sh.reshape(B, S, diff_w)
            else:
                (proj,) = _proj(x2d, streams, seq=S, rope_tabs=rope_tabs)
            lambda_init = 0.8 - 0.6 * math.exp(-0.3 * layer)
            y = _diff_attn(proj.reshape(B, S, -1), k_sh, vt_sh, b_lambda_q1[j], b_lambda_k1[j],
                           b_lambda_q2[j], b_lambda_k2[j], b_subln_g[j], n_heads=n_diff,
                           lambda_init=lambda_init)
            mq_col = diff_w // mem_w
        last = layer == depth - 1
        x2d = _post(x2d, y, proj, mq_col, memkv, wo_all, ffn_norm_g[layer],
                    wgu_all, wd_all, layer=layer, seq=S,
                    final_g=final_norm_g if last else None)
    return x2d.reshape(B, S, D)
```

```python
import functools
import math

import jax
import jax.numpy as jnp
import numpy as np
from jax import lax
from jax.experimental import pallas as pl
from jax.experimental.pallas import tpu as pltpu

HEAD_DIM = 64
ROPE_THETA = 10000.0
NORM_EPS = 1e-6
LANES = 128
ATTN_TILE = 512
PROJ_TM = ATTN_TILE
FFN_TM = 1024
FFN_TF = 256
VMEM_CAP = 56 * 1024 * 1024
VMEM_TEMPS = 16 * 1024 * 1024
LOG2E = math.log2(math.e)

F32 = jnp.float32
BF16 = jnp.bfloat16
_NT = (((1,), (1,)), ((), ()))
_RESIDENT = pl.Buffered(1)


def _nbytes(shape, dtype):
    return math.prod(shape) * jnp.dtype(dtype).itemsize


def _params(n_grid, block_bytes):
    limit = min(VMEM_CAP, block_bytes + VMEM_TEMPS)
    return pltpu.CompilerParams(dimension_semantics=("arbitrary",) * n_grid,
                                vmem_limit_bytes=limit)


def _rms(x, g):
    ms = jnp.mean(x * x, axis=-1, keepdims=True)
    return x * lax.rsqrt(ms + NORM_EPS) * g


def _split_bf16(v, n):
    pieces = []
    for _ in range(n - 1):
        p = v.astype(BF16)
        pieces.append(p)
        v = v - p.astype(F32)
    pieces.append(v.astype(BF16))
    return pieces


def _proj_kernel(*refs, streams, n_gates, n_seq_tiles):
    gates = n_gates > 0
    it = iter(refs)
    x_ref = next(it)
    gw = [(next(it), next(it)) for _ in streams]
    any_rope = any(s[0] for s in streams)
    if any_rope:
        cos_ref, sin_ref = next(it), next(it)
    if gates:
        wf_ref, bf_ref = next(it), next(it)
    outs = []
    for (_, _, vt_cols) in streams:
        row_ref = next(it)
        outs.append((row_ref, next(it) if vt_cols else None))
    if gates:
        qa_ref, ka_ref, carry_ref = next(it), next(it), next(it)

    tm = x_ref.shape[0]
    x = x_ref[...]
    xn = x * lax.rsqrt(jnp.mean(x * x, axis=-1, keepdims=True) + NORM_EPS)
    if any_rope:
        cos = cos_ref[...]
        sin = sin_ref[...]
        lane = lax.broadcasted_iota(jnp.int32, (tm, LANES), 1)
        first_half = (lane % HEAD_DIM) < (HEAD_DIM // 2)

    def gate_pieces(hn32, hn):
        lo = (hn32 - hn.astype(F32)).astype(BF16)
        wf = wf_ref[...]
        zz = jnp.dot(hn, wf, preferred_element_type=F32)
        z = (zz[:, :LANES] + zz[:, LANES:]
             + jnp.dot(lo, wf[:, :LANES], preferred_element_type=F32) + bf_ref[...])
        lf = jnp.minimum(z, 0.0) - jnp.log1p(jnp.exp(-jnp.abs(z)))
        l_hi, l_mid, l_lo = _split_bf16(lf, 3)
        row = lax.broadcasted_iota(jnp.int32, (tm, tm), 0)
        col = lax.broadcasted_iota(jnp.int32, (tm, tm), 1)
        tri = (row >= col).astype(BF16)
        cc = jnp.dot(tri, jnp.concatenate([l_hi, l_mid], axis=1), preferred_element_type=F32)
        cs = cc[:, :LANES] + cc[:, LANES:] + jnp.dot(tri, l_lo, preferred_element_type=F32)

        @pl.when((pl.program_id(0) % n_seq_tiles) == 0)
        def _():
            carry_ref[...] = jnp.zeros_like(carry_ref)

        c = cs + carry_ref[...]
        carry_ref[...] = c[tm - 1:tm, :]
        p_hi, p_mid, p_lo = [p.astype(F32) for p in _split_bf16(c * LOG2E, 3)]
        lane_c = lax.broadcasted_iota(jnp.int32, (tm, LANES), 1)
        return jnp.where(lane_c < n_gates, p_hi, jnp.where(
            lane_c < 2 * n_gates, pltpu.roll(p_mid, n_gates, 1),
            pltpu.roll(p_lo, 2 * n_gates, 1)))

    for si, ((rope_cols, row_cols, vt_cols), (g_ref, w_ref), (row_ref, vt_ref)) in enumerate(
            zip(streams, gw, outs)):
        hn32 = xn * g_ref[...]
        hn = hn32.astype(BF16)
        if gates and si == 0:
            packed = gate_pieces(hn32, hn)
        res = jnp.dot(hn, w_ref[...], preferred_element_type=F32)
        for c0 in range(0, rope_cols, LANES):
            t = res[:, c0:c0 + LANES]
            up = pltpu.roll(t, HEAD_DIM // 2, 1)
            dn = pltpu.roll(t, LANES - HEAD_DIM // 2, 1)
            sw = jnp.where(first_half, dn, up)
            row_ref[:, c0:c0 + LANES] = (t * cos + sw * sin).astype(BF16)
        row_from = 2 * n_gates * HEAD_DIM if (gates and si == 0) else 0
        plain = max(rope_cols, row_from)
        if row_cols > plain:
            row_ref[:, plain - row_from:row_cols - row_from] = res[:, plain:row_cols].astype(BF16)
        for gi in range(vt_cols // LANES):
            c0 = row_cols + gi * LANES
            vt_ref[0, gi, 0] = res[:, c0:c0 + LANES].T.astype(BF16)

        if gates and si == 0:
            d = HEAD_DIM
            lane_c = lax.broadcasted_iota(jnp.int32, (tm, LANES), 1)

            def lanes_of(base):
                return ((lane_c == base) | (lane_c == base + n_gates)
                        | (lane_c == base + 2 * n_gates))

            for p in range(n_gates // 2):
                cp = pltpu.roll(packed, d - 2 * p, 1)
                qp = res[:, p * LANES:(p + 1) * LANES]
                kp = res[:, n_gates * d + p * LANES:n_gates * d + (p + 1) * LANES]
                for h2 in range(2):
                    base = (1 - h2) * d
                    is_head = (lane_c >= h2 * d) & (lane_c < h2 * d + d)
                    in_a, in_b = lanes_of(base), lanes_of(base + 1)
                    kc = cp if h2 == 0 else pltpu.roll(cp, d - 1, 1)
                    qc = pltpu.roll(cp, 1 if h2 == 0 else d, 1)
                    k_extra = jnp.where(in_a, kc, jnp.where(in_b, 1.0, 0.0))
                    q_extra = jnp.where(in_b, qc, jnp.where(in_a, -1.0, 0.0))
                    c0 = (2 * p + h2) * LANES
                    ka_ref[:, c0:c0 + LANES] = jnp.where(is_head, kp, k_extra).astype(BF16)
                    qa_ref[:, c0:c0 + LANES] = jnp.where(is_head, qp, q_extra).astype(BF16)


def _proj(x2d, streams, *, seq, rope_tabs=None, gate_w=None, gate_b=None, n_gates=0):
    T, D = x2d.shape
    tm = min(PROJ_TM, T)
    assert T % tm == 0
    gates = n_gates > 0
    assert n_gates % 2 == 0 and HEAD_DIM + 2 + 2 * n_gates <= LANES
    any_rope = any(s[2] for s in streams)
    any_vt = any(s[4] for s in streams)
    assert seq % tm == 0 or not (gates or any_rope or any_vt)
    n_seq_tiles = max(1, seq // tm)
    n_batch = T // seq

    in_specs = [pl.BlockSpec((tm, D), lambda i: (i, 0))]
    args = [x2d]
    block_bytes = 2 * _nbytes((tm, D), F32)
    for g, w, _, _, _ in streams:
        in_specs += [pl.BlockSpec((1, D), lambda i: (0, 0)),
                     pl.BlockSpec(w.shape, lambda i: (0, 0), pipeline_mode=_RESIDENT)]
        args += [g.reshape(1, D), w]
        block_bytes += _nbytes(w.shape, BF16) + _nbytes((tm, w.shape[1]), F32)
    if any_rope:
        tab_spec = pl.BlockSpec((tm, LANES), lambda i: (i % n_seq_tiles, 0))
        in_specs += [tab_spec, tab_spec]
        args += list(rope_tabs)
        block_bytes += 4 * _nbytes((tm, LANES), F32)
    if gates:
        in_specs += [pl.BlockSpec(gate_w.shape, lambda i: (0, 0), pipeline_mode=_RESIDENT),
                     pl.BlockSpec((1, LANES), lambda i: (0, 0))]
        args += [gate_w, gate_b]
        block_bytes += _nbytes(gate_w.shape, BF16)

    out_shape, out_specs = [], []
    for si, (_, w, _, row_cols, vt_cols) in enumerate(streams):
        if gates and si == 0:
            row_cols -= 2 * n_gates * HEAD_DIM
        out_shape.append(jax.ShapeDtypeStruct((T, row_cols), BF16))
        out_specs.append(pl.BlockSpec((tm, row_cols), lambda i: (i, 0)))
        block_bytes += 2 * _nbytes((tm, w.shape[1]), BF16)
        if vt_cols:
            n_groups = vt_cols // LANES
            out_shape.append(
                jax.ShapeDtypeStruct((n_batch, n_groups, n_seq_tiles, LANES, tm), BF16))
            out_specs.append(pl.BlockSpec(
                (1, n_groups, 1, LANES, tm),
                lambda i: (i // n_seq_tiles, 0, i % n_seq_tiles, 0, 0)))
    scratch = []
    if gates:
        aug_cols = n_gates * LANES
        for _ in range(2):
            out_shape.append(jax.ShapeDtypeStruct((T, aug_cols), BF16))
            out_specs.append(pl.BlockSpec((tm, aug_cols), lambda i: (i, 0)))
            block_bytes += 2 * _nbytes((tm, aug_cols), BF16)
        scratch.append(pltpu.VMEM((1, LANES), F32))

    cfg = tuple((s[2], s[3], s[4]) for s in streams)
    return pl.pallas_call(
        functools.partial(_proj_kernel, streams=cfg, n_gates=n_gates, n_seq_tiles=n_seq_tiles),
        grid=(T // tm,),
        in_specs=in_specs,
        out_specs=out_specs,
        out_shape=out_shape,
        scratch_shapes=scratch,
        compiler_params=_params(1, block_bytes),
        name="proj_gates" if gates else ("proj_rope" if any_rope else "proj"),
    )(*args)


ONES_ROWS = 16
OFF_UNROLL = 7


def _flash_scratch(n_maps, n_tiles, tile, dv):
    scores = pltpu.VMEM((n_maps, tile, tile), F32)
    row = pltpu.VMEM((n_maps, 1, tile), F32)
    return [scores, row, scores, row,
            pltpu.VMEM((n_tiles, n_maps, 1, tile), F32),
            pltpu.VMEM((n_tiles, n_maps, dv + ONES_ROWS, tile), F32)]


def _flash_scratch_bytes(n_maps, n_tiles, tile, dv):
    return (2 * _nbytes((n_maps, tile, tile), F32) + 2 * _nbytes((n_maps, 8, tile), F32)
            + _nbytes((n_tiles, n_maps, 8, tile), F32)
            + _nbytes((n_tiles, n_maps, dv + ONES_ROWS, tile), F32))


def _causal_flash(n_tiles, tile, q_ops_of, k_block, vt_block, scratch):
    buf_a, buf_b, m_all, acc_all = scratch[:2], scratch[2:4], scratch[4], scratch[5]
    n_off = n_tiles * (n_tiles - 1) // 2
    assert n_tiles % 2 == 0 and n_off % 2 == 0
    ones = jnp.ones((ONES_ROWS, tile), BF16)
    kk = lax.broadcasted_iota(jnp.int32, (tile, tile), 0)
    qq = lax.broadcasted_iota(jnp.int32, (tile, tile), 1)
    causal = kk <= qq

    def issue(qi, j, buf):
        s_ref, bm_ref = buf
        for i, (ka, qa) in enumerate(zip(k_block(j), q_ops_of(qi))):
            st = lax.dot_general(ka, qa, _NT, preferred_element_type=F32)
            s_ref[i] = st
            bm_ref[i] = jnp.max(st, axis=0, keepdims=True)

    half = tile // 2

    def issue_diag(qi, buf):
        s_ref, _ = buf
        for i, (ka, qa) in enumerate(zip(k_block(qi), q_ops_of(qi))):
            s_ref[i, :half, :half] = lax.dot_general(ka[:half], qa[:half], _NT,
                                                     preferred_element_type=F32)
            s_ref[i, :, half:] = lax.dot_general(ka, qa[half:], _NT, preferred_element_type=F32)

    def process_diag(buf, qi):
        s_ref, _ = buf
        for i, vt in enumerate(vt_block(qi)):
            vta = jnp.concatenate([vt, ones], axis=0)
            for rows, cols in ((slice(0, half), slice(0, half)), (slice(0, tile), slice(half, tile))):
                st = jnp.where(causal[rows, cols], s_ref[i, rows, cols], -jnp.inf)
                m_new = jnp.max(st, axis=0, keepdims=True)
                p = jnp.exp2(st - m_new).astype(BF16)
                acc_all[qi, i, :, cols] = jnp.dot(vta[:, rows], p, preferred_element_type=F32)
                m_all[qi, i, :, cols] = m_new

    def process(buf, qi, j):
        s_ref, bm_ref = buf
        for i, vt in enumerate(vt_block(j)):
            m = m_all[qi, i]
            m_new = jnp.maximum(m, bm_ref[i])
            alpha = jnp.exp2(m - m_new)
            p = jnp.exp2(s_ref[i] - m_new).astype(BF16)
            pv = jnp.dot(jnp.concatenate([vt, ones], axis=0), p, preferred_element_type=F32)
            acc_all[qi, i] = alpha * acc_all[qi, i] + pv
            m_all[qi, i] = m_new

    issue_diag(0, buf_a)

    for qi in range(0, n_tiles, 2):
        issue_diag(qi + 1, buf_b)
        process_diag(buf_a, qi)
        if qi + 2 < n_tiles:
            issue_diag(qi + 2, buf_a)
        else:
            issue(1, 0, buf_a)
        process_diag(buf_b, qi + 1)

    def following(qi, j):
        wrap = j + 1 == qi
        return jnp.where(wrap, qi + 1, qi), jnp.where(wrap, 0, j + 1)

    def off_pair(qi, j):
        qi1, j1 = following(qi, j)
        issue(qi1, j1, buf_b)
        process(buf_a, qi, j)
        qi2, j2 = following(qi1, j1)
        past = qi2 >= n_tiles
        qi2, j2 = jnp.where(past, 1, qi2), jnp.where(past, 0, j2)
        issue(qi2, j2, buf_a)
        process(buf_b, qi1, j1)
        return qi2, j2

    def off_trip(u, carry):
        for _ in range(OFF_UNROLL):
            carry = off_pair(*carry)
        return carry

    assert n_off % (2 * OFF_UNROLL) == 0
    lax.fori_loop(0, n_off // (2 * OFF_UNROLL), off_trip, (jnp.int32(1), jnp.int32(0)))


def _fox_attn_kernel(qa_ref, ka_ref, vt_ref, o_ref, *flash_scratch, tile):
    seq = qa_ref.shape[1]
    n_tiles = seq // tile
    d = HEAD_DIM

    def q_ops_of(qi):
        q0 = pl.multiple_of(qi * tile, tile)
        return [qa_ref[0, pl.ds(q0, tile), h2 * LANES:(h2 + 1) * LANES] for h2 in range(2)]

    def k_block(j):
        start = pl.multiple_of(j * tile, tile)
        return [ka_ref[0, pl.ds(start, tile), h2 * LANES:(h2 + 1) * LANES] for h2 in range(2)]

    def vt_block(j):
        vb = vt_ref[0, 0, j]
        return [vb, vb]

    _causal_flash(n_tiles, tile, q_ops_of, k_block, vt_block, flash_scratch)
    acc_all = flash_scratch[5]

    def emit(qi, carry):
        a0 = acc_all[qi, 0]
        a1 = acc_all[qi, 1]
        yt = jnp.concatenate([a0[:d] / a0[2 * d:2 * d + 1], a1[d:2 * d] / a1[2 * d:2 * d + 1]],
                             axis=0)
        o_ref[0, 0, qi] = yt.astype(o_ref.dtype)
        return carry

    lax.fori_loop(0, n_tiles, emit, 0)


def _fox_attn(qa, ka, vt):
    B, S, width = qa.shape
    tile = ATTN_TILE
    n_tiles = S // tile
    n_pairs = width // (2 * LANES)
    pair_block = pl.BlockSpec((1, S, 2 * LANES), lambda b, h: (b, 0, h))
    tiles_block = pl.BlockSpec((1, 1, n_tiles, LANES, tile), lambda b, h: (b, h, 0, 0, 0))
    block_bytes = (2 * 6 * _nbytes((S, LANES), BF16)
                   + _flash_scratch_bytes(2, n_tiles, tile, 2 * HEAD_DIM))
    return pl.pallas_call(
        functools.partial(_fox_attn_kernel, tile=tile),
        grid=(B, n_pairs),
        in_specs=[pair_block, pair_block, tiles_block],
        out_specs=tiles_block,
        out_shape=jax.ShapeDtypeStruct((B, n_pairs, n_tiles, LANES, tile), BF16),
        scratch_shapes=_flash_scratch(2, n_tiles, tile, 2 * HEAD_DIM),
        compiler_params=_params(2, block_bytes),
        name="fox_attn",
    )(qa, ka, vt)


def _diff_attn_kernel(q_ref, k_ref, vt_ref, lq1_ref, lk1_ref, lq2_ref, lk2_ref, g_ref, o_ref,
                      *flash_scratch, tile, lambda_init):
    seq = q_ref.shape[1]
    n_tiles = seq // tile
    lane = lax.broadcasted_iota(jnp.int32, (tile, LANES), 1)
    lam = (jnp.exp(jnp.sum(lq1_ref[...] * lk1_ref[...], axis=1, keepdims=True))
           - jnp.exp(jnp.sum(lq2_ref[...] * lk2_ref[...], axis=1, keepdims=True)) + lambda_init)

    dv = 2 * HEAD_DIM

    def q_ops_of(qi):
        qp = q_ref[0, pl.ds(pl.multiple_of(qi * tile, tile), tile), :]
        zero = jnp.zeros_like(qp)
        return [jnp.where(lane < HEAD_DIM, qp, zero), jnp.where(lane >= HEAD_DIM, qp, zero)]

    def k_block(j):
        kb = k_ref[0, pl.ds(pl.multiple_of(j * tile, tile), tile), :]
        return [kb, kb]

    def vt_block(j):
        vb = vt_ref[0, 0, j]
        return [vb, vb]

    _causal_flash(n_tiles, tile, q_ops_of, k_block, vt_block, flash_scratch)
    acc_all = flash_scratch[5]

    def emit(qi, carry):
        a1 = acc_all[qi, 0]
        a2 = acc_all[qi, 1]
        yt = a1[:dv] / a1[dv:dv + 1] - lam * (a2[:dv] / a2[dv:dv + 1])
        ms = jnp.mean(yt * yt, axis=0, keepdims=True)
        y = yt * lax.rsqrt(ms + NORM_EPS) * g_ref[...] * (1.0 - lambda_init)
        o_ref[0, 0, qi] = y.astype(o_ref.dtype)
        return carry

    lax.fori_loop(0, n_tiles, emit, 0)


def _diff_attn(qproj, k_sh, vt, lq1, lk1, lq2, lk2, subln_g, *, n_heads, lambda_init):
    B, S, _ = qproj.shape
    tile = ATTN_TILE
    n_tiles = S // tile
    vec = pl.BlockSpec((1, HEAD_DIM), lambda b, h: (0, 0))
    block_bytes = (2 * 4 * _nbytes((S, LANES), BF16)
                   + _flash_scratch_bytes(2, n_tiles, tile, 2 * HEAD_DIM))
    return pl.pallas_call(
        functools.partial(_diff_attn_kernel, tile=tile, lambda_init=lambda_init),
        grid=(B, n_heads),
        in_specs=[
            pl.BlockSpec((1, S, LANES), lambda b, h: (b, 0, h)),
            pl.BlockSpec((1, S, LANES), lambda b, h: (b, 0, h)),
            pl.BlockSpec((1, 1, n_tiles, LANES, tile), lambda b, h: (b, h, 0, 0, 0)),
            vec, vec, vec, vec,
            pl.BlockSpec((2 * HEAD_DIM, 1), lambda b, h: (0, 0)),
        ],
        out_specs=pl.BlockSpec((1, 1, n_tiles, LANES, tile), lambda b, h: (b, h, 0, 0, 0)),
        out_shape=jax.ShapeDtypeStruct((B, n_heads, n_tiles, LANES, tile), BF16),
        scratch_shapes=_flash_scratch(2, n_tiles, tile, 2 * HEAD_DIM),
        compiler_params=_params(2, block_bytes),
        name="diff_attn",
    )(qproj, k_sh, vt, lq1.reshape(1, -1), lk1.reshape(1, -1), lq2.reshape(1, -1),
      lk2.reshape(1, -1), subln_g.reshape(-1, 1))


def _mix_out(x, yt_ref, mq_ref, mk_ref, mv_ref, wo_ref):
    tm = x.shape[0]
    mw = mq_ref.shape[1]
    n_groups, n_sub = yt_ref.shape[1], yt_ref.shape[2]
    y_width = n_groups * LANES
    y = jnp.concatenate(
        [jnp.concatenate([yt_ref[0, g, t].astype(F32).T.astype(BF16) for t in range(n_sub)], axis=0)
         for g in range(n_groups)], axis=1)
    mq = mq_ref[...]
    mk = mk_ref[0]
    mv = mv_ref[0]
    q_head = lax.broadcasted_iota(jnp.int32, (tm, mw), 1) // HEAD_DIM
    v_head = lax.broadcasted_iota(jnp.int32, mv.shape, 1) // HEAD_DIM
    ymem = jnp.zeros((tm, mw), F32)
    for h in range(mw // HEAD_DIM):
        qh = jnp.where(q_head == h, mq, jnp.zeros_like(mq))
        s = lax.dot_general(qh, mk, _NT, preferred_element_type=F32)
        p = jnp.exp2(s - jnp.max(s, axis=1, keepdims=True))
        l = jnp.sum(p, axis=1, keepdims=True)
        vh = jnp.where(v_head == h, mv, jnp.zeros_like(mv))
        ymem = ymem + jnp.dot(p.astype(BF16), vh, preferred_element_type=F32) / l
    acc = jnp.dot(y, wo_ref[0:y_width, :], preferred_element_type=F32)
    acc = acc + jnp.dot(ymem.astype(BF16), wo_ref[y_width:y_width + mw, :],
                        preferred_element_type=F32)
    return x + acc


def _post_kernel(*refs, final, n_chunks):
    x_ref, y_ref, mq_ref, mk_ref, mv_ref, wo_ref, g_ref, wgu_ref, wd_ref = refs[:9]
    wo_ref, wgu_ref, wd_ref = wo_ref.at[0], wgu_ref.at[0], wd_ref.at[0]
    if final:
        fg_ref, o_ref, hn_ref, acc_ref = refs[9:]
    else:
        o_ref, hn_ref, acc_ref = refs[9:]
    x = _mix_out(x_ref[...], y_ref, mq_ref, mk_ref, mv_ref, wo_ref)
    hn_ref[...] = _rms(x, g_ref[...]).astype(BF16)
    acc_ref[...] = x

    d_ff = wd_ref.shape[0]
    tf = d_ff // n_chunks
    for k in range(n_chunks):
        h = hn_ref[...]
        gate = jnp.dot(h, wgu_ref[:, k * tf:(k + 1) * tf], preferred_element_type=F32)
        up = jnp.dot(h, wgu_ref[:, d_ff + k * tf:d_ff + (k + 1) * tf],
                     preferred_element_type=F32)
        a = gate * jax.nn.sigmoid(gate) * up
        acc_ref[...] += jnp.dot(a.astype(BF16), wd_ref[k * tf:(k + 1) * tf, :],
                                preferred_element_type=F32)
    out = acc_ref[...]
    if final:
        out = _rms(out, fg_ref[...])
    o_ref[...] = out


def _post(x2d, yt, proj2d, mq_col, memkv, wo, g, wgu, wd, *, layer, seq, final_g=None):
    T, D = x2d.shape
    n_groups, tile = yt.shape[1], yt.shape[4]
    yw = n_groups * LANES
    n_mem, mw = memkv.shape[1], memkv.shape[2] // 2
    d_ff = wd.shape[1]

    def slab(w):
        return pl.BlockSpec((1,) + w.shape[1:], lambda i: (layer, 0, 0), pipeline_mode=_RESIDENT)

    tm = FFN_TM
    assert T % tm == 0 and seq % tm == 0 and tm % tile == 0 and d_ff % FFN_TF == 0
    n_chunks = d_ff // FFN_TF
    n_seq_tiles = seq // tm
    final = final_g is not None
    in_specs = [
        pl.BlockSpec((tm, D), lambda i: (i, 0)),
        pl.BlockSpec((1, n_groups, tm // tile, LANES, tile),
                     lambda i: (i // n_seq_tiles, 0, i % n_seq_tiles, 0, 0)),
        pl.BlockSpec((tm, mw), lambda i: (i, mq_col)),
        pl.BlockSpec((1, n_mem, mw), lambda i: (i // n_seq_tiles, 0, 0)),
        pl.BlockSpec((1, n_mem, mw), lambda i: (i // n_seq_tiles, 0, 1)),
        slab(wo),
        pl.BlockSpec((1, D), lambda i: (0, 0)),
        slab(wgu),
        slab(wd),
    ]
    args = [x2d, yt, proj2d, memkv, memkv, wo, g.reshape(1, D), wgu, wd]
    if final:
        in_specs.append(pl.BlockSpec((1, D), lambda i: (0, 0)))
        args.append(final_g.reshape(1, D))
    block_bytes = (5 * _nbytes((tm, D), F32) + _nbytes((tm, D), BF16)
                   + 2 * _nbytes((tm, yw + mw), BF16) + 4 * _nbytes((n_mem, mw), BF16)
                   + _nbytes(wo.shape[1:], BF16) + _nbytes(wgu.shape[1:], BF16)
                   + _nbytes(wd.shape[1:], BF16))
    return pl.pallas_call(
        functools.partial(_post_kernel, final=final, n_chunks=n_chunks),
        grid=(T // tm,),
        in_specs=in_specs,
        out_specs=pl.BlockSpec((tm, D), lambda i: (i, 0)),
        out_shape=jax.ShapeDtypeStruct((T, D), F32),
        scratch_shapes=[pltpu.VMEM((tm, D), BF16), pltpu.VMEM((tm, D), F32)],
        compiler_params=_params(1, block_bytes),
        name="post_final" if final else "post",
    )(*args)


def _rope_tables(seq):
    half = HEAD_DIM // 2
    f32 = np.float32
    inv_freq = np.power(f32(ROPE_THETA), -np.arange(half, dtype=f32) * f32(2.0 / HEAD_DIM))
    ang = np.arange(seq, dtype=f32)[:, None] * inv_freq[None, :]
    cos, sin = np.cos(ang).astype(f32), np.sin(ang).astype(f32)
    reps = LANES // HEAD_DIM
    cos_t = np.tile(np.concatenate([cos, cos], axis=-1), (1, reps))
    sin_t = np.tile(np.concatenate([-sin, sin], axis=-1), (1, reps))
    return jnp.asarray(cos_t), jnp.asarray(sin_t)


def kernel(x, mem, attn_norm_g, mem_norm_g, w_mem_kv, w_out, ffn_norm_g, w_gate_up, w_down,
           a_w_in, a_b_f, b_w_in, b_lambda_q1, b_lambda_k1, b_lambda_q2, b_lambda_k2,
           b_subln_g, kv_norm_g, w_kv_shared, final_norm_g):
    B, S, D = x.shape
    depth = attn_norm_g.shape[0]
    n_a = a_w_in.shape[0]
    n_mem = mem.shape[1]
    mem_w = w_mem_kv.shape[2] // 2
    n_fox = a_b_f.shape[1]
    fox_w = n_fox * HEAD_DIM
    diff_w = b_w_in.shape[1] - mem_w
    n_diff = diff_w // (2 * HEAD_DIM)
    scale = HEAD_DIM ** -0.5 * LOG2E
    T = B * S

    x2d = x.reshape(T, D)
    mem2d = mem.reshape(B * n_mem, D)
    rope_tabs = _rope_tables(S)
    k_sh = vt_sh = None
    wo_all = w_out.astype(BF16)
    wgu_all = w_gate_up.astype(BF16)
    wd_all = w_down.astype(BF16)

    for layer in range(depth):
        (memkv,) = _proj(mem2d, [(mem_norm_g[layer], w_mem_kv[layer].astype(BF16),
                                  0, 2 * mem_w, 0)], seq=n_mem)
        memkv = memkv.reshape(B, n_mem, 2 * mem_w)
        if layer < n_a:
            w_in = a_w_in[layer]
            w_main = jnp.concatenate(
                [w_in[:, :fox_w] * scale, w_in[:, fox_w:2 * fox_w],
                 w_in[:, 3 * fox_w + n_fox:] * scale, w_in[:, 2 * fox_w:3 * fox_w]],
                axis=1).astype(BF16)
            gw = jnp.pad(w_in[:, 3 * fox_w:3 * fox_w + n_fox], ((0, 0), (0, LANES - n_fox)))
            gw_hi = gw.astype(BF16)
            gw_lo = (gw - gw_hi.astype(F32)).astype(BF16)
            gate_w = jnp.concatenate([gw_hi, gw_lo], axis=1)
            gate_b = jnp.pad(a_b_f[layer], (0, LANES - n_fox)).reshape(1, LANES)
            proj, vt, qa, ka = _proj(
                x2d, [(attn_norm_g[layer], w_main, 0, 2 * fox_w + mem_w, fox_w)], seq=S,
                gate_w=gate_w, gate_b=gate_b, n_gates=n_fox)
            y = _fox_attn(qa.reshape(B, S, -1), ka.reshape(B, S, -1), vt)
            mq_col = 0
        else:
            j = layer - n_a
            streams = [(attn_norm_g[layer], (b_w_in[j] * scale).astype(BF16),
                        diff_w, diff_w + mem_w, 0)]
            if layer == n_a:
                streams.append((kv_norm_g, w_kv_shared.astype(BF16), diff_w, diff_w,
                                w_kv_shared.shape[1] - diff_w))
                proj, k_sh, vt_sh = _proj(x2d, streams, seq=S, rope_tabs=rope_tabs)
                k_sh = k_sh.reshape(B, S, diff_w)
            else:
                (proj,) = _proj(x2d, streams, seq=S, rope_tabs=rope_tabs)
            lambda_init = 0.8 - 0.6 * math.exp(-0.3 * layer)
            y = _diff_attn(proj.reshape(B, S, -1), k_sh, vt_sh, b_lambda_q1[j], b_lambda_k1[j],
                           b_lambda_q2[j], b_lambda_k2[j], b_subln_g[j], n_heads=n_diff,
                           lambda_init=lambda_init)
            mq_col = diff_w // mem_w
        last = layer == depth - 1
        x2d = _post(x2d, y, proj, mq_col, memkv, wo_all, ffn_norm_g[layer],
                    wgu_all, wd_all, layer=layer, seq=S,
                    final_g=final_norm_g if last else None)
    return x2d.reshape(B, S, D)
```

```python
import functools
import math

import jax
import jax.numpy as jnp
from jax import lax
from jax.experimental import pallas as pl
from jax.experimental.pallas import tpu as pltpu

HEAD_DIM = 64
ROPE_THETA = 10000.0
NORM_EPS = 1e-6
LANES = 128
ATTN_TILE = 512
PROJ_TM = ATTN_TILE
PROJ_SUB = 2
FFN_TM = 1024
FFN_TF = 256
VMEM_CAP = 56 * 1024 * 1024
VMEM_TEMPS = 16 * 1024 * 1024
LOG2E = math.log2(math.e)

F32 = jnp.float32
BF16 = jnp.bfloat16
_NT = (((1,), (1,)), ((), ()))
_RESIDENT = pl.Buffered(1)


def _nbytes(shape, dtype):
    return math.prod(shape) * jnp.dtype(dtype).itemsize


def _params(n_grid, block_bytes):
    limit = min(VMEM_CAP, block_bytes + VMEM_TEMPS)
    return pltpu.CompilerParams(dimension_semantics=("arbitrary",) * n_grid,
                                vmem_limit_bytes=limit)


def _rms(x, g):
    ms = jnp.mean(x * x, axis=-1, keepdims=True)
    return x * lax.rsqrt(ms + NORM_EPS) * g


def _split_bf16(v, n):
    pieces = []
    for _ in range(n - 1):
        p = v.astype(BF16)
        pieces.append(p)
        v = v - p.astype(F32)
    pieces.append(v.astype(BF16))
    return pieces


def _proj_kernel(*refs, n_sub, **cfg):
    for sub in range(n_sub):
        _proj_tile(*refs, sub=sub, n_sub=n_sub, **cfg)


def _proj_tile(*refs, streams, n_gates, n_seq_tiles, sub, n_sub):
    gates = n_gates > 0
    it = iter(refs)
    x_ref = next(it)
    gw = [(next(it), next(it)) for _ in streams]
    any_rope = any(s[0] for s in streams)
    if any_rope:
        cos_ref, sin_ref = next(it), next(it)
    if gates:
        wf_ref, bf_ref = next(it), next(it)
    outs = []
    for (_, _, vt_cols) in streams:
        row_ref = next(it)
        outs.append((row_ref, next(it) if vt_cols else None))
    if gates:
        qa_ref, ka_ref, carry_ref = next(it), next(it), next(it)

    tm = x_ref.shape[0] // n_sub
    rows = slice(sub * tm, (sub + 1) * tm)
    x = x_ref[rows, :]
    xn = x * lax.rsqrt(jnp.mean(x * x, axis=-1, keepdims=True) + NORM_EPS)
    if any_rope:
        cos = cos_ref[rows, :]
        sin = sin_ref[rows, :]
        lane = lax.broadcasted_iota(jnp.int32, (tm, LANES), 1)
        first_half = (lane % HEAD_DIM) < (HEAD_DIM // 2)

    def gate_pieces(hn32, hn):
        lo = (hn32 - hn.astype(F32)).astype(BF16)
        wf = wf_ref[...]
        zz = jnp.dot(hn, wf, preferred_element_type=F32)
        z = (zz[:, :LANES] + zz[:, LANES:]
             + jnp.dot(lo, wf[:, :LANES], preferred_element_type=F32) + bf_ref[...])
        lf = jnp.minimum(z, 0.0) - jnp.log1p(jnp.exp(-jnp.abs(z)))
        l_hi, l_mid, l_lo = _split_bf16(lf, 3)
        row = lax.broadcasted_iota(jnp.int32, (tm, tm), 0)
        col = lax.broadcasted_iota(jnp.int32, (tm, tm), 1)
        tri = (row >= col).astype(BF16)
        cc = jnp.dot(tri, jnp.concatenate([l_hi, l_mid], axis=1), preferred_element_type=F32)
        cs = cc[:, :LANES] + cc[:, LANES:] + jnp.dot(tri, l_lo, preferred_element_type=F32)

        @pl.when(((pl.program_id(0) * n_sub + sub) % n_seq_tiles) == 0)
        def _():
            carry_ref[...] = jnp.zeros_like(carry_ref)

        c = cs + carry_ref[...]
        carry_ref[...] = c[tm - 1:tm, :]
        p_hi, p_mid, p_lo = [p.astype(F32) for p in _split_bf16(c * LOG2E, 3)]
        lane_c = lax.broadcasted_iota(jnp.int32, (tm, LANES), 1)
        return jnp.where(lane_c < n_gates, p_hi, jnp.where(
            lane_c < 2 * n_gates, pltpu.roll(p_mid, n_gates, 1),
            pltpu.roll(p_lo, 2 * n_gates, 1)))

    for si, ((rope_cols, row_cols, vt_cols), (g_ref, w_ref), (row_ref, vt_ref)) in enumerate(
            zip(streams, gw, outs)):
        hn32 = xn * g_ref[...]
        hn = hn32.astype(BF16)
        if gates and si == 0:
            packed = gate_pieces(hn32, hn)
        res = jnp.dot(hn, w_ref[...], preferred_element_type=F32)
        for c0 in range(0, rope_cols, LANES):
            t = res[:, c0:c0 + LANES]
            up = pltpu.roll(t, HEAD_DIM // 2, 1)
            dn = pltpu.roll(t, LANES - HEAD_DIM // 2, 1)
            sw = jnp.where(first_half, dn, up)
            row_ref[rows, c0:c0 + LANES] = (t * cos + sw * sin).astype(BF16)
        row_from = 2 * n_gates * HEAD_DIM if (gates and si == 0) else 0
        plain = max(rope_cols, row_from)
        if row_cols > plain:
            row_ref[rows, plain - row_from:row_cols - row_from] = (
                res[:, plain:row_cols].astype(BF16))
        for gi in range(vt_cols // LANES):
            c0 = row_cols + gi * LANES
            vt_ref[0, gi, sub] = res[:, c0:c0 + LANES].T.astype(BF16)

        if gates and si == 0:
            d = HEAD_DIM
            lane_c = lax.broadcasted_iota(jnp.int32, (tm, LANES), 1)

            def lanes_of(base):
                return ((lane_c == base) | (lane_c == base + n_gates)
                        | (lane_c == base + 2 * n_gates))

            for p in range(n_gates // 2):
                cp = pltpu.roll(packed, d - 2 * p, 1)
                qp = res[:, p * LANES:(p + 1) * LANES]
                kp = res[:, n_gates * d + p * LANES:n_gates * d + (p + 1) * LANES]
                for h2 in range(2):
                    base = (1 - h2) * d
                    is_head = (lane_c >= h2 * d) & (lane_c < h2 * d + d)
                    in_a, in_b = lanes_of(base), lanes_of(base + 1)
                    kc = cp if h2 == 0 else pltpu.roll(cp, d - 1, 1)
                    qc = pltpu.roll(cp, 1 if h2 == 0 else d, 1)
                    k_extra = jnp.where(in_a, kc, jnp.where(in_b, 1.0, 0.0))
                    q_extra = jnp.where(in_b, qc, jnp.where(in_a, -1.0, 0.0))
                    c0 = (2 * p + h2) * LANES
                    ka_ref[rows, c0:c0 + LANES] = jnp.where(is_head, kp, k_extra).astype(BF16)
                    qa_ref[rows, c0:c0 + LANES] = jnp.where(is_head, qp, q_extra).astype(BF16)


def _proj(x2d, streams, *, seq, rope_tabs=None, gate_w=None, gate_b=None, n_gates=0):
    T, D = x2d.shape
    tm = min(PROJ_TM, T)
    assert T % tm == 0
    gates = n_gates > 0
    assert n_gates % 2 == 0 and HEAD_DIM + 2 + 2 * n_gates <= LANES
    any_rope = any(s[2] for s in streams)
    any_vt = any(s[4] for s in streams)
    assert seq % tm == 0 or not (gates or any_rope or any_vt)
    n_seq_tiles = max(1, seq // tm)
    n_batch = T // seq
    n_sub = PROJ_SUB if T % (PROJ_SUB * tm) == 0 and (
        n_seq_tiles % PROJ_SUB == 0 or not (gates or any_rope or any_vt)) else 1
    bt = n_sub * tm
    steps_per_seq = max(1, n_seq_tiles // n_sub)

    in_specs = [pl.BlockSpec((bt, D), lambda i: (i, 0))]
    args = [x2d]
    block_bytes = 2 * _nbytes((bt, D), F32)
    for g, w, _, _, _ in streams:
        in_specs += [pl.BlockSpec((1, D), lambda i: (0, 0)),
                     pl.BlockSpec(w.shape, lambda i: (0, 0), pipeline_mode=_RESIDENT)]
        args += [g.reshape(1, D), w]
        block_bytes += _nbytes(w.shape, BF16) + _nbytes((bt, w.shape[1]), F32)
    if any_rope:
        tab_spec = pl.BlockSpec((bt, LANES), lambda i: (i % steps_per_seq, 0))
        in_specs += [tab_spec, tab_spec]
        args += list(rope_tabs)
        block_bytes += 4 * _nbytes((bt, LANES), F32)
    if gates:
        in_specs += [pl.BlockSpec(gate_w.shape, lambda i: (0, 0), pipeline_mode=_RESIDENT),
                     pl.BlockSpec((1, LANES), lambda i: (0, 0))]
        args += [gate_w, gate_b]
        block_bytes += _nbytes(gate_w.shape, BF16)

    out_shape, out_specs = [], []
    for si, (_, w, _, row_cols, vt_cols) in enumerate(streams):
        if gates and si == 0:
            row_cols -= 2 * n_gates * HEAD_DIM
        out_shape.append(jax.ShapeDtypeStruct((T, row_cols), BF16))
        out_specs.append(pl.BlockSpec((bt, row_cols), lambda i: (i, 0)))
        block_bytes += 2 * _nbytes((bt, w.shape[1]), BF16)
        if vt_cols:
            n_groups = vt_cols // LANES
            out_shape.append(
                jax.ShapeDtypeStruct((n_batch, n_groups, n_seq_tiles, LANES, tm), BF16))
            out_specs.append(pl.BlockSpec(
                (1, n_groups, n_sub, LANES, tm),
                lambda i: (i // steps_per_seq, 0, i % steps_per_seq, 0, 0)))
    scratch = []
    if gates:
        aug_cols = n_gates * LANES
        for _ in range(2):
            out_shape.append(jax.ShapeDtypeStruct((T, aug_cols), BF16))
            out_specs.append(pl.BlockSpec((bt, aug_cols), lambda i: (i, 0)))
            block_bytes += 2 * _nbytes((bt, aug_cols), BF16)
        scratch.append(pltpu.VMEM((1, LANES), F32))

    cfg = tuple((s[2], s[3], s[4]) for s in streams)
    return pl.pallas_call(
        functools.partial(_proj_kernel, streams=cfg, n_gates=n_gates, n_seq_tiles=n_seq_tiles,
                          n_sub=n_sub),
        grid=(T // bt,),
        in_specs=in_specs,
        out_specs=out_specs,
        out_shape=out_shape,
        scratch_shapes=scratch,
        compiler_params=_params(1, block_bytes),
        name="proj_gates" if gates else ("proj_rope" if any_rope else "proj"),
    )(*args)


ONES_ROWS = 16
OFF_UNROLL = 7


def _flash_scratch(n_maps, n_tiles, tile, dv):
    scores = pltpu.VMEM((n_maps, tile, tile), F32)
    row = pltpu.VMEM((n_maps, 1, tile), F32)
    return [scores, row, scores, row,
            pltpu.VMEM((n_tiles, n_maps, 1, tile), F32),
            pltpu.VMEM((n_tiles, n_maps, dv + ONES_ROWS, tile), F32)]


def _flash_scratch_bytes(n_maps, n_tiles, tile, dv):
    return (2 * _nbytes((n_maps, tile, tile), F32) + 2 * _nbytes((n_maps, 8, tile), F32)
            + _nbytes((n_tiles, n_maps, 8, tile), F32)
            + _nbytes((n_tiles, n_maps, dv + ONES_ROWS, tile), F32))


def _causal_flash(n_tiles, tile, q_ops_of, k_block, vt_block, scratch):
    buf_a, buf_b, m_all, acc_all = scratch[:2], scratch[2:4], scratch[4], scratch[5]
    n_off = n_tiles * (n_tiles - 1) // 2
    assert n_tiles % 2 == 0 and n_off % 2 == 0
    ones = jnp.ones((ONES_ROWS, tile), BF16)
    kk = lax.broadcasted_iota(jnp.int32, (tile, tile), 0)
    qq = lax.broadcasted_iota(jnp.int32, (tile, tile), 1)
    causal = kk <= qq

    def issue(qi, j, buf):
        s_ref, bm_ref = buf
        for i, (ka, qa) in enumerate(zip(k_block(j), q_ops_of(qi))):
            st = lax.dot_general(ka, qa, _NT, preferred_element_type=F32)
            s_ref[i] = st
            bm_ref[i] = jnp.max(st, axis=0, keepdims=True)

    half = tile // 2

    def issue_diag(qi, buf):
        s_ref, _ = buf
        for i, (ka, qa) in enumerate(zip(k_block(qi), q_ops_of(qi))):
            s_ref[i, :half, :half] = lax.dot_general(ka[:half], qa[:half], _NT,
                                                     preferred_element_type=F32)
            s_ref[i, :, half:] = lax.dot_general(ka, qa[half:], _NT, preferred_element_type=F32)

    def process_diag(buf, qi):
        s_ref, _ = buf
        for i, vt in enumerate(vt_block(qi)):
            vta = jnp.concatenate([vt, ones], axis=0)
            for rows, cols in ((slice(0, half), slice(0, half)), (slice(0, tile), slice(half, tile))):
                st = jnp.where(causal[rows, cols], s_ref[i, rows, cols], -jnp.inf)
                m_new = jnp.max(st, axis=0, keepdims=True)
                p = jnp.exp2(st - m_new).astype(BF16)
                acc_all[qi, i, :, cols] = jnp.dot(vta[:, rows], p, preferred_element_type=F32)
                m_all[qi, i, :, cols] = m_new

    def process(buf, qi, j):
        s_ref, bm_ref = buf
        for i, vt in enumerate(vt_block(j)):
            m = m_all[qi, i]
            m_new = jnp.maximum(m, bm_ref[i])
            alpha = jnp.exp2(m - m_new)
            p = jnp.exp2(s_ref[i] - m_new).astype(BF16)
            pv = jnp.dot(jnp.concatenate([vt, ones], axis=0), p, preferred_element_type=F32)
            acc_all[qi, i] = alpha * acc_all[qi, i] + pv
            m_all[qi, i] = m_new

    issue_diag(0, buf_a)

    for qi in range(0, n_tiles, 2):
        issue_diag(qi + 1, buf_b)
        process_diag(buf_a, qi)
        if qi + 2 < n_tiles:
            issue_diag(qi + 2, buf_a)
        else:
            issue(1, 0, buf_a)
        process_diag(buf_b, qi + 1)

    def following(qi, j):
        wrap = j + 1 == qi
        return jnp.where(wrap, qi + 1, qi), jnp.where(wrap, 0, j + 1)

    def off_pair(qi, j):
        qi1, j1 = following(qi, j)
        issue(qi1, j1, buf_b)
        process(buf_a, qi, j)
        qi2, j2 = following(qi1, j1)
        past = qi2 >= n_tiles
        qi2, j2 = jnp.where(past, 1, qi2), jnp.where(past, 0, j2)
        issue(qi2, j2, buf_a)
        process(buf_b, qi1, j1)
        return qi2, j2

    def off_trip(u, carry):
        for _ in range(OFF_UNROLL):
            carry = off_pair(*carry)
        return carry

    assert n_off % (2 * OFF_UNROLL) == 0
    lax.fori_loop(0, n_off // (2 * OFF_UNROLL), off_trip, (jnp.int32(1), jnp.int32(0)))


def _fox_attn_kernel(qa_ref, ka_ref, vt_ref, o_ref, *flash_scratch, tile):
    seq = qa_ref.shape[1]
    n_tiles = seq // tile
    d = HEAD_DIM

    def q_ops_of(qi):
        q0 = pl.multiple_of(qi * tile, tile)
        return [qa_ref[0, pl.ds(q0, tile), h2 * LANES:(h2 + 1) * LANES] for h2 in range(2)]

    def k_block(j):
        start = pl.multiple_of(j * tile, tile)
        return [ka_ref[0, pl.ds(start, tile), h2 * LANES:(h2 + 1) * LANES] for h2 in range(2)]

    def vt_block(j):
        vb = vt_ref[0, 0, j]
        return [vb, vb]

    _causal_flash(n_tiles, tile, q_ops_of, k_block, vt_block, flash_scratch)
    acc_all = flash_scratch[5]

    def emit(qi, carry):
        a0 = acc_all[qi, 0]
        a1 = acc_all[qi, 1]
        yt = jnp.concatenate([a0[:d] / a0[2 * d:2 * d + 1], a1[d:2 * d] / a1[2 * d:2 * d + 1]],
                             axis=0)
        o_ref[0, 0, qi] = yt.astype(o_ref.dtype)
        return carry

    lax.fori_loop(0, n_tiles, emit, 0)


def _fox_attn(qa, ka, vt):
    B, S, width = qa.shape
    tile = ATTN_TILE
    n_tiles = S // tile
    n_pairs = width // (2 * LANES)
    pair_block = pl.BlockSpec((1, S, 2 * LANES), lambda b, h: (b, 0, h))
    tiles_block = pl.BlockSpec((1, 1, n_tiles, LANES, tile), lambda b, h: (b, h, 0, 0, 0))
    block_bytes = (2 * 6 * _nbytes((S, LANES), BF16)
                   + _flash_scratch_bytes(2, n_tiles, tile, 2 * HEAD_DIM))
    return pl.pallas_call(
        functools.partial(_fox_attn_kernel, tile=tile),
        grid=(B, n_pairs),
        in_specs=[pair_block, pair_block, tiles_block],
        out_specs=tiles_block,
        out_shape=jax.ShapeDtypeStruct((B, n_pairs, n_tiles, LANES, tile), BF16),
        scratch_shapes=_flash_scratch(2, n_tiles, tile, 2 * HEAD_DIM),
        compiler_params=_params(2, block_bytes),
        name="fox_attn",
    )(qa, ka, vt)


def _diff_attn_kernel(q_ref, k_ref, vt_ref, lq1_ref, lk1_ref, lq2_ref, lk2_ref, g_ref, o_ref,
                      *flash_scratch, tile, lambda_init):
    seq = q_ref.shape[1]
    n_tiles = seq // tile
    lane = lax.broadcasted_iota(jnp.int32, (tile, LANES), 1)
    lam = (jnp.exp(jnp.sum(lq1_ref[...] * lk1_ref[...], axis=1, keepdims=True))
           - jnp.exp(jnp.sum(lq2_ref[...] * lk2_ref[...], axis=1, keepdims=True)) + lambda_init)

    dv = 2 * HEAD_DIM

    def q_ops_of(qi):
        qp = q_ref[0, pl.ds(pl.multiple_of(qi * tile, tile), tile), :]
        zero = jnp.zeros_like(qp)
        return [jnp.where(lane < HEAD_DIM, qp, zero), jnp.where(lane >= HEAD_DIM, qp, zero)]

    def k_block(j):
        kb = k_ref[0, pl.ds(pl.multiple_of(j * tile, tile), tile), :]
        return [kb, kb]

    def vt_block(j):
        vb = vt_ref[0, 0, j]
        return [vb, vb]

    _causal_flash(n_tiles, tile, q_ops_of, k_block, vt_block, flash_scratch)
    acc_all = flash_scratch[5]

    def emit(qi, carry):
        a1 = acc_all[qi, 0]
        a2 = acc_all[qi, 1]
        yt = a1[:dv] / a1[dv:dv + 1] - lam * (a2[:dv] / a2[dv:dv + 1])
        ms = jnp.mean(yt * yt, axis=0, keepdims=True)
        y = yt * lax.rsqrt(ms + NORM_EPS) * g_ref[...] * (1.0 - lambda_init)
        o_ref[0, 0, qi] = y.astype(o_ref.dtype)
        return carry

    lax.fori_loop(0, n_tiles, emit, 0)


def _diff_attn(qproj, k_sh, vt, lq1, lk1, lq2, lk2, subln_g, *, n_heads, lambda_init):
    B, S, _ = qproj.shape
    tile = ATTN_TILE
    n_tiles = S // tile
    vec = pl.BlockSpec((1, HEAD_DIM), lambda b, h: (0, 0))
    block_bytes = (2 * 4 * _nbytes((S, LANES), BF16)
                   + _flash_scratch_bytes(2, n_tiles, tile, 2 * HEAD_DIM))
    return pl.pallas_call(
        functools.partial(_diff_attn_kernel, tile=tile, lambda_init=lambda_init),
        grid=(B, n_heads),
        in_specs=[
            pl.BlockSpec((1, S, LANES), lambda b, h: (b, 0, h)),
            pl.BlockSpec((1, S, LANES), lambda b, h: (b, 0, h)),
            pl.BlockSpec((1, 1, n_tiles, LANES, tile), lambda b, h: (b, h, 0, 0, 0)),
            vec, vec, vec, vec,
            pl.BlockSpec((2 * HEAD_DIM, 1), lambda b, h: (0, 0)),
        ],
        out_specs=pl.BlockSpec((1, 1, n_tiles, LANES, tile), lambda b, h: (b, h, 0, 0, 0)),
        out_shape=jax.ShapeDtypeStruct((B, n_heads, n_tiles, LANES, tile), BF16),
        scratch_shapes=_flash_scratch(2, n_tiles, tile, 2 * HEAD_DIM),
        compiler_params=_params(2, block_bytes),
        name="diff_attn",
    )(qproj, k_sh, vt, lq1.reshape(1, -1), lk1.reshape(1, -1), lq2.reshape(1, -1),
      lk2.reshape(1, -1), subln_g.reshape(-1, 1))


def _mix_out(x, yt_ref, mq_ref, mk_ref, mv_ref, wo_ref):
    tm = x.shape[0]
    mw = mq_ref.shape[1]
    n_groups, n_sub = yt_ref.shape[1], yt_ref.shape[2]
    y_width = n_groups * LANES
    y = jnp.concatenate(
        [jnp.concatenate([yt_ref[0, g, t].astype(F32).T.astype(BF16) for t in range(n_sub)], axis=0)
         for g in range(n_groups)], axis=1)
    mq = mq_ref[...]
    mk = mk_ref[0]
    mv = mv_ref[0]
    q_head = lax.broadcasted_iota(jnp.int32, (tm, mw), 1) // HEAD_DIM
    v_head = lax.broadcasted_iota(jnp.int32, mv.shape, 1) // HEAD_DIM
    ymem = jnp.zeros((tm, mw), F32)
    for h in range(mw // HEAD_DIM):
        qh = jnp.where(q_head == h, mq, jnp.zeros_like(mq))
        s = lax.dot_general(qh, mk, _NT, preferred_element_type=F32)
        p = jnp.exp2(s - jnp.max(s, axis=1, keepdims=True))
        l = jnp.sum(p, axis=1, keepdims=True)
        vh = jnp.where(v_head == h, mv, jnp.zeros_like(mv))
        ymem = ymem + jnp.dot(p.astype(BF16), vh, preferred_element_type=F32) / l
    acc = jnp.dot(y, wo_ref[0:y_width, :], preferred_element_type=F32)
    acc = acc + jnp.dot(ymem.astype(BF16), wo_ref[y_width:y_width + mw, :],
                        preferred_element_type=F32)
    return x + acc


def _post_kernel(*refs, final, n_chunks):
    x_ref, y_ref, mq_ref, mk_ref, mv_ref, wo_ref, g_ref, wgu_ref, wd_ref = refs[:9]
    wo_ref, wgu_ref, wd_ref = wo_ref.at[0], wgu_ref.at[0], wd_ref.at[0]
    if final:
        fg_ref, o_ref, hn_ref, acc_ref = refs[9:]
    else:
        o_ref, hn_ref, acc_ref = refs[9:]
    x = _mix_out(x_ref[...], y_ref, mq_ref, mk_ref, mv_ref, wo_ref)
    hn_ref[...] = _rms(x, g_ref[...]).astype(BF16)
    acc_ref[...] = x

    d_ff = wd_ref.shape[0]
    tf = d_ff // n_chunks
    for k in range(n_chunks):
        h = hn_ref[...]
        gate = jnp.dot(h, wgu_ref[:, k * tf:(k + 1) * tf], preferred_element_type=F32)
        up = jnp.dot(h, wgu_ref[:, d_ff + k * tf:d_ff + (k + 1) * tf],
                     preferred_element_type=F32)
        a = gate * jax.nn.sigmoid(gate) * up
        acc_ref[...] += jnp.dot(a.astype(BF16), wd_ref[k * tf:(k + 1) * tf, :],
                                preferred_element_type=F32)
    out = acc_ref[...]
    if final:
        out = _rms(out, fg_ref[...])
    o_ref[...] = out


def _post(x2d, yt, proj2d, mq_col, memkv, wo, g, wgu, wd, *, layer, seq, final_g=None):
    T, D = x2d.shape
    n_groups, tile = yt.shape[1], yt.shape[4]
    yw = n_groups * LANES
    n_mem, mw = memkv.shape[1], memkv.shape[2] // 2
    d_ff = wd.shape[1]

    def slab(w):
        return pl.BlockSpec((1,) + w.shape[1:], lambda i: (layer, 0, 0), pipeline_mode=_RESIDENT)

    tm = FFN_TM
    assert T % tm == 0 and seq % tm == 0 and tm % tile == 0 and d_ff % FFN_TF == 0
    n_chunks = d_ff // FFN_TF
    n_seq_tiles = seq // tm
    final = final_g is not None
    in_specs = [
        pl.BlockSpec((tm, D), lambda i: (i, 0)),
        pl.BlockSpec((1, n_groups, tm // tile, LANES, tile),
                     lambda i: (i // n_seq_tiles, 0, i % n_seq_tiles, 0, 0)),
        pl.BlockSpec((tm, mw), lambda i: (i, mq_col)),
        pl.BlockSpec((1, n_mem, mw), lambda i: (i // n_seq_tiles, 0, 0)),
        pl.BlockSpec((1, n_mem, mw), lambda i: (i // n_seq_tiles, 0, 1)),
        slab(wo),
        pl.BlockSpec((1, D), lambda i: (0, 0)),
        slab(wgu),
        slab(wd),
    ]
    args = [x2d, yt, proj2d, memkv, memkv, wo, g.reshape(1, D), wgu, wd]
    if final:
        in_specs.append(pl.BlockSpec((1, D), lambda i: (0, 0)))
        args.append(final_g.reshape(1, D))
    block_bytes = (5 * _nbytes((tm, D), F32) + _nbytes((tm, D), BF16)
                   + 2 * _nbytes((tm, yw + mw), BF16) + 4 * _nbytes((n_mem, mw), BF16)
                   + _nbytes(wo.shape[1:], BF16) + _nbytes(wgu.shape[1:], BF16)
                   + _nbytes(wd.shape[1:], BF16))
    return pl.pallas_call(
        functools.partial(_post_kernel, final=final, n_chunks=n_chunks),
        grid=(T // tm,),
        in_specs=in_specs,
        out_specs=pl.BlockSpec((tm, D), lambda i: (i, 0)),
        out_shape=jax.ShapeDtypeStruct((T, D), F32),
        scratch_shapes=[pltpu.VMEM((tm, D), BF16), pltpu.VMEM((tm, D), F32)],
        compiler_params=_params(1, block_bytes),
        name="post_final" if final else "post",
    )(*args)


def _rope_tables(seq):
    half = HEAD_DIM // 2
    inv_freq = jnp.power(ROPE_THETA, -jnp.arange(half, dtype=F32) * (2.0 / HEAD_DIM))
    ang = jnp.arange(seq, dtype=F32)[:, None] * inv_freq[None, :]
    cos, sin = jnp.cos(ang), jnp.sin(ang)
    reps = LANES // HEAD_DIM
    cos_t = jnp.tile(jnp.concatenate([cos, cos], axis=-1), (1, reps))
    sin_t = jnp.tile(jnp.concatenate([-sin, sin], axis=-1), (1, reps))
    return cos_t, sin_t


def kernel(x, mem, attn_norm_g, mem_norm_g, w_mem_kv, w_out, ffn_norm_g, w_gate_up, w_down,
           a_w_in, a_b_f, b_w_in, b_lambda_q1, b_lambda_k1, b_lambda_q2, b_lambda_k2,
           b_subln_g, kv_norm_g, w_kv_shared, final_norm_g):
    B, S, D = x.shape
    depth = attn_norm_g.shape[0]
    n_a = a_w_in.shape[0]
    n_mem = mem.shape[1]
    mem_w = w_mem_kv.shape[2] // 2
    n_fox = a_b_f.shape[1]
    fox_w = n_fox * HEAD_DIM
    diff_w = b_w_in.shape[1] - mem_w
    n_diff = diff_w // (2 * HEAD_DIM)
    scale = HEAD_DIM ** -0.5 * LOG2E
    T = B * S

    x2d = x.reshape(T, D)
    mem2d = mem.reshape(B * n_mem, D)
    rope_tabs = _rope_tables(S)
    k_sh = vt_sh = None
    wo_all = w_out.astype(BF16)
    wgu_all = w_gate_up.astype(BF16)
    wd_all = w_down.astype(BF16)

    for layer in range(depth):
        (memkv,) = _proj(mem2d, [(mem_norm_g[layer], w_mem_kv[layer].astype(BF16),
                                  0, 2 * mem_w, 0)], seq=n_mem)
        memkv = memkv.reshape(B, n_mem, 2 * mem_w)
        if layer < n_a:
            w_in = a_w_in[layer]
            w_main = jnp.concatenate(
                [w_in[:, :fox_w] * scale, w_in[:, fox_w:2 * fox_w],
                 w_in[:, 3 * fox_w + n_fox:] * scale, w_in[:, 2 * fox_w:3 * fox_w]],
                axis=1).astype(BF16)
            gw = jnp.pad(w_in[:, 3 * fox_w:3 * fox_w + n_fox], ((0, 0), (0, LANES - n_fox)))
            gw_hi = gw.astype(BF16)
            gw_lo = (gw - gw_hi.astype(F32)).astype(BF16)
            gate_w = jnp.concatenate([gw_hi, gw_lo], axis=1)
            gate_b = jnp.pad(a_b_f[layer], (0, LANES - n_fox)).reshape(1, LANES)
            proj, vt, qa, ka = _proj(
                x2d, [(attn_norm_g[layer], w_main, 0, 2 * fox_w + mem_w, fox_w)], seq=S,
                gate_w=gate_w, gate_b=gate_b, n_gates=n_fox)
            y = _fox_attn(qa.reshape(B, S, -1), ka.reshape(B, S, -1), vt)
            mq_col = 0
        else:
            j = layer - n_a
            streams = [(attn_norm_g[layer], (b_w_in[j] * scale).astype(BF16),
                        diff_w, diff_w + mem_w, 0)]
            if layer == n_a:
                streams.append((kv_norm_g, w_kv_shared.astype(BF16), diff_w, diff_w,
                                w_kv_shared.shape[1] - diff_w))
                proj, k_sh, vt_sh = _proj(x2d, streams, seq=S, rope_tabs=rope_tabs)
                k_sh = k_sh.reshape(B, S, diff_w)
            else:
                (proj,) = _proj(x2d, streams, seq=S, rope_tabs=rope_tabs)
            lambda_init = 0.8 - 0.6 * math.exp(-0.3 * layer)
            y = _diff_attn(proj.reshape(B, S, -1), k_sh, vt_sh, b_lambda_q1[j], b_lambda_k1[j],
                           b_lambda_q2[j], b_lambda_k2[j], b_subln_g[j], n_heads=n_diff,
                           lambda_init=lambda_init)
            mq_col = diff_w // mem_w
        last = layer == depth - 1
        x2d = _post(x2d, y, proj, mq_col, memkv, wo_all, ffn_norm_g[layer],
                    wgu_all, wd_all, layer=layer, seq=S,
                    final_g=final_norm_g if last else None)
    return x2d.reshape(B, S, D)
```

```python
import functools
import math

import jax
import jax.numpy as jnp
import numpy as np
from jax import lax
from jax.experimental import pallas as pl
from jax.experimental.pallas import tpu as pltpu

HEAD_DIM = 64
ROPE_THETA = 10000.0
NORM_EPS = 1e-6
LANES = 128
ATTN_TILE = 512
PROJ_TM = ATTN_TILE
PROJ_SUB = 2
FFN_TM = 1024
FFN_TF = 256
VMEM_CAP = 56 * 1024 * 1024
VMEM_TEMPS = 16 * 1024 * 1024
LOG2E = math.log2(math.e)

F32 = jnp.float32
BF16 = jnp.bfloat16
_NT = (((1,), (1,)), ((), ()))
_RESIDENT = pl.Buffered(1)


def _nbytes(shape, dtype):
    return math.prod(shape) * jnp.dtype(dtype).itemsize


def _params(n_grid, block_bytes):
    limit = min(VMEM_CAP, block_bytes + VMEM_TEMPS)
    return pltpu.CompilerParams(dimension_semantics=("arbitrary",) * n_grid,
                                vmem_limit_bytes=limit)


def _rms(x, g):
    ms = jnp.mean(x * x, axis=-1, keepdims=True)
    return x * lax.rsqrt(ms + NORM_EPS) * g


def _split_bf16(v, n):
    pieces = []
    for _ in range(n - 1):
        p = v.astype(BF16)
        pieces.append(p)
        v = v - p.astype(F32)
    pieces.append(v.astype(BF16))
    return pieces


def _proj_kernel(*refs, n_sub, **cfg):
    for sub in range(n_sub):
        _proj_tile(*refs, sub=sub, n_sub=n_sub, **cfg)


def _proj_tile(*refs, streams, n_gates, n_seq_tiles, sub, n_sub):
    gates = n_gates > 0
    it = iter(refs)
    x_ref = next(it)
    gw = [(next(it), next(it)) for _ in streams]
    any_rope = any(s[0] for s in streams)
    if any_rope:
        cos_ref, sin_ref = next(it), next(it)
    if gates:
        wf_ref, bf_ref = next(it), next(it)
    outs = []
    for (_, _, vt_cols, _) in streams:
        row_ref = next(it)
        outs.append((row_ref, next(it) if vt_cols else None))
    if gates:
        qa_ref, ka_ref, carry_ref = next(it), next(it), next(it)

    tm = x_ref.shape[0] // n_sub
    rows = slice(sub * tm, (sub + 1) * tm)
    x = x_ref[rows, :]
    xn = x * lax.rsqrt(jnp.mean(x * x, axis=-1, keepdims=True) + NORM_EPS)
    if any_rope:
        cos = cos_ref[rows, :]
        sin = sin_ref[rows, :]
        lane = lax.broadcasted_iota(jnp.int32, (tm, LANES), 1)
        first_half = (lane % HEAD_DIM) < (HEAD_DIM // 2)

    def gate_pieces(hn32, hn):
        lo = (hn32 - hn.astype(F32)).astype(BF16)
        wf = wf_ref[...]
        zz = jnp.dot(hn, wf, preferred_element_type=F32)
        z = (zz[:, :LANES] + zz[:, LANES:]
             + jnp.dot(lo, wf[:, :LANES], preferred_element_type=F32) + bf_ref[...])
        lf = jnp.minimum(z, 0.0) - jnp.log1p(jnp.exp(-jnp.abs(z)))
        l_hi, l_mid, l_lo = _split_bf16(lf, 3)
        row = lax.broadcasted_iota(jnp.int32, (tm, tm), 0)
        col = lax.broadcasted_iota(jnp.int32, (tm, tm), 1)
        tri = (row >= col).astype(BF16)
        cc = jnp.dot(tri, jnp.concatenate([l_hi, l_mid], axis=1), preferred_element_type=F32)
        cs = cc[:, :LANES] + cc[:, LANES:] + jnp.dot(tri, l_lo, preferred_element_type=F32)

        @pl.when(((pl.program_id(0) * n_sub + sub) % n_seq_tiles) == 0)
        def _():
            carry_ref[...] = jnp.zeros_like(carry_ref)

        c = cs + carry_ref[...]
        carry_ref[...] = c[tm - 1:tm, :]
        p_hi, p_mid, p_lo = [p.astype(F32) for p in _split_bf16(c * LOG2E, 3)]
        lane_c = lax.broadcasted_iota(jnp.int32, (tm, LANES), 1)
        return jnp.where(lane_c < n_gates, p_hi, jnp.where(
            lane_c < 2 * n_gates, pltpu.roll(p_mid, n_gates, 1),
            pltpu.roll(p_lo, 2 * n_gates, 1)))

    for si, ((rope_cols, row_cols, vt_cols, split), (g_ref, w_ref), (row_ref, vt_ref)) in enumerate(
            zip(streams, gw, outs)):
        hn32 = xn * g_ref[...]
        hn = hn32.astype(BF16)
        if gates and si == 0:
            packed = gate_pieces(hn32, hn)
        res = jnp.dot(hn, w_ref[...], preferred_element_type=F32)
        for c0 in range(0, rope_cols, LANES):
            t = res[:, c0:c0 + LANES]
            up = pltpu.roll(t, HEAD_DIM // 2, 1)
            dn = pltpu.roll(t, LANES - HEAD_DIM // 2, 1)
            sw = jnp.where(first_half, dn, up)
            r = t * cos + sw * sin
            if split:
                zero = jnp.zeros_like(r)
                row_ref[rows, 2 * c0:2 * c0 + LANES] = jnp.where(
                    lane < HEAD_DIM, r, zero).astype(BF16)
                row_ref[rows, 2 * c0 + LANES:2 * c0 + 2 * LANES] = jnp.where(
                    lane >= HEAD_DIM, r, zero).astype(BF16)
            else:
                row_ref[rows, c0:c0 + LANES] = r.astype(BF16)
        row_from = 2 * n_gates * HEAD_DIM if (gates and si == 0) else 0
        plain = max(rope_cols, row_from)
        shift = rope_cols if split else 0
        if row_cols > plain:
            row_ref[rows, plain - row_from + shift:row_cols - row_from + shift] = (
                res[:, plain:row_cols].astype(BF16))
        for gi in range(vt_cols // LANES):
            c0 = row_cols + gi * LANES
            vt_ref[0, gi, sub] = res[:, c0:c0 + LANES].T.astype(BF16)

        if gates and si == 0:
            d = HEAD_DIM
            lane_c = lax.broadcasted_iota(jnp.int32, (tm, LANES), 1)

            def lanes_of(base):
                return ((lane_c == base) | (lane_c == base + n_gates)
                        | (lane_c == base + 2 * n_gates))

            for p in range(n_gates // 2):
                cp = pltpu.roll(packed, d - 2 * p, 1)
                qp = res[:, p * LANES:(p + 1) * LANES]
                kp = res[:, n_gates * d + p * LANES:n_gates * d + (p + 1) * LANES]
                for h2 in range(2):
                    base = (1 - h2) * d
                    is_head = (lane_c >= h2 * d) & (lane_c < h2 * d + d)
                    in_a, in_b = lanes_of(base), lanes_of(base + 1)
                    kc = cp if h2 == 0 else pltpu.roll(cp, d - 1, 1)
                    qc = pltpu.roll(cp, 1 if h2 == 0 else d, 1)
                    k_extra = jnp.where(in_a, kc, jnp.where(in_b, 1.0, 0.0))
                    q_extra = jnp.where(in_b, qc, jnp.where(in_a, -1.0, 0.0))
                    c0 = (2 * p + h2) * LANES
                    ka_ref[rows, c0:c0 + LANES] = jnp.where(is_head, kp, k_extra).astype(BF16)
                    qa_ref[rows, c0:c0 + LANES] = jnp.where(is_head, qp, q_extra).astype(BF16)


def _proj(x2d, streams, *, seq, rope_tabs=None, gate_w=None, gate_b=None, n_gates=0):
    T, D = x2d.shape
    tm = min(PROJ_TM, T)
    assert T % tm == 0
    gates = n_gates > 0
    assert n_gates % 2 == 0 and HEAD_DIM + 2 + 2 * n_gates <= LANES
    any_rope = any(s[2] for s in streams)
    any_vt = any(s[4] for s in streams)
    assert seq % tm == 0 or not (gates or any_rope or any_vt)
    n_seq_tiles = max(1, seq // tm)
    n_batch = T // seq
    n_sub = PROJ_SUB if T % (PROJ_SUB * tm) == 0 and (
        n_seq_tiles % PROJ_SUB == 0 or not (gates or any_rope or any_vt)) else 1
    bt = n_sub * tm
    steps_per_seq = max(1, n_seq_tiles // n_sub)

    in_specs = [pl.BlockSpec((bt, D), lambda i: (i, 0))]
    args = [x2d]
    block_bytes = 2 * _nbytes((bt, D), F32)
    for g, w, *_ in streams:
        in_specs += [pl.BlockSpec((1, D), lambda i: (0, 0)),
                     pl.BlockSpec(w.shape, lambda i: (0, 0), pipeline_mode=_RESIDENT)]
        args += [g.reshape(1, D), w]
        block_bytes += _nbytes(w.shape, BF16) + _nbytes((bt, w.shape[1]), F32)
    if any_rope:
        tab_spec = pl.BlockSpec((bt, LANES), lambda i: (i % steps_per_seq, 0))
        in_specs += [tab_spec, tab_spec]
        args += list(rope_tabs)
        block_bytes += 4 * _nbytes((bt, LANES), F32)
    if gates:
        in_specs += [pl.BlockSpec(gate_w.shape, lambda i: (0, 0), pipeline_mode=_RESIDENT),
                     pl.BlockSpec((1, LANES), lambda i: (0, 0))]
        args += [gate_w, gate_b]
        block_bytes += _nbytes(gate_w.shape, BF16)

    out_shape, out_specs = [], []
    for si, (_, w, rope_cols, row_cols, vt_cols, *split) in enumerate(streams):
        if gates and si == 0:
            row_cols -= 2 * n_gates * HEAD_DIM
        if split and split[0]:
            row_cols += rope_cols
        out_shape.append(jax.ShapeDtypeStruct((T, row_cols), BF16))
        out_specs.append(pl.BlockSpec((bt, row_cols), lambda i: (i, 0)))
        block_bytes += 2 * _nbytes((bt, w.shape[1]), BF16)
        if vt_cols:
            n_groups = vt_cols // LANES
            out_shape.append(
                jax.ShapeDtypeStruct((n_batch, n_groups, n_seq_tiles, LANES, tm), BF16))
            out_specs.append(pl.BlockSpec(
                (1, n_groups, n_sub, LANES, tm),
                lambda i: (i // steps_per_seq, 0, i % steps_per_seq, 0, 0)))
    scratch = []
    if gates:
        aug_cols = n_gates * LANES
        for _ in range(2):
            out_shape.append(jax.ShapeDtypeStruct((T, aug_cols), BF16))
            out_specs.append(pl.BlockSpec((bt, aug_cols), lambda i: (i, 0)))
            block_bytes += 2 * _nbytes((bt, aug_cols), BF16)
        scratch.append(pltpu.VMEM((1, LANES), F32))

    cfg = tuple((s[2], s[3], s[4], len(s) > 5 and s[5]) for s in streams)
    return pl.pallas_call(
        functools.partial(_proj_kernel, streams=cfg, n_gates=n_gates, n_seq_tiles=n_seq_tiles,
                          n_sub=n_sub),
        grid=(T // bt,),
        in_specs=in_specs,
        out_specs=out_specs,
        out_shape=out_shape,
        scratch_shapes=scratch,
        compiler_params=_params(1, block_bytes),
        name="proj_gates" if gates else ("proj_rope" if any_rope else "proj"),
    )(*args)


ONES_ROWS = 16
OFF_UNROLL = 7


def _flash_scratch(n_maps, n_tiles, tile, dv):
    scores = pltpu.VMEM((n_maps, tile, tile), F32)
    row = pltpu.VMEM((n_maps, 1, tile), F32)
    return [scores, row, scores, row,
            pltpu.VMEM((n_tiles, n_maps, 1, tile), F32),
            pltpu.VMEM((n_tiles, n_maps, dv + ONES_ROWS, tile), F32)]


def _flash_scratch_bytes(n_maps, n_tiles, tile, dv):
    return (2 * _nbytes((n_maps, tile, tile), F32) + 2 * _nbytes((n_maps, 8, tile), F32)
            + _nbytes((n_tiles, n_maps, 8, tile), F32)
            + _nbytes((n_tiles, n_maps, dv + ONES_ROWS, tile), F32))


def _causal_flash(n_tiles, tile, q_ops_of, k_block, vt_block, scratch):
    buf_a, buf_b, m_all, acc_all = scratch[:2], scratch[2:4], scratch[4], scratch[5]
    n_off = n_tiles * (n_tiles - 1) // 2
    assert n_tiles % 2 == 0 and n_off % 2 == 0
    ones = jnp.ones((ONES_ROWS, tile), BF16)
    kk = lax.broadcasted_iota(jnp.int32, (tile, tile), 0)
    qq = lax.broadcasted_iota(jnp.int32, (tile, tile), 1)
    causal = kk <= qq

    def issue(qi, j, buf):
        s_ref, bm_ref = buf
        for i, (ka, qa) in enumerate(zip(k_block(j), q_ops_of(qi))):
            st = lax.dot_general(ka, qa, _NT, preferred_element_type=F32)
            s_ref[i] = st
            bm_ref[i] = jnp.max(st, axis=0, keepdims=True)

    half = tile // 2

    def issue_diag(qi, buf):
        s_ref, _ = buf
        for i, (ka, qa) in enumerate(zip(k_block(qi), q_ops_of(qi))):
            s_ref[i, :half, :half] = lax.dot_general(ka[:half], qa[:half], _NT,
                                                     preferred_element_type=F32)
            s_ref[i, :, half:] = lax.dot_general(ka, qa[half:], _NT, preferred_element_type=F32)

    def process_diag(buf, qi):
        s_ref, _ = buf
        for i, vt in enumerate(vt_block(qi)):
            vta = jnp.concatenate([vt, ones], axis=0)
            for rows, cols in ((slice(0, half), slice(0, half)), (slice(0, tile), slice(half, tile))):
                st = jnp.where(causal[rows, cols], s_ref[i, rows, cols], -jnp.inf)
                m_new = jnp.max(st, axis=0, keepdims=True)
                p = jnp.exp2(st - m_new).astype(BF16)
                acc_all[qi, i, :, cols] = jnp.dot(vta[:, rows], p, preferred_element_type=F32)
                m_all[qi, i, :, cols] = m_new

    def process(buf, qi, j):
        s_ref, bm_ref = buf
        for i, vt in enumerate(vt_block(j)):
            m = m_all[qi, i]
            m_new = jnp.maximum(m, bm_ref[i])
            alpha = jnp.exp2(m - m_new)
            p = jnp.exp2(s_ref[i] - m_new).astype(BF16)
            pv = jnp.dot(jnp.concatenate([vt, ones], axis=0), p, preferred_element_type=F32)
            acc_all[qi, i] = alpha * acc_all[qi, i] + pv
            m_all[qi, i] = m_new

    issue_diag(0, buf_a)

    for qi in range(0, n_tiles, 2):
        issue_diag(qi + 1, buf_b)
        process_diag(buf_a, qi)
        if qi + 2 < n_tiles:
            issue_diag(qi + 2, buf_a)
        else:
            issue(1, 0, buf_a)
        process_diag(buf_b, qi + 1)

    def following(qi, j):
        wrap = j + 1 == qi
        return jnp.where(wrap, qi + 1, qi), jnp.where(wrap, 0, j + 1)

    def off_pair(qi, j):
        qi1, j1 = following(qi, j)
        issue(qi1, j1, buf_b)
        process(buf_a, qi, j)
        qi2, j2 = following(qi1, j1)
        past = qi2 >= n_tiles
        qi2, j2 = jnp.where(past, 1, qi2), jnp.where(past, 0, j2)
        issue(qi2, j2, buf_a)
        process(buf_b, qi1, j1)
        return qi2, j2

    def off_trip(u, carry):
        for _ in range(OFF_UNROLL):
            carry = off_pair(*carry)
        return carry

    assert n_off % (2 * OFF_UNROLL) == 0
    lax.fori_loop(0, n_off // (2 * OFF_UNROLL), off_trip, (jnp.int32(1), jnp.int32(0)))


def _fox_attn_kernel(qa_ref, ka_ref, vt_ref, o_ref, *flash_scratch, tile):
    seq = qa_ref.shape[1]
    n_tiles = seq // tile
    d = HEAD_DIM

    def q_ops_of(qi):
        q0 = pl.multiple_of(qi * tile, tile)
        return [qa_ref[0, pl.ds(q0, tile), h2 * LANES:(h2 + 1) * LANES] for h2 in range(2)]

    def k_block(j):
        start = pl.multiple_of(j * tile, tile)
        return [ka_ref[0, pl.ds(start, tile), h2 * LANES:(h2 + 1) * LANES] for h2 in range(2)]

    def vt_block(j):
        vb = vt_ref[0, 0, j]
        return [vb, vb]

    _causal_flash(n_tiles, tile, q_ops_of, k_block, vt_block, flash_scratch)
    acc_all = flash_scratch[5]

    def emit(qi, carry):
        a0 = acc_all[qi, 0]
        a1 = acc_all[qi, 1]
        yt = jnp.concatenate([a0[:d] / a0[2 * d:2 * d + 1], a1[d:2 * d] / a1[2 * d:2 * d + 1]],
                             axis=0)
        o_ref[0, 0, qi] = yt.astype(o_ref.dtype)
        return carry

    lax.fori_loop(0, n_tiles, emit, 0)


def _fox_attn(qa, ka, vt):
    B, S, width = qa.shape
    tile = ATTN_TILE
    n_tiles = S // tile
    n_pairs = width // (2 * LANES)
    pair_block = pl.BlockSpec((1, S, 2 * LANES), lambda b, h: (b, 0, h))
    tiles_block = pl.BlockSpec((1, 1, n_tiles, LANES, tile), lambda b, h: (b, h, 0, 0, 0))
    block_bytes = (2 * 6 * _nbytes((S, LANES), BF16)
                   + _flash_scratch_bytes(2, n_tiles, tile, 2 * HEAD_DIM))
    return pl.pallas_call(
        functools.partial(_fox_attn_kernel, tile=tile),
        grid=(B, n_pairs),
        in_specs=[pair_block, pair_block, tiles_block],
        out_specs=tiles_block,
        out_shape=jax.ShapeDtypeStruct((B, n_pairs, n_tiles, LANES, tile), BF16),
        scratch_shapes=_flash_scratch(2, n_tiles, tile, 2 * HEAD_DIM),
        compiler_params=_params(2, block_bytes),
        name="fox_attn",
    )(qa, ka, vt)


def _diff_attn_kernel(q_ref, k_ref, vt_ref, lq1_ref, lk1_ref, lq2_ref, lk2_ref, g_ref, o_ref,
                      *flash_scratch, tile, lambda_init):
    seq = q_ref.shape[1]
    n_tiles = seq // tile
    lam = (jnp.exp(jnp.sum(lq1_ref[...] * lk1_ref[...], axis=1, keepdims=True))
           - jnp.exp(jnp.sum(lq2_ref[...] * lk2_ref[...], axis=1, keepdims=True)) + lambda_init)

    dv = 2 * HEAD_DIM

    def q_ops_of(qi):
        q0 = pl.multiple_of(qi * tile, tile)
        return [q_ref[0, pl.ds(q0, tile), m * LANES:(m + 1) * LANES] for m in range(2)]

    def k_block(j):
        kb = k_ref[0, pl.ds(pl.multiple_of(j * tile, tile), tile), :]
        return [kb, kb]

    def vt_block(j):
        vb = vt_ref[0, 0, j]
        return [vb, vb]

    _causal_flash(n_tiles, tile, q_ops_of, k_block, vt_block, flash_scratch)
    acc_all = flash_scratch[5]

    def emit(qi, carry):
        a1 = acc_all[qi, 0]
        a2 = acc_all[qi, 1]
        yt = a1[:dv] / a1[dv:dv + 1] - lam * (a2[:dv] / a2[dv:dv + 1])
        ms = jnp.mean(yt * yt, axis=0, keepdims=True)
        y = yt * lax.rsqrt(ms + NORM_EPS) * g_ref[...] * (1.0 - lambda_init)
        o_ref[0, 0, qi] = y.astype(o_ref.dtype)
        return carry

    lax.fori_loop(0, n_tiles, emit, 0)


def _diff_attn(qproj, k_sh, vt, lq1, lk1, lq2, lk2, subln_g, *, n_heads, lambda_init):
    B, S, _ = qproj.shape
    tile = ATTN_TILE
    n_tiles = S // tile
    vec = pl.BlockSpec((1, HEAD_DIM), lambda b, h: (0, 0))
    block_bytes = (2 * 4 * _nbytes((S, LANES), BF16)
                   + _flash_scratch_bytes(2, n_tiles, tile, 2 * HEAD_DIM))
    return pl.pallas_call(
        functools.partial(_diff_attn_kernel, tile=tile, lambda_init=lambda_init),
        grid=(B, n_heads),
        in_specs=[
            pl.BlockSpec((1, S, 2 * LANES), lambda b, h: (b, 0, h)),
            pl.BlockSpec((1, S, LANES), lambda b, h: (b, 0, h)),
            pl.BlockSpec((1, 1, n_tiles, LANES, tile), lambda b, h: (b, h, 0, 0, 0)),
            vec, vec, vec, vec,
            pl.BlockSpec((2 * HEAD_DIM, 1), lambda b, h: (0, 0)),
        ],
        out_specs=pl.BlockSpec((1, 1, n_tiles, LANES, tile), lambda b, h: (b, h, 0, 0, 0)),
        out_shape=jax.ShapeDtypeStruct((B, n_heads, n_tiles, LANES, tile), BF16),
        scratch_shapes=_flash_scratch(2, n_tiles, tile, 2 * HEAD_DIM),
        compiler_params=_params(2, block_bytes),
        name="diff_attn",
    )(qproj, k_sh, vt, lq1.reshape(1, -1), lk1.reshape(1, -1), lq2.reshape(1, -1),
      lk2.reshape(1, -1), subln_g.reshape(-1, 1))


def _mix_out(x, yt_ref, mq_ref, mk_ref, mv_ref, wo_ref):
    tm = x.shape[0]
    mw = mq_ref.shape[1]
    n_groups, n_sub = yt_ref.shape[1], yt_ref.shape[2]
    y_width = n_groups * LANES
    y = jnp.concatenate(
        [jnp.concatenate([yt_ref[0, g, t].astype(F32).T.astype(BF16) for t in range(n_sub)], axis=0)
         for g in range(n_groups)], axis=1)
    mq = mq_ref[...]
    mk = mk_ref[0]
    mv = mv_ref[0]
    q_head = lax.broadcasted_iota(jnp.int32, (tm, mw), 1) // HEAD_DIM
    v_head = lax.broadcasted_iota(jnp.int32, mv.shape, 1) // HEAD_DIM
    ymem = jnp.zeros((tm, mw), F32)
    for h in range(mw // HEAD_DIM):
        qh = jnp.where(q_head == h, mq, jnp.zeros_like(mq))
        s = lax.dot_general(qh, mk, _NT, preferred_element_type=F32)
        p = jnp.exp2(s - jnp.max(s, axis=1, keepdims=True))
        l = jnp.sum(p, axis=1, keepdims=True)
        vh = jnp.where(v_head == h, mv, jnp.zeros_like(mv))
        ymem = ymem + jnp.dot(p.astype(BF16), vh, preferred_element_type=F32) / l
    acc = jnp.dot(y, wo_ref[0:y_width, :], preferred_element_type=F32)
    acc = acc + jnp.dot(ymem.astype(BF16), wo_ref[y_width:y_width + mw, :],
                        preferred_element_type=F32)
    return x + acc


def _post_kernel(*refs, final, n_chunks):
    x_ref, y_ref, mq_ref, mk_ref, mv_ref, wo_ref, g_ref, wgu_ref, wd_ref = refs[:9]
    wo_ref, wgu_ref, wd_ref = wo_ref.at[0], wgu_ref.at[0], wd_ref.at[0]
    if final:
        fg_ref, o_ref, hn_ref, acc_ref = refs[9:]
    else:
        o_ref, hn_ref, acc_ref = refs[9:]
    x = _mix_out(x_ref[...], y_ref, mq_ref, mk_ref, mv_ref, wo_ref)
    hn_ref[...] = _rms(x, g_ref[...]).astype(BF16)
    acc_ref[...] = x

    d_ff = wd_ref.shape[0]
    tf = d_ff // n_chunks
    for k in range(n_chunks):
        h = hn_ref[...]
        gate = jnp.dot(h, wgu_ref[:, k * tf:(k + 1) * tf], preferred_element_type=F32)
        up = jnp.dot(h, wgu_ref[:, d_ff + k * tf:d_ff + (k + 1) * tf],
                     preferred_element_type=F32)
        a = gate * jax.nn.sigmoid(gate) * up
        acc_ref[...] += jnp.dot(a.astype(BF16), wd_ref[k * tf:(k + 1) * tf, :],
                                preferred_element_type=F32)
    out = acc_ref[...]
    if final:
        out = _rms(out, fg_ref[...])
    o_ref[...] = out


def _post(x2d, yt, proj2d, mq_col, memkv, wo, g, wgu, wd, *, layer, seq, final_g=None):
    T, D = x2d.shape
    n_groups, tile = yt.shape[1], yt.shape[4]
    yw = n_groups * LANES
    n_mem, mw = memkv.shape[1], memkv.shape[2] // 2
    d_ff = wd.shape[1]

    def slab(w):
        return pl.BlockSpec((1,) + w.shape[1:], lambda i: (layer, 0, 0), pipeline_mode=_RESIDENT)

    tm = FFN_TM
    assert T % tm == 0 and seq % tm == 0 and tm % tile == 0 and d_ff % FFN_TF == 0
    n_chunks = d_ff // FFN_TF
    n_seq_tiles = seq // tm
    final = final_g is not None
    in_specs = [
        pl.BlockSpec((tm, D), lambda i: (i, 0)),
        pl.BlockSpec((1, n_groups, tm // tile, LANES, tile),
                     lambda i: (i // n_seq_tiles, 0, i % n_seq_tiles, 0, 0)),
        pl.BlockSpec((tm, mw), lambda i: (i, mq_col)),
        pl.BlockSpec((1, n_mem, mw), lambda i: (i // n_seq_tiles, 0, 0)),
        pl.BlockSpec((1, n_mem, mw), lambda i: (i // n_seq_tiles, 0, 1)),
        slab(wo),
        pl.BlockSpec((1, D), lambda i: (0, 0)),
        slab(wgu),
        slab(wd),
    ]
    args = [x2d, yt, proj2d, memkv, memkv, wo, g.reshape(1, D), wgu, wd]
    if final:
        in_specs.append(pl.BlockSpec((1, D), lambda i: (0, 0)))
        args.append(final_g.reshape(1, D))
    block_bytes = (5 * _nbytes((tm, D), F32) + _nbytes((tm, D), BF16)
                   + 2 * _nbytes((tm, yw + mw), BF16) + 4 * _nbytes((n_mem, mw), BF16)
                   + _nbytes(wo.shape[1:], BF16) + _nbytes(wgu.shape[1:], BF16)
                   + _nbytes(wd.shape[1:], BF16))
    return pl.pallas_call(
        functools.partial(_post_kernel, final=final, n_chunks=n_chunks),
        grid=(T // tm,),
        in_specs=in_specs,
        out_specs=pl.BlockSpec((tm, D), lambda i: (i, 0)),
        out_shape=jax.ShapeDtypeStruct((T, D), F32),
        scratch_shapes=[pltpu.VMEM((tm, D), BF16), pltpu.VMEM((tm, D), F32)],
        compiler_params=_params(1, block_bytes),
        name="post_final" if final else "post",
    )(*args)


def _rope_tables(seq):
    half = HEAD_DIM // 2
    f32 = np.float32
    inv_freq = np.power(f32(ROPE_THETA), -np.arange(half, dtype=f32) * f32(2.0 / HEAD_DIM))
    ang = np.arange(seq, dtype=f32)[:, None] * inv_freq[None, :]
    cos, sin = np.cos(ang).astype(f32), np.sin(ang).astype(f32)
    reps = LANES // HEAD_DIM
    cos_t = np.tile(np.concatenate([cos, cos], axis=-1), (1, reps))
    sin_t = np.tile(np.concatenate([-sin, sin], axis=-1), (1, reps))
    return jnp.asarray(cos_t), jnp.asarray(sin_t)


def kernel(x, mem, attn_norm_g, mem_norm_g, w_mem_kv, w_out, ffn_norm_g, w_gate_up, w_down,
           a_w_in, a_b_f, b_w_in, b_lambda_q1, b_lambda_k1, b_lambda_q2, b_lambda_k2,
           b_subln_g, kv_norm_g, w_kv_shared, final_norm_g):
    B, S, D = x.shape
    depth = attn_norm_g.shape[0]
    n_a = a_w_in.shape[0]
    n_mem = mem.shape[1]
    mem_w = w_mem_kv.shape[2] // 2
    n_fox = a_b_f.shape[1]
    fox_w = n_fox * HEAD_DIM
    diff_w = b_w_in.shape[1] - mem_w
    n_diff = diff_w // (2 * HEAD_DIM)
    scale = HEAD_DIM ** -0.5 * LOG2E
    T = B * S

    x2d = x.reshape(T, D)
    mem2d = mem.reshape(B * n_mem, D)
    rope_tabs = _rope_tables(S)
    k_sh = vt_sh = None
    wo_all = w_out.astype(BF16)
    wgu_all = w_gate_up.astype(BF16)
    wd_all = w_down.astype(BF16)

    for layer in range(depth):
        (memkv,) = _proj(mem2d, [(mem_norm_g[layer], w_mem_kv[layer].astype(BF16),
                                  0, 2 * mem_w, 0)], seq=n_mem)
        memkv = memkv.reshape(B, n_mem, 2 * mem_w)
        if layer < n_a:
            w_in = a_w_in[layer]
            w_main = jnp.concatenate(
                [w_in[:, :fox_w] * scale, w_in[:, fox_w:2 * fox_w],
                 w_in[:, 3 * fox_w + n_fox:] * scale, w_in[:, 2 * fox_w:3 * fox_w]],
                axis=1).astype(BF16)
            gw = jnp.pad(w_in[:, 3 * fox_w:3 * fox_w + n_fox], ((0, 0), (0, LANES - n_fox)))
            gw_hi = gw.astype(BF16)
            gw_lo = (gw - gw_hi.astype(F32)).astype(BF16)
            gate_w = jnp.concatenate([gw_hi, gw_lo], axis=1)
            gate_b = jnp.pad(a_b_f[layer], (0, LANES - n_fox)).reshape(1, LANES)
            proj, vt, qa, ka = _proj(
                x2d, [(attn_norm_g[layer], w_main, 0, 2 * fox_w + mem_w, fox_w)], seq=S,
                gate_w=gate_w, gate_b=gate_b, n_gates=n_fox)
            y = _fox_attn(qa.reshape(B, S, -1), ka.reshape(B, S, -1), vt)
            mq_col = 0
        else:
            j = layer - n_a
            streams = [(attn_norm_g[layer], (b_w_in[j] * scale).astype(BF16),
                        diff_w, diff_w + mem_w, 0, True)]
            if layer == n_a:
                streams.append((kv_norm_g, w_kv_shared.astype(BF16), diff_w, diff_w,
                                w_kv_shared.shape[1] - diff_w))
                proj, k_sh, vt_sh = _proj(x2d, streams, seq=S, rope_tabs=rope_tabs)
                k_sh = k_sh.reshape(B, S, diff_w)
            else:
                (proj,) = _proj(x2d, streams, seq=S, rope_tabs=rope_tabs)
            lambda_init = 0.8 - 0.6 * math.exp(-0.3 * layer)
            y = _diff_attn(proj.reshape(B, S, -1), k_sh, vt_sh, b_lambda_q1[j], b_lambda_k1[j],
                           b_lambda_q2[j], b_lambda_k2[j], b_subln_g[j], n_heads=n_diff,
                           lambda_init=lambda_init)
            mq_col = 2 * diff_w // mem_w
        last = layer == depth - 1
        x2d = _post(x2d, y, proj, mq_col, memkv, wo_all, ffn_norm_g[layer],
                    wgu_all, wd_all, layer=layer, seq=S,
                    final_g=final_norm_g if last else None)
    return x2d.reshape(B, S, D)
```
